```python
import math
import jax, jax.numpy as jnp
from jax import lax
import numpy as np

D_MODEL = 2048
BATCH = 1
SEQ = 8192
DEPTH = 1
DEC_BATCH = 16
DEC_SEQ = 64
PAST_LEN = 2048

CHUNK = 64
N_MEM = 256
EPS = 1e-6

D_S5 = 1024
S5_GROUP = 16
S5_GROUPS = D_S5 // S5_GROUP
S5_STATE = 64
S5_DT_MIN = 1e-3
S5_DT_MAX = 1e-1

D_GLA = 1024
GLA_HEADS = 4
GLA_DK = 128
GLA_DV = D_GLA // GLA_HEADS
GLA_RANK = 16
GLA_TAU = 16.0

D_MIX = D_S5 + D_GLA
SPLITS = (D_S5,
          D_S5 + GLA_HEADS * GLA_DK,
          D_S5 + 2 * GLA_HEADS * GLA_DK,
          D_S5 + 2 * GLA_HEADS * GLA_DK + D_GLA,
          D_S5 + 2 * GLA_HEADS * GLA_DK + D_GLA + GLA_RANK)
D_IN = D_S5 + 2 * GLA_HEADS * GLA_DK + D_GLA + GLA_RANK + D_GLA

X_HEADS = 4
X_HD = D_MODEL // X_HEADS

N_EXPERTS = 64
TOP_K = 8
N_GROUPS = 8
TOPK_GROUPS = 4
D_EXPERT = 256
D_SHARED = 256
ROUTE_SCALE = 2.5

kernel_name = "hymba_s5_gla_memxattn_moe_stream_step"


def rms_norm(x, g):
    xf = x.astype(jnp.float32)
    y = xf * lax.rsqrt(jnp.mean(xf * xf, axis=-1, keepdims=True) + EPS)
    return (y * g.astype(jnp.float32)).astype(x.dtype)


def _cmul(ar, ai, br, bi):
    return ar * br - ai * bi, ar * bi + ai * br


def s5_mixer(u, h0_re, h0_im, lam_re, lam_im, log_dt, b_re, b_im, c_re, c_im, d_skip, w_glu, b_glu):
    f32 = jnp.float32
    bsz, seq = u.shape[0], u.shape[1]
    uf = u.astype(f32).reshape(bsz, seq, S5_GROUPS, S5_GROUP)
    dt = jnp.exp(log_dt.astype(f32))[:, None]
    lr, li = lam_re.astype(f32), lam_im.astype(f32)
    mag = jnp.exp(lr * dt)
    abar_re, abar_im = mag * jnp.cos(li * dt), mag * jnp.sin(li * dt)
    den = lr * lr + li * li
    nr = abar_re - 1.0
    coef_re = (nr * lr + abar_im * li) / den
    coef_im = (abar_im * lr - nr * li) / den
    bbar_re, bbar_im = _cmul(coef_re[..., None], coef_im[..., None],
                             b_re.astype(f32), b_im.astype(f32))
    bu_re = jnp.einsum('blgh,gph->blgp', uf, bbar_re)
    bu_im = jnp.einsum('blgh,gph->blgp', uf, bbar_im)
    i_re, i_im = _cmul(abar_re, abar_im, h0_re.astype(f32), h0_im.astype(f32))
    bu_re = bu_re.at[:, 0].add(i_re)
    bu_im = bu_im.at[:, 0].add(i_im)
    a_re = jnp.broadcast_to(abar_re, bu_re.shape)
    a_im = jnp.broadcast_to(abar_im, bu_im.shape)

    def combine(e1, e2):
        a1r, a1i, b1r, b1i = e1
        a2r, a2i, b2r, b2i = e2
        ar, ai = _cmul(a2r, a2i, a1r, a1i)
        br, bi = _cmul(a2r, a2i, b1r, b1i)
        return ar, ai, br + b2r, bi + b2i

    _, _, h_re, h_im = lax.associative_scan(combine, (a_re, a_im, bu_re, bu_im), axis=1)
    y = (jnp.einsum('blgp,ghp->blgh', h_re, c_re.astype(f32))
         - jnp.einsum('blgp,ghp->blgh', h_im, c_im.astype(f32))
         + d_skip.astype(f32) * uf)
    z = jax.nn.gelu(y.reshape(bsz, seq, D_S5)).astype(u.dtype)
    out = z * jax.nn.sigmoid(z @ w_glu + b_glu)
    return out, h_re[:, -1], h_im[:, -1]


def gla_mixer(q, k, v, a_lr, r, s0, w_a2, b_a2, g_head):
    f32 = jnp.float32
    bsz, seq = q.shape[0], q.shape[1]
    log_a = jax.nn.log_sigmoid(a_lr.astype(f32) @ w_a2.astype(f32) + b_a2.astype(f32)) / GLA_TAU
    qf = q.astype(f32).reshape(bsz, seq, GLA_HEADS, GLA_DK) * (GLA_DK ** -0.5)
    kf = k.astype(f32).reshape(bsz, seq, GLA_HEADS, GLA_DK)
    vf = v.astype(f32).reshape(bsz, seq, GLA_HEADS, GLA_DV)
    gf = log_a.reshape(bsz, seq, GLA_HEADS, GLA_DK)
    c = min(CHUNK, seq)
    n_chunks = -(-seq // c)
    pad = n_chunks * c - seq

    def to_chunks(t):
        t = jnp.pad(t, ((0, 0), (0, pad), (0, 0), (0, 0)))
        return t.reshape(bsz, n_chunks, c, GLA_HEADS, t.shape[-1]).transpose(1, 0, 3, 2, 4)

    causal = jnp.tril(jnp.ones((c, c), dtype=bool))

    def step(state, inp):
        qc, kc, vc, gc = inp
        bcum = jnp.cumsum(gc, axis=2)
        qe = qc * jnp.exp(bcum)
        ke = kc * jnp.exp(-bcum)
        att = jnp.where(causal, jnp.einsum('bhid,bhjd->bhij', qe, ke), 0.0)
        o = jnp.einsum('bhij,bhjv->bhiv', att, vc) + jnp.einsum('bhid,bhdv->bhiv', qe, state)
        blast = bcum[:, :, -1]
        state = (jnp.exp(blast)[..., None] * state
                 + jnp.einsum('bhjd,bhjv->bhdv', kc * jnp.exp(blast[:, :, None] - bcum), vc))
        return state, o

    s_final, o = lax.scan(step, s0.astype(f32),
                          (to_chunks(qf), to_chunks(kf), to_chunks(vf), to_chunks(gf)))
    o = o.transpose(1, 0, 3, 2, 4).reshape(bsz, n_chunks * c, GLA_HEADS, GLA_DV)[:, :seq]
    o = rms_norm(o, g_head).reshape(bsz, seq, D_GLA).astype(r.dtype)
    return o * jax.nn.silu(r), s_final


def memory_kv(mem, g_mem, w_k, w_v):
    bsz = mem.shape[0]
    m = rms_norm(mem, g_mem)
    mk = (m @ w_k).reshape(bsz, N_MEM, X_HEADS, X_HD)
    mv = (m @ w_v).reshape(bsz, N_MEM, X_HEADS, X_HD)
    return mk, mv


def memory_attention(x, mk, mv, w_q, w_o):
    bsz, seq, _ = x.shape
    q = (x @ w_q).reshape(bsz, seq, X_HEADS, X_HD)
    s = jnp.einsum('blhd,bmhd->bhlm', q.astype(jnp.float32), mk.astype(jnp.float32)) * (X_HD ** -0.5)
    p = jax.nn.softmax(s, axis=-1).astype(x.dtype)
    o = jnp.einsum('bhlm,bmhd->blhd', p, mv.astype(x.dtype)).reshape(bsz, seq, D_MODEL)
    return o @ w_o


def moe_ffn(x, w_router, router_bias, w_gate, w_up, w_down, ws_gate, ws_up, ws_down):
    f32 = jnp.float32
    bsz, seq, d = x.shape
    xt = x.reshape(-1, d)
    n_tok = xt.shape[0]
    scores = jax.nn.sigmoid((xt @ w_router).astype(f32))
    sel = scores + router_bias.astype(f32)
    grp = sel.reshape(n_tok, N_GROUPS, N_EXPERTS // N_GROUPS)
    grp_score = jnp.sum(lax.top_k(grp, 2)[0], axis=-1)
    _, gidx = lax.top_k(grp_score, TOPK_GROUPS)
    gmask = jnp.sum(jax.nn.one_hot(gidx, N_GROUPS, dtype=f32), axis=1) > 0
    emask = jnp.repeat(gmask, N_EXPERTS // N_GROUPS, axis=1)
    _, eidx = lax.top_k(jnp.where(emask, sel, -jnp.inf), TOP_K)
    gates = jnp.take_along_axis(scores, eidx, axis=-1)
    gates = gates / jnp.sum(gates, axis=-1, keepdims=True) * ROUTE_SCALE
    combine = jnp.sum(jax.nn.one_hot(eidx, N_EXPERTS, dtype=f32) * gates[..., None], axis=1)
    h = (jax.nn.silu(jnp.einsum('td,edf->tef', xt, w_gate))
         * jnp.einsum('td,edf->tef', xt, w_up))
    h = h * combine.astype(x.dtype)[..., None]
    routed = jnp.einsum('tef,efd->td', h, w_down)
    shared = (jax.nn.silu(xt @ ws_gate) * (xt @ ws_up)) @ ws_down
    return (routed + shared).reshape(bsz, seq, d)


def trunk_layer(x, mem_k, mem_v, s5_re0, s5_im0, gla0, lp):
    nx = rms_norm(x, lp['g_mix'])
    proj = nx @ lp['w_in']
    u, q, k, v, a_lr, r = jnp.split(proj, SPLITS, axis=-1)
    s5_out, h_re, h_im = s5_mixer(u, s5_re0, s5_im0, lp['lam_re'], lp['lam_im'], lp['log_dt'],
                                  lp['b_re'], lp['b_im'], lp['c_re'], lp['c_im'], lp['d_skip'],
                                  lp['w_glu'], lp['b_glu'])
    s5_out = rms_norm(s5_out, lp['g_s5_out'])
    gla_out, s_gla = gla_mixer(q, k, v, a_lr, r, gla0, lp['w_a2'], lp['b_a2'], lp['g_gla_head'])
    x = x + jnp.concatenate([s5_out, gla_out], axis=-1) @ lp['w_out']
    x = x + memory_attention(rms_norm(x, lp['g_xattn']), mem_k, mem_v, lp['w_q'], lp['w_o'])
    x = x + moe_ffn(rms_norm(x, lp['g_ffn']), lp['w_router'], lp['router_bias'], lp['w_gate'],
                    lp['w_up'], lp['w_down'], lp['ws_gate'], lp['ws_up'], lp['ws_down'])
    return x, h_re, h_im, s_gla


def setup_inputs(seed: int = 0) -> dict:
    key = jax.random.key(seed)
    ks = iter(jax.random.split(key, 64))
    f32 = jnp.float32

    def nrm(shape, scale):
        return jax.random.normal(next(ks), shape, f32) * scale

    def gain(shape):
        return 1.0 + nrm(shape, 0.02)

    lam_im = (math.pi * jnp.arange(S5_STATE, dtype=f32))[None, None, :] + nrm((DEPTH, S5_GROUPS, S5_STATE), 0.01)
    log_dt = jax.random.uniform(next(ks), (DEPTH, S5_GROUPS), f32,
                                math.log(S5_DT_MIN), math.log(S5_DT_MAX))
    return {
        "x_prompt": nrm((BATCH, SEQ, D_MODEL), 1.0),
        "x_sample": nrm((DEC_BATCH, DEC_SEQ, D_MODEL), 1.0),
        "cache_mem_k": nrm((DEPTH, DEC_BATCH, N_MEM, X_HEADS, X_HD), 1.0),
        "cache_mem_v": nrm((DEPTH, DEC_BATCH, N_MEM, X_HEADS, X_HD), 1.0),
        "state_s5_re": nrm((DEPTH, DEC_BATCH, S5_GROUPS, S5_STATE), 0.5),
        "state_s5_im": nrm((DEPTH, DEC_BATCH, S5_GROUPS, S5_STATE), 0.5),
        "state_gla": nrm((DEPTH, DEC_BATCH, GLA_HEADS, GLA_DK, GLA_DV), 0.5),
        "mem_prompt": nrm((BATCH, N_MEM, D_MODEL), 1.0),
        "w_in": nrm((DEPTH, D_MODEL, D_IN), D_MODEL ** -0.5),
        "w_a2": nrm((DEPTH, GLA_RANK, GLA_HEADS * GLA_DK), GLA_RANK ** -0.5),
        "b_a2": nrm((DEPTH, GLA_HEADS * GLA_DK), 0.1),
        "lam_re": -0.5 + nrm((DEPTH, S5_GROUPS, S5_STATE), 0.01),
        "lam_im": lam_im,
        "log_dt": log_dt,
        "b_re": nrm((DEPTH, S5_GROUPS, S5_STATE, S5_GROUP), (2 * S5_GROUP) ** -0.5),
        "b_im": nrm((DEPTH, S5_GROUPS, S5_STATE, S5_GROUP), (2 * S5_GROUP) ** -0.5),
        "c_re": nrm((DEPTH, S5_GROUPS, S5_GROUP, S5_STATE), S5_STATE ** -0.5),
        "c_im": nrm((DEPTH, S5_GROUPS, S5_GROUP, S5_STATE), S5_STATE ** -0.5),
        "d_skip": nrm((DEPTH, S5_GROUPS, S5_GROUP), 1.0),
        "w_glu": nrm((DEPTH, D_S5, D_S5), D_S5 ** -0.5),
        "b_glu": nrm((DEPTH, D_S5), 0.02),
        "g_s5_out": gain((DEPTH, D_S5)),
        "g_gla_head": gain((DEPTH, GLA_DV)),
        "w_out": nrm((DEPTH, D_MIX, D_MODEL), D_MIX ** -0.5),
        "g_mix": gain((DEPTH, D_MODEL)),
        "g_xattn": gain((DEPTH, D_MODEL)),
        "g_mem": gain((DEPTH, D_MODEL)),
        "w_q": nrm((DEPTH, D_MODEL, D_MODEL), D_MODEL ** -0.5),
        "w_k": nrm((DEPTH, D_MODEL, D_MODEL), D_MODEL ** -0.5),
        "w_v": nrm((DEPTH, D_MODEL, D_MODEL), D_MODEL ** -0.5),
        "w_o": nrm((DEPTH, D_MODEL, D_MODEL), D_MODEL ** -0.5),
        "g_ffn": gain((DEPTH, D_MODEL)),
        "w_router": nrm((DEPTH, D_MODEL, N_EXPERTS), D_MODEL ** -0.5),
        "router_bias": nrm((DEPTH, N_EXPERTS), 0.01),
        "w_gate": nrm((DEPTH, N_EXPERTS, D_MODEL, D_EXPERT), D_MODEL ** -0.5),
        "w_up": nrm((DEPTH, N_EXPERTS, D_MODEL, D_EXPERT), D_MODEL ** -0.5),
        "w_down": nrm((DEPTH, N_EXPERTS, D_EXPERT, D_MODEL), D_EXPERT ** -0.5),
        "ws_gate": nrm((DEPTH, D_MODEL, D_SHARED), D_MODEL ** -0.5),
        "ws_up": nrm((DEPTH, D_MODEL, D_SHARED), D_MODEL ** -0.5),
        "ws_down": nrm((DEPTH, D_SHARED, D_MODEL), D_SHARED ** -0.5),
        "g_final": gain((D_MODEL,)),
    }


def reference(x_prompt, x_sample, cache_mem_k, cache_mem_v, state_s5_re, state_s5_im, state_gla,
              mem_prompt, w_in, w_a2, b_a2, lam_re, lam_im, log_dt, b_re, b_im, c_re, c_im, d_skip,
              w_glu, b_glu, g_s5_out, g_gla_head, w_out, g_mix, g_xattn, g_mem, w_q, w_k, w_v, w_o,
              g_ffn, w_router, router_bias, w_gate, w_up, w_down, ws_gate, ws_up, ws_down, g_final):
    f32 = jnp.float32
    bp = x_prompt.shape[0]
    yp, ys = x_prompt, x_sample
    mk_list, mv_list = [], []
    re_p_list, im_p_list, gla_p_list = [], [], []
    re_s_list, im_s_list, gla_s_list = [], [], []
    for l in range(DEPTH):
        lp = dict(w_in=w_in[l], w_a2=w_a2[l], b_a2=b_a2[l], lam_re=lam_re[l], lam_im=lam_im[l],
                  log_dt=log_dt[l], b_re=b_re[l], b_im=b_im[l], c_re=c_re[l], c_im=c_im[l],
                  d_skip=d_skip[l], w_glu=w_glu[l], b_glu=b_glu[l], g_s5_out=g_s5_out[l],
                  g_gla_head=g_gla_head[l], w_out=w_out[l], g_mix=g_mix[l], g_xattn=g_xattn[l],
                  w_q=w_q[l], w_o=w_o[l], g_ffn=g_ffn[l], w_router=w_router[l],
                  router_bias=router_bias[l], w_gate=w_gate[l], w_up=w_up[l], w_down=w_down[l],
                  ws_gate=ws_gate[l], ws_up=ws_up[l], ws_down=ws_down[l])
        mk_p, mv_p = memory_kv(mem_prompt, g_mem[l], w_k[l], w_v[l])
        zero_s5 = jnp.zeros((bp, S5_GROUPS, S5_STATE), f32)
        zero_gla = jnp.zeros((bp, GLA_HEADS, GLA_DK, GLA_DV), f32)
        yp, re_p, im_p, gla_p = trunk_layer(yp, mk_p, mv_p, zero_s5, zero_s5, zero_gla, lp)
        ys, re_s, im_s, gla_s = trunk_layer(ys, cache_mem_k[l], cache_mem_v[l], state_s5_re[l],
                                            state_s5_im[l], state_gla[l], lp)
        mk_list.append(mk_p)
        mv_list.append(mv_p)
        re_p_list.append(re_p)
        im_p_list.append(im_p)
        gla_p_list.append(gla_p)
        re_s_list.append(re_s)
        im_s_list.append(im_s)
        gla_s_list.append(gla_s)
    y_prompt = rms_norm(yp, g_final)
    y_sample = rms_norm(ys, g_final)
    mem_k_prompt = jnp.stack(mk_list)
    mem_v_prompt = jnp.stack(mv_list)
    s5_re_prompt = jnp.stack(re_p_list)
    s5_im_prompt = jnp.stack(im_p_list)
    gla_prompt = jnp.stack(gla_p_list)
    s5_re_sample = jnp.stack(re_s_list)
    s5_im_sample = jnp.stack(im_s_list)
    gla_sample = jnp.stack(gla_s_list)
    return (y_prompt, y_sample, mem_k_prompt, mem_v_prompt, s5_re_prompt, s5_im_prompt, gla_prompt,
            s5_re_sample, s5_im_sample, gla_sample)
```

```python
import functools
import math

import jax
import jax.numpy as jnp
from jax import lax
from jax.experimental import pallas as pl
from jax.experimental.pallas import tpu as pltpu

F32 = jnp.float32
BF16 = jnp.bfloat16

EPS = 1e-6
CHUNK = 64
S5_GROUP = 16
S5_STATE = 64
GLA_HEADS = 4
GLA_DK = 128
GLA_RANK = 16
GLA_TAU = 16.0
X_HEADS = 4
N_EXPERTS = 64
N_GROUPS = 8
TOPK_GROUPS = 4
TOP_K = 8
ROUTE_SCALE = 2.5

LANES = 128
SUBLANES = 8
S5_BLOCK = SUBLANES
GROUPS_PER_SLAB = LANES // S5_GROUP
VMEM_LIMIT = 56 * 1024 * 1024


def _cparams(sem):
    return pltpu.CompilerParams(dimension_semantics=sem, vmem_limit_bytes=VMEM_LIMIT)


def _const_spec(shape):
    nd = len(shape)
    return pl.BlockSpec(shape, lambda *_: (0,) * nd, pipeline_mode=pl.Buffered(1))


def _rms(x, g):
    return x * lax.rsqrt(jnp.mean(x * x, axis=-1, keepdims=True) + EPS) * g


def _split_bf16(x):
    hi = x.astype(BF16)
    lo = (x - hi.astype(F32)).astype(BF16)
    return hi, lo


def _dot(a, b):
    return jnp.dot(a, b, preferred_element_type=F32)


def _dot_nt(a, b):
    return lax.dot_general(a, b, (((1,), (1,)), ((), ())), preferred_element_type=F32)


def _dot_tn(a, b):
    return lax.dot_general(a, b, (((0,), (0,)), ((), ())), preferred_element_type=F32)


def _memkv_kernel(mem_ref, g_ref, wk_ref, wv_ref, mk_ref, mv_ref):
    m = _rms(mem_ref[...], g_ref[...]).astype(BF16)
    mk_ref[...] = _dot(m, wk_ref[...])
    mv_ref[...] = _dot(m, wv_ref[...])


def _memory_kv(mem, g_mem, wk, wv):
    n, d = mem.shape
    tn = 512
    return pl.pallas_call(
        _memkv_kernel,
        grid=(d // tn,),
        in_specs=[_const_spec((n, d)), _const_spec((1, d)),
                  pl.BlockSpec((d, tn), lambda j: (0, j)),
                  pl.BlockSpec((d, tn), lambda j: (0, j))],
        out_specs=[pl.BlockSpec((n, tn), lambda j: (0, j))] * 2,
        out_shape=[jax.ShapeDtypeStruct((n, d), F32)] * 2,
        compiler_params=_cparams(("arbitrary",)),
        name="memory_kv",
    )(mem, g_mem.reshape(1, d), wk, wv)


def _inproj_kernel(x_ref, g_ref, wuh_ref, wul_ref, wqkv_ref, wa_ref, wr_ref,
                   u_ref, q_ref, k_ref, v_ref, a_ref, r_ref):
    nx = _rms(x_ref[...], g_ref[...])
    hi, lo = _split_bf16(nx)
    u = _dot(hi, wuh_ref[...]) + _dot(lo, wuh_ref[...]) + _dot(hi, wul_ref[...])
    for j in range(u_ref.shape[0]):
        u_ref[j] = u[:, j * LANES:(j + 1) * LANES]
    qkv = _dot(hi, wqkv_ref[...])
    dq = q_ref.shape[1]
    q_ref[...] = qkv[:, :dq].astype(BF16)
    k_ref[...] = qkv[:, dq:2 * dq].astype(BF16)
    v_ref[...] = qkv[:, 2 * dq:].astype(BF16)
    a_ref[...] = _dot(hi, wa_ref[...]) + _dot(lo, wa_ref[...])
    r_ref[...] = _dot(hi, wr_ref[...]).astype(BF16)


def _in_proj(x, g_mix, w_in, d_s5, d_qk, d_gla):
    t, d = x.shape
    tm = 512
    o1 = d_s5
    o2 = o1 + 2 * d_qk + d_gla
    o3 = o2 + GLA_RANK
    wu_hi, wu_lo = _split_bf16(w_in[:, :o1])
    wqkv = w_in[:, o1:o2].astype(BF16)
    wa = jnp.pad(w_in[:, o2:o3], ((0, 0), (0, LANES - GLA_RANK))).astype(BF16)
    wr = w_in[:, o3:].astype(BF16)
    n_slab = d_s5 // LANES
    row = lambda i: (i, 0)
    return pl.pallas_call(
        _inproj_kernel,
        grid=(t // tm,),
        in_specs=[pl.BlockSpec((tm, d), row), _const_spec((1, d)),
                  _const_spec(wu_hi.shape), _const_spec(wu_lo.shape), _const_spec(wqkv.shape),
                  _const_spec(wa.shape), _const_spec(wr.shape)],
        out_specs=[pl.BlockSpec((n_slab, tm, LANES), lambda i: (0, i, 0)),
                   pl.BlockSpec((tm, d_qk), row), pl.BlockSpec((tm, d_qk), row),
                   pl.BlockSpec((tm, d_gla), row), pl.BlockSpec((tm, LANES), row),
                   pl.BlockSpec((tm, d_gla), row)],
        out_shape=[jax.ShapeDtypeStruct((n_slab, t, LANES), F32),
                   jax.ShapeDtypeStruct((t, d_qk), BF16), jax.ShapeDtypeStruct((t, d_qk), BF16),
                   jax.ShapeDtypeStruct((t, d_gla), BF16), jax.ShapeDtypeStruct((t, LANES), F32),
                   jax.ShapeDtypeStruct((t, d_gla), BF16)],
        compiler_params=_cparams(("arbitrary",)),
        name="in_proj",
    )(x, g_mix.reshape(1, d), wu_hi, wu_lo, wqkv, wa, wr)


def _s5_tables(lam_re, lam_im, log_dt, b_re, b_im, c_re, c_im, d_skip):
    g, p = lam_re.shape
    h = b_re.shape[-1]
    nb = S5_BLOCK
    dt = jnp.exp(log_dt.astype(F32))[:, None]
    lr, li = lam_re.astype(F32), lam_im.astype(F32)
    mag = jnp.exp(lr * dt)
    ar, ai = mag * jnp.cos(li * dt), mag * jnp.sin(li * dt)
    den = lr * lr + li * li
    nr = ar - 1.0
    cf_r = (nr * lr + ai * li) / den
    cf_i = (ai * lr - nr * li) / den
    bb_r = cf_r[..., None] * b_re - cf_i[..., None] * b_im
    bb_i = cf_r[..., None] * b_im + cf_i[..., None] * b_re

    def cpow(n):
        m = jnp.exp(lr * dt * n)
        return m * jnp.cos(li * dt * n), m * jnp.sin(li * dt * n)

    pw = [cpow(float(n)) for n in range(nb + 1)]
    pr = jnp.stack([q[0] for q in pw])
    pi = jnp.stack([q[1] for q in pw])

    ns = g // GROUPS_PER_SLAB
    eye_g = jnp.eye(GROUPS_PER_SLAB, dtype=F32)

    cb_r = jnp.einsum('gcp,gph->gpch', c_re, bb_r) - jnp.einsum('gcp,gph->gpch', c_im, bb_i)
    cb_i = jnp.einsum('gcp,gph->gpch', c_re, bb_i) + jnp.einsum('gcp,gph->gpch', c_im, bb_r)
    taps = (jnp.einsum('ngp,gpch->ngch', pr[:nb], cb_r)
            - jnp.einsum('ngp,gpch->ngch', pi[:nb], cb_i))
    taps = taps.at[0].add(d_skip[:, :, None] * jnp.eye(h, dtype=F32)[None])
    s_idx = jnp.arange(nb)
    lag = s_idx[None, :] - s_idx[:, None]
    tk = taps[jnp.clip(lag, 0, nb - 1)]
    tk = jnp.where((lag >= 0)[:, :, None, None, None], tk, 0.0)
    tk = tk.reshape(nb, nb, ns, GROUPS_PER_SLAB, h, h)
    tm = jnp.einsum('stjgch,gk->jsghtkc', tk, eye_g)
    tm = tm.reshape(ns, nb * LANES, nb * LANES)

    wr = pr[nb - 1 - s_idx]
    wi = pi[nb - 1 - s_idx]
    inj_r = wr[..., None] * bb_r[None] - wi[..., None] * bb_i[None]
    inj_i = wr[..., None] * bb_i[None] + wi[..., None] * bb_r[None]
    inj = jnp.stack([inj_r, inj_i])
    inj = inj.reshape(2, nb, ns, GROUPS_PER_SLAB, p, h)
    pm = jnp.einsum('rsjgph,gk->jsghrkp', inj, eye_g)
    pm = pm.reshape(ns, nb * LANES, 2 * GROUPS_PER_SLAB * p)

    er, ei = pr[1:nb + 1], pi[1:nb + 1]
    q_r = c_re[None] * er[:, :, None, :] - c_im[None] * ei[:, :, None, :]
    q_i = -(c_re[None] * ei[:, :, None, :] + c_im[None] * er[:, :, None, :])
    qq = jnp.stack([q_r, q_i]).reshape(2, nb, ns, GROUPS_PER_SLAB, h, p)
    qm = jnp.einsum('rtjgcp,gk->jrgptkc', qq, eye_g)
    qm = qm.reshape(ns, 2 * GROUPS_PER_SLAB * p, nb * LANES)

    def slab(xr, xi):
        lead = xr.shape[:-2]
        xr = jnp.moveaxis(xr.reshape(lead + (ns, GROUPS_PER_SLAB * p)), -2, 0)
        xi = jnp.moveaxis(xi.reshape(lead + (ns, GROUPS_PER_SLAB * p)), -2, 0)
        return jnp.concatenate([xr, xi], axis=-1)

    rows = jnp.arange(SUBLANES)
    steps = []
    for sh in (1, 2, 4):
        mr, mi = cpow(float(nb * sh))
        keep = (rows >= sh).astype(F32)[:, None, None]
        steps.append(slab(mr[None] * keep, mi[None] * keep))
    step_m = jnp.stack(steps, axis=1)
    cr = jnp.stack([cpow(float(nb * r))[0] for r in range(SUBLANES)])
    ci = jnp.stack([cpow(float(nb * r))[1] for r in range(SUBLANES)])
    carry_m = slab(cr, ci)
    a8 = slab(*[z[None] for z in cpow(float(nb))])
    return tm, pm, qm, step_m, carry_m, a8


def _cmul(xr, xi, mr, mi):
    return xr * mr - xi * mi, xr * mi + xi * mr


def _s5_kernel(u_ref, ph_ref, pl_ref, tm_ref, qm_ref, step_ref, carry_ref, a8_ref, h0_ref,
               y_ref, hout_ref, s_scr, hin_scr, c_scr, *, rows_per_seq):
    ti = pl.program_id(1)
    rows = s_scr.shape[0]
    half = s_scr.shape[1] // 2
    nb = S5_BLOCK
    u8 = jnp.concatenate([u_ref[0, pl.ds(s, rows, stride=nb), :] for s in range(nb)], axis=1)
    hi, lo = _split_bf16(u8)
    s_scr[...] = _dot(hi, ph_ref[0]) + _dot(lo, ph_ref[0]) + _dot(hi, pl_ref[0])

    per_block_seq = rows_per_seq == SUBLANES
    if not per_block_seq:
        @pl.when(ti == 0)
        def _():
            c_scr[...] = h0_ref[0]

    not_first = (lax.broadcasted_iota(jnp.int32, (SUBLANES, 1), 0) >= 1).astype(F32)
    a8r, a8i = a8_ref[0, :, :half], a8_ref[0, :, half:]
    cmr, cmi = carry_ref[0, :, :half], carry_ref[0, :, half:]

    def body(rb, _):
        r0 = pl.multiple_of(rb * SUBLANES, SUBLANES)
        sb = s_scr[pl.ds(r0, SUBLANES), :]
        x = pltpu.roll(sb, 1, 0) * not_first
        xr, xi = x[:, :half], x[:, half:]
        for k, sh in enumerate((1, 2, 4)):
            m = step_ref[0, k]
            pr, pi = _cmul(pltpu.roll(xr, sh, 0), pltpu.roll(xi, sh, 0), m[:, :half], m[:, half:])
            xr, xi = xr + pr, xi + pi
        c = h0_ref[0, pl.ds(rb, 1), :] if per_block_seq else c_scr[...]
        cr, ci = c[:, :half], c[:, half:]
        pr, pi = _cmul(cr, ci, cmr, cmi)
        hr, hi_ = xr + pr, xi + pi
        hin_scr[pl.ds(r0, SUBLANES), :half] = hr
        hin_scr[pl.ds(r0, SUBLANES), half:] = hi_
        nr, ni = _cmul(hr[SUBLANES - 1:], hi_[SUBLANES - 1:], a8r, a8i)
        cn = jnp.concatenate([nr, ni], axis=1) + sb[SUBLANES - 1:]
        if per_block_seq:
            hout_ref[0, pl.ds(rb, 1), :] = cn
        else:
            c_scr[...] = cn
        return 0

    lax.fori_loop(0, rows // SUBLANES, body, 0)
    if not per_block_seq:
        hout_ref[0] = c_scr[...]

    y8 = _dot(hi, tm_ref[0]) + _dot(hin_scr[...].astype(BF16), qm_ref[0])
    for t in range(nb):
        y_ref[0, pl.ds(t, rows, stride=nb), :] = y8[:, t * LANES:(t + 1) * LANES]


def _s5_mixer(u_slabs, tables, h0, t_start, t, rows_per_seq, tile_rows):
    tm, pm, qm, step_m, carry_m, a8 = tables
    ns = u_slabs.shape[0]
    rows = t // S5_BLOCK
    n_tiles = rows // tile_rows
    assert rows % tile_rows == 0 and t_start % (tile_rows * S5_BLOCK) == 0
    off = t_start // (tile_rows * S5_BLOCK)
    n_seq = h0.shape[1]
    w = pm.shape[-1]
    pm_hi, pm_lo = _split_bf16(pm)
    slab3 = lambda j, i: (j, 0, 0)
    kern = functools.partial(_s5_kernel, rows_per_seq=rows_per_seq)
    return pl.pallas_call(
        kern,
        grid=(ns, n_tiles),
        in_specs=[pl.BlockSpec((1, tile_rows * S5_BLOCK, LANES), lambda j, i: (j, i + off, 0)),
                  pl.BlockSpec((1,) + pm.shape[1:], slab3),
                  pl.BlockSpec((1,) + pm.shape[1:], slab3),
                  pl.BlockSpec((1,) + tm.shape[1:], slab3),
                  pl.BlockSpec((1,) + qm.shape[1:], slab3),
                  pl.BlockSpec((1,) + step_m.shape[1:], lambda j, i: (j, 0, 0, 0)),
                  pl.BlockSpec((1,) + carry_m.shape[1:], slab3),
                  pl.BlockSpec((1,) + a8.shape[1:], slab3),
                  pl.BlockSpec((1, n_seq, w), slab3)],
        out_specs=[pl.BlockSpec((1, tile_rows * S5_BLOCK, LANES), lambda j, i: (j, i, 0)),
                   pl.BlockSpec((1, n_seq, w), slab3)],
        out_shape=[jax.ShapeDtypeStruct((ns, t, LANES), F32),
                   jax.ShapeDtypeStruct((ns, n_seq, w), F32)],
        scratch_shapes=[pltpu.VMEM((tile_rows, w), F32), pltpu.VMEM((tile_rows, w), F32),
                        pltpu.VMEM((1, w), F32)],
        compiler_params=_cparams(("arbitrary", "arbitrary")),
        name="s5_mixer",
    )(u_slabs, pm_hi, pm_lo, tm.astype(BF16), qm.astype(BF16), step_m, carry_m, a8, h0)


def _pack_s5_state(re, im, ns):
    b = re.shape[0]
    r = re.astype(F32).reshape(b, ns, -1)
    i = im.astype(F32).reshape(b, ns, -1)
    return jnp.transpose(jnp.concatenate([r, i], axis=-1), (1, 0, 2))


def _unpack_s5_state(hc, g, p):
    ns, b, w = hc.shape
    hc = jnp.transpose(hc, (1, 0, 2))
    re = hc[:, :, :w // 2].reshape(b, g, p)
    im = hc[:, :, w // 2:].reshape(b, g, p)
    return re, im


def _gla_kernel(q_ref, k_ref, v_ref, a_ref, r_ref, wa2_ref, ba2_ref, gh_ref, s0_ref,
                o_ref, sout_ref, st_scr, *, carry, n_chunks):
    step = pl.program_id(0)
    dk = GLA_DK
    dv = v_ref.shape[1] // GLA_HEADS
    c = CHUNK
    scale = dk ** -0.5
    ri = lax.broadcasted_iota(jnp.int32, (c, c), 0)
    ci = lax.broadcasted_iota(jnp.int32, (c, c), 1)
    causal = ri >= ci
    tril = causal.astype(BF16)
    eye_dk = (lax.broadcasted_iota(jnp.int32, (dk, dk), 0)
              == lax.broadcasted_iota(jnp.int32, (dk, dk), 1))

    if carry:
        @pl.when(step == 0)
        def _():
            st_scr[...] = s0_ref[0]

    wa_hi, wa_lo = _split_bf16(wa2_ref[...])
    for n in range(n_chunks):
        rows = slice(n * c, (n + 1) * c)
        a_hi, a_lo = _split_bf16(a_ref[rows, :])
        logit = _dot(a_hi, wa_hi) + _dot(a_lo, wa_hi) + _dot(a_hi, wa_lo) + ba2_ref[...]
        g = jax.nn.log_sigmoid(logit) * (1.0 / GLA_TAU)
        g1 = g.astype(BF16)
        g2r = g - g1.astype(F32)
        g2 = g2r.astype(BF16)
        g3 = (g2r - g2.astype(F32)).astype(BF16)
        bcum = _dot(tril, g1) + _dot(tril, g2) + _dot(tril, g3)
        for h in range(GLA_HEADS):
            ks = slice(h * dk, (h + 1) * dk)
            vs = slice(h * dv, (h + 1) * dv)
            b = bcum[:, ks]
            qh = q_ref[rows, ks].astype(F32) * scale
            kh = k_ref[rows, ks].astype(F32)
            vh = v_ref[rows, vs]
            state = s0_ref[n, h] if not carry else st_scr[h]
            qe = (qh * jnp.exp(b)).astype(BF16)
            ke = (kh * jnp.exp(-b)).astype(BF16)
            att = jnp.where(causal, _dot_nt(qe, ke), 0.0)
            o = _dot(att.astype(BF16), vh) + _dot(qe, state.astype(BF16))
            blast = b[c - 1:c, :]
            kd = (kh * jnp.exp(blast - b)).astype(BF16)
            decay = jnp.sum(jnp.where(eye_dk, jnp.exp(blast), 0.0), axis=1, keepdims=True)
            new_state = decay * state + _dot_tn(kd, vh)
            if carry:
                st_scr[h] = new_state
            else:
                sout_ref[n, h] = new_state
            on = _rms(o, gh_ref[...])
            rr = r_ref[rows, vs].astype(F32)
            o_ref[rows, vs] = (on * (rr * jax.nn.sigmoid(rr))).astype(BF16)
    if carry:
        sout_ref[0] = st_scr[...]


def _gla_mixer(q, k, v, a, r, wa2, ba2, g_head, s0, t_start, t, carry, n_chunks):
    dqk = q.shape[1]
    dvt = v.shape[1]
    hh, dk, dv = s0.shape[1:]
    rows = n_chunks * CHUNK
    assert t % rows == 0 and t_start % rows == 0
    off = t_start // rows
    row = lambda i: (i, 0)
    row_in = lambda i: (i + off, 0)
    if carry:
        sblk, smap = (1, hh, dk, dv), (lambda i: (0, 0, 0, 0))
    else:
        sblk, smap = (n_chunks, hh, dk, dv), (lambda i: (i, 0, 0, 0))
    kern = functools.partial(_gla_kernel, carry=carry, n_chunks=n_chunks)
    return pl.pallas_call(
        kern,
        grid=(t // rows,),
        in_specs=[pl.BlockSpec((rows, dqk), row_in), pl.BlockSpec((rows, dqk), row_in),
                  pl.BlockSpec((rows, dvt), row_in), pl.BlockSpec((rows, LANES), row_in),
                  pl.BlockSpec((rows, dvt), row_in),
                  _const_spec(wa2.shape), _const_spec((1, dqk)), _const_spec((1, dv)),
                  pl.BlockSpec(sblk, smap)],
        out_specs=[pl.BlockSpec((rows, dvt), row), pl.BlockSpec(sblk, smap)],
        out_shape=[jax.ShapeDtypeStruct((t, dvt), BF16), jax.ShapeDtypeStruct(s0.shape, F32)],
        scratch_shapes=[pltpu.VMEM((hh, dk, dv), F32)],
        compiler_params=_cparams(("arbitrary",)),
        name="gla_mixer",
    )(q, k, v, a, r, wa2, ba2.reshape(1, dqk), g_head.reshape(1, dv), s0)


def _mixout_kernel(y_ref, gla_ref, x_ref, wglu_ref, bglu_ref, gs5_ref, wout_ref, gx_ref, wq_ref,
                   x1_ref, q_ref):
    y = jnp.concatenate([y_ref[j] for j in range(y_ref.shape[0])], axis=1)
    z = jax.nn.gelu(y)
    gate = jax.nn.sigmoid(_dot(z.astype(BF16), wglu_ref[...]) + bglu_ref[...])
    s5 = _rms(z * gate, gs5_ref[...])
    cat = jnp.concatenate([s5.astype(BF16), gla_ref[...]], axis=1)
    x1 = x_ref[...] + _dot(cat, wout_ref[...])
    x1_ref[...] = x1
    q_ref[...] = _dot(_rms(x1, gx_ref[...]).astype(BF16), wq_ref[...]).astype(BF16)


def _mix_out(y_slabs, gla, x, w_glu, b_glu, g_s5, w_out, g_x, w_q):
    t, d = x.shape
    ns = y_slabs.shape[0]
    ds5 = ns * LANES
    dg = gla.shape[1]
    tm = 512
    row = lambda i: (i, 0)
    return pl.pallas_call(
        _mixout_kernel,
        grid=(t // tm,),
        in_specs=[pl.BlockSpec((ns, tm, LANES), lambda i: (0, i, 0)),
                  pl.BlockSpec((tm, dg), row), pl.BlockSpec((tm, d), row),
                  _const_spec((ds5, ds5)), _const_spec((1, ds5)), _const_spec((1, ds5)),
                  _const_spec((ds5 + dg, d)), _const_spec((1, d)), _const_spec((d, d))],
        out_specs=[pl.BlockSpec((tm, d), row), pl.BlockSpec((tm, d), row)],
        out_shape=[jax.ShapeDtypeStruct((t, d), F32), jax.ShapeDtypeStruct((t, d), BF16)],
        compiler_params=_cparams(("arbitrary",)),
        name="mix_out",
    )(y_slabs, gla, x, w_glu.astype(BF16), b_glu.reshape(1, ds5), g_s5.reshape(1, ds5),
      w_out.astype(BF16), g_x.reshape(1, d), w_q.astype(BF16))


def _xattn_kernel(q_ref, mk_ref, mv_ref, x1_ref, wo_ref, gf_ref, wrh_ref, wrl_ref,
                  x2_ref, xn_ref, lg_ref):
    nb, lt, d = q_ref.shape
    hd = d // X_HEADS
    scale = hd ** -0.5
    for b in range(nb):
        mk = mk_ref[b].astype(BF16)
        mv = mv_ref[b].astype(BF16)
        outs = []
        for h in range(X_HEADS):
            hs = slice(h * hd, (h + 1) * hd)
            s = _dot_nt(q_ref[b, :, hs], mk[:, hs]) * scale
            p = jnp.exp(s - jnp.max(s, axis=-1, keepdims=True))
            denom = jnp.sum(p, axis=-1, keepdims=True)
            outs.append((_dot(p.astype(BF16), mv[:, hs]) / denom).astype(BF16))
        o = jnp.concatenate(outs, axis=1)
        x2 = x1_ref[b] + _dot(o, wo_ref[...])
        x2_ref[b] = x2
        xn = _rms(x2, gf_ref[...])
        hi, lo = _split_bf16(xn)
        xn_ref[b] = hi
        lg_ref[:, b * lt:(b + 1) * lt] = (_dot_nt(wrh_ref[...], hi) + _dot_nt(wrh_ref[...], lo)
                                          + _dot_nt(wrl_ref[...], hi))


def _xattn(q, mk, mv, x1, w_o, g_ffn, wr_hi, wr_lo, nb, lt):
    bsz, l, d = q.shape
    nm = mk.shape[1]
    ne = wr_hi.shape[0]
    n_lt = l // lt
    blk = lambda i, j: (i, j, 0)
    mblk = (lambda i, j: (i, 0, 0)) if mk.shape[0] == bsz else (lambda i, j: (0, 0, 0))
    mb = nb if mk.shape[0] == bsz else 1
    assert mb == nb
    return pl.pallas_call(
        _xattn_kernel,
        grid=(bsz // nb, n_lt),
        in_specs=[pl.BlockSpec((nb, lt, d), blk),
                  pl.BlockSpec((mb, nm, d), mblk), pl.BlockSpec((mb, nm, d), mblk),
                  pl.BlockSpec((nb, lt, d), blk),
                  _const_spec((d, d)), _const_spec((1, d)),
                  _const_spec((ne, d)), _const_spec((ne, d))],
        out_specs=[pl.BlockSpec((nb, lt, d), blk), pl.BlockSpec((nb, lt, d), blk),
                   pl.BlockSpec((ne, nb * lt), lambda i, j: (0, i * n_lt + j))],
        out_shape=[jax.ShapeDtypeStruct((bsz, l, d), F32), jax.ShapeDtypeStruct((bsz, l, d), BF16),
                   jax.ShapeDtypeStruct((ne, bsz * l), F32)],
        compiler_params=_cparams(("arbitrary", "arbitrary")),
        name="mem_xattn",
    )(q, mk, mv, x1, w_o, g_ffn.reshape(1, d), wr_hi, wr_lo)


def _route_kernel(lg_ref, bias_ref, comb_ref):
    ne, tn = lg_ref.shape
    gsz = ne // N_GROUPS
    sc = jax.nn.sigmoid(lg_ref[...])
    sel = sc + bias_ref[...]
    s3 = sel.reshape(N_GROUPS, gsz, tn)
    ie = lax.broadcasted_iota(jnp.int32, s3.shape, 1).astype(F32)
    m1 = jnp.max(s3, axis=1, keepdims=True)
    first = jnp.min(jnp.where(s3 == m1, ie, float(gsz)), axis=1, keepdims=True)
    m2 = jnp.max(jnp.where(ie == first, -jnp.inf, s3), axis=1, keepdims=True)
    gs = m1 + m2
    ig = lax.broadcasted_iota(jnp.int32, gs.shape, 0)
    grank = jnp.zeros(gs.shape, jnp.int32)
    for g in range(N_GROUPS):
        other = gs[g:g + 1]
        ahead = jnp.where(other > gs, 1, jnp.where(other == gs, (ig > g).astype(jnp.int32), 0))
        grank = grank + ahead
    gkeep = jnp.broadcast_to(grank < TOPK_GROUPS, s3.shape)
    v = jnp.where(gkeep, s3, -jnp.inf).reshape(ne, tn)
    iv = lax.broadcasted_iota(jnp.int32, v.shape, 0)
    rank = jnp.zeros(v.shape, jnp.int32)
    for e in range(ne):
        other = v[e:e + 1]
        ahead = jnp.where(other > v, 1, jnp.where(other == v, (iv > e).astype(jnp.int32), 0))
        rank = rank + ahead
    gate = jnp.where(rank < TOP_K, sc, 0.0)
    comb_ref[...] = gate / jnp.sum(gate, axis=0, keepdims=True) * ROUTE_SCALE


def _route(logits_t, router_bias):
    ne, t = logits_t.shape
    tn = 512
    return pl.pallas_call(
        _route_kernel,
        grid=(t // tn,),
        in_specs=[pl.BlockSpec((ne, tn), lambda i: (0, i)), _const_spec((ne, 1))],
        out_specs=pl.BlockSpec((ne, tn), lambda i: (0, i)),
        out_shape=jax.ShapeDtypeStruct((ne, t), F32),
        compiler_params=_cparams(("arbitrary",)),
        name="router",
    )(logits_t, router_bias.astype(F32).reshape(ne, 1))


def _moe_kernel(xn_ref, comb_ref, wg_ref, wu_ref, wd_ref, sg_ref, su_ref, sd_ref, x2_ref, gf_ref,
                y_ref, acc_ref):
    e = pl.program_id(1)
    x = xn_ref[...]

    @pl.when(e == 0)
    def _():
        hs = jax.nn.silu(_dot(x, sg_ref[...])) * _dot(x, su_ref[...])
        acc_ref[...] = _dot(hs.astype(BF16), sd_ref[...])

    h = jax.nn.silu(_dot(x, wg_ref[0])) * _dot(x, wu_ref[0])
    ne = comb_ref.shape[1]
    pick = (lax.broadcasted_iota(jnp.int32, (ne, h.shape[1]), 0) == e).astype(BF16)
    c_hi, c_lo = _split_bf16(comb_ref[...])
    gate = _dot(c_hi, pick) + _dot(c_lo, pick)
    acc_ref[...] += _dot((h * gate).astype(BF16), wd_ref[0])

    @pl.when(e == pl.num_programs(1) - 1)
    def _():
        y_ref[...] = _rms(x2_ref[...] + acc_ref[...], gf_ref[...])


def _moe(xn, comb, w_gate, w_up, w_down, ws_gate, ws_up, ws_down, x2, g_final):
    t, d = xn.shape
    ne, _, de = w_gate.shape
    dsh = ws_gate.shape[1]
    tm = 512
    w_gate, w_up, w_down = w_gate.astype(BF16), w_up.astype(BF16), w_down.astype(BF16)
    row = lambda i, e: (i, 0)
    return pl.pallas_call(
        _moe_kernel,
        grid=(t // tm, ne),
        in_specs=[pl.BlockSpec((tm, d), row), pl.BlockSpec((tm, ne), row),
                  pl.BlockSpec((1, d, de), lambda i, e: (e, 0, 0)),
                  pl.BlockSpec((1, d, de), lambda i, e: (e, 0, 0)),
                  pl.BlockSpec((1, de, d), lambda i, e: (e, 0, 0)),
                  _const_spec((d, dsh)), _const_spec((d, dsh)), _const_spec((dsh, d)),
                  pl.BlockSpec((tm, d), row), _const_spec((1, d))],
        out_specs=pl.BlockSpec((tm, d), row),
        out_shape=jax.ShapeDtypeStruct((t, d), F32),
        scratch_shapes=[pltpu.VMEM((tm, d), F32)],
        compiler_params=_cparams(("arbitrary", "arbitrary")),
        name="moe_ffn",
    )(xn, comb, w_gate, w_up, w_down, ws_gate.astype(BF16), ws_up.astype(BF16),
      ws_down.astype(BF16), x2, g_final.reshape(1, d))


def kernel(x_prompt, x_sample, cache_mem_k, cache_mem_v, state_s5_re, state_s5_im, state_gla, mem_prompt, w_in, w_a2, b_a2, lam_re, lam_im, log_dt, b_re, b_im, c_re, c_im, d_skip, w_glu, b_glu, g_s5_out, g_gla_head, w_out, g_mix, g_xattn, g_mem, w_q, w_k, w_v, w_o, g_ffn, w_router, router_bias, w_gate, w_up, w_down, ws_gate, ws_up, ws_down, g_final):
    depth = w_in.shape[0]
    assert depth == 1, "one trunk layer"
    bp, lp, d = x_prompt.shape
    bs, ls, _ = x_sample.shape
    assert bp == 1 and ls == CHUNK and lp % CHUNK == 0
    n_mem = mem_prompt.shape[1]
    g, p = lam_re.shape[1:]
    d_s5 = g * S5_GROUP
    d_qk = GLA_HEADS * GLA_DK
    d_gla = w_out.shape[1] - d_s5
    ns = d_s5 // LANES
    tp, tsamp = bp * lp, bs * ls

    x_all = jnp.concatenate([x_prompt.reshape(tp, d), x_sample.reshape(tsamp, d)], axis=0)

    mk_p, mv_p = _memory_kv(mem_prompt[0], g_mem[0], w_k[0].astype(BF16), w_v[0].astype(BF16))

    u, q, k, v, a, r = _in_proj(x_all, g_mix[0], w_in[0], d_s5, d_qk, d_gla)

    tables = _s5_tables(lam_re[0], lam_im[0], log_dt[0], b_re[0], b_im[0], c_re[0], c_im[0],
                        d_skip[0])
    zero_h = jnp.zeros((ns, bp, 2 * GROUPS_PER_SLAB * p), F32)
    rows_p = lp // S5_BLOCK
    y_p, h_p = _s5_mixer(u, tables, zero_h, 0, tp, rows_per_seq=rows_p,
                         tile_rows=math.gcd(rows_p, 512))
    h0_s = _pack_s5_state(state_s5_re[0], state_s5_im[0], ns)
    y_s, h_s = _s5_mixer(u, tables, h0_s, tp, tsamp, rows_per_seq=ls // S5_BLOCK,
                         tile_rows=tsamp // S5_BLOCK)
    y_slabs = jnp.concatenate([y_p, y_s], axis=1)

    wa2 = jnp.pad(w_a2[0], ((0, LANES - GLA_RANK), (0, 0)))
    zero_s = jnp.zeros((bp,) + state_gla.shape[2:], F32)
    gla_p, sg_p = _gla_mixer(q, k, v, a, r, wa2, b_a2[0], g_gla_head[0], zero_s, 0, tp,
                             carry=True, n_chunks=4)
    gla_s, sg_s = _gla_mixer(q, k, v, a, r, wa2, b_a2[0], g_gla_head[0], state_gla[0], tp, tsamp,
                             carry=False, n_chunks=4)
    gla = jnp.concatenate([gla_p, gla_s], axis=0)

    x1, xq = _mix_out(y_slabs, gla, x_all, w_glu[0], b_glu[0], g_s5_out[0], w_out[0], g_xattn[0],
                      w_q[0])

    wr_hi, wr_lo = _split_bf16(jnp.transpose(w_router[0]))
    wo = w_o[0].astype(BF16)
    x2_p, xn_p, lg_p = _xattn(xq[:tp].reshape(bp, lp, d), mk_p[None], mv_p[None],
                              x1[:tp].reshape(bp, lp, d), wo, g_ffn[0], wr_hi, wr_lo, nb=1, lt=512)
    x2_s, xn_s, lg_s = _xattn(xq[tp:].reshape(bs, ls, d),
                              cache_mem_k[0].reshape(bs, n_mem, d),
                              cache_mem_v[0].reshape(bs, n_mem, d),
                              x1[tp:].reshape(bs, ls, d), wo, g_ffn[0], wr_hi, wr_lo, nb=2, lt=ls)
    x2 = jnp.concatenate([x2_p.reshape(tp, d), x2_s.reshape(tsamp, d)], axis=0)
    xn = jnp.concatenate([xn_p.reshape(tp, d), xn_s.reshape(tsamp, d)], axis=0)
    logits_t = jnp.concatenate([lg_p, lg_s], axis=1)

    comb_t = _route(logits_t, router_bias[0])
    y_all = _moe(xn, jnp.transpose(comb_t), w_gate[0], w_up[0], w_down[0], ws_gate[0], ws_up[0],
                 ws_down[0], x2, g_final)

    y_prompt = y_all[:tp].reshape(bp, lp, d)
    y_sample = y_all[tp:].reshape(bs, ls, d)
    xh = d // X_HEADS
    mem_k_prompt = mk_p.reshape(1, bp, n_mem, X_HEADS, xh)
    mem_v_prompt = mv_p.reshape(1, bp, n_mem, X_HEADS, xh)
    re_p, im_p = _unpack_s5_state(h_p, g, p)
    re_s, im_s = _unpack_s5_state(h_s, g, p)
    return (y_prompt, y_sample, mem_k_prompt, mem_v_prompt, re_p[None], im_p[None], sg_p[None],
            re_s[None], im_s[None], sg_s[None])
```

```python
import functools
import math

import jax
import jax.numpy as jnp
from jax import lax
from jax.experimental import pallas as pl
from jax.experimental.pallas import tpu as pltpu

F32 = jnp.float32
BF16 = jnp.bfloat16

EPS = 1e-6
CHUNK = 64
S5_GROUP = 16
S5_STATE = 64
GLA_HEADS = 4
GLA_DK = 128
GLA_RANK = 16
GLA_TAU = 16.0
X_HEADS = 4
N_EXPERTS = 64
N_GROUPS = 8
TOPK_GROUPS = 4
TOP_K = 8
ROUTE_SCALE = 2.5

LANES = 128
SUBLANES = 8
S5_BLOCK = SUBLANES
GROUPS_PER_SLAB = LANES // S5_GROUP
EXPERT_TILE = 256
MOE_TOKEN_TILE = 256
VMEM_LIMIT = 56 * 1024 * 1024


def _cparams(sem):
    return pltpu.CompilerParams(dimension_semantics=sem, vmem_limit_bytes=VMEM_LIMIT)


def _const_spec(shape):
    nd = len(shape)
    return pl.BlockSpec(shape, lambda *_: (0,) * nd, pipeline_mode=pl.Buffered(1))


def _rms(x, g):
    return x * lax.rsqrt(jnp.mean(x * x, axis=-1, keepdims=True) + EPS) * g


def _split_bf16(x):
    hi = x.astype(BF16)
    lo = (x - hi.astype(F32)).astype(BF16)
    return hi, lo


def _dot(a, b):
    return jnp.dot(a, b, preferred_element_type=F32)


def _dot_nt(a, b):
    return lax.dot_general(a, b, (((1,), (1,)), ((), ())), preferred_element_type=F32)


def _dot_tn(a, b):
    return lax.dot_general(a, b, (((0,), (0,)), ((), ())), preferred_element_type=F32)


def _bits(x):
    return lax.bitcast_convert_type(x, jnp.uint32)


def _pack_rows(x):
    w = x.shape[1] // 2
    lo = _bits(x[:, :w].astype(BF16).astype(F32))
    hi = _bits(x[:, w:].astype(BF16).astype(F32))
    return lax.shift_right_logical(lo, jnp.uint32(16)) | hi


def _unpack_rows_f32(p):
    lo = lax.bitcast_convert_type(lax.shift_left(p, jnp.uint32(16)), F32)
    hi = lax.bitcast_convert_type(p & jnp.uint32(0xFFFF0000), F32)
    return jnp.concatenate([lo, hi], axis=1)


def _memkv_kernel(mem_ref, g_ref, wk_ref, wv_ref, mk_ref, mv_ref):
    m = _rms(mem_ref[...], g_ref[...]).astype(BF16)
    mk_ref[...] = _dot(m, wk_ref[...])
    mv_ref[...] = _dot(m, wv_ref[...])


def _memory_kv(mem, g_mem, wk, wv):
    n, d = mem.shape
    tn = 512
    return pl.pallas_call(
        _memkv_kernel,
        grid=(d // tn,),
        in_specs=[_const_spec((n, d)), _const_spec((1, d)),
                  pl.BlockSpec((d, tn), lambda j: (0, j)),
                  pl.BlockSpec((d, tn), lambda j: (0, j))],
        out_specs=[pl.BlockSpec((n, tn), lambda j: (0, j))] * 2,
        out_shape=[jax.ShapeDtypeStruct((n, d), F32)] * 2,
        compiler_params=_cparams(("arbitrary",)),
        name="memory_kv",
    )(mem, g_mem.reshape(1, d), wk, wv)


def _inproj_kernel(x_ref, g_ref, wuh_ref, wul_ref, wqkv_ref, wa_ref, wr_ref,
                   u_ref, q_ref, k_ref, v_ref, a_ref, r_ref):
    nx = _rms(x_ref[...], g_ref[...])
    hi, lo = _split_bf16(nx)
    u = _dot(hi, wuh_ref[...]) + _dot(lo, wuh_ref[...]) + _dot(hi, wul_ref[...])
    for j in range(u_ref.shape[0]):
        u_ref[j] = u[:, j * LANES:(j + 1) * LANES]
    qkv = _dot(hi, wqkv_ref[...])
    dq = q_ref.shape[1]
    q_ref[...] = qkv[:, :dq].astype(BF16)
    k_ref[...] = qkv[:, dq:2 * dq].astype(BF16)
    v_ref[...] = qkv[:, 2 * dq:].astype(BF16)
    a_ref[...] = _dot(hi, wa_ref[...]) + _dot(lo, wa_ref[...])
    r_ref[...] = _dot(hi, wr_ref[...]).astype(BF16)


def _in_proj(x, g_mix, w_in, d_s5, d_qk, d_gla):
    t, d = x.shape
    tm = 512
    o1 = d_s5
    o2 = o1 + 2 * d_qk + d_gla
    o3 = o2 + GLA_RANK
    wu_hi, wu_lo = _split_bf16(w_in[:, :o1])
    wqkv = w_in[:, o1:o2].astype(BF16)
    wa = jnp.pad(w_in[:, o2:o3], ((0, 0), (0, LANES - GLA_RANK))).astype(BF16)
    wr = w_in[:, o3:].astype(BF16)
    n_slab = d_s5 // LANES
    row = lambda i: (i, 0)
    return pl.pallas_call(
        _inproj_kernel,
        grid=(t // tm,),
        in_specs=[pl.BlockSpec((tm, d), row), _const_spec((1, d)),
                  _const_spec(wu_hi.shape), _const_spec(wu_lo.shape), _const_spec(wqkv.shape),
                  _const_spec(wa.shape), _const_spec(wr.shape)],
        out_specs=[pl.BlockSpec((n_slab, tm, LANES), lambda i: (0, i, 0)),
                   pl.BlockSpec((tm, d_qk), row), pl.BlockSpec((tm, d_qk), row),
                   pl.BlockSpec((tm, d_gla), row), pl.BlockSpec((tm, LANES), row),
                   pl.BlockSpec((tm, d_gla), row)],
        out_shape=[jax.ShapeDtypeStruct((n_slab, t, LANES), F32),
                   jax.ShapeDtypeStruct((t, d_qk), BF16), jax.ShapeDtypeStruct((t, d_qk), BF16),
                   jax.ShapeDtypeStruct((t, d_gla), BF16), jax.ShapeDtypeStruct((t, LANES), F32),
                   jax.ShapeDtypeStruct((t, d_gla), BF16)],
        compiler_params=_cparams(("arbitrary",)),
        name="in_proj",
    )(x, g_mix.reshape(1, d), wu_hi, wu_lo, wqkv, wa, wr)


def _s5_tables(lam_re, lam_im, log_dt, b_re, b_im, c_re, c_im, d_skip):
    g, p = lam_re.shape
    h = b_re.shape[-1]
    nb = S5_BLOCK
    dt = jnp.exp(log_dt.astype(F32))[:, None]
    lr, li = lam_re.astype(F32), lam_im.astype(F32)
    mag = jnp.exp(lr * dt)
    ar, ai = mag * jnp.cos(li * dt), mag * jnp.sin(li * dt)
    den = lr * lr + li * li
    nr = ar - 1.0
    cf_r = (nr * lr + ai * li) / den
    cf_i = (ai * lr - nr * li) / den
    bb_r = cf_r[..., None] * b_re - cf_i[..., None] * b_im
    bb_i = cf_r[..., None] * b_im + cf_i[..., None] * b_re

    def cpow(n):
        m = jnp.exp(lr * dt * n)
        return m * jnp.cos(li * dt * n), m * jnp.sin(li * dt * n)

    pw = [cpow(float(n)) for n in range(nb + 1)]
    pr = jnp.stack([q[0] for q in pw])
    pi = jnp.stack([q[1] for q in pw])

    ns = g // GROUPS_PER_SLAB
    eye_g = jnp.eye(GROUPS_PER_SLAB, dtype=F32)

    cb_r = jnp.einsum('gcp,gph->gpch', c_re, bb_r) - jnp.einsum('gcp,gph->gpch', c_im, bb_i)
    cb_i = jnp.einsum('gcp,gph->gpch', c_re, bb_i) + jnp.einsum('gcp,gph->gpch', c_im, bb_r)
    taps = (jnp.einsum('ngp,gpch->ngch', pr[:nb], cb_r)
            - jnp.einsum('ngp,gpch->ngch', pi[:nb], cb_i))
    taps = taps.at[0].add(d_skip[:, :, None] * jnp.eye(h, dtype=F32)[None])
    s_idx = jnp.arange(nb)
    lag = s_idx[None, :] - s_idx[:, None]
    tk = taps[jnp.clip(lag, 0, nb - 1)]
    tk = jnp.where((lag >= 0)[:, :, None, None, None], tk, 0.0)
    tk = tk.reshape(nb, nb, ns, GROUPS_PER_SLAB, h, h)
    tm = jnp.einsum('stjgch,gk->jsghtkc', tk, eye_g)
    tm = tm.reshape(ns, nb * LANES, nb * LANES)

    wr = pr[nb - 1 - s_idx]
    wi = pi[nb - 1 - s_idx]
    inj_r = wr[..., None] * bb_r[None] - wi[..., None] * bb_i[None]
    inj_i = wr[..., None] * bb_i[None] + wi[..., None] * bb_r[None]
    inj = jnp.stack([inj_r, inj_i])
    inj = inj.reshape(2, nb, ns, GROUPS_PER_SLAB, p, h)
    pm = jnp.einsum('rsjgph,gk->jsghrkp', inj, eye_g)
    pm = pm.reshape(ns, nb * LANES, 2 * GROUPS_PER_SLAB * p)

    er, ei = pr[1:nb + 1], pi[1:nb + 1]
    q_r = c_re[None] * er[:, :, None, :] - c_im[None] * ei[:, :, None, :]
    q_i = -(c_re[None] * ei[:, :, None, :] + c_im[None] * er[:, :, None, :])
    qq = jnp.stack([q_r, q_i]).reshape(2, nb, ns, GROUPS_PER_SLAB, h, p)
    qm = jnp.einsum('rtjgcp,gk->jrgptkc', qq, eye_g)
    qm = qm.reshape(ns, 2 * GROUPS_PER_SLAB * p, nb * LANES)

    def slab(xr, xi):
        lead = xr.shape[:-2]
        xr = jnp.moveaxis(xr.reshape(lead + (ns, GROUPS_PER_SLAB * p)), -2, 0)
        xi = jnp.moveaxis(xi.reshape(lead + (ns, GROUPS_PER_SLAB * p)), -2, 0)
        return jnp.concatenate([xr, xi], axis=-1)

    rows = jnp.arange(SUBLANES)
    steps = []
    for sh in (1, 2, 4):
        mr, mi = cpow(float(nb * sh))
        keep = (rows >= sh).astype(F32)[:, None, None]
        steps.append(slab(mr[None] * keep, mi[None] * keep))
    step_m = jnp.stack(steps, axis=1)
    cr = jnp.stack([cpow(float(nb * r))[0] for r in range(SUBLANES)])
    ci = jnp.stack([cpow(float(nb * r))[1] for r in range(SUBLANES)])
    carry_m = slab(cr, ci)
    a8 = slab(*[z[None] for z in cpow(float(nb))])
    return tm, pm, qm, step_m, carry_m, a8


def _cmul(xr, xi, mr, mi):
    return xr * mr - xi * mi, xr * mi + xi * mr


def _s5_kernel(u_ref, ph_ref, pl_ref, tm_ref, qm_ref, step_ref, carry_ref, a8_ref, h0_ref,
               y_ref, hout_ref, s_scr, hin_scr, c_scr, *, rows_per_seq):
    ti = pl.program_id(1)
    rows = s_scr.shape[0]
    half = s_scr.shape[1] // 2
    nb = S5_BLOCK
    u8 = jnp.concatenate([u_ref[0, pl.ds(s, rows, stride=nb), :] for s in range(nb)], axis=1)
    hi, lo = _split_bf16(u8)
    s_scr[...] = _dot(hi, ph_ref[0]) + _dot(lo, ph_ref[0]) + _dot(hi, pl_ref[0])

    per_block_seq = rows_per_seq == SUBLANES
    if not per_block_seq:
        @pl.when(ti == 0)
        def _():
            c_scr[...] = h0_ref[0]

    not_first = (lax.broadcasted_iota(jnp.int32, (SUBLANES, 1), 0) >= 1).astype(F32)
    a8r, a8i = a8_ref[0, :, :half], a8_ref[0, :, half:]
    cmr, cmi = carry_ref[0, :, :half], carry_ref[0, :, half:]

    def body(rb, _):
        r0 = pl.multiple_of(rb * SUBLANES, SUBLANES)
        sb = s_scr[pl.ds(r0, SUBLANES), :]
        x = pltpu.roll(sb, 1, 0) * not_first
        xr, xi = x[:, :half], x[:, half:]
        for k, sh in enumerate((1, 2, 4)):
            m = step_ref[0, k]
            pr, pi = _cmul(pltpu.roll(xr, sh, 0), pltpu.roll(xi, sh, 0), m[:, :half], m[:, half:])
            xr, xi = xr + pr, xi + pi
        c = h0_ref[0, pl.ds(rb, 1), :] if per_block_seq else c_scr[...]
        cr, ci = c[:, :half], c[:, half:]
        pr, pi = _cmul(cr, ci, cmr, cmi)
        hr, hi_ = xr + pr, xi + pi
        hin_scr[pl.ds(r0, SUBLANES), :half] = hr
        hin_scr[pl.ds(r0, SUBLANES), half:] = hi_
        nr, ni = _cmul(hr[SUBLANES - 1:], hi_[SUBLANES - 1:], a8r, a8i)
        cn = jnp.concatenate([nr, ni], axis=1) + sb[SUBLANES - 1:]
        if per_block_seq:
            hout_ref[0, pl.ds(rb, 1), :] = cn
        else:
            c_scr[...] = cn
        return 0

    lax.fori_loop(0, rows // SUBLANES, body, 0)
    if not per_block_seq:
        hout_ref[0] = c_scr[...]

    y8 = _dot(hi, tm_ref[0]) + _dot(hin_scr[...].astype(BF16), qm_ref[0])
    for t in range(nb):
        y_ref[0, pl.ds(t, rows, stride=nb), :] = y8[:, t * LANES:(t + 1) * LANES]


def _s5_mixer(u_slabs, tables, h0, t_start, t, rows_per_seq, tile_rows):
    tm, pm, qm, step_m, carry_m, a8 = tables
    ns = u_slabs.shape[0]
    rows = t // S5_BLOCK
    n_tiles = rows // tile_rows
    assert rows % tile_rows == 0 and t_start % (tile_rows * S5_BLOCK) == 0
    off = t_start // (tile_rows * S5_BLOCK)
    n_seq = h0.shape[1]
    w = pm.shape[-1]
    pm_hi, pm_lo = _split_bf16(pm)
    slab3 = lambda j, i: (j, 0, 0)
    kern = functools.partial(_s5_kernel, rows_per_seq=rows_per_seq)
    return pl.pallas_call(
        kern,
        grid=(ns, n_tiles),
        in_specs=[pl.BlockSpec((1, tile_rows * S5_BLOCK, LANES), lambda j, i: (j, i + off, 0)),
                  pl.BlockSpec((1,) + pm.shape[1:], slab3),
                  pl.BlockSpec((1,) + pm.shape[1:], slab3),
                  pl.BlockSpec((1,) + tm.shape[1:], slab3),
                  pl.BlockSpec((1,) + qm.shape[1:], slab3),
                  pl.BlockSpec((1,) + step_m.shape[1:], lambda j, i: (j, 0, 0, 0)),
                  pl.BlockSpec((1,) + carry_m.shape[1:], slab3),
                  pl.BlockSpec((1,) + a8.shape[1:], slab3),
                  pl.BlockSpec((1, n_seq, w), slab3)],
        out_specs=[pl.BlockSpec((1, tile_rows * S5_BLOCK, LANES), lambda j, i: (j, i, 0)),
                   pl.BlockSpec((1, n_seq, w), slab3)],
        out_shape=[jax.ShapeDtypeStruct((ns, t, LANES), F32),
                   jax.ShapeDtypeStruct((ns, n_seq, w), F32)],
        scratch_shapes=[pltpu.VMEM((tile_rows, w), F32), pltpu.VMEM((tile_rows, w), F32),
                        pltpu.VMEM((1, w), F32)],
        compiler_params=_cparams(("arbitrary", "arbitrary")),
        name="s5_mixer",
    )(u_slabs, pm_hi, pm_lo, tm.astype(BF16), qm.astype(BF16), step_m, carry_m, a8, h0)


def _pack_s5_state(re, im, ns):
    b = re.shape[0]
    r = re.astype(F32).reshape(b, ns, -1)
    i = im.astype(F32).reshape(b, ns, -1)
    return jnp.transpose(jnp.concatenate([r, i], axis=-1), (1, 0, 2))


def _unpack_s5_state(hc, g, p):
    ns, b, w = hc.shape
    hc = jnp.transpose(hc, (1, 0, 2))
    re = hc[:, :, :w // 2].reshape(b, g, p)
    im = hc[:, :, w // 2:].reshape(b, g, p)
    return re, im


def _gla_kernel(q_ref, k_ref, v_ref, a_ref, r_ref, wa2_ref, ba2_ref, gh_ref, s0_ref,
                o_ref, sout_ref, st_scr, *, carry, n_chunks):
    step = pl.program_id(0)
    dk = GLA_DK
    dv = v_ref.shape[1] // GLA_HEADS
    c = CHUNK
    scale = dk ** -0.5
    ri = lax.broadcasted_iota(jnp.int32, (c, c), 0)
    ci = lax.broadcasted_iota(jnp.int32, (c, c), 1)
    causal = ri >= ci
    tril = causal.astype(BF16)
    eye_dk = (lax.broadcasted_iota(jnp.int32, (dk, dk), 0)
              == lax.broadcasted_iota(jnp.int32, (dk, dk), 1))

    if carry:
        @pl.when(step == 0)
        def _():
            st_scr[...] = s0_ref[0]

    wa_hi, wa_lo = _split_bf16(wa2_ref[...])
    for n in range(n_chunks):
        rows = slice(n * c, (n + 1) * c)
        a_hi, a_lo = _split_bf16(a_ref[rows, :])
        logit = _dot(a_hi, wa_hi) + _dot(a_lo, wa_hi) + _dot(a_hi, wa_lo) + ba2_ref[...]
        g = jax.nn.log_sigmoid(logit) * (1.0 / GLA_TAU)
        g1 = g.astype(BF16)
        g2r = g - g1.astype(F32)
        g2 = g2r.astype(BF16)
        g3 = (g2r - g2.astype(F32)).astype(BF16)
        bcum = _dot(tril, g1) + _dot(tril, g2) + _dot(tril, g3)
        for h in range(GLA_HEADS):
            ks = slice(h * dk, (h + 1) * dk)
            vs = slice(h * dv, (h + 1) * dv)
            b = bcum[:, ks]
            qh = q_ref[rows, ks].astype(F32) * scale
            kh = k_ref[rows, ks].astype(F32)
            vh = v_ref[rows, vs]
            state = s0_ref[n, h] if not carry else st_scr[h]
            qe = (qh * jnp.exp(b)).astype(BF16)
            ke = (kh * jnp.exp(-b)).astype(BF16)
            att = jnp.where(causal, _dot_nt(qe, ke), 0.0)
            o = _dot(att.astype(BF16), vh) + _dot(qe, state.astype(BF16))
            blast = b[c - 1:c, :]
            kd = (kh * jnp.exp(blast - b)).astype(BF16)
            decay = jnp.sum(jnp.where(eye_dk, jnp.exp(blast), 0.0), axis=1, keepdims=True)
            new_state = decay * state + _dot_tn(kd, vh)
            if carry:
                st_scr[h] = new_state
            else:
                sout_ref[n, h] = new_state
            on = _rms(o, gh_ref[...])
            rr = r_ref[rows, vs].astype(F32)
            o_ref[rows, vs] = (on * (rr * jax.nn.sigmoid(rr))).astype(BF16)
    if carry:
        sout_ref[0] = st_scr[...]


def _gla_mixer(q, k, v, a, r, wa2, ba2, g_head, s0, t_start, t, carry, n_chunks):
    dqk = q.shape[1]
    dvt = v.shape[1]
    hh, dk, dv = s0.shape[1:]
    rows = n_chunks * CHUNK
    assert t % rows == 0 and t_start % rows == 0
    off = t_start // rows
    row = lambda i: (i, 0)
    row_in = lambda i: (i + off, 0)
    if carry:
        sblk, smap = (1, hh, dk, dv), (lambda i: (0, 0, 0, 0))
    else:
        sblk, smap = (n_chunks, hh, dk, dv), (lambda i: (i, 0, 0, 0))
    kern = functools.partial(_gla_kernel, carry=carry, n_chunks=n_chunks)
    return pl.pallas_call(
        kern,
        grid=(t // rows,),
        in_specs=[pl.BlockSpec((rows, dqk), row_in), pl.BlockSpec((rows, dqk), row_in),
                  pl.BlockSpec((rows, dvt), row_in), pl.BlockSpec((rows, LANES), row_in),
                  pl.BlockSpec((rows, dvt), row_in),
                  _const_spec(wa2.shape), _const_spec((1, dqk)), _const_spec((1, dv)),
                  pl.BlockSpec(sblk, smap)],
        out_specs=[pl.BlockSpec((rows, dvt), row), pl.BlockSpec(sblk, smap)],
        out_shape=[jax.ShapeDtypeStruct((t, dvt), BF16), jax.ShapeDtypeStruct(s0.shape, F32)],
        scratch_shapes=[pltpu.VMEM((hh, dk, dv), F32)],
        compiler_params=_cparams(("arbitrary",)),
        name="gla_mixer",
    )(q, k, v, a, r, wa2, ba2.reshape(1, dqk), g_head.reshape(1, dv), s0)


def _mixout_kernel(y_ref, gla_ref, x_ref, wglu_ref, bglu_ref, gs5_ref, wout_ref, gx_ref, wq_ref,
                   x1_ref, q_ref):
    y = jnp.concatenate([y_ref[j] for j in range(y_ref.shape[0])], axis=1)
    z = jax.nn.gelu(y)
    gate = jax.nn.sigmoid(_dot(z.astype(BF16), wglu_ref[...]) + bglu_ref[...])
    s5 = _rms(z * gate, gs5_ref[...])
    cat = jnp.concatenate([s5.astype(BF16), gla_ref[...]], axis=1)
    x1 = x_ref[...] + _dot(cat, wout_ref[...])
    x1_ref[...] = x1
    q_ref[...] = _dot(_rms(x1, gx_ref[...]).astype(BF16), wq_ref[...]).astype(BF16)


def _mix_out(y_slabs, gla, x, w_glu, b_glu, g_s5, w_out, g_x, w_q):
    t, d = x.shape
    ns = y_slabs.shape[0]
    ds5 = ns * LANES
    dg = gla.shape[1]
    tm = 512
    row = lambda i: (i, 0)
    return pl.pallas_call(
        _mixout_kernel,
        grid=(t // tm,),
        in_specs=[pl.BlockSpec((ns, tm, LANES), lambda i: (0, i, 0)),
                  pl.BlockSpec((tm, dg), row), pl.BlockSpec((tm, d), row),
                  _const_spec((ds5, ds5)), _const_spec((1, ds5)), _const_spec((1, ds5)),
                  _const_spec((ds5 + dg, d)), _const_spec((1, d)), _const_spec((d, d))],
        out_specs=[pl.BlockSpec((tm, d), row), pl.BlockSpec((tm, d), row)],
        out_shape=[jax.ShapeDtypeStruct((t, d), F32), jax.ShapeDtypeStruct((t, d), BF16)],
        compiler_params=_cparams(("arbitrary",)),
        name="mix_out",
    )(y_slabs, gla, x, w_glu.astype(BF16), b_glu.reshape(1, ds5), g_s5.reshape(1, ds5),
      w_out.astype(BF16), g_x.reshape(1, d), w_q.astype(BF16))


def _xattn_kernel(q_ref, mk_ref, mv_ref, x1_ref, wo_ref, gf_ref, wrh_ref, wrl_ref,
                  x2_ref, xn_ref, lg_ref):
    nb, lt, d = q_ref.shape
    hd = d // X_HEADS
    scale = hd ** -0.5
    for b in range(nb):
        mk = mk_ref[b].astype(BF16)
        mv = mv_ref[b].astype(BF16)
        outs = []
        for h in range(X_HEADS):
            hs = slice(h * hd, (h + 1) * hd)
            s = _dot_nt(q_ref[b, :, hs], mk[:, hs]) * scale
            p = jnp.exp(s - jnp.max(s, axis=-1, keepdims=True))
            denom = jnp.sum(p, axis=-1, keepdims=True)
            outs.append((_dot(p.astype(BF16), mv[:, hs]) / denom).astype(BF16))
        o = jnp.concatenate(outs, axis=1)
        x2 = x1_ref[b] + _dot(o, wo_ref[...])
        x2_ref[b] = x2
        xn = _rms(x2, gf_ref[...])
        hi, lo = _split_bf16(xn)
        xn_ref[b] = _pack_rows(xn)
        lg_ref[:, b * lt:(b + 1) * lt] = (_dot_nt(wrh_ref[...], hi) + _dot_nt(wrh_ref[...], lo)
                                          + _dot_nt(wrl_ref[...], hi))


def _xattn(q, mk, mv, x1, w_o, g_ffn, wr_hi, wr_lo, nb, lt):
    bsz, l, d = q.shape
    nm = mk.shape[1]
    ne = wr_hi.shape[0]
    n_lt = l // lt
    blk = lambda i, j: (i, j, 0)
    mblk = (lambda i, j: (i, 0, 0)) if mk.shape[0] == bsz else (lambda i, j: (0, 0, 0))
    mb = nb if mk.shape[0] == bsz else 1
    assert mb == nb
    return pl.pallas_call(
        _xattn_kernel,
        grid=(bsz // nb, n_lt),
        in_specs=[pl.BlockSpec((nb, lt, d), blk),
                  pl.BlockSpec((mb, nm, d), mblk), pl.BlockSpec((mb, nm, d), mblk),
                  pl.BlockSpec((nb, lt, d), blk),
                  _const_spec((d, d)), _const_spec((1, d)),
                  _const_spec((ne, d)), _const_spec((ne, d))],
        out_specs=[pl.BlockSpec((nb, lt, d), blk), pl.BlockSpec((nb, lt, d // 2), blk),
                   pl.BlockSpec((ne, nb * lt), lambda i, j: (0, i * n_lt + j))],
        out_shape=[jax.ShapeDtypeStruct((bsz, l, d), F32),
                   jax.ShapeDtypeStruct((bsz, l, d // 2), jnp.uint32),
                   jax.ShapeDtypeStruct((ne, bsz * l), F32)],
        compiler_params=_cparams(("arbitrary", "arbitrary")),
        name="mem_xattn",
    )(q, mk, mv, x1, w_o, g_ffn.reshape(1, d), wr_hi, wr_lo)


def _route_kernel(lg_ref, bias_ref, eidx_ref, gate_ref, pos_ref, cnt_ref, run_scr):
    ne, tn = lg_ref.shape

    @pl.when(pl.program_id(0) == 0)
    def _():
        run_scr[...] = jnp.zeros_like(run_scr)

    gsz = ne // N_GROUPS
    sc = jax.nn.sigmoid(lg_ref[...])
    sel = sc + bias_ref[...]
    s3 = sel.reshape(N_GROUPS, gsz, tn)
    ie = lax.broadcasted_iota(jnp.int32, s3.shape, 1).astype(F32)
    m1 = jnp.max(s3, axis=1, keepdims=True)
    first = jnp.min(jnp.where(s3 == m1, ie, float(gsz)), axis=1, keepdims=True)
    m2 = jnp.max(jnp.where(ie == first, -jnp.inf, s3), axis=1, keepdims=True)
    gs = m1 + m2
    ig = lax.broadcasted_iota(jnp.int32, gs.shape, 0)
    grank = jnp.zeros(gs.shape, jnp.int32)
    for g in range(N_GROUPS):
        other = gs[g:g + 1]
        ahead = jnp.where(other > gs, 1, jnp.where(other == gs, (ig > g).astype(jnp.int32), 0))
        grank = grank + ahead
    gkeep = jnp.broadcast_to(grank < TOPK_GROUPS, s3.shape)
    v = jnp.where(gkeep, s3, -jnp.inf).reshape(ne, tn)
    iv = lax.broadcasted_iota(jnp.int32, v.shape, 0)
    rank = jnp.zeros(v.shape, jnp.int32)
    for e in range(ne):
        other = v[e:e + 1]
        ahead = jnp.where(other > v, 1, jnp.where(other == v, (iv > e).astype(jnp.int32), 0))
        rank = rank + ahead
    chosen = rank < TOP_K
    gate = jnp.where(chosen, sc, 0.0)
    gate = gate / jnp.sum(gate, axis=0, keepdims=True) * ROUTE_SCALE
    ti = lax.broadcasted_iota(jnp.int32, (tn, tn), 0)
    tj = lax.broadcasted_iota(jnp.int32, (tn, tn), 1)
    incl = _dot(jnp.where(chosen, 1.0, 0.0).astype(BF16), (ti <= tj).astype(BF16))
    pos = run_scr[...] + incl - 1.0
    run_scr[...] = run_scr[...] + incl[:, tn - 1:tn]
    cnt_ref[...] = run_scr[...].astype(jnp.int32)
    ef = iv.astype(F32)
    for k in range(TOP_K):
        hit = rank == k
        pick = lambda a: jnp.sum(jnp.where(hit, a, 0.0), axis=0, keepdims=True)
        eidx_ref[k:k + 1, :] = pick(ef).astype(jnp.int32)
        gate_ref[k:k + 1, :] = pick(gate)
        pos_ref[k:k + 1, :] = pick(pos).astype(jnp.int32)


def _route(logits_t, router_bias):
    ne, t = logits_t.shape
    tn = 512
    kt = lambda i: (0, i)
    return pl.pallas_call(
        _route_kernel,
        grid=(t // tn,),
        in_specs=[pl.BlockSpec((ne, tn), kt), _const_spec((ne, 1))],
        out_specs=[pl.BlockSpec((TOP_K, tn), kt), pl.BlockSpec((TOP_K, tn), kt),
                   pl.BlockSpec((TOP_K, tn), kt), pl.BlockSpec((ne, 1), lambda i: (0, 0))],
        out_shape=[jax.ShapeDtypeStruct((TOP_K, t), jnp.int32),
                   jax.ShapeDtypeStruct((TOP_K, t), F32),
                   jax.ShapeDtypeStruct((TOP_K, t), jnp.int32),
                   jax.ShapeDtypeStruct((ne, 1), jnp.int32)],
        scratch_shapes=[pltpu.VMEM((ne, 1), F32)],
        compiler_params=_cparams(("arbitrary",)),
        name="router",
    )(logits_t, router_bias.astype(F32).reshape(ne, 1))


def _moe_plan(eidx, posk, counts, n_tiles_max):
    ne = counts.shape[0]
    ntile = (counts + EXPERT_TILE - 1) // EXPERT_TILE
    cum = jnp.cumsum(ntile)
    base = (cum - ntile) * EXPERT_TILE
    n_tiles = cum[-1:]
    experts = jnp.arange(ne, dtype=jnp.int32)
    last_used = jnp.max(jnp.where(ntile > 0, experts, 0))
    tile_expert = jnp.sum(jnp.arange(n_tiles_max, dtype=jnp.int32)[:, None] >= cum[None, :], axis=1)
    tile_expert = jnp.minimum(tile_expert, last_used).astype(jnp.int32)
    tail = jnp.where(counts % EXPERT_TILE != 0, base + (ntile - 1) * EXPERT_TILE, -1)
    slot = jnp.sum(jnp.where(eidx[..., None] == experts, base, 0), axis=-1) + posk
    return slot.astype(jnp.int32), tile_expert, n_tiles.astype(jnp.int32), tail.astype(jnp.int32)


def _row_copy_wait(ref, n_rows, sem):
    pltpu.make_async_copy(ref.at[pl.ds(0, n_rows)], ref.at[pl.ds(0, n_rows)], sem).wait()


def _dispatch_kernel(tail_ref, slot_ref, x_ref, xs_hbm, zero_scr, sem, zsem):
    tm = x_ref.shape[0]

    @pl.when(pl.program_id(0) == 0)
    def _():
        zero_scr[...] = jnp.zeros_like(zero_scr)

        def fill(e, _):
            @pl.when(tail_ref[e] >= 0)
            def _():
                row0 = pl.multiple_of(tail_ref[e], EXPERT_TILE)
                pltpu.make_async_copy(zero_scr, xs_hbm.at[pl.ds(row0, EXPERT_TILE)], zsem).start()
            return 0

        def drain(e, _):
            @pl.when(tail_ref[e] >= 0)
            def _():
                pltpu.make_async_copy(zero_scr, xs_hbm.at[pl.ds(0, EXPERT_TILE)], zsem).wait()
            return 0

        lax.fori_loop(0, tail_ref.shape[0], fill, 0)
        lax.fori_loop(0, tail_ref.shape[0], drain, 0)

    def body(r, _):
        for k in range(TOP_K):
            s = slot_ref[0, 0, r * TOP_K + k]
            pltpu.make_async_copy(x_ref.at[pl.ds(r, 1)], xs_hbm.at[pl.ds(s, 1)], sem).start()
        return 0

    lax.fori_loop(0, tm, body, 0)
    _row_copy_wait(xs_hbm, tm * TOP_K, sem)


def _dispatch(xn_packed, slot_tiles, tail, n_rows):
    t, w = xn_packed.shape
    tm = MOE_TOKEN_TILE
    return pl.pallas_call(
        _dispatch_kernel,
        grid_spec=pltpu.PrefetchScalarGridSpec(
            num_scalar_prefetch=1,
            grid=(t // tm,),
            in_specs=[pl.BlockSpec((1, 1, tm * TOP_K), lambda i, *_: (i, 0, 0),
                                   memory_space=pltpu.SMEM),
                      pl.BlockSpec((tm, w), lambda i, *_: (i, 0))],
            out_specs=pl.BlockSpec(memory_space=pl.ANY),
            scratch_shapes=[pltpu.VMEM((EXPERT_TILE, w), jnp.uint32),
                            pltpu.SemaphoreType.DMA, pltpu.SemaphoreType.DMA]),
        out_shape=jax.ShapeDtypeStruct((n_rows, w), jnp.uint32),
        compiler_params=_cparams(("arbitrary",)),
        name="moe_dispatch",
    )(tail, slot_tiles, xn_packed)


def _expert_kernel(te_ref, nt_ref, xs_ref, wg_ref, wu_ref, wd_ref, ys_ref, wg_scr, wu_scr, wd_scr):
    i = pl.program_id(0)

    @pl.when(i < nt_ref[0])
    def _():
        @pl.when((i == 0) | (te_ref[i] != te_ref[jnp.maximum(i - 1, 0)]))
        def _():
            wg_scr[...] = wg_ref[0].astype(BF16)
            wu_scr[...] = wu_ref[0].astype(BF16)
            wd_scr[...] = wd_ref[0].astype(BF16)

        x = _unpack_rows_f32(xs_ref[...]).astype(BF16)
        h = jax.nn.silu(_dot(x, wg_scr[...])) * _dot(x, wu_scr[...])
        ys_ref[...] = _pack_rows(_dot(h.astype(BF16), wd_scr[...]))


def _experts(xs, tile_expert, n_tiles, w_gate, w_up, w_down):
    n_rows, w = xs.shape
    ne, d, de = w_gate.shape
    tile = lambda i, te, nt: (jnp.minimum(i, nt[0] - 1), 0)
    wsel = lambda i, te, nt: (te[i], 0, 0)
    return pl.pallas_call(
        _expert_kernel,
        grid_spec=pltpu.PrefetchScalarGridSpec(
            num_scalar_prefetch=2,
            grid=(n_rows // EXPERT_TILE,),
            in_specs=[pl.BlockSpec((EXPERT_TILE, w), tile),
                      pl.BlockSpec((1, d, de), wsel), pl.BlockSpec((1, d, de), wsel),
                      pl.BlockSpec((1, de, d), wsel)],
            out_specs=pl.BlockSpec((EXPERT_TILE, w), tile),
            scratch_shapes=[pltpu.VMEM((d, de), BF16), pltpu.VMEM((d, de), BF16),
                            pltpu.VMEM((de, d), BF16)]),
        out_shape=jax.ShapeDtypeStruct((n_rows, w), jnp.uint32),
        compiler_params=_cparams(("arbitrary",)),
        name="moe_experts",
    )(tile_expert, n_tiles, xs, w_gate, w_up, w_down)


def _combine_kernel(slot_ref, gate_ref, xn_ref, x2_ref, sg_ref, su_ref, sd_ref, gf_ref, ys_hbm,
                    y_ref, buf, sem):
    tm = xn_ref.shape[0]

    def body(r, _):
        for k in range(TOP_K):
            s = slot_ref[0, 0, r * TOP_K + k]
            pltpu.make_async_copy(ys_hbm.at[pl.ds(s, 1)], buf.at[k, pl.ds(r, 1)], sem).start()
        return 0

    lax.fori_loop(0, tm, body, 0)
    x = _unpack_rows_f32(xn_ref[...]).astype(BF16)
    hs = jax.nn.silu(_dot(x, sg_ref[...])) * _dot(x, su_ref[...])
    acc = x2_ref[...] + _dot(hs.astype(BF16), sd_ref[...])
    _row_copy_wait(ys_hbm, tm * TOP_K, sem)
    g = gate_ref[...]
    for k in range(TOP_K):
        acc = acc + g[:, k:k + 1] * _unpack_rows_f32(buf[k])
    y_ref[...] = _rms(acc, gf_ref[...])


def _combine(slot_tiles, gates, xn_packed, x2, ws_gate, ws_up, ws_down, g_final, ys):
    t, w = xn_packed.shape
    d = x2.shape[1]
    dsh = ws_gate.shape[1]
    tm = MOE_TOKEN_TILE
    row = lambda i: (i, 0)
    return pl.pallas_call(
        _combine_kernel,
        grid=(t // tm,),
        in_specs=[pl.BlockSpec((1, 1, tm * TOP_K), lambda i: (i, 0, 0), memory_space=pltpu.SMEM),
                  pl.BlockSpec((tm, TOP_K), row), pl.BlockSpec((tm, w), row),
                  pl.BlockSpec((tm, d), row),
                  _const_spec((d, dsh)), _const_spec((d, dsh)), _const_spec((dsh, d)),
                  _const_spec((1, d)), pl.BlockSpec(memory_space=pl.ANY)],
        out_specs=pl.BlockSpec((tm, d), row),
        out_shape=jax.ShapeDtypeStruct((t, d), F32),
        scratch_shapes=[pltpu.VMEM((TOP_K, tm, w), jnp.uint32), pltpu.SemaphoreType.DMA],
        compiler_params=_cparams(("arbitrary",)),
        name="moe_combine",
    )(slot_tiles, gates, xn_packed, x2, ws_gate.astype(BF16), ws_up.astype(BF16),
      ws_down.astype(BF16), g_final.reshape(1, d), ys)


def _moe(xn_packed, eidx, gate, posk, counts, w_gate, w_up, w_down, ws_gate, ws_up, ws_down,
         x2, g_final):
    t = xn_packed.shape[0]
    ne = w_gate.shape[0]
    n_tiles_max = t * TOP_K // EXPERT_TILE + ne
    slot, tile_expert, n_tiles, tail = _moe_plan(eidx, posk, counts[:, 0], n_tiles_max)
    slot_tiles = jnp.transpose(slot).reshape(t // MOE_TOKEN_TILE, 1, MOE_TOKEN_TILE * TOP_K)
    xs = _dispatch(xn_packed, slot_tiles, tail, n_tiles_max * EXPERT_TILE)
    ys = _experts(xs, tile_expert, n_tiles, w_gate, w_up, w_down)
    return _combine(slot_tiles, jnp.transpose(gate), xn_packed, x2, ws_gate, ws_up, ws_down,
                    g_final, ys)


def kernel(x_prompt, x_sample, cache_mem_k, cache_mem_v, state_s5_re, state_s5_im, state_gla, mem_prompt, w_in, w_a2, b_a2, lam_re, lam_im, log_dt, b_re, b_im, c_re, c_im, d_skip, w_glu, b_glu, g_s5_out, g_gla_head, w_out, g_mix, g_xattn, g_mem, w_q, w_k, w_v, w_o, g_ffn, w_router, router_bias, w_gate, w_up, w_down, ws_gate, ws_up, ws_down, g_final):
    depth = w_in.shape[0]
    assert depth == 1, "one trunk layer"
    bp, lp, d = x_prompt.shape
    bs, ls, _ = x_sample.shape
    assert bp == 1 and ls == CHUNK and lp % CHUNK == 0
    n_mem = mem_prompt.shape[1]
    g, p = lam_re.shape[1:]
    d_s5 = g * S5_GROUP
    d_qk = GLA_HEADS * GLA_DK
    d_gla = w_out.shape[1] - d_s5
    ns = d_s5 // LANES
    tp, tsamp = bp * lp, bs * ls

    x_all = jnp.concatenate([x_prompt.reshape(tp, d), x_sample.reshape(tsamp, d)], axis=0)

    mk_p, mv_p = _memory_kv(mem_prompt[0], g_mem[0], w_k[0].astype(BF16), w_v[0].astype(BF16))

    u, q, k, v, a, r = _in_proj(x_all, g_mix[0], w_in[0], d_s5, d_qk, d_gla)

    tables = _s5_tables(lam_re[0], lam_im[0], log_dt[0], b_re[0], b_im[0], c_re[0], c_im[0],
                        d_skip[0])
    zero_h = jnp.zeros((ns, bp, 2 * GROUPS_PER_SLAB * p), F32)
    rows_p = lp // S5_BLOCK
    y_p, h_p = _s5_mixer(u, tables, zero_h, 0, tp, rows_per_seq=rows_p,
                         tile_rows=math.gcd(rows_p, 512))
    h0_s = _pack_s5_state(state_s5_re[0], state_s5_im[0], ns)
    y_s, h_s = _s5_mixer(u, tables, h0_s, tp, tsamp, rows_per_seq=ls // S5_BLOCK,
                         tile_rows=tsamp // S5_BLOCK)
    y_slabs = jnp.concatenate([y_p, y_s], axis=1)

    wa2 = jnp.pad(w_a2[0], ((0, LANES - GLA_RANK), (0, 0)))
    zero_s = jnp.zeros((bp,) + state_gla.shape[2:], F32)
    gla_p, sg_p = _gla_mixer(q, k, v, a, r, wa2, b_a2[0], g_gla_head[0], zero_s, 0, tp,
                             carry=True, n_chunks=4)
    gla_s, sg_s = _gla_mixer(q, k, v, a, r, wa2, b_a2[0], g_gla_head[0], state_gla[0], tp, tsamp,
                             carry=False, n_chunks=4)
    gla = jnp.concatenate([gla_p, gla_s], axis=0)

    x1, xq = _mix_out(y_slabs, gla, x_all, w_glu[0], b_glu[0], g_s5_out[0], w_out[0], g_xattn[0],
                      w_q[0])

    wr_hi, wr_lo = _split_bf16(jnp.transpose(w_router[0]))
    wo = w_o[0].astype(BF16)
    x2_p, xn_p, lg_p = _xattn(xq[:tp].reshape(bp, lp, d), mk_p[None], mv_p[None],
                              x1[:tp].reshape(bp, lp, d), wo, g_ffn[0], wr_hi, wr_lo, nb=1, lt=512)
    x2_s, xn_s, lg_s = _xattn(xq[tp:].reshape(bs, ls, d),
                              cache_mem_k[0].reshape(bs, n_mem, d),
                              cache_mem_v[0].reshape(bs, n_mem, d),
                              x1[tp:].reshape(bs, ls, d), wo, g_ffn[0], wr_hi, wr_lo, nb=2, lt=ls)
    x2 = jnp.concatenate([x2_p.reshape(tp, d), x2_s.reshape(tsamp, d)], axis=0)
    xn = jnp.concatenate([xn_p.reshape(tp, d // 2), xn_s.reshape(tsamp, d // 2)], axis=0)
    logits_t = jnp.concatenate([lg_p, lg_s], axis=1)

    eidx, gate, posk, counts = _route(logits_t, router_bias[0])
    y_all = _moe(xn, eidx, gate, posk, counts, w_gate[0], w_up[0], w_down[0], ws_gate[0], ws_up[0],
                 ws_down[0], x2, g_final)

    y_prompt = y_all[:tp].reshape(bp, lp, d)
    y_sample = y_all[tp:].reshape(bs, ls, d)
    xh = d // X_HEADS
    mem_k_prompt = mk_p.reshape(1, bp, n_mem, X_HEADS, xh)
    mem_v_prompt = mv_p.reshape(1, bp, n_mem, X_HEADS, xh)
    re_p, im_p = _unpack_s5_state(h_p, g, p)
    re_s, im_s = _unpack_s5_state(h_s, g, p)
    return (y_prompt, y_sample, mem_k_prompt, mem_v_prompt, re_p[None], im_p[None], sg_p[None],
            re_s[None], im_s[None], sg_s[None])
```

```python
import functools
import math

import jax
import jax.numpy as jnp
from jax import lax
from jax.experimental import pallas as pl
from jax.experimental.pallas import tpu as pltpu

F32 = jnp.float32
BF16 = jnp.bfloat16

EPS = 1e-6
CHUNK = 64
S5_GROUP = 16
GLA_HEADS = 4
GLA_DK = 128
GLA_RANK = 16
GLA_TAU = 16.0
X_HEADS = 4
N_GROUPS = 8
TOPK_GROUPS = 4
TOP_K = 8
ROUTE_SCALE = 2.5

LANES = 128
SUBLANES = 8
S5_BLOCK = SUBLANES
GROUPS_PER_SLAB = LANES // S5_GROUP
TOKEN_TILE = 512
EXPERT_TILE = 256
MOE_TOKEN_TILE = 256
VMEM_LIMIT = 56 * 1024 * 1024


def _cparams(sem):
    return pltpu.CompilerParams(dimension_semantics=sem, vmem_limit_bytes=VMEM_LIMIT)


def _const_spec(shape):
    nd = len(shape)
    return pl.BlockSpec(shape, lambda *_: (0,) * nd, pipeline_mode=pl.Buffered(1))


def _pair_specs(block, axis, n_first):
    def at(f):
        return lambda i, *_: tuple(f(i) if a == axis else 0 for a in range(len(block)))
    return [pl.BlockSpec(block, at(lambda i: jnp.minimum(i, n_first - 1))),
            pl.BlockSpec(block, at(lambda i: jnp.maximum(i - n_first, 0)))]


def _pick(first, a_ref, b_ref):
    return jnp.where(first, a_ref[...], b_ref[...])


def _rms(x, g):
    return x * lax.rsqrt(jnp.mean(x * x, axis=-1, keepdims=True) + EPS) * g


def _split_bf16(x):
    hi = x.astype(BF16)
    lo = (x - hi.astype(F32)).astype(BF16)
    return hi, lo


def _split3_bf16(x):
    a = x.astype(BF16)
    r = x - a.astype(F32)
    b = r.astype(BF16)
    c = (r - b.astype(F32)).astype(BF16)
    return a, b, c


def _dot(a, b):
    return jnp.dot(a, b, preferred_element_type=F32)


def _dot_nt(a, b):
    return lax.dot_general(a, b, (((1,), (1,)), ((), ())), preferred_element_type=F32)


def _dot_tn(a, b):
    return lax.dot_general(a, b, (((0,), (0,)), ((), ())), preferred_element_type=F32)


def _memkv_kernel(mem_ref, g_ref, wk_ref, wv_ref, mk_ref, mv_ref):
    m = _rms(mem_ref[...], g_ref[...]).astype(BF16)
    mk_ref[...] = _dot(m, wk_ref[...])
    mv_ref[...] = _dot(m, wv_ref[...])


def _memory_kv(mem, g_mem, wk, wv):
    n, d = mem.shape
    tn = 512
    return pl.pallas_call(
        _memkv_kernel,
        grid=(d // tn,),
        in_specs=[_const_spec((n, d)), _const_spec((1, d)),
                  pl.BlockSpec((d, tn), lambda j: (0, j)),
                  pl.BlockSpec((d, tn), lambda j: (0, j))],
        out_specs=[pl.BlockSpec((n, tn), lambda j: (0, j))] * 2,
        out_shape=[jax.ShapeDtypeStruct((n, d), F32)] * 2,
        compiler_params=_cparams(("arbitrary",)),
        name="memory_kv",
    )(mem, g_mem.reshape(1, d), wk, wv)


def _inproj_kernel(xp_ref, xs_ref, g_ref, wuh_ref, wul_ref, wqkv_ref, wa_ref, wr_ref,
                   u_ref, q_ref, k_ref, v_ref, a_ref, r_ref, *, n_first):
    x = _pick(pl.program_id(0) < n_first, xp_ref, xs_ref)
    nx = _rms(x, g_ref[...])
    hi, lo = _split_bf16(nx)
    u = _dot(hi, wuh_ref[...]) + _dot(lo, wuh_ref[...]) + _dot(hi, wul_ref[...])
    for j in range(u_ref.shape[0]):
        u_ref[j] = u[:, j * LANES:(j + 1) * LANES]
    qkv = _dot(hi, wqkv_ref[...])
    dq = q_ref.shape[1]
    q_ref[...] = qkv[:, :dq].astype(BF16)
    k_ref[...] = qkv[:, dq:2 * dq].astype(BF16)
    v_ref[...] = qkv[:, 2 * dq:].astype(BF16)
    a_ref[...] = _dot(hi, wa_ref[...]) + _dot(lo, wa_ref[...])
    r_ref[...] = _dot(hi, wr_ref[...]).astype(BF16)


def _in_proj(xp, xs, g_mix, w_in, d_s5, d_qk, d_gla):
    d = xp.shape[1]
    t = xp.shape[0] + xs.shape[0]
    tm = TOKEN_TILE
    n_first = xp.shape[0] // tm
    o1 = d_s5
    o2 = o1 + 2 * d_qk + d_gla
    o3 = o2 + GLA_RANK
    wu_hi, wu_lo = _split_bf16(w_in[:, :o1])
    wqkv = w_in[:, o1:o2].astype(BF16)
    wa = jnp.pad(w_in[:, o2:o3], ((0, 0), (0, LANES - GLA_RANK))).astype(BF16)
    wr = w_in[:, o3:].astype(BF16)
    n_slab = d_s5 // LANES
    row = lambda i: (i, 0)
    return pl.pallas_call(
        functools.partial(_inproj_kernel, n_first=n_first),
        grid=(t // tm,),
        in_specs=_pair_specs((tm, d), 0, n_first) + [
            _const_spec((1, d)), _const_spec(wu_hi.shape), _const_spec(wu_lo.shape),
            _const_spec(wqkv.shape), _const_spec(wa.shape), _const_spec(wr.shape)],
        out_specs=[pl.BlockSpec((n_slab, tm, LANES), lambda i: (0, i, 0)),
                   pl.BlockSpec((tm, d_qk), row), pl.BlockSpec((tm, d_qk), row),
                   pl.BlockSpec((tm, d_gla), row), pl.BlockSpec((tm, LANES), row),
                   pl.BlockSpec((tm, d_gla), row)],
        out_shape=[jax.ShapeDtypeStruct((n_slab, t, LANES), F32),
                   jax.ShapeDtypeStruct((t, d_qk), BF16), jax.ShapeDtypeStruct((t, d_qk), BF16),
                   jax.ShapeDtypeStruct((t, d_gla), BF16), jax.ShapeDtypeStruct((t, LANES), F32),
                   jax.ShapeDtypeStruct((t, d_gla), BF16)],
        compiler_params=_cparams(("arbitrary",)),
        name="in_proj",
    )(xp, xs, g_mix.reshape(1, d), wu_hi, wu_lo, wqkv, wa, wr)


def _s5_coeffs(lam_re, lam_im, log_dt, b_re, b_im, c_re, c_im, d_skip):
    g, p = lam_re.shape
    h = b_re.shape[-1]
    nb = S5_BLOCK
    ns = g // GROUPS_PER_SLAB
    gl = GROUPS_PER_SLAB
    dt = jnp.exp(log_dt.astype(F32))[:, None]
    lr, li = lam_re.astype(F32), lam_im.astype(F32)
    mag = jnp.exp(lr * dt)
    ar, ai = mag * jnp.cos(li * dt), mag * jnp.sin(li * dt)
    den = lr * lr + li * li
    nr = ar - 1.0
    cf_r = (nr * lr + ai * li) / den
    cf_i = (ai * lr - nr * li) / den
    bb_r = cf_r[..., None] * b_re - cf_i[..., None] * b_im
    bb_i = cf_r[..., None] * b_im + cf_i[..., None] * b_re

    def cpow(n):
        e = n[:, None, None] * dt[None]
        m = jnp.exp(lr[None] * e)
        return m * jnp.cos(li[None] * e), m * jnp.sin(li[None] * e)

    pr, pi = cpow(jnp.arange(nb + 1, dtype=F32))

    cb_r = jnp.einsum('gcp,gph->gpch', c_re, bb_r) - jnp.einsum('gcp,gph->gpch', c_im, bb_i)
    cb_i = jnp.einsum('gcp,gph->gpch', c_re, bb_i) + jnp.einsum('gcp,gph->gpch', c_im, bb_r)
    taps = (jnp.einsum('ngp,gpch->ngch', pr[:nb], cb_r)
            - jnp.einsum('ngp,gpch->ngch', pi[:nb], cb_i))
    taps = taps.at[0].add(d_skip[:, :, None] * jnp.eye(h, dtype=F32)[None])
    ksm = jnp.transpose(taps.reshape(nb, ns, gl, h, h), (1, 0, 2, 4, 3))
    ksm = ksm.reshape(ns, nb, LANES, h)

    wr, wi = pr[nb - 1::-1][:nb], pi[nb - 1::-1][:nb]
    inj_r = wr[..., None] * bb_r[None] - wi[..., None] * bb_i[None]
    inj_i = wr[..., None] * bb_i[None] + wi[..., None] * bb_r[None]
    inj = jnp.stack([inj_r, inj_i], axis=0).reshape(2, nb, ns, gl, p, h)
    psm = jnp.transpose(inj, (2, 1, 3, 5, 0, 4)).reshape(ns, nb, LANES, 2 * p)

    er, ei = pr[1:nb + 1], pi[1:nb + 1]
    q_r = c_re[None] * er[:, :, None, :] - c_im[None] * ei[:, :, None, :]
    q_i = -(c_re[None] * ei[:, :, None, :] + c_im[None] * er[:, :, None, :])
    qq = jnp.stack([q_r, q_i], axis=0).reshape(2, nb, ns, gl, h, p)
    qsm = jnp.transpose(qq, (2, 0, 3, 5, 1, 4)).reshape(ns, 2 * gl * p, nb * h)

    def slab(xr, xi):
        k = xr.shape[0]
        xr = jnp.transpose(xr.reshape(k, ns, gl * p), (1, 0, 2))
        xi = jnp.transpose(xi.reshape(k, ns, gl * p), (1, 0, 2))
        return jnp.concatenate([xr, xi], axis=-1)

    rows = jnp.arange(SUBLANES, dtype=F32)
    cr, ci = cpow(nb * rows)
    carry_m = slab(cr, ci)
    sr, si = cpow(nb * jnp.array([1.0, 2.0, 4.0], F32))
    keep = (rows[None, :] >= jnp.array([1.0, 2.0, 4.0], F32)[:, None]).astype(F32)
    step_m = slab((sr[:, None] * keep[:, :, None, None]).reshape(3 * SUBLANES, g, p),
                  (si[:, None] * keep[:, :, None, None]).reshape(3 * SUBLANES, g, p))
    step_m = step_m.reshape(ns, 3, SUBLANES, 2 * gl * p)
    a8 = carry_m[:, 1:2]
    return ksm, psm, qsm, step_m, carry_m, a8


def _s5_expand_kernel(ksm_ref, psm_ref, qsm_ref, tm_ref, ph_ref, pl_ref, qm_ref):
    nb = S5_BLOCK
    h = S5_GROUP
    w = ph_ref.shape[2]
    p = w // (2 * GROUPS_PER_SLAB)

    def spread(x, sel, keep):
        a, b, c = _split3_bf16(x)
        return jnp.where(keep, _dot(a, sel) + _dot(b, sel) + _dot(c, sel), 0.0)

    def iota(shape, axis):
        return lax.broadcasted_iota(jnp.int32, shape, axis)

    sel_c = (iota((h, LANES), 1) % h == iota((h, LANES), 0)).astype(BF16)
    keep_t = iota((LANES, LANES), 0) // h == iota((LANES, LANES), 1) // h
    tm_ref[...] = jnp.zeros_like(tm_ref)
    for tau in range(nb):
        blk = spread(ksm_ref[0, tau], sel_c, keep_t).astype(BF16)
        for s in range(nb - tau):
            t = s + tau
            tm_ref[0, s * LANES:(s + 1) * LANES, t * LANES:(t + 1) * LANES] = blk

    half = w // 2
    keep_p = iota((LANES, half), 0) // h == iota((LANES, half), 1) // p
    for ri in range(2):
        sel_p = (iota((2 * p, half), 0) == iota((2 * p, half), 1) % p + ri * p).astype(BF16)
        for s in range(nb):
            blk = spread(psm_ref[0, s], sel_p, keep_p)
            hi, lo = _split_bf16(blk)
            ph_ref[0, s * LANES:(s + 1) * LANES, ri * half:(ri + 1) * half] = hi
            pl_ref[0, s * LANES:(s + 1) * LANES, ri * half:(ri + 1) * half] = lo

    keep_q = (iota((w, LANES), 0) % half) // p == iota((w, LANES), 1) // h
    for t in range(nb):
        sel_q = (iota((nb * h, LANES), 0) == iota((nb * h, LANES), 1) % h + t * h).astype(BF16)
        qm_ref[0, :, t * LANES:(t + 1) * LANES] = spread(qsm_ref[0], sel_q, keep_q).astype(BF16)


def _s5_expand(ksm, psm, qsm):
    ns = ksm.shape[0]
    w = qsm.shape[1]
    k = S5_BLOCK * LANES
    blk = lambda shape: pl.BlockSpec((1,) + shape, lambda j: (j,) + (0,) * len(shape))
    return pl.pallas_call(
        _s5_expand_kernel,
        grid=(ns,),
        in_specs=[blk(ksm.shape[1:]), blk(psm.shape[1:]), blk(qsm.shape[1:])],
        out_specs=[blk((k, k)), blk((k, w)), blk((k, w)), blk((w, k))],
        out_shape=[jax.ShapeDtypeStruct((ns, k, k), BF16), jax.ShapeDtypeStruct((ns, k, w), BF16),
                   jax.ShapeDtypeStruct((ns, k, w), BF16), jax.ShapeDtypeStruct((ns, w, k), BF16)],
        compiler_params=_cparams(("arbitrary",)),
        name="s5_tables",
    )(ksm, psm, qsm)


def _cmul(xr, xi, mr, mi):
    return xr * mr - xi * mi, xr * mi + xi * mr


def _s5_kernel(u_ref, ph_ref, pl_ref, tm_ref, qm_ref, step_ref, carry_ref, a8_ref, h0_ref,
               y_ref, hout_ref, s_scr, hin_scr, c_scr, *, rows_per_seq):
    ti = pl.program_id(1)
    rows = s_scr.shape[0]
    half = s_scr.shape[1] // 2
    nb = S5_BLOCK
    u8 = jnp.concatenate([u_ref[0, pl.ds(s, rows, stride=nb), :] for s in range(nb)], axis=1)
    hi, lo = _split_bf16(u8)
    s_scr[...] = _dot(hi, ph_ref[0]) + _dot(lo, ph_ref[0]) + _dot(hi, pl_ref[0])

    per_block_seq = rows_per_seq == SUBLANES
    if not per_block_seq:
        @pl.when(ti == 0)
        def _():
            c_scr[...] = h0_ref[0]

    not_first = (lax.broadcasted_iota(jnp.int32, (SUBLANES, 1), 0) >= 1).astype(F32)
    a8r, a8i = a8_ref[0, :, :half], a8_ref[0, :, half:]
    cmr, cmi = carry_ref[0, :, :half], carry_ref[0, :, half:]

    def body(rb, _):
        r0 = pl.multiple_of(rb * SUBLANES, SUBLANES)
        sb = s_scr[pl.ds(r0, SUBLANES), :]
        x = pltpu.roll(sb, 1, 0) * not_first
        xr, xi = x[:, :half], x[:, half:]
        for k, sh in enumerate((1, 2, 4)):
            m = step_ref[0, k]
            pr, pi = _cmul(pltpu.roll(xr, sh, 0), pltpu.roll(xi, sh, 0), m[:, :half], m[:, half:])
            xr, xi = xr + pr, xi + pi
        c = h0_ref[0, pl.ds(rb, 1), :] if per_block_seq else c_scr[...]
        cr, ci = c[:, :half], c[:, half:]
        pr, pi = _cmul(cr, ci, cmr, cmi)
        hr, hi_ = xr + pr, xi + pi
        hin_scr[pl.ds(r0, SUBLANES), :half] = hr
        hin_scr[pl.ds(r0, SUBLANES), half:] = hi_
        nr, ni = _cmul(hr[SUBLANES - 1:], hi_[SUBLANES - 1:], a8r, a8i)
        cn = jnp.concatenate([nr, ni], axis=1) + sb[SUBLANES - 1:]
        if per_block_seq:
            hout_ref[0, pl.ds(rb, 1), :] = cn
        else:
            c_scr[...] = cn
        return 0

    lax.fori_loop(0, rows // SUBLANES, body, 0)
    if not per_block_seq:
        hout_ref[0] = c_scr[...]

    y8 = _dot(hi, tm_ref[0]) + _dot(hin_scr[...].astype(BF16), qm_ref[0])
    for t in range(nb):
        y_ref[0, pl.ds(t, rows, stride=nb), :] = y8[:, t * LANES:(t + 1) * LANES]


def _s5_mixer(u_slabs, tables, h0, t_start, t, rows_per_seq, tile_rows):
    tm, pm_hi, pm_lo, qm, step_m, carry_m, a8 = tables
    ns = u_slabs.shape[0]
    rows = t // S5_BLOCK
    n_tiles = rows // tile_rows
    assert rows % tile_rows == 0 and t_start % (tile_rows * S5_BLOCK) == 0
    off = t_start // (tile_rows * S5_BLOCK)
    n_seq = h0.shape[1]
    w = pm_hi.shape[-1]
    slab3 = lambda j, i: (j, 0, 0)
    kern = functools.partial(_s5_kernel, rows_per_seq=rows_per_seq)
    return pl.pallas_call(
        kern,
        grid=(ns, n_tiles),
        in_specs=[pl.BlockSpec((1, tile_rows * S5_BLOCK, LANES), lambda j, i: (j, i + off, 0)),
                  pl.BlockSpec((1,) + pm_hi.shape[1:], slab3),
                  pl.BlockSpec((1,) + pm_lo.shape[1:], slab3),
                  pl.BlockSpec((1,) + tm.shape[1:], slab3),
                  pl.BlockSpec((1,) + qm.shape[1:], slab3),
                  pl.BlockSpec((1,) + step_m.shape[1:], lambda j, i: (j, 0, 0, 0)),
                  pl.BlockSpec((1,) + carry_m.shape[1:], slab3),
                  pl.BlockSpec((1,) + a8.shape[1:], slab3),
                  pl.BlockSpec((1, n_seq, w), slab3)],
        out_specs=[pl.BlockSpec((1, tile_rows * S5_BLOCK, LANES), lambda j, i: (j, i, 0)),
                   pl.BlockSpec((1, n_seq, w), slab3)],
        out_shape=[jax.ShapeDtypeStruct((ns, t, LANES), F32),
                   jax.ShapeDtypeStruct((ns, n_seq, w), F32)],
        scratch_shapes=[pltpu.VMEM((tile_rows, w), F32), pltpu.VMEM((tile_rows, w), F32),
                        pltpu.VMEM((1, w), F32)],
        compiler_params=_cparams(("arbitrary", "arbitrary")),
        name="s5_mixer",
    )(u_slabs, pm_hi, pm_lo, tm, qm, step_m, carry_m, a8, h0)


def _pack_s5_state(re, im, ns):
    b = re.shape[0]
    r = re.astype(F32).reshape(b, ns, -1)
    i = im.astype(F32).reshape(b, ns, -1)
    return jnp.transpose(jnp.concatenate([r, i], axis=-1), (1, 0, 2))


def _unpack_s5_state(hc, g, p):
    ns, b, w = hc.shape
    hc = jnp.transpose(hc, (1, 0, 2))
    re = hc[:, :, :w // 2].reshape(b, g, p)
    im = hc[:, :, w // 2:].reshape(b, g, p)
    return re, im


def _gla_kernel(q_ref, k_ref, v_ref, a_ref, r_ref, wa2_ref, ba2_ref, gh_ref, s0_ref,
                o_ref, sout_ref, st_scr, *, carry, n_chunks):
    step = pl.program_id(0)
    dk = GLA_DK
    dv = v_ref.shape[1] // GLA_HEADS
    c = CHUNK
    scale = dk ** -0.5
    ri = lax.broadcasted_iota(jnp.int32, (c, c), 0)
    ci = lax.broadcasted_iota(jnp.int32, (c, c), 1)
    causal = ri >= ci
    tril = causal.astype(BF16)
    eye_dk = (lax.broadcasted_iota(jnp.int32, (dk, dk), 0)
              == lax.broadcasted_iota(jnp.int32, (dk, dk), 1))

    if carry:
        @pl.when(step == 0)
        def _():
            st_scr[...] = s0_ref[0]

    wa_hi, wa_lo = _split_bf16(wa2_ref[...])
    for n in range(n_chunks):
        rows = slice(n * c, (n + 1) * c)
        a_hi, a_lo = _split_bf16(a_ref[rows, :])
        logit = _dot(a_hi, wa_hi) + _dot(a_lo, wa_hi) + _dot(a_hi, wa_lo) + ba2_ref[...]
        g = jax.nn.log_sigmoid(logit) * (1.0 / GLA_TAU)
        g1, g2, g3 = _split3_bf16(g)
        bcum = _dot(tril, g1) + _dot(tril, g2) + _dot(tril, g3)
        for h in range(GLA_HEADS):
            ks = slice(h * dk, (h + 1) * dk)
            vs = slice(h * dv, (h + 1) * dv)
            b = bcum[:, ks]
            qh = q_ref[rows, ks].astype(F32) * scale
            kh = k_ref[rows, ks].astype(F32)
            vh = v_ref[rows, vs]
            state = s0_ref[n, h] if not carry else st_scr[h]
            qe = (qh * jnp.exp(b)).astype(BF16)
            ke = (kh * jnp.exp(-b)).astype(BF16)
            att = jnp.where(causal, _dot_nt(qe, ke), 0.0)
            o = _dot(att.astype(BF16), vh) + _dot(qe, state.astype(BF16))
            blast = b[c - 1:c, :]
            kd = (kh * jnp.exp(blast - b)).astype(BF16)
            decay = jnp.sum(jnp.where(eye_dk, jnp.exp(blast), 0.0), axis=1, keepdims=True)
            new_state = decay * state + _dot_tn(kd, vh)
            if carry:
                st_scr[h] = new_state
            else:
                sout_ref[n, h] = new_state
            on = _rms(o, gh_ref[...])
            rr = r_ref[rows, vs].astype(F32)
            o_ref[rows, vs] = (on * (rr * jax.nn.sigmoid(rr))).astype(BF16)
    if carry:
        sout_ref[0] = st_scr[...]


def _gla_mixer(q, k, v, a, r, wa2, ba2, g_head, s0, t_start, t, carry, n_chunks):
    dqk = q.shape[1]
    dvt = v.shape[1]
    hh, dk, dv = s0.shape[1:]
    rows = n_chunks * CHUNK
    assert t % rows == 0 and t_start % rows == 0
    off = t_start // rows
    row = lambda i: (i, 0)
    row_in = lambda i: (i + off, 0)
    if carry:
        sblk, smap = (1, hh, dk, dv), (lambda i: (0, 0, 0, 0))
    else:
        sblk, smap = (n_chunks, hh, dk, dv), (lambda i: (i, 0, 0, 0))
    kern = functools.partial(_gla_kernel, carry=carry, n_chunks=n_chunks)
    return pl.pallas_call(
        kern,
        grid=(t // rows,),
        in_specs=[pl.BlockSpec((rows, dqk), row_in), pl.BlockSpec((rows, dqk), row_in),
                  pl.BlockSpec((rows, dvt), row_in), pl.BlockSpec((rows, LANES), row_in),
                  pl.BlockSpec((rows, dvt), row_in),
                  _const_spec(wa2.shape), _const_spec((1, dqk)), _const_spec((1, dv)),
                  pl.BlockSpec(sblk, smap)],
        out_specs=[pl.BlockSpec((rows, dvt), row), pl.BlockSpec(sblk, smap)],
        out_shape=[jax.ShapeDtypeStruct((t, dvt), BF16), jax.ShapeDtypeStruct(s0.shape, F32)],
        scratch_shapes=[pltpu.VMEM((hh, dk, dv), F32)],
        compiler_params=_cparams(("arbitrary",)),
        name="gla_mixer",
    )(q, k, v, a, r, wa2, ba2.reshape(1, dqk), g_head.reshape(1, dv), s0)


def _mixout_kernel(yp_ref, ys_ref, gp_ref, gs_ref, xp_ref, xs_ref, wglu_ref, bglu_ref, gs5_ref,
                   wout_ref, gx_ref, wq_ref, x1_ref, q_ref, *, n_first):
    first = pl.program_id(0) < n_first
    yb = _pick(first, yp_ref, ys_ref)
    y = jnp.concatenate([yb[j] for j in range(yb.shape[0])], axis=1)
    z = jax.nn.gelu(y)
    gate = jax.nn.sigmoid(_dot(z.astype(BF16), wglu_ref[...]) + bglu_ref[...])
    s5 = _rms(z * gate, gs5_ref[...])
    cat = jnp.concatenate([s5.astype(BF16), _pick(first, gp_ref, gs_ref)], axis=1)
    x1 = _pick(first, xp_ref, xs_ref) + _dot(cat, wout_ref[...])
    x1_ref[...] = x1
    q_ref[...] = _dot(_rms(x1, gx_ref[...]).astype(BF16), wq_ref[...]).astype(BF16)


def _mix_out(y_p, y_s, gla_p, gla_s, xp, xs, w_glu, b_glu, g_s5, w_out, g_x, w_q):
    d = xp.shape[1]
    t = xp.shape[0] + xs.shape[0]
    ns = y_p.shape[0]
    ds5 = ns * LANES
    dg = gla_p.shape[1]
    tm = TOKEN_TILE // 2
    n_first = xp.shape[0] // tm
    row = lambda i: (i, 0)
    return pl.pallas_call(
        functools.partial(_mixout_kernel, n_first=n_first),
        grid=(t // tm,),
        in_specs=(_pair_specs((ns, tm, LANES), 1, n_first) + _pair_specs((tm, dg), 0, n_first)
                  + _pair_specs((tm, d), 0, n_first)
                  + [_const_spec((ds5, ds5)), _const_spec((1, ds5)), _const_spec((1, ds5)),
                     _const_spec((ds5 + dg, d)), _const_spec((1, d)), _const_spec((d, d))]),
        out_specs=[pl.BlockSpec((tm, d), row), pl.BlockSpec((tm, d), row)],
        out_shape=[jax.ShapeDtypeStruct((t, d), F32), jax.ShapeDtypeStruct((t, d), BF16)],
        compiler_params=_cparams(("arbitrary",)),
        name="mix_out",
    )(y_p, y_s, gla_p, gla_s, xp, xs, w_glu.astype(BF16), b_glu.reshape(1, ds5),
      g_s5.reshape(1, ds5), w_out.astype(BF16), g_x.reshape(1, d), w_q.astype(BF16))


def _xattn_kernel(q_ref, mk_ref, mv_ref, x1_ref, wo_ref, gf_ref, wrh_ref, wrl_ref,
                  x2_ref, xn_ref, lg_ref):
    nb = mk_ref.shape[0]
    d = q_ref.shape[1]
    lt = q_ref.shape[0] // nb
    hd = d // X_HEADS
    scale = hd ** -0.5
    for b in range(nb):
        rows = slice(b * lt, (b + 1) * lt)
        mk = mk_ref[b].astype(BF16)
        mv = mv_ref[b].astype(BF16)
        outs = []
        for h in range(X_HEADS):
            hs = slice(h * hd, (h + 1) * hd)
            s = _dot_nt(q_ref[rows, hs], mk[:, hs]) * scale
            p = jnp.exp(s - jnp.max(s, axis=-1, keepdims=True))
            denom = jnp.sum(p, axis=-1, keepdims=True)
            outs.append((_dot(p.astype(BF16), mv[:, hs]) / denom).astype(BF16))
        o = jnp.concatenate(outs, axis=1)
        x2 = x1_ref[rows, :] + _dot(o, wo_ref[...])
        x2_ref[rows, :] = x2
        xn = _rms(x2, gf_ref[...])
        xn_ref[rows, :] = xn
        hi, lo = _split_bf16(xn)
        lg_ref[:, rows] = (_dot_nt(wrh_ref[...], hi) + _dot_nt(wrh_ref[...], lo)
                           + _dot_nt(wrl_ref[...], hi))


def _xattn(q, x1, mk, mv, w_o, g_ffn, wr_hi, wr_lo, t_start, t, nb, lt):
    d = q.shape[1]
    nm = mk.shape[1]
    ne = wr_hi.shape[0]
    rows = nb * lt
    assert t % rows == 0 and t_start % rows == 0
    off = t_start // rows
    shared = mk.shape[0] == 1
    assert nb == 1 if shared else mk.shape[0] * lt == t
    row = lambda i: (i, 0)
    row_in = lambda i: (i + off, 0)
    mblk = (lambda i: (0, 0, 0)) if shared else (lambda i: (i, 0, 0))
    return pl.pallas_call(
        _xattn_kernel,
        grid=(t // rows,),
        in_specs=[pl.BlockSpec((rows, d), row_in),
                  pl.BlockSpec((nb, nm, d), mblk), pl.BlockSpec((nb, nm, d), mblk),
                  pl.BlockSpec((rows, d), row_in),
                  _const_spec((d, d)), _const_spec((1, d)),
                  _const_spec((ne, d)), _const_spec((ne, d))],
        out_specs=[pl.BlockSpec((rows, d), row), pl.BlockSpec((rows, d), row),
                   pl.BlockSpec((ne, rows), lambda i: (0, i))],
        out_shape=[jax.ShapeDtypeStruct((t, d), F32), jax.ShapeDtypeStruct((t, d), F32),
                   jax.ShapeDtypeStruct((ne, t), F32)],
        compiler_params=_cparams(("arbitrary",)),
        name="mem_xattn",
    )(q, mk, mv, x1, w_o, g_ffn.reshape(1, d), wr_hi, wr_lo)


def _route_kernel(lgp_ref, lgs_ref, bias_ref, eidx_ref, gate_ref, pos_ref, cnt_ref, run_scr,
                  *, n_first):
    ne, tn = lgp_ref.shape

    @pl.when(pl.program_id(0) == 0)
    def _():
        run_scr[...] = jnp.zeros_like(run_scr)

    gsz = ne // N_GROUPS
    sc = jax.nn.sigmoid(_pick(pl.program_id(0) < n_first, lgp_ref, lgs_ref))
    sel = sc + bias_ref[...]
    s3 = sel.reshape(N_GROUPS, gsz, tn)
    ie = lax.broadcasted_iota(jnp.int32, s3.shape, 1).astype(F32)
    m1 = jnp.max(s3, axis=1, keepdims=True)
    first = jnp.min(jnp.where(s3 == m1, ie, float(gsz)), axis=1, keepdims=True)
    m2 = jnp.max(jnp.where(ie == first, -jnp.inf, s3), axis=1, keepdims=True)
    gs = m1 + m2
    ig = lax.broadcasted_iota(jnp.int32, gs.shape, 0)
    grank = jnp.zeros(gs.shape, jnp.int32)
    for g in range(N_GROUPS):
        other = gs[g:g + 1]
        ahead = jnp.where(other > gs, 1, jnp.where(other == gs, (ig > g).astype(jnp.int32), 0))
        grank = grank + ahead
    gkeep = jnp.broadcast_to(grank < TOPK_GROUPS, s3.shape)
    v = jnp.where(gkeep, s3, -jnp.inf).reshape(ne, tn)
    iv = lax.broadcasted_iota(jnp.int32, v.shape, 0)
    rank = jnp.zeros(v.shape, jnp.int32)
    for e in range(ne):
        other = v[e:e + 1]
        ahead = jnp.where(other > v, 1, jnp.where(other == v, (iv > e).astype(jnp.int32), 0))
        rank = rank + ahead
    chosen = rank < TOP_K
    gate = jnp.where(chosen, sc, 0.0)
    gate = gate / jnp.sum(gate, axis=0, keepdims=True) * ROUTE_SCALE
    ti = lax.broadcasted_iota(jnp.int32, (tn, tn), 0)
    tj = lax.broadcasted_iota(jnp.int32, (tn, tn), 1)
    incl = _dot(jnp.where(chosen, 1.0, 0.0).astype(BF16), (ti <= tj).astype(BF16))
    pos = run_scr[...] + incl - 1.0
    run_scr[...] = run_scr[...] + incl[:, tn - 1:tn]
    cnt_ref[...] = run_scr[...].astype(jnp.int32)
    ef = iv.astype(F32)
    for k in range(TOP_K):
        hit = rank == k
        pick = lambda a: jnp.sum(jnp.where(hit, a, 0.0), axis=0, keepdims=True)
        eidx_ref[k:k + 1, :] = pick(ef).astype(jnp.int32)
        gate_ref[k:k + 1, :] = pick(gate)
        pos_ref[k:k + 1, :] = pick(pos).astype(jnp.int32)


def _route(lg_p, lg_s, router_bias):
    ne = lg_p.shape[0]
    t = lg_p.shape[1] + lg_s.shape[1]
    tn = TOKEN_TILE
    n_first = lg_p.shape[1] // tn
    kt = lambda i: (0, i)
    return pl.pallas_call(
        functools.partial(_route_kernel, n_first=n_first),
        grid=(t // tn,),
        in_specs=_pair_specs((ne, tn), 1, n_first) + [_const_spec((ne, 1))],
        out_specs=[pl.BlockSpec((TOP_K, tn), kt), pl.BlockSpec((TOP_K, tn), kt),
                   pl.BlockSpec((TOP_K, tn), kt), pl.BlockSpec((ne, 1), lambda i: (0, 0))],
        out_shape=[jax.ShapeDtypeStruct((TOP_K, t), jnp.int32),
                   jax.ShapeDtypeStruct((TOP_K, t), F32),
                   jax.ShapeDtypeStruct((TOP_K, t), jnp.int32),
                   jax.ShapeDtypeStruct((ne, 1), jnp.int32)],
        scratch_shapes=[pltpu.VMEM((ne, 1), F32)],
        compiler_params=_cparams(("arbitrary",)),
        name="router",
    )(lg_p, lg_s, router_bias.astype(F32).reshape(ne, 1))


def _moe_plan(eidx, posk, counts, n_tiles_max):
    ne = counts.shape[0]
    ntile = (counts + EXPERT_TILE - 1) // EXPERT_TILE
    cum = jnp.cumsum(ntile)
    base = (cum - ntile) * EXPERT_TILE
    n_tiles = cum[-1:]
    experts = jnp.arange(ne, dtype=jnp.int32)
    last_used = jnp.max(jnp.where(ntile > 0, experts, 0))
    tiles = jnp.arange(n_tiles_max, dtype=jnp.int32)
    tile_expert = jnp.sum(tiles[:, None] >= cum[None, :], axis=1)
    tile_expert = jnp.minimum(tile_expert, last_used).astype(jnp.int32)
    partial = jnp.any((tiles[:, None] == cum[None, :] - 1) & (counts % EXPERT_TILE != 0)[None, :],
                      axis=1)
    zfill = (partial | (tiles >= n_tiles[0])).astype(jnp.int32)
    slot = jnp.sum(jnp.where(eidx[..., None] == experts, base, 0), axis=-1) + posk
    return slot.astype(jnp.int32), tile_expert, n_tiles.astype(jnp.int32), zfill


def _row_copy_wait(ref, n_rows, sem):
    pltpu.make_async_copy(ref.at[pl.ds(0, n_rows)], ref.at[pl.ds(0, n_rows)], sem).wait()


def _dispatch_kernel(zfill_ref, slot_ref, xp_ref, xs_ref, xs_hbm, zero_scr, sem, zsem, *, n_first):
    tm = xp_ref.shape[0]
    i = pl.program_id(0)

    @pl.when(i == 0)
    def _():
        zero_scr[...] = jnp.zeros_like(zero_scr)

        def zero_tile(n):
            row0 = pl.multiple_of(n * EXPERT_TILE, EXPERT_TILE)
            return pltpu.make_async_copy(zero_scr, xs_hbm.at[pl.ds(row0, EXPERT_TILE)], zsem)

        def fill(n, _):
            @pl.when(zfill_ref[n] != 0)
            def _():
                zero_tile(n).start()
            return 0

        def drain(n, _):
            @pl.when(zfill_ref[n] != 0)
            def _():
                zero_tile(n).wait()
            return 0

        lax.fori_loop(0, zfill_ref.shape[0], fill, 0)
        lax.fori_loop(0, zfill_ref.shape[0], drain, 0)

    def scatter_rows(x_ref):
        def body(r, _):
            for k in range(TOP_K):
                s = slot_ref[0, 0, r * TOP_K + k]
                pltpu.make_async_copy(x_ref.at[pl.ds(r, 1)], xs_hbm.at[pl.ds(s, 1)], sem).start()
            return 0
        lax.fori_loop(0, tm, body, 0)

    @pl.when(i < n_first)
    def _():
        scatter_rows(xp_ref)

    @pl.when(i >= n_first)
    def _():
        scatter_rows(xs_ref)

    _row_copy_wait(xs_hbm, tm * TOP_K, sem)


def _dispatch(xn_p, xn_s, slot_tiles, zfill, n_rows):
    w = xn_p.shape[1]
    t = xn_p.shape[0] + xn_s.shape[0]
    tm = MOE_TOKEN_TILE
    n_first = xn_p.shape[0] // tm
    return pl.pallas_call(
        functools.partial(_dispatch_kernel, n_first=n_first),
        grid_spec=pltpu.PrefetchScalarGridSpec(
            num_scalar_prefetch=1,
            grid=(t // tm,),
            in_specs=[pl.BlockSpec((1, 1, tm * TOP_K), lambda i, *_: (i, 0, 0),
                                   memory_space=pltpu.SMEM)] + _pair_specs((tm, w), 0, n_first),
            out_specs=pl.BlockSpec(memory_space=pl.ANY),
            scratch_shapes=[pltpu.VMEM((EXPERT_TILE, w), F32),
                            pltpu.SemaphoreType.DMA, pltpu.SemaphoreType.DMA]),
        out_shape=jax.ShapeDtypeStruct((n_rows, w), F32),
        compiler_params=_cparams(("arbitrary",)),
        name="moe_dispatch",
    )(zfill, slot_tiles, xn_p, xn_s)


def _expert_kernel(te_ref, nt_ref, xs_ref, wg_ref, wu_ref, wd_ref, ys_ref, wg_scr, wu_scr, wd_scr):
    i = pl.program_id(0)

    @pl.when(i < nt_ref[0])
    def _():
        @pl.when((i == 0) | (te_ref[i] != te_ref[jnp.maximum(i - 1, 0)]))
        def _():
            wg_scr[...] = wg_ref[0].astype(BF16)
            wu_scr[...] = wu_ref[0].astype(BF16)
            wd_scr[...] = wd_ref[0].astype(BF16)

        x = xs_ref[...].astype(BF16)
        h = jax.nn.silu(_dot(x, wg_scr[...])) * _dot(x, wu_scr[...])
        ys_ref[...] = _dot(h.astype(BF16), wd_scr[...])

    @pl.when(i >= nt_ref[0])
    def _():
        ys_ref[...] = jnp.zeros_like(ys_ref)


def _experts(xs, tile_expert, n_tiles, w_gate, w_up, w_down):
    n_rows, w = xs.shape
    ne, d, de = w_gate.shape
    tile = lambda i, te, nt: (jnp.minimum(i, nt[0] - 1), 0)
    out_tile = lambda i, te, nt: (i, 0)
    wsel = lambda i, te, nt: (te[i], 0, 0)
    return pl.pallas_call(
        _expert_kernel,
        grid_spec=pltpu.PrefetchScalarGridSpec(
            num_scalar_prefetch=2,
            grid=(n_rows // EXPERT_TILE,),
            in_specs=[pl.BlockSpec((EXPERT_TILE, w), tile),
                      pl.BlockSpec((1, d, de), wsel), pl.BlockSpec((1, d, de), wsel),
                      pl.BlockSpec((1, de, d), wsel)],
            out_specs=pl.BlockSpec((EXPERT_TILE, w), out_tile),
            scratch_shapes=[pltpu.VMEM((d, de), BF16), pltpu.VMEM((d, de), BF16),
                            pltpu.VMEM((de, d), BF16)]),
        out_shape=jax.ShapeDtypeStruct((n_rows, w), F32),
        compiler_params=_cparams(("arbitrary",)),
        name="moe_experts",
    )(tile_expert, n_tiles, xs, w_gate, w_up, w_down)


def _combine_kernel(slot_ref, gate_ref, xnp_ref, xns_ref, x2p_ref, x2s_ref, sg_ref, su_ref, sd_ref,
                    gf_ref, ys_hbm, yp_ref, ysamp_ref, buf, sem, *, n_first):
    tm = xnp_ref.shape[0]
    i = pl.program_id(0)
    first = i < n_first

    def body(r, _):
        for k in range(TOP_K):
            s = slot_ref[0, 0, r * TOP_K + k]
            pltpu.make_async_copy(ys_hbm.at[pl.ds(s, 1)], buf.at[k, pl.ds(r, 1)], sem).start()
        return 0

    lax.fori_loop(0, tm, body, 0)
    x = _pick(first, xnp_ref, xns_ref).astype(BF16)
    hs = jax.nn.silu(_dot(x, sg_ref[...])) * _dot(x, su_ref[...])
    acc = _pick(first, x2p_ref, x2s_ref) + _dot(hs.astype(BF16), sd_ref[...])
    _row_copy_wait(ys_hbm, tm * TOP_K, sem)
    g = gate_ref[...]
    for k in range(TOP_K):
        acc = acc + g[:, k:k + 1] * buf[k]
    y = _rms(acc, gf_ref[...])

    @pl.when(first)
    def _():
        yp_ref[...] = y

    @pl.when(jnp.logical_not(first))
    def _():
        ysamp_ref[...] = y


def _combine(slot_tiles, gates, xn_p, xn_s, x2_p, x2_s, ws_gate, ws_up, ws_down, g_final, ys):
    d = x2_p.shape[1]
    tp, tsamp = xn_p.shape[0], xn_s.shape[0]
    dsh = ws_gate.shape[1]
    tm = MOE_TOKEN_TILE
    n_first = tp // tm
    row = lambda i: (i, 0)
    pair = lambda: _pair_specs((tm, d), 0, n_first)
    return pl.pallas_call(
        functools.partial(_combine_kernel, n_first=n_first),
        grid=((tp + tsamp) // tm,),
        in_specs=[pl.BlockSpec((1, 1, tm * TOP_K), lambda i: (i, 0, 0), memory_space=pltpu.SMEM),
                  pl.BlockSpec((tm, TOP_K), row)] + pair() + pair()
                 + [_const_spec((d, dsh)), _const_spec((d, dsh)), _const_spec((dsh, d)),
                    _const_spec((1, d)), pl.BlockSpec(memory_space=pl.ANY)],
        out_specs=pair(),
        out_shape=[jax.ShapeDtypeStruct((tp, d), F32), jax.ShapeDtypeStruct((tsamp, d), F32)],
        scratch_shapes=[pltpu.VMEM((TOP_K, tm, d), F32), pltpu.SemaphoreType.DMA],
        compiler_params=_cparams(("arbitrary",)),
        name="moe_combine",
    )(slot_tiles, gates, xn_p, xn_s, x2_p, x2_s, ws_gate.astype(BF16), ws_up.astype(BF16),
      ws_down.astype(BF16), g_final.reshape(1, d), ys)


def _moe(xn_p, xn_s, x2_p, x2_s, eidx, gate, posk, counts, w_gate, w_up, w_down, ws_gate, ws_up,
         ws_down, g_final):
    t = xn_p.shape[0] + xn_s.shape[0]
    ne = w_gate.shape[0]
    n_tiles_max = t * TOP_K // EXPERT_TILE + ne
    slot, tile_expert, n_tiles, zfill = _moe_plan(eidx, posk, counts[:, 0], n_tiles_max)
    slot_tiles = jnp.transpose(slot).reshape(t // MOE_TOKEN_TILE, 1, MOE_TOKEN_TILE * TOP_K)
    xs = _dispatch(xn_p, xn_s, slot_tiles, zfill, n_tiles_max * EXPERT_TILE)
    ys = _experts(xs, tile_expert, n_tiles, w_gate, w_up, w_down)
    return _combine(slot_tiles, jnp.transpose(gate), xn_p, xn_s, x2_p, x2_s, ws_gate, ws_up,
                    ws_down, g_final, ys)


def kernel(x_prompt, x_sample, cache_mem_k, cache_mem_v, state_s5_re, state_s5_im, state_gla, mem_prompt, w_in, w_a2, b_a2, lam_re, lam_im, log_dt, b_re, b_im, c_re, c_im, d_skip, w_glu, b_glu, g_s5_out, g_gla_head, w_out, g_mix, g_xattn, g_mem, w_q, w_k, w_v, w_o, g_ffn, w_router, router_bias, w_gate, w_up, w_down, ws_gate, ws_up, ws_down, g_final):
    depth = w_in.shape[0]
    assert depth == 1, "one trunk layer"
    bp, lp, d = x_prompt.shape
    bs, ls, _ = x_sample.shape
    assert bp == 1 and ls == CHUNK and lp % CHUNK == 0
    n_mem = mem_prompt.shape[1]
    g, p = lam_re.shape[1:]
    d_s5 = g * S5_GROUP
    d_qk = GLA_HEADS * GLA_DK
    d_gla = w_out.shape[1] - d_s5
    ns = d_s5 // LANES
    ne = w_router.shape[2]
    tp, tsamp = bp * lp, bs * ls
    xp = x_prompt.reshape(tp, d)
    xs = x_sample.reshape(tsamp, d)
    sq = lambda a: a.reshape(a.shape[1:])

    mk_p, mv_p = _memory_kv(mem_prompt.reshape(n_mem, d), sq(g_mem), sq(w_k).astype(BF16),
                            sq(w_v).astype(BF16))

    u, q, k, v, a, r = _in_proj(xp, xs, sq(g_mix), sq(w_in), d_s5, d_qk, d_gla)

    ksm, psm, qsm, step_m, carry_m, a8 = _s5_coeffs(
        sq(lam_re), sq(lam_im), sq(log_dt), sq(b_re), sq(b_im), sq(c_re), sq(c_im), sq(d_skip))
    tables = tuple(_s5_expand(ksm, psm, qsm)) + (step_m, carry_m, a8)
    zero_h = jnp.zeros((ns, bp, 2 * GROUPS_PER_SLAB * p), F32)
    rows_p = lp // S5_BLOCK
    y_p, h_p = _s5_mixer(u, tables, zero_h, 0, tp, rows_per_seq=rows_p,
                         tile_rows=math.gcd(rows_p, 512))
    h0_s = _pack_s5_state(sq(state_s5_re), sq(state_s5_im), ns)
    y_s, h_s = _s5_mixer(u, tables, h0_s, tp, tsamp, rows_per_seq=ls // S5_BLOCK,
                         tile_rows=tsamp // S5_BLOCK)

    wa2 = jnp.pad(sq(w_a2), ((0, LANES - GLA_RANK), (0, 0)))
    zero_s = jnp.zeros((bp,) + state_gla.shape[2:], F32)
    gla_p, sg_p = _gla_mixer(q, k, v, a, r, wa2, sq(b_a2), sq(g_gla_head), zero_s, 0, tp,
                             carry=True, n_chunks=4)
    gla_s, sg_s = _gla_mixer(q, k, v, a, r, wa2, sq(b_a2), sq(g_gla_head), sq(state_gla), tp,
                             tsamp, carry=False, n_chunks=4)

    x1, xq = _mix_out(y_p, y_s, gla_p, gla_s, xp, xs, sq(w_glu), sq(b_glu), sq(g_s5_out),
                      sq(w_out), sq(g_xattn), sq(w_q))

    wr_hi, wr_lo = _split_bf16(jnp.transpose(sq(w_router)))
    wo = sq(w_o).astype(BF16)
    x2_p, xn_p, lg_p = _xattn(xq, x1, mk_p[None], mv_p[None], wo, sq(g_ffn), wr_hi, wr_lo,
                              0, tp, nb=1, lt=TOKEN_TILE)
    x2_s, xn_s, lg_s = _xattn(xq, x1, cache_mem_k.reshape(bs, n_mem, d),
                              cache_mem_v.reshape(bs, n_mem, d), wo, sq(g_ffn), wr_hi, wr_lo,
                              tp, tsamp, nb=2, lt=ls)

    eidx, gate, posk, counts = _route(lg_p, lg_s, sq(router_bias))
    y_p2, y_s2 = _moe(xn_p, xn_s, x2_p, x2_s, eidx, gate, posk, counts, sq(w_gate), sq(w_up),
                      sq(w_down), sq(ws_gate), sq(ws_up), sq(ws_down), g_final)

    xh = d // X_HEADS
    re_p, im_p = _unpack_s5_state(h_p, g, p)
    re_s, im_s = _unpack_s5_state(h_s, g, p)
    return (y_p2.reshape(bp, lp, d), y_s2.reshape(bs, ls, d),
            mk_p.reshape(1, bp, n_mem, X_HEADS, xh), mv_p.reshape(1, bp, n_mem, X_HEADS, xh),
            re_p[None], im_p[None], sg_p[None], re_s[None], im_s[None], sg_s[None])
```

```python
import functools
import math

import jax
import jax.numpy as jnp
from jax import lax
from jax.experimental import pallas as pl
from jax.experimental.pallas import tpu as pltpu

F32 = jnp.float32
BF16 = jnp.bfloat16

EPS = 1e-6
CHUNK = 64
S5_GROUP = 16
GLA_HEADS = 4
GLA_DK = 128
GLA_RANK = 16
GLA_TAU = 16.0
X_HEADS = 4
N_GROUPS = 8
TOPK_GROUPS = 4
TOP_K = 8
ROUTE_SCALE = 2.5

LANES = 128
SUBLANES = 8
S5_BLOCK = SUBLANES
GROUPS_PER_SLAB = LANES // S5_GROUP
TOKEN_TILE = 512
EXPERT_TILE = 256
MOE_TOKEN_TILE = 256
VMEM_LIMIT = 56 * 1024 * 1024


def _cparams(sem):
    return pltpu.CompilerParams(dimension_semantics=sem, vmem_limit_bytes=VMEM_LIMIT)


def _const_spec(shape):
    nd = len(shape)
    return pl.BlockSpec(shape, lambda *_: (0,) * nd, pipeline_mode=pl.Buffered(1))


def _pair_specs(block, axis, n_first):
    def at(f):
        return lambda i, *_: tuple(f(i) if a == axis else 0 for a in range(len(block)))
    return [pl.BlockSpec(block, at(lambda i: jnp.minimum(i, n_first - 1))),
            pl.BlockSpec(block, at(lambda i: jnp.maximum(i - n_first, 0)))]


def _pick(first, a_ref, b_ref):
    return jnp.where(first, a_ref[...], b_ref[...])


def _rms(x, g):
    return x * lax.rsqrt(jnp.mean(x * x, axis=-1, keepdims=True) + EPS) * g


def _split_bf16(x):
    hi = x.astype(BF16)
    lo = (x - hi.astype(F32)).astype(BF16)
    return hi, lo


def _split3_bf16(x):
    a = x.astype(BF16)
    r = x - a.astype(F32)
    b = r.astype(BF16)
    c = (r - b.astype(F32)).astype(BF16)
    return a, b, c


def _dot(a, b):
    return jnp.dot(a, b, preferred_element_type=F32)


def _dot_nt(a, b):
    return lax.dot_general(a, b, (((1,), (1,)), ((), ())), preferred_element_type=F32)


def _dot_tn(a, b):
    return lax.dot_general(a, b, (((0,), (0,)), ((), ())), preferred_element_type=F32)


def _memkv_kernel(mem_ref, g_ref, wk_ref, wv_ref, mk_ref, mv_ref):
    m = _rms(mem_ref[...], g_ref[...]).astype(BF16)
    mk_ref[...] = _dot(m, wk_ref[...])
    mv_ref[...] = _dot(m, wv_ref[...])


def _memory_kv(mem, g_mem, wk, wv):
    n, d = mem.shape
    tn = 512
    return pl.pallas_call(
        _memkv_kernel,
        grid=(d // tn,),
        in_specs=[_const_spec((n, d)), _const_spec((1, d)),
                  pl.BlockSpec((d, tn), lambda j: (0, j)),
                  pl.BlockSpec((d, tn), lambda j: (0, j))],
        out_specs=[pl.BlockSpec((n, tn), lambda j: (0, j))] * 2,
        out_shape=[jax.ShapeDtypeStruct((n, d), F32)] * 2,
        compiler_params=_cparams(("arbitrary",)),
        name="memory_kv",
    )(mem, g_mem.reshape(1, d), wk, wv)


def _inproj_kernel(xp_ref, xs_ref, g_ref, wuh_ref, wul_ref, wqkv_ref, wa_ref, wr_ref,
                   u_ref, q_ref, k_ref, v_ref, a_ref, r_ref, *, n_first):
    first = pl.program_id(0) < n_first
    x = _pick(first, xp_ref, xs_ref)
    nx = _rms(x, g_ref[...])
    hi, lo = _split_bf16(nx)
    u = _dot(hi, wuh_ref[...])
    for j in range(u_ref.shape[0]):
        u_ref[j] = u[:, j * LANES:(j + 1) * LANES]

    @pl.when(jnp.logical_not(first))
    def _():
        fix = _dot(lo, wuh_ref[...]) + _dot(hi, wul_ref[...])
        for j in range(u_ref.shape[0]):
            u_ref[j] += fix[:, j * LANES:(j + 1) * LANES]

    qkv = _dot(hi, wqkv_ref[...])
    dq = q_ref.shape[1]
    q_ref[...] = qkv[:, :dq].astype(BF16)
    k_ref[...] = qkv[:, dq:2 * dq].astype(BF16)
    v_ref[...] = qkv[:, 2 * dq:].astype(BF16)
    a_ref[...] = _dot(hi, wa_ref[...]) + _dot(lo, wa_ref[...])
    r_ref[...] = _dot(hi, wr_ref[...]).astype(BF16)


def _in_proj(xp, xs, g_mix, w_in, d_s5, d_qk, d_gla):
    d = xp.shape[1]
    t = xp.shape[0] + xs.shape[0]
    tm = TOKEN_TILE
    n_first = xp.shape[0] // tm
    o1 = d_s5
    o2 = o1 + 2 * d_qk + d_gla
    o3 = o2 + GLA_RANK
    wu_hi, wu_lo = _split_bf16(w_in[:, :o1])
    wqkv = w_in[:, o1:o2].astype(BF16)
    wa = jnp.pad(w_in[:, o2:o3], ((0, 0), (0, LANES - GLA_RANK))).astype(BF16)
    wr = w_in[:, o3:].astype(BF16)
    n_slab = d_s5 // LANES
    row = lambda i: (i, 0)
    return pl.pallas_call(
        functools.partial(_inproj_kernel, n_first=n_first),
        grid=(t // tm,),
        in_specs=_pair_specs((tm, d), 0, n_first) + [
            _const_spec((1, d)), _const_spec(wu_hi.shape), _const_spec(wu_lo.shape),
            _const_spec(wqkv.shape), _const_spec(wa.shape), _const_spec(wr.shape)],
        out_specs=[pl.BlockSpec((n_slab, tm, LANES), lambda i: (0, i, 0)),
                   pl.BlockSpec((tm, d_qk), row), pl.BlockSpec((tm, d_qk), row),
                   pl.BlockSpec((tm, d_gla), row), pl.BlockSpec((tm, LANES), row),
                   pl.BlockSpec((tm, d_gla), row)],
        out_shape=[jax.ShapeDtypeStruct((n_slab, t, LANES), F32),
                   jax.ShapeDtypeStruct((t, d_qk), BF16), jax.ShapeDtypeStruct((t, d_qk), BF16),
                   jax.ShapeDtypeStruct((t, d_gla), BF16), jax.ShapeDtypeStruct((t, LANES), F32),
                   jax.ShapeDtypeStruct((t, d_gla), BF16)],
        compiler_params=_cparams(("arbitrary",)),
        name="in_proj",
    )(xp, xs, g_mix.reshape(1, d), wu_hi, wu_lo, wqkv, wa, wr)


def _s5_coeffs(lam_re, lam_im, log_dt, b_re, b_im, c_re, c_im, d_skip):
    g, p = lam_re.shape
    h = b_re.shape[-1]
    nb = S5_BLOCK
    ns = g // GROUPS_PER_SLAB
    gl = GROUPS_PER_SLAB
    dt = jnp.exp(log_dt.astype(F32))[:, None]
    lr, li = lam_re.astype(F32), lam_im.astype(F32)
    mag = jnp.exp(lr * dt)
    ar, ai = mag * jnp.cos(li * dt), mag * jnp.sin(li * dt)
    den = lr * lr + li * li
    nr = ar - 1.0
    cf_r = (nr * lr + ai * li) / den
    cf_i = (ai * lr - nr * li) / den
    bb_r = cf_r[..., None] * b_re - cf_i[..., None] * b_im
    bb_i = cf_r[..., None] * b_im + cf_i[..., None] * b_re

    def cpow(n):
        e = n[:, None, None] * dt[None]
        m = jnp.exp(lr[None] * e)
        return m * jnp.cos(li[None] * e), m * jnp.sin(li[None] * e)

    pr, pi = cpow(jnp.arange(nb + 1, dtype=F32))

    cb_r = jnp.einsum('gcp,gph->gpch', c_re, bb_r) - jnp.einsum('gcp,gph->gpch', c_im, bb_i)
    cb_i = jnp.einsum('gcp,gph->gpch', c_re, bb_i) + jnp.einsum('gcp,gph->gpch', c_im, bb_r)
    taps = (jnp.einsum('ngp,gpch->ngch', pr[:nb], cb_r)
            - jnp.einsum('ngp,gpch->ngch', pi[:nb], cb_i))
    taps = taps.at[0].add(d_skip[:, :, None] * jnp.eye(h, dtype=F32)[None])
    ksm = jnp.transpose(taps.reshape(nb, ns, gl, h, h), (1, 0, 2, 4, 3))
    ksm = ksm.reshape(ns, nb, LANES, h)

    wr, wi = pr[nb - 1::-1][:nb], pi[nb - 1::-1][:nb]
    inj_r = wr[..., None] * bb_r[None] - wi[..., None] * bb_i[None]
    inj_i = wr[..., None] * bb_i[None] + wi[..., None] * bb_r[None]
    inj = jnp.stack([inj_r, inj_i], axis=0).reshape(2, nb, ns, gl, p, h)
    psm = jnp.transpose(inj, (2, 1, 3, 5, 0, 4)).reshape(ns, nb, LANES, 2 * p)

    er, ei = pr[1:nb + 1], pi[1:nb + 1]
    q_r = c_re[None] * er[:, :, None, :] - c_im[None] * ei[:, :, None, :]
    q_i = -(c_re[None] * ei[:, :, None, :] + c_im[None] * er[:, :, None, :])
    qq = jnp.stack([q_r, q_i], axis=0).reshape(2, nb, ns, gl, h, p)
    qsm = jnp.transpose(qq, (2, 0, 3, 5, 1, 4)).reshape(ns, 2 * gl * p, nb * h)

    def slab(xr, xi):
        k = xr.shape[0]
        xr = jnp.transpose(xr.reshape(k, ns, gl * p), (1, 0, 2))
        xi = jnp.transpose(xi.reshape(k, ns, gl * p), (1, 0, 2))
        return jnp.concatenate([xr, xi], axis=-1)

    rows = jnp.arange(SUBLANES, dtype=F32)
    cr, ci = cpow(nb * rows)
    carry_m = slab(cr, ci)
    sr, si = cpow(nb * jnp.array([1.0, 2.0, 4.0], F32))
    keep = (rows[None, :] >= jnp.array([1.0, 2.0, 4.0], F32)[:, None]).astype(F32)
    step_m = slab((sr[:, None] * keep[:, :, None, None]).reshape(3 * SUBLANES, g, p),
                  (si[:, None] * keep[:, :, None, None]).reshape(3 * SUBLANES, g, p))
    step_m = step_m.reshape(ns, 3, SUBLANES, 2 * gl * p)
    a8 = carry_m[:, 1:2]
    return ksm, psm, qsm, step_m, carry_m, a8


def _s5_expand_kernel(ksm_ref, psm_ref, qsm_ref, tm_ref, ph_ref, pl_ref, qm_ref):
    nb = S5_BLOCK
    h = S5_GROUP
    w = ph_ref.shape[2]
    p = w // (2 * GROUPS_PER_SLAB)

    def spread(x, sel, keep):
        a, b, c = _split3_bf16(x)
        return jnp.where(keep, _dot(a, sel) + _dot(b, sel) + _dot(c, sel), 0.0)

    def iota(shape, axis):
        return lax.broadcasted_iota(jnp.int32, shape, axis)

    sel_c = (iota((h, LANES), 1) % h == iota((h, LANES), 0)).astype(BF16)
    keep_t = iota((LANES, LANES), 0) // h == iota((LANES, LANES), 1) // h
    tm_ref[...] = jnp.zeros_like(tm_ref)
    for tau in range(nb):
        blk = spread(ksm_ref[0, tau], sel_c, keep_t).astype(BF16)
        for s in range(nb - tau):
            t = s + tau
            tm_ref[0, s * LANES:(s + 1) * LANES, t * LANES:(t + 1) * LANES] = blk

    half = w // 2
    keep_p = iota((LANES, half), 0) // h == iota((LANES, half), 1) // p
    for ri in range(2):
        sel_p = (iota((2 * p, half), 0) == iota((2 * p, half), 1) % p + ri * p).astype(BF16)
        for s in range(nb):
            blk = spread(psm_ref[0, s], sel_p, keep_p)
            hi, lo = _split_bf16(blk)
            ph_ref[0, s * LANES:(s + 1) * LANES, ri * half:(ri + 1) * half] = hi
            pl_ref[0, s * LANES:(s + 1) * LANES, ri * half:(ri + 1) * half] = lo

    keep_q = (iota((w, LANES), 0) % half) // p == iota((w, LANES), 1) // h
    for t in range(nb):
        sel_q = (iota((nb * h, LANES), 0) == iota((nb * h, LANES), 1) % h + t * h).astype(BF16)
        qm_ref[0, :, t * LANES:(t + 1) * LANES] = spread(qsm_ref[0], sel_q, keep_q).astype(BF16)


def _s5_expand(ksm, psm, qsm):
    ns = ksm.shape[0]
    w = qsm.shape[1]
    k = S5_BLOCK * LANES
    blk = lambda shape: pl.BlockSpec((1,) + shape, lambda j: (j,) + (0,) * len(shape))
    return pl.pallas_call(
        _s5_expand_kernel,
        grid=(ns,),
        in_specs=[blk(ksm.shape[1:]), blk(psm.shape[1:]), blk(qsm.shape[1:])],
        out_specs=[blk((k, k)), blk((k, w)), blk((k, w)), blk((w, k))],
        out_shape=[jax.ShapeDtypeStruct((ns, k, k), BF16), jax.ShapeDtypeStruct((ns, k, w), BF16),
                   jax.ShapeDtypeStruct((ns, k, w), BF16), jax.ShapeDtypeStruct((ns, w, k), BF16)],
        compiler_params=_cparams(("arbitrary",)),
        name="s5_tables",
    )(ksm, psm, qsm)


def _cmul(xr, xi, mr, mi):
    return xr * mr - xi * mi, xr * mi + xi * mr


def _s5_kernel(u_ref, ph_ref, pl_ref, tm_ref, qm_ref, step_ref, carry_ref, a8_ref, h0_ref,
               y_ref, hout_ref, s_scr, hin_scr, c_scr, *, rows_per_seq):
    ti = pl.program_id(1)
    rows = s_scr.shape[0]
    half = s_scr.shape[1] // 2
    nb = S5_BLOCK
    u8 = jnp.concatenate([u_ref[0, pl.ds(s, rows, stride=nb), :] for s in range(nb)], axis=1)
    hi, lo = _split_bf16(u8)
    per_block_seq = rows_per_seq == SUBLANES
    if per_block_seq:
        s_scr[...] = _dot(hi, ph_ref[0]) + _dot(lo, ph_ref[0]) + _dot(hi, pl_ref[0])
    else:
        s_scr[...] = _dot(hi, ph_ref[0])

    if not per_block_seq:
        @pl.when(ti == 0)
        def _():
            c_scr[...] = h0_ref[0]

    not_first = (lax.broadcasted_iota(jnp.int32, (SUBLANES, 1), 0) >= 1).astype(F32)
    a8r, a8i = a8_ref[0, :, :half], a8_ref[0, :, half:]
    cmr, cmi = carry_ref[0, :, :half], carry_ref[0, :, half:]

    def body(rb, _):
        r0 = pl.multiple_of(rb * SUBLANES, SUBLANES)
        sb = s_scr[pl.ds(r0, SUBLANES), :]
        x = pltpu.roll(sb, 1, 0) * not_first
        xr, xi = x[:, :half], x[:, half:]
        for k, sh in enumerate((1, 2, 4)):
            m = step_ref[0, k]
            pr, pi = _cmul(pltpu.roll(xr, sh, 0), pltpu.roll(xi, sh, 0), m[:, :half], m[:, half:])
            xr, xi = xr + pr, xi + pi
        c = h0_ref[0, pl.ds(rb, 1), :] if per_block_seq else c_scr[...]
        cr, ci = c[:, :half], c[:, half:]
        pr, pi = _cmul(cr, ci, cmr, cmi)
        hr, hi_ = xr + pr, xi + pi
        hin_scr[pl.ds(r0, SUBLANES), :half] = hr
        hin_scr[pl.ds(r0, SUBLANES), half:] = hi_
        nr, ni = _cmul(hr[SUBLANES - 1:], hi_[SUBLANES - 1:], a8r, a8i)
        cn = jnp.concatenate([nr, ni], axis=1) + sb[SUBLANES - 1:]
        if per_block_seq:
            hout_ref[0, pl.ds(rb, 1), :] = cn
        else:
            c_scr[...] = cn
        return 0

    lax.fori_loop(0, rows // SUBLANES, body, 0)
    if not per_block_seq:
        hout_ref[0] = c_scr[...]

    y8 = _dot(hi, tm_ref[0]) + _dot(hin_scr[...].astype(BF16), qm_ref[0])
    for t in range(nb):
        y_ref[0, pl.ds(t, rows, stride=nb), :] = y8[:, t * LANES:(t + 1) * LANES]


def _s5_mixer(u_slabs, tables, h0, t_start, t, rows_per_seq, tile_rows):
    tm, pm_hi, pm_lo, qm, step_m, carry_m, a8 = tables
    ns = u_slabs.shape[0]
    rows = t // S5_BLOCK
    n_tiles = rows // tile_rows
    assert rows % tile_rows == 0 and t_start % (tile_rows * S5_BLOCK) == 0
    off = t_start // (tile_rows * S5_BLOCK)
    n_seq = h0.shape[1]
    w = pm_hi.shape[-1]
    slab3 = lambda j, i: (j, 0, 0)
    kern = functools.partial(_s5_kernel, rows_per_seq=rows_per_seq)
    return pl.pallas_call(
        kern,
        grid=(ns, n_tiles),
        in_specs=[pl.BlockSpec((1, tile_rows * S5_BLOCK, LANES), lambda j, i: (j, i + off, 0)),
                  pl.BlockSpec((1,) + pm_hi.shape[1:], slab3),
                  pl.BlockSpec((1,) + pm_lo.shape[1:], slab3),
                  pl.BlockSpec((1,) + tm.shape[1:], slab3),
                  pl.BlockSpec((1,) + qm.shape[1:], slab3),
                  pl.BlockSpec((1,) + step_m.shape[1:], lambda j, i: (j, 0, 0, 0)),
                  pl.BlockSpec((1,) + carry_m.shape[1:], slab3),
                  pl.BlockSpec((1,) + a8.shape[1:], slab3),
                  pl.BlockSpec((1, n_seq, w), slab3)],
        out_specs=[pl.BlockSpec((1, tile_rows * S5_BLOCK, LANES), lambda j, i: (j, i, 0)),
                   pl.BlockSpec((1, n_seq, w), slab3)],
        out_shape=[jax.ShapeDtypeStruct((ns, t, LANES), F32),
                   jax.ShapeDtypeStruct((ns, n_seq, w), F32)],
        scratch_shapes=[pltpu.VMEM((tile_rows, w), F32), pltpu.VMEM((tile_rows, w), F32),
                        pltpu.VMEM((1, w), F32)],
        compiler_params=_cparams(("arbitrary", "arbitrary")),
        name="s5_mixer",
    )(u_slabs, pm_hi, pm_lo, tm, qm, step_m, carry_m, a8, h0)


def _pack_s5_state(re, im, ns):
    b = re.shape[0]
    r = re.astype(F32).reshape(b, ns, -1)
    i = im.astype(F32).reshape(b, ns, -1)
    return jnp.transpose(jnp.concatenate([r, i], axis=-1), (1, 0, 2))


def _unpack_s5_state(hc, g, p):
    ns, b, w = hc.shape
    hc = jnp.transpose(hc, (1, 0, 2))
    re = hc[:, :, :w // 2].reshape(b, g, p)
    im = hc[:, :, w // 2:].reshape(b, g, p)
    return re, im


def _gla_kernel(q_ref, k_ref, v_ref, a_ref, r_ref, wa2_ref, ba2_ref, gh_ref, s0_ref,
                o_ref, sout_ref, st_scr, *, carry, n_chunks):
    step = pl.program_id(0)
    dk = GLA_DK
    dv = v_ref.shape[1] // GLA_HEADS
    c = CHUNK
    scale = dk ** -0.5
    ri = lax.broadcasted_iota(jnp.int32, (c, c), 0)
    ci = lax.broadcasted_iota(jnp.int32, (c, c), 1)
    causal = ri >= ci
    tril = causal.astype(BF16)
    eye_dk = (lax.broadcasted_iota(jnp.int32, (dk, dk), 0)
              == lax.broadcasted_iota(jnp.int32, (dk, dk), 1))

    if carry:
        @pl.when(step == 0)
        def _():
            st_scr[...] = s0_ref[0]

    wa_hi, wa_lo = _split_bf16(wa2_ref[...])
    for n in range(n_chunks):
        rows = slice(n * c, (n + 1) * c)
        a_hi, a_lo = _split_bf16(a_ref[rows, :])
        logit = _dot(a_hi, wa_hi) + _dot(a_lo, wa_hi) + _dot(a_hi, wa_lo) + ba2_ref[...]
        g = jax.nn.log_sigmoid(logit) * (1.0 / GLA_TAU)
        g1, g2, g3 = _split3_bf16(g)
        bcum = _dot(tril, g1) + _dot(tril, g2) + _dot(tril, g3)
        for h in range(GLA_HEADS):
            ks = slice(h * dk, (h + 1) * dk)
            vs = slice(h * dv, (h + 1) * dv)
            b = bcum[:, ks]
            qh = q_ref[rows, ks].astype(F32) * scale
            kh = k_ref[rows, ks].astype(F32)
            vh = v_ref[rows, vs]
            state = s0_ref[n, h] if not carry else st_scr[h]
            qe = (qh * jnp.exp(b)).astype(BF16)
            ke = (kh * jnp.exp(-b)).astype(BF16)
            att = jnp.where(causal, _dot_nt(qe, ke), 0.0)
            o = _dot(att.astype(BF16), vh) + _dot(qe, state.astype(BF16))
            blast = b[c - 1:c, :]
            kd = (kh * jnp.exp(blast - b)).astype(BF16)
            decay = jnp.sum(jnp.where(eye_dk, jnp.exp(blast), 0.0), axis=1, keepdims=True)
            new_state = decay * state + _dot_tn(kd, vh)
            if carry:
                st_scr[h] = new_state
            else:
                sout_ref[n, h] = new_state
            on = _rms(o, gh_ref[...])
            rr = r_ref[rows, vs].astype(F32)
            o_ref[rows, vs] = (on * (rr * jax.nn.sigmoid(rr))).astype(BF16)
    if carry:
        sout_ref[0] = st_scr[...]


def _gla_mixer(q, k, v, a, r, wa2, ba2, g_head, s0, t_start, t, carry, n_chunks):
    dqk = q.shape[1]
    dvt = v.shape[1]
    hh, dk, dv = s0.shape[1:]
    rows = n_chunks * CHUNK
    assert t % rows == 0 and t_start % rows == 0
    off = t_start // rows
    row = lambda i: (i, 0)
    row_in = lambda i: (i + off, 0)
    if carry:
        sblk, smap = (1, hh, dk, dv), (lambda i: (0, 0, 0, 0))
    else:
        sblk, smap = (n_chunks, hh, dk, dv), (lambda i: (i, 0, 0, 0))
    kern = functools.partial(_gla_kernel, carry=carry, n_chunks=n_chunks)
    return pl.pallas_call(
        kern,
        grid=(t // rows,),
        in_specs=[pl.BlockSpec((rows, dqk), row_in), pl.BlockSpec((rows, dqk), row_in),
                  pl.BlockSpec((rows, dvt), row_in), pl.BlockSpec((rows, LANES), row_in),
                  pl.BlockSpec((rows, dvt), row_in),
                  _const_spec(wa2.shape), _const_spec((1, dqk)), _const_spec((1, dv)),
                  pl.BlockSpec(sblk, smap)],
        out_specs=[pl.BlockSpec((rows, dvt), row), pl.BlockSpec(sblk, smap)],
        out_shape=[jax.ShapeDtypeStruct((t, dvt), BF16), jax.ShapeDtypeStruct(s0.shape, F32)],
        scratch_shapes=[pltpu.VMEM((hh, dk, dv), F32)],
        compiler_params=_cparams(("arbitrary",)),
        name="gla_mixer",
    )(q, k, v, a, r, wa2, ba2.reshape(1, dqk), g_head.reshape(1, dv), s0)


def _mixout_kernel(yp_ref, ys_ref, gp_ref, gs_ref, xp_ref, xs_ref, wglu_ref, bglu_ref, gs5_ref,
                   wout_ref, gx_ref, wq_ref, x1_ref, q_ref, *, n_first):
    first = pl.program_id(0) < n_first
    yb = _pick(first, yp_ref, ys_ref)
    y = jnp.concatenate([yb[j] for j in range(yb.shape[0])], axis=1)
    z = jax.nn.gelu(y)
    gate = jax.nn.sigmoid(_dot(z.astype(BF16), wglu_ref[...]) + bglu_ref[...])
    s5 = _rms(z * gate, gs5_ref[...])
    cat = jnp.concatenate([s5.astype(BF16), _pick(first, gp_ref, gs_ref)], axis=1)
    x1 = _pick(first, xp_ref, xs_ref) + _dot(cat, wout_ref[...])
    x1_ref[...] = x1
    q_ref[...] = _dot(_rms(x1, gx_ref[...]).astype(BF16), wq_ref[...]).astype(BF16)


def _mix_out(y_p, y_s, gla_p, gla_s, xp, xs, w_glu, b_glu, g_s5, w_out, g_x, w_q):
    d = xp.shape[1]
    t = xp.shape[0] + xs.shape[0]
    ns = y_p.shape[0]
    ds5 = ns * LANES
    dg = gla_p.shape[1]
    tm = TOKEN_TILE // 2
    n_first = xp.shape[0] // tm
    row = lambda i: (i, 0)
    return pl.pallas_call(
        functools.partial(_mixout_kernel, n_first=n_first),
        grid=(t // tm,),
        in_specs=(_pair_specs((ns, tm, LANES), 1, n_first) + _pair_specs((tm, dg), 0, n_first)
                  + _pair_specs((tm, d), 0, n_first)
                  + [_const_spec((ds5, ds5)), _const_spec((1, ds5)), _const_spec((1, ds5)),
                     _const_spec((ds5 + dg, d)), _const_spec((1, d)), _const_spec((d, d))]),
        out_specs=[pl.BlockSpec((tm, d), row), pl.BlockSpec((tm, d), row)],
        out_shape=[jax.ShapeDtypeStruct((t, d), F32), jax.ShapeDtypeStruct((t, d), BF16)],
        compiler_params=_cparams(("arbitrary",)),
        name="mix_out",
    )(y_p, y_s, gla_p, gla_s, xp, xs, w_glu.astype(BF16), b_glu.reshape(1, ds5),
      g_s5.reshape(1, ds5), w_out.astype(BF16), g_x.reshape(1, d), w_q.astype(BF16))


def _xattn_kernel(q_ref, mk_ref, mv_ref, x1_ref, wo_ref, gf_ref, wrh_ref, wrl_ref,
                  x2_ref, xn_ref, lg_ref):
    nb = mk_ref.shape[0]
    d = q_ref.shape[1]
    lt = q_ref.shape[0] // nb
    hd = d // X_HEADS
    scale = hd ** -0.5
    for b in range(nb):
        rows = slice(b * lt, (b + 1) * lt)
        mk = mk_ref[b].astype(BF16)
        mv = mv_ref[b].astype(BF16)
        outs = []
        for h in range(X_HEADS):
            hs = slice(h * hd, (h + 1) * hd)
            s = _dot_nt(q_ref[rows, hs], mk[:, hs]) * scale
            p = jnp.exp(s - jnp.max(s, axis=-1, keepdims=True))
            denom = jnp.sum(p, axis=-1, keepdims=True)
            outs.append((_dot(p.astype(BF16), mv[:, hs]) / denom).astype(BF16))
        o = jnp.concatenate(outs, axis=1)
        x2 = x1_ref[rows, :] + _dot(o, wo_ref[...])
        x2_ref[rows, :] = x2
        xn = _rms(x2, gf_ref[...])
        xn_ref[rows, :] = xn
        hi, lo = _split_bf16(xn)
        lg_ref[:, rows] = (_dot_nt(wrh_ref[...], hi) + _dot_nt(wrh_ref[...], lo)
                           + _dot_nt(wrl_ref[...], hi))


def _xattn(q, x1, mk, mv, w_o, g_ffn, wr_hi, wr_lo, t_start, t, nb, lt):
    d = q.shape[1]
    nm = mk.shape[1]
    ne = wr_hi.shape[0]
    rows = nb * lt
    assert t % rows == 0 and t_start % rows == 0
    off = t_start // rows
    shared = mk.shape[0] == 1
    assert nb == 1 if shared else mk.shape[0] * lt == t
    row = lambda i: (i, 0)
    row_in = lambda i: (i + off, 0)
    mblk = (lambda i: (0, 0, 0)) if shared else (lambda i: (i, 0, 0))
    return pl.pallas_call(
        _xattn_kernel,
        grid=(t // rows,),
        in_specs=[pl.BlockSpec((rows, d), row_in),
                  pl.BlockSpec((nb, nm, d), mblk), pl.BlockSpec((nb, nm, d), mblk),
                  pl.BlockSpec((rows, d), row_in),
                  _const_spec((d, d)), _const_spec((1, d)),
                  _const_spec((ne, d)), _const_spec((ne, d))],
        out_specs=[pl.BlockSpec((rows, d), row), pl.BlockSpec((rows, d), row),
                   pl.BlockSpec((ne, rows), lambda i: (0, i))],
        out_shape=[jax.ShapeDtypeStruct((t, d), F32), jax.ShapeDtypeStruct((t, d), F32),
                   jax.ShapeDtypeStruct((ne, t), F32)],
        compiler_params=_cparams(("arbitrary",)),
        name="mem_xattn",
    )(q, mk, mv, x1, w_o, g_ffn.reshape(1, d), wr_hi, wr_lo)


def _route_kernel(lgp_ref, lgs_ref, bias_ref, eidx_ref, gate_ref, pos_ref, cnt_ref, run_scr,
                  *, n_first):
    ne, tn = lgp_ref.shape

    @pl.when(pl.program_id(0) == 0)
    def _():
        run_scr[...] = jnp.zeros_like(run_scr)

    gsz = ne // N_GROUPS
    sc = jax.nn.sigmoid(_pick(pl.program_id(0) < n_first, lgp_ref, lgs_ref))
    sel = sc + bias_ref[...]
    s3 = sel.reshape(N_GROUPS, gsz, tn)
    ie = lax.broadcasted_iota(jnp.int32, s3.shape, 1).astype(F32)
    m1 = jnp.max(s3, axis=1, keepdims=True)
    first = jnp.min(jnp.where(s3 == m1, ie, float(gsz)), axis=1, keepdims=True)
    m2 = jnp.max(jnp.where(ie == first, -jnp.inf, s3), axis=1, keepdims=True)
    gs = m1 + m2
    ig = lax.broadcasted_iota(jnp.int32, gs.shape, 0)
    grank = jnp.zeros(gs.shape, jnp.int32)
    for g in range(N_GROUPS):
        other = gs[g:g + 1]
        ahead = jnp.where(other > gs, 1, jnp.where(other == gs, (ig > g).astype(jnp.int32), 0))
        grank = grank + ahead
    gkeep = jnp.broadcast_to(grank < TOPK_GROUPS, s3.shape)
    v = jnp.where(gkeep, s3, -jnp.inf).reshape(ne, tn)
    iv = lax.broadcasted_iota(jnp.int32, v.shape, 0)
    rank = jnp.zeros(v.shape, jnp.int32)
    for e in range(ne):
        other = v[e:e + 1]
        ahead = jnp.where(other > v, 1, jnp.where(other == v, (iv > e).astype(jnp.int32), 0))
        rank = rank + ahead
    chosen = rank < TOP_K
    gate = jnp.where(chosen, sc, 0.0)
    gate = gate / jnp.sum(gate, axis=0, keepdims=True) * ROUTE_SCALE
    ti = lax.broadcasted_iota(jnp.int32, (tn, tn), 0)
    tj = lax.broadcasted_iota(jnp.int32, (tn, tn), 1)
    incl = _dot(jnp.where(chosen, 1.0, 0.0).astype(BF16), (ti <= tj).astype(BF16))
    pos = run_scr[...] + incl - 1.0
    run_scr[...] = run_scr[...] + incl[:, tn - 1:tn]
    cnt_ref[...] = run_scr[...].astype(jnp.int32)
    ef = iv.astype(F32)
    for k in range(TOP_K):
        hit = rank == k
        pick = lambda a: jnp.sum(jnp.where(hit, a, 0.0), axis=0, keepdims=True)
        eidx_ref[k:k + 1, :] = pick(ef).astype(jnp.int32)
        gate_ref[k:k + 1, :] = pick(gate)
        pos_ref[k:k + 1, :] = pick(pos).astype(jnp.int32)


def _route(lg_p, lg_s, router_bias):
    ne = lg_p.shape[0]
    t = lg_p.shape[1] + lg_s.shape[1]
    tn = TOKEN_TILE
    n_first = lg_p.shape[1] // tn
    kt = lambda i: (0, i)
    return pl.pallas_call(
        functools.partial(_route_kernel, n_first=n_first),
        grid=(t // tn,),
        in_specs=_pair_specs((ne, tn), 1, n_first) + [_const_spec((ne, 1))],
        out_specs=[pl.BlockSpec((TOP_K, tn), kt), pl.BlockSpec((TOP_K, tn), kt),
                   pl.BlockSpec((TOP_K, tn), kt), pl.BlockSpec((ne, 1), lambda i: (0, 0))],
        out_shape=[jax.ShapeDtypeStruct((TOP_K, t), jnp.int32),
                   jax.ShapeDtypeStruct((TOP_K, t), F32),
                   jax.ShapeDtypeStruct((TOP_K, t), jnp.int32),
                   jax.ShapeDtypeStruct((ne, 1), jnp.int32)],
        scratch_shapes=[pltpu.VMEM((ne, 1), F32)],
        compiler_params=_cparams(("arbitrary",)),
        name="router",
    )(lg_p, lg_s, router_bias.astype(F32).reshape(ne, 1))


def _moe_plan(eidx, posk, counts, n_tiles_max):
    ne = counts.shape[0]
    ntile = (counts + EXPERT_TILE - 1) // EXPERT_TILE
    cum = jnp.cumsum(ntile)
    base = (cum - ntile) * EXPERT_TILE
    n_tiles = cum[-1:]
    experts = jnp.arange(ne, dtype=jnp.int32)
    last_used = jnp.max(jnp.where(ntile > 0, experts, 0))
    tiles = jnp.arange(n_tiles_max, dtype=jnp.int32)
    tile_expert = jnp.sum(tiles[:, None] >= cum[None, :], axis=1)
    tile_expert = jnp.minimum(tile_expert, last_used).astype(jnp.int32)
    partial = jnp.any((tiles[:, None] == cum[None, :] - 1) & (counts % EXPERT_TILE != 0)[None, :],
                      axis=1)
    zfill = (partial | (tiles >= n_tiles[0])).astype(jnp.int32)
    slot = jnp.sum(jnp.where(eidx[..., None] == experts, base, 0), axis=-1) + posk
    return slot.astype(jnp.int32), tile_expert, n_tiles.astype(jnp.int32), zfill


def _row_copy_wait(ref, n_rows, sem):
    pltpu.make_async_copy(ref.at[pl.ds(0, n_rows)], ref.at[pl.ds(0, n_rows)], sem).wait()


def _dispatch_kernel(zfill_ref, slot_ref, xp_ref, xs_ref, xs_hbm, zero_scr, sem, zsem, *, n_first):
    tm = xp_ref.shape[0]
    i = pl.program_id(0)

    @pl.when(i == 0)
    def _():
        zero_scr[...] = jnp.zeros_like(zero_scr)

        def zero_tile(n):
            row0 = pl.multiple_of(n * EXPERT_TILE, EXPERT_TILE)
            return pltpu.make_async_copy(zero_scr, xs_hbm.at[pl.ds(row0, EXPERT_TILE)], zsem)

        def fill(n, _):
            @pl.when(zfill_ref[n] != 0)
            def _():
                zero_tile(n).start()
            return 0

        def drain(n, _):
            @pl.when(zfill_ref[n] != 0)
            def _():
                zero_tile(n).wait()
            return 0

        lax.fori_loop(0, zfill_ref.shape[0], fill, 0)
        lax.fori_loop(0, zfill_ref.shape[0], drain, 0)

    def scatter_rows(x_ref):
        def body(r, _):
            for k in range(TOP_K):
                s = slot_ref[0, 0, r * TOP_K + k]
                pltpu.make_async_copy(x_ref.at[pl.ds(r, 1)], xs_hbm.at[pl.ds(s, 1)],
                                      sem).start(priority=k % 2)
            return 0
        lax.fori_loop(0, tm, body, 0)

    @pl.when(i < n_first)
    def _():
        scatter_rows(xp_ref)

    @pl.when(i >= n_first)
    def _():
        scatter_rows(xs_ref)

    _row_copy_wait(xs_hbm, tm * TOP_K, sem)


def _dispatch(xn_p, xn_s, slot_tiles, zfill, n_rows):
    w = xn_p.shape[1]
    t = xn_p.shape[0] + xn_s.shape[0]
    tm = MOE_TOKEN_TILE
    n_first = xn_p.shape[0] // tm
    return pl.pallas_call(
        functools.partial(_dispatch_kernel, n_first=n_first),
        grid_spec=pltpu.PrefetchScalarGridSpec(
            num_scalar_prefetch=1,
            grid=(t // tm,),
            in_specs=[pl.BlockSpec((1, 1, tm * TOP_K), lambda i, *_: (i, 0, 0),
                                   memory_space=pltpu.SMEM)] + _pair_specs((tm, w), 0, n_first),
            out_specs=pl.BlockSpec(memory_space=pl.ANY),
            scratch_shapes=[pltpu.VMEM((EXPERT_TILE, w), F32),
                            pltpu.SemaphoreType.DMA, pltpu.SemaphoreType.DMA]),
        out_shape=jax.ShapeDtypeStruct((n_rows, w), F32),
        compiler_params=_cparams(("arbitrary",)),
        name="moe_dispatch",
    )(zfill, slot_tiles, xn_p, xn_s)


def _expert_kernel(te_ref, nt_ref, xs_ref, wg_ref, wu_ref, wd_ref, ys_ref, wg_scr, wu_scr, wd_scr):
    i = pl.program_id(0)

    @pl.when(i < nt_ref[0])
    def _():
        @pl.when((i == 0) | (te_ref[i] != te_ref[jnp.maximum(i - 1, 0)]))
        def _():
            wg_scr[...] = wg_ref[0].astype(BF16)
            wu_scr[...] = wu_ref[0].astype(BF16)
            wd_scr[...] = wd_ref[0].astype(BF16)

        x = xs_ref[...].astype(BF16)
        h = jax.nn.silu(_dot(x, wg_scr[...])) * _dot(x, wu_scr[...])
        ys_ref[...] = _dot(h.astype(BF16), wd_scr[...])

    @pl.when(i >= nt_ref[0])
    def _():
        ys_ref[...] = jnp.zeros_like(ys_ref)


def _experts(xs, tile_expert, n_tiles, w_gate, w_up, w_down):
    n_rows, w = xs.shape
    ne, d, de = w_gate.shape
    tile = lambda i, te, nt: (jnp.minimum(i, nt[0] - 1), 0)
    out_tile = lambda i, te, nt: (i, 0)
    wsel = lambda i, te, nt: (te[i], 0, 0)
    return pl.pallas_call(
        _expert_kernel,
        grid_spec=pltpu.PrefetchScalarGridSpec(
            num_scalar_prefetch=2,
            grid=(n_rows // EXPERT_TILE,),
            in_specs=[pl.BlockSpec((EXPERT_TILE, w), tile),
                      pl.BlockSpec((1, d, de), wsel), pl.BlockSpec((1, d, de), wsel),
                      pl.BlockSpec((1, de, d), wsel)],
            out_specs=pl.BlockSpec((EXPERT_TILE, w), out_tile),
            scratch_shapes=[pltpu.VMEM((d, de), BF16), pltpu.VMEM((d, de), BF16),
                            pltpu.VMEM((de, d), BF16)]),
        out_shape=jax.ShapeDtypeStruct((n_rows, w), F32),
        compiler_params=_cparams(("arbitrary",)),
        name="moe_experts",
    )(tile_expert, n_tiles, xs, w_gate, w_up, w_down)


def _combine_kernel(slot_ref, gate_ref, xnp_ref, xns_ref, x2p_ref, x2s_ref, sg_ref, su_ref, sd_ref,
                    gf_ref, ys_hbm, yp_ref, ysamp_ref, buf, sem, *, n_first):
    tm = xnp_ref.shape[0]
    i = pl.program_id(0)
    first = i < n_first

    def body(r, _):
        for k in range(TOP_K):
            s = slot_ref[0, 0, r * TOP_K + k]
            pltpu.make_async_copy(ys_hbm.at[pl.ds(s, 1)], buf.at[k, pl.ds(r, 1)],
                                  sem).start(priority=k % 2)
        return 0

    lax.fori_loop(0, tm, body, 0)
    x = _pick(first, xnp_ref, xns_ref).astype(BF16)
    hs = jax.nn.silu(_dot(x, sg_ref[...])) * _dot(x, su_ref[...])
    acc = _pick(first, x2p_ref, x2s_ref) + _dot(hs.astype(BF16), sd_ref[...])
    _row_copy_wait(ys_hbm, tm * TOP_K, sem)
    g = gate_ref[...]
    for k in range(TOP_K):
        acc = acc + g[:, k:k + 1] * buf[k]
    y = _rms(acc, gf_ref[...])

    @pl.when(first)
    def _():
        yp_ref[...] = y

    @pl.when(jnp.logical_not(first))
    def _():
        ysamp_ref[...] = y


def _combine(slot_tiles, gates, xn_p, xn_s, x2_p, x2_s, ws_gate, ws_up, ws_down, g_final, ys):
    d = x2_p.shape[1]
    tp, tsamp = xn_p.shape[0], xn_s.shape[0]
    dsh = ws_gate.shape[1]
    tm = MOE_TOKEN_TILE
    n_first = tp // tm
    row = lambda i: (i, 0)
    pair = lambda: _pair_specs((tm, d), 0, n_first)
    return pl.pallas_call(
        functools.partial(_combine_kernel, n_first=n_first),
        grid=((tp + tsamp) // tm,),
        in_specs=[pl.BlockSpec((1, 1, tm * TOP_K), lambda i: (i, 0, 0), memory_space=pltpu.SMEM),
                  pl.BlockSpec((tm, TOP_K), row)] + pair() + pair()
                 + [_const_spec((d, dsh)), _const_spec((d, dsh)), _const_spec((dsh, d)),
                    _const_spec((1, d)), pl.BlockSpec(memory_space=pl.ANY)],
        out_specs=pair(),
        out_shape=[jax.ShapeDtypeStruct((tp, d), F32), jax.ShapeDtypeStruct((tsamp, d), F32)],
        scratch_shapes=[pltpu.VMEM((TOP_K, tm, d), F32), pltpu.SemaphoreType.DMA],
        compiler_params=_cparams(("arbitrary",)),
        name="moe_combine",
    )(slot_tiles, gates, xn_p, xn_s, x2_p, x2_s, ws_gate.astype(BF16), ws_up.astype(BF16),
      ws_down.astype(BF16), g_final.reshape(1, d), ys)


def _moe(xn_p, xn_s, x2_p, x2_s, eidx, gate, posk, counts, w_gate, w_up, w_down, ws_gate, ws_up,
         ws_down, g_final):
    t = xn_p.shape[0] + xn_s.shape[0]
    ne = w_gate.shape[0]
    n_tiles_max = t * TOP_K // EXPERT_TILE + ne
    slot, tile_expert, n_tiles, zfill = _moe_plan(eidx, posk, counts[:, 0], n_tiles_max)
    slot_tiles = jnp.transpose(slot).reshape(t // MOE_TOKEN_TILE, 1, MOE_TOKEN_TILE * TOP_K)
    xs = _dispatch(xn_p, xn_s, slot_tiles, zfill, n_tiles_max * EXPERT_TILE)
    ys = _experts(xs, tile_expert, n_tiles, w_gate, w_up, w_down)
    return _combine(slot_tiles, jnp.transpose(gate), xn_p, xn_s, x2_p, x2_s, ws_gate, ws_up,
                    ws_down, g_final, ys)


def kernel(x_prompt, x_sample, cache_mem_k, cache_mem_v, state_s5_re, state_s5_im, state_gla, mem_prompt, w_in, w_a2, b_a2, lam_re, lam_im, log_dt, b_re, b_im, c_re, c_im, d_skip, w_glu, b_glu, g_s5_out, g_gla_head, w_out, g_mix, g_xattn, g_mem, w_q, w_k, w_v, w_o, g_ffn, w_router, router_bias, w_gate, w_up, w_down, ws_gate, ws_up, ws_down, g_final):
    depth = w_in.shape[0]
    assert depth == 1, "one trunk layer"
    bp, lp, d = x_prompt.shape
    bs, ls, _ = x_sample.shape
    assert bp == 1 and ls == CHUNK and lp % CHUNK == 0
    n_mem = mem_prompt.shape[1]
    g, p = lam_re.shape[1:]
    d_s5 = g * S5_GROUP
    d_qk = GLA_HEADS * GLA_DK
    d_gla = w_out.shape[1] - d_s5
    ns = d_s5 // LANES
    ne = w_router.shape[2]
    tp, tsamp = bp * lp, bs * ls
    xp = x_prompt.reshape(tp, d)
    xs = x_sample.reshape(tsamp, d)
    sq = lambda a: a.reshape(a.shape[1:])

    mk_p, mv_p = _memory_kv(mem_prompt.reshape(n_mem, d), sq(g_mem), sq(w_k).astype(BF16),
                            sq(w_v).astype(BF16))

    u, q, k, v, a, r = _in_proj(xp, xs, sq(g_mix), sq(w_in), d_s5, d_qk, d_gla)

    ksm, psm, qsm, step_m, carry_m, a8 = _s5_coeffs(
        sq(lam_re), sq(lam_im), sq(log_dt), sq(b_re), sq(b_im), sq(c_re), sq(c_im), sq(d_skip))
    tables = tuple(_s5_expand(ksm, psm, qsm)) + (step_m, carry_m, a8)
    zero_h = jnp.zeros((ns, bp, 2 * GROUPS_PER_SLAB * p), F32)
    rows_p = lp // S5_BLOCK
    y_p, h_p = _s5_mixer(u, tables, zero_h, 0, tp, rows_per_seq=rows_p,
                         tile_rows=math.gcd(rows_p, 512))
    h0_s = _pack_s5_state(sq(state_s5_re), sq(state_s5_im), ns)
    y_s, h_s = _s5_mixer(u, tables, h0_s, tp, tsamp, rows_per_seq=ls // S5_BLOCK,
                         tile_rows=tsamp // S5_BLOCK)

    wa2 = jnp.pad(sq(w_a2), ((0, LANES - GLA_RANK), (0, 0)))
    zero_s = jnp.zeros((bp,) + state_gla.shape[2:], F32)
    gla_p, sg_p = _gla_mixer(q, k, v, a, r, wa2, sq(b_a2), sq(g_gla_head), zero_s, 0, tp,
                             carry=True, n_chunks=4)
    gla_s, sg_s = _gla_mixer(q, k, v, a, r, wa2, sq(b_a2), sq(g_gla_head), sq(state_gla), tp,
                             tsamp, carry=False, n_chunks=4)

    x1, xq = _mix_out(y_p, y_s, gla_p, gla_s, xp, xs, sq(w_glu), sq(b_glu), sq(g_s5_out),
                      sq(w_out), sq(g_xattn), sq(w_q))

    wr_hi, wr_lo = _split_bf16(jnp.transpose(sq(w_router)))
    wo = sq(w_o).astype(BF16)
    x2_p, xn_p, lg_p = _xattn(xq, x1, mk_p[None], mv_p[None], wo, sq(g_ffn), wr_hi, wr_lo,
                              0, tp, nb=1, lt=TOKEN_TILE)
    x2_s, xn_s, lg_s = _xattn(xq, x1, cache_mem_k.reshape(bs, n_mem, d),
                              cache_mem_v.reshape(bs, n_mem, d), wo, sq(g_ffn), wr_hi, wr_lo,
                              tp, tsamp, nb=2, lt=ls)

    eidx, gate, posk, counts = _route(lg_p, lg_s, sq(router_bias))
    y_p2, y_s2 = _moe(xn_p, xn_s, x2_p, x2_s, eidx, gate, posk, counts, sq(w_gate), sq(w_up),
                      sq(w_down), sq(ws_gate), sq(ws_up), sq(ws_down), g_final)

    xh = d // X_HEADS
    re_p, im_p = _unpack_s5_state(h_p, g, p)
    re_s, im_s = _unpack_s5_state(h_s, g, p)
    return (y_p2.reshape(bp, lp, d), y_s2.reshape(bs, ls, d),
            mk_p.reshape(1, bp, n_mem, X_HEADS, xh), mv_p.reshape(1, bp, n_mem, X_HEADS, xh),
            re_p[None], im_p[None], sg_p[None], re_s[None], im_s[None], sg_s[None])
```

```python
import functools
import math

import jax
import jax.numpy as jnp
from jax import lax
from jax.experimental import pallas as pl
from jax.experimental.pallas import tpu as pltpu

F32 = jnp.float32
BF16 = jnp.bfloat16

EPS = 1e-6
CHUNK = 64
S5_GROUP = 16
GLA_HEADS = 4
GLA_DK = 128
GLA_RANK = 16
GLA_TAU = 16.0
X_HEADS = 4
N_GROUPS = 8
TOPK_GROUPS = 4
TOP_K = 8
ROUTE_SCALE = 2.5

LANES = 128
SUBLANES = 8
S5_BLOCK = SUBLANES
GROUPS_PER_SLAB = LANES // S5_GROUP
TOKEN_TILE = 512
EXPERT_TILE = 256
MOE_TOKEN_TILE = 256
VMEM_LIMIT = 56 * 1024 * 1024


def _cparams(sem):
    return pltpu.CompilerParams(dimension_semantics=sem, vmem_limit_bytes=VMEM_LIMIT)


def _const_spec(shape):
    nd = len(shape)
    return pl.BlockSpec(shape, lambda *_: (0,) * nd, pipeline_mode=pl.Buffered(1))


def _pair_specs(block, axis, n_first):
    def at(f):
        return lambda i, *_: tuple(f(i) if a == axis else 0 for a in range(len(block)))
    return [pl.BlockSpec(block, at(lambda i: jnp.minimum(i, n_first - 1))),
            pl.BlockSpec(block, at(lambda i: jnp.maximum(i - n_first, 0)))]


def _pick(first, a_ref, b_ref):
    return jnp.where(first, a_ref[...], b_ref[...])


def _rms(x, g):
    return x * lax.rsqrt(jnp.mean(x * x, axis=-1, keepdims=True) + EPS) * g


def _split_bf16(x):
    hi = x.astype(BF16)
    lo = (x - hi.astype(F32)).astype(BF16)
    return hi, lo


def _split3_bf16(x):
    a = x.astype(BF16)
    r = x - a.astype(F32)
    b = r.astype(BF16)
    c = (r - b.astype(F32)).astype(BF16)
    return a, b, c


def _dot(a, b):
    return jnp.dot(a, b, preferred_element_type=F32)


def _dot_nt(a, b):
    return lax.dot_general(a, b, (((1,), (1,)), ((), ())), preferred_element_type=F32)


def _dot_tn(a, b):
    return lax.dot_general(a, b, (((0,), (0,)), ((), ())), preferred_element_type=F32)


def _memkv_kernel(mem_ref, g_ref, wk_ref, wv_ref, mk_ref, mv_ref):
    m = _rms(mem_ref[...], g_ref[...]).astype(BF16)
    mk_ref[...] = _dot(m, wk_ref[...])
    mv_ref[...] = _dot(m, wv_ref[...])


def _memory_kv(mem, g_mem, wk, wv):
    n, d = mem.shape
    tn = 512
    return pl.pallas_call(
        _memkv_kernel,
        grid=(d // tn,),
        in_specs=[_const_spec((n, d)), _const_spec((1, d)),
                  pl.BlockSpec((d, tn), lambda j: (0, j)),
                  pl.BlockSpec((d, tn), lambda j: (0, j))],
        out_specs=[pl.BlockSpec((n, tn), lambda j: (0, j))] * 2,
        out_shape=[jax.ShapeDtypeStruct((n, d), F32)] * 2,
        compiler_params=_cparams(("arbitrary",)),
        name="memory_kv",
    )(mem, g_mem.reshape(1, d), wk, wv)


def _inproj_kernel(xp_ref, xs_ref, g_ref, wuh_ref, wul_ref, wqkv_ref, wa_ref, wr_ref,
                   u_ref, q_ref, k_ref, v_ref, a_ref, r_ref, *, n_first):
    first = pl.program_id(0) < n_first
    x = _pick(first, xp_ref, xs_ref)
    nx = _rms(x, g_ref[...])
    hi, lo = _split_bf16(nx)
    u = _dot(hi, wuh_ref[...])
    for j in range(u_ref.shape[0]):
        u_ref[j] = u[:, j * LANES:(j + 1) * LANES]

    @pl.when(jnp.logical_not(first))
    def _():
        fix = _dot(lo, wuh_ref[...]) + _dot(hi, wul_ref[...])
        for j in range(u_ref.shape[0]):
            u_ref[j] += fix[:, j * LANES:(j + 1) * LANES]

    qkv = _dot(hi, wqkv_ref[...])
    dq = q_ref.shape[1]
    q_ref[...] = qkv[:, :dq].astype(BF16)
    k_ref[...] = qkv[:, dq:2 * dq].astype(BF16)
    v_ref[...] = qkv[:, 2 * dq:].astype(BF16)
    a_ref[...] = _dot(hi, wa_ref[...]) + _dot(lo, wa_ref[...])
    r_ref[...] = _dot(hi, wr_ref[...]).astype(BF16)


def _in_proj(xp, xs, g_mix, w_in, d_s5, d_qk, d_gla):
    d = xp.shape[1]
    t = xp.shape[0] + xs.shape[0]
    tm = TOKEN_TILE
    n_first = xp.shape[0] // tm
    o1 = d_s5
    o2 = o1 + 2 * d_qk + d_gla
    o3 = o2 + GLA_RANK
    wu_hi, wu_lo = _split_bf16(w_in[:, :o1])
    wqkv = w_in[:, o1:o2].astype(BF16)
    wa = jnp.pad(w_in[:, o2:o3], ((0, 0), (0, LANES - GLA_RANK))).astype(BF16)
    wr = w_in[:, o3:].astype(BF16)
    n_slab = d_s5 // LANES
    row = lambda i: (i, 0)
    return pl.pallas_call(
        functools.partial(_inproj_kernel, n_first=n_first),
        grid=(t // tm,),
        in_specs=_pair_specs((tm, d), 0, n_first) + [
            _const_spec((1, d)), _const_spec(wu_hi.shape), _const_spec(wu_lo.shape),
            _const_spec(wqkv.shape), _const_spec(wa.shape), _const_spec(wr.shape)],
        out_specs=[pl.BlockSpec((n_slab, tm, LANES), lambda i: (0, i, 0)),
                   pl.BlockSpec((tm, d_qk), row), pl.BlockSpec((tm, d_qk), row),
                   pl.BlockSpec((tm, d_gla), row), pl.BlockSpec((tm, LANES), row),
                   pl.BlockSpec((tm, d_gla), row)],
        out_shape=[jax.ShapeDtypeStruct((n_slab, t, LANES), F32),
                   jax.ShapeDtypeStruct((t, d_qk), BF16), jax.ShapeDtypeStruct((t, d_qk), BF16),
                   jax.ShapeDtypeStruct((t, d_gla), BF16), jax.ShapeDtypeStruct((t, LANES), F32),
                   jax.ShapeDtypeStruct((t, d_gla), BF16)],
        compiler_params=_cparams(("arbitrary",)),
        name="in_proj",
    )(xp, xs, g_mix.reshape(1, d), wu_hi, wu_lo, wqkv, wa, wr)


def _s5_coeffs(lam_re, lam_im, log_dt, b_re, b_im, c_re, c_im, d_skip):
    g, p = lam_re.shape
    h = b_re.shape[-1]
    nb = S5_BLOCK
    ns = g // GROUPS_PER_SLAB
    gl = GROUPS_PER_SLAB
    dt = jnp.exp(log_dt.astype(F32))[:, None]
    lr, li = lam_re.astype(F32), lam_im.astype(F32)
    mag = jnp.exp(lr * dt)
    ar, ai = mag * jnp.cos(li * dt), mag * jnp.sin(li * dt)
    den = lr * lr + li * li
    nr = ar - 1.0
    cf_r = (nr * lr + ai * li) / den
    cf_i = (ai * lr - nr * li) / den
    bb_r = cf_r[..., None] * b_re - cf_i[..., None] * b_im
    bb_i = cf_r[..., None] * b_im + cf_i[..., None] * b_re

    def cpow(n):
        e = n[:, None, None] * dt[None]
        m = jnp.exp(lr[None] * e)
        return m * jnp.cos(li[None] * e), m * jnp.sin(li[None] * e)

    pr, pi = cpow(jnp.arange(nb + 1, dtype=F32))

    cb_r = jnp.einsum('gcp,gph->gpch', c_re, bb_r) - jnp.einsum('gcp,gph->gpch', c_im, bb_i)
    cb_i = jnp.einsum('gcp,gph->gpch', c_re, bb_i) + jnp.einsum('gcp,gph->gpch', c_im, bb_r)
    taps = (jnp.einsum('ngp,gpch->ngch', pr[:nb], cb_r)
            - jnp.einsum('ngp,gpch->ngch', pi[:nb], cb_i))
    taps = taps.at[0].add(d_skip[:, :, None] * jnp.eye(h, dtype=F32)[None])
    ksm = jnp.transpose(taps.reshape(nb, ns, gl, h, h), (1, 0, 2, 4, 3))
    ksm = ksm.reshape(ns, nb, LANES, h)

    wr, wi = pr[nb - 1::-1][:nb], pi[nb - 1::-1][:nb]
    inj_r = wr[..., None] * bb_r[None] - wi[..., None] * bb_i[None]
    inj_i = wr[..., None] * bb_i[None] + wi[..., None] * bb_r[None]
    inj = jnp.stack([inj_r, inj_i], axis=0).reshape(2, nb, ns, gl, p, h)
    psm = jnp.transpose(inj, (2, 1, 3, 5, 0, 4)).reshape(ns, nb, LANES, 2 * p)

    er, ei = pr[1:nb + 1], pi[1:nb + 1]
    q_r = c_re[None] * er[:, :, None, :] - c_im[None] * ei[:, :, None, :]
    q_i = -(c_re[None] * ei[:, :, None, :] + c_im[None] * er[:, :, None, :])
    qq = jnp.stack([q_r, q_i], axis=0).reshape(2, nb, ns, gl, h, p)
    qsm = jnp.transpose(qq, (2, 0, 3, 5, 1, 4)).reshape(ns, 2 * gl * p, nb * h)

    def slab(xr, xi):
        k = xr.shape[0]
        xr = jnp.transpose(xr.reshape(k, ns, gl * p), (1, 0, 2))
        xi = jnp.transpose(xi.reshape(k, ns, gl * p), (1, 0, 2))
        return jnp.concatenate([xr, xi], axis=-1)

    rows = jnp.arange(SUBLANES, dtype=F32)
    cr, ci = cpow(nb * rows)
    carry_m = slab(cr, ci)
    sr, si = cpow(nb * jnp.array([1.0, 2.0, 4.0], F32))
    keep = (rows[None, :] >= jnp.array([1.0, 2.0, 4.0], F32)[:, None]).astype(F32)
    step_m = slab((sr[:, None] * keep[:, :, None, None]).reshape(3 * SUBLANES, g, p),
                  (si[:, None] * keep[:, :, None, None]).reshape(3 * SUBLANES, g, p))
    step_m = step_m.reshape(ns, 3, SUBLANES, 2 * gl * p)
    a8 = carry_m[:, 1:2]
    return ksm, psm, qsm, step_m, carry_m, a8


def _s5_expand_kernel(ksm_ref, psm_ref, qsm_ref, tm_ref, ph_ref, pl_ref, qm_ref):
    nb = S5_BLOCK
    h = S5_GROUP
    w = ph_ref.shape[2]
    p = w // (2 * GROUPS_PER_SLAB)

    def spread(x, sel, keep):
        a, b, c = _split3_bf16(x)
        return jnp.where(keep, _dot(a, sel) + _dot(b, sel) + _dot(c, sel), 0.0)

    def iota(shape, axis):
        return lax.broadcasted_iota(jnp.int32, shape, axis)

    sel_c = (iota((h, LANES), 1) % h == iota((h, LANES), 0)).astype(BF16)
    keep_t = iota((LANES, LANES), 0) // h == iota((LANES, LANES), 1) // h
    tm_ref[...] = jnp.zeros_like(tm_ref)
    for tau in range(nb):
        blk = spread(ksm_ref[0, tau], sel_c, keep_t).astype(BF16)
        for s in range(nb - tau):
            t = s + tau
            tm_ref[0, s * LANES:(s + 1) * LANES, t * LANES:(t + 1) * LANES] = blk

    half = w // 2
    keep_p = iota((LANES, half), 0) // h == iota((LANES, half), 1) // p
    for ri in range(2):
        sel_p = (iota((2 * p, half), 0) == iota((2 * p, half), 1) % p + ri * p).astype(BF16)
        for s in range(nb):
            blk = spread(psm_ref[0, s], sel_p, keep_p)
            hi, lo = _split_bf16(blk)
            ph_ref[0, s * LANES:(s + 1) * LANES, ri * half:(ri + 1) * half] = hi
            pl_ref[0, s * LANES:(s + 1) * LANES, ri * half:(ri + 1) * half] = lo

    keep_q = (iota((w, LANES), 0) % half) // p == iota((w, LANES), 1) // h
    for t in range(nb):
        sel_q = (iota((nb * h, LANES), 0) == iota((nb * h, LANES), 1) % h + t * h).astype(BF16)
        qm_ref[0, :, t * LANES:(t + 1) * LANES] = spread(qsm_ref[0], sel_q, keep_q).astype(BF16)


def _s5_expand(ksm, psm, qsm):
    ns = ksm.shape[0]
    w = qsm.shape[1]
    k = S5_BLOCK * LANES
    blk = lambda shape: pl.BlockSpec((1,) + shape, lambda j: (j,) + (0,) * len(shape))
    return pl.pallas_call(
        _s5_expand_kernel,
        grid=(ns,),
        in_specs=[blk(ksm.shape[1:]), blk(psm.shape[1:]), blk(qsm.shape[1:])],
        out_specs=[blk((k, k)), blk((k, w)), blk((k, w)), blk((w, k))],
        out_shape=[jax.ShapeDtypeStruct((ns, k, k), BF16), jax.ShapeDtypeStruct((ns, k, w), BF16),
                   jax.ShapeDtypeStruct((ns, k, w), BF16), jax.ShapeDtypeStruct((ns, w, k), BF16)],
        compiler_params=_cparams(("arbitrary",)),
        name="s5_tables",
    )(ksm, psm, qsm)


def _cmul(xr, xi, mr, mi):
    return xr * mr - xi * mi, xr * mi + xi * mr


def _s5_kernel(u_ref, ph_ref, pl_ref, tm_ref, qm_ref, step_ref, carry_ref, a8_ref, h0_ref,
               y_ref, hout_ref, s_scr, hin_scr, c_scr, *, rows_per_seq):
    ti = pl.program_id(1)
    rows = s_scr.shape[0]
    half = s_scr.shape[1] // 2
    nb = S5_BLOCK
    u8 = jnp.concatenate([u_ref[0, pl.ds(s, rows, stride=nb), :] for s in range(nb)], axis=1)
    hi, lo = _split_bf16(u8)
    per_block_seq = rows_per_seq == SUBLANES
    if per_block_seq:
        s_scr[...] = _dot(hi, ph_ref[0]) + _dot(lo, ph_ref[0]) + _dot(hi, pl_ref[0])
    else:
        s_scr[...] = _dot(hi, ph_ref[0])

    if not per_block_seq:
        @pl.when(ti == 0)
        def _():
            c_scr[...] = h0_ref[0]

    not_first = (lax.broadcasted_iota(jnp.int32, (SUBLANES, 1), 0) >= 1).astype(F32)
    a8r, a8i = a8_ref[0, :, :half], a8_ref[0, :, half:]
    cmr, cmi = carry_ref[0, :, :half], carry_ref[0, :, half:]

    def body(rb, _):
        r0 = pl.multiple_of(rb * SUBLANES, SUBLANES)
        sb = s_scr[pl.ds(r0, SUBLANES), :]
        x = pltpu.roll(sb, 1, 0) * not_first
        xr, xi = x[:, :half], x[:, half:]
        for k, sh in enumerate((1, 2, 4)):
            m = step_ref[0, k]
            pr, pi = _cmul(pltpu.roll(xr, sh, 0), pltpu.roll(xi, sh, 0), m[:, :half], m[:, half:])
            xr, xi = xr + pr, xi + pi
        c = h0_ref[0, pl.ds(rb, 1), :] if per_block_seq else c_scr[...]
        cr, ci = c[:, :half], c[:, half:]
        pr, pi = _cmul(cr, ci, cmr, cmi)
        hr, hi_ = xr + pr, xi + pi
        hin_scr[pl.ds(r0, SUBLANES), :half] = hr
        hin_scr[pl.ds(r0, SUBLANES), half:] = hi_
        nr, ni = _cmul(hr[SUBLANES - 1:], hi_[SUBLANES - 1:], a8r, a8i)
        cn = jnp.concatenate([nr, ni], axis=1) + sb[SUBLANES - 1:]
        if per_block_seq:
            hout_ref[0, pl.ds(rb, 1), :] = cn
        else:
            c_scr[...] = cn
        return 0

    lax.fori_loop(0, rows // SUBLANES, body, 0)
    if not per_block_seq:
        hout_ref[0] = c_scr[...]

    y8 = _dot(hi, tm_ref[0]) + _dot(hin_scr[...].astype(BF16), qm_ref[0])
    for t in range(nb):
        y_ref[0, pl.ds(t, rows, stride=nb), :] = y8[:, t * LANES:(t + 1) * LANES]


def _s5_mixer(u_slabs, tables, h0, t_start, t, rows_per_seq, tile_rows):
    tm, pm_hi, pm_lo, qm, step_m, carry_m, a8 = tables
    ns = u_slabs.shape[0]
    rows = t // S5_BLOCK
    n_tiles = rows // tile_rows
    assert rows % tile_rows == 0 and t_start % (tile_rows * S5_BLOCK) == 0
    off = t_start // (tile_rows * S5_BLOCK)
    n_seq = h0.shape[1]
    w = pm_hi.shape[-1]
    slab3 = lambda j, i: (j, 0, 0)
    kern = functools.partial(_s5_kernel, rows_per_seq=rows_per_seq)
    return pl.pallas_call(
        kern,
        grid=(ns, n_tiles),
        in_specs=[pl.BlockSpec((1, tile_rows * S5_BLOCK, LANES), lambda j, i: (j, i + off, 0)),
                  pl.BlockSpec((1,) + pm_hi.shape[1:], slab3),
                  pl.BlockSpec((1,) + pm_lo.shape[1:], slab3),
                  pl.BlockSpec((1,) + tm.shape[1:], slab3),
                  pl.BlockSpec((1,) + qm.shape[1:], slab3),
                  pl.BlockSpec((1,) + step_m.shape[1:], lambda j, i: (j, 0, 0, 0)),
                  pl.BlockSpec((1,) + carry_m.shape[1:], slab3),
                  pl.BlockSpec((1,) + a8.shape[1:], slab3),
                  pl.BlockSpec((1, n_seq, w), slab3)],
        out_specs=[pl.BlockSpec((1, tile_rows * S5_BLOCK, LANES), lambda j, i: (j, i, 0)),
                   pl.BlockSpec((1, n_seq, w), slab3)],
        out_shape=[jax.ShapeDtypeStruct((ns, t, LANES), F32),
                   jax.ShapeDtypeStruct((ns, n_seq, w), F32)],
        scratch_shapes=[pltpu.VMEM((tile_rows, w), F32), pltpu.VMEM((tile_rows, w), F32),
                        pltpu.VMEM((1, w), F32)],
        compiler_params=_cparams(("arbitrary", "arbitrary")),
        name="s5_mixer",
    )(u_slabs, pm_hi, pm_lo, tm, qm, step_m, carry_m, a8, h0)


def _pack_s5_state(re, im, ns):
    b = re.shape[0]
    r = re.astype(F32).reshape(b, ns, -1)
    i = im.astype(F32).reshape(b, ns, -1)
    return jnp.transpose(jnp.concatenate([r, i], axis=-1), (1, 0, 2))


def _unpack_s5_state(hc, g, p):
    ns, b, w = hc.shape
    hc = jnp.transpose(hc, (1, 0, 2))
    re = hc[:, :, :w // 2].reshape(b, g, p)
    im = hc[:, :, w // 2:].reshape(b, g, p)
    return re, im


def _gla_kernel(q_ref, k_ref, v_ref, a_ref, r_ref, wa2_ref, ba2_ref, gh_ref, s0_ref,
                o_ref, sout_ref, st_scr, *, carry, n_chunks):
    step = pl.program_id(0)
    dk = GLA_DK
    dv = v_ref.shape[1] // GLA_HEADS
    c = CHUNK
    scale = dk ** -0.5
    ri = lax.broadcasted_iota(jnp.int32, (c, c), 0)
    ci = lax.broadcasted_iota(jnp.int32, (c, c), 1)
    causal = ri >= ci
    tril = causal.astype(BF16)
    eye_dk = (lax.broadcasted_iota(jnp.int32, (dk, dk), 0)
              == lax.broadcasted_iota(jnp.int32, (dk, dk), 1))

    if carry:
        @pl.when(step == 0)
        def _():
            st_scr[...] = s0_ref[0]

    wa_hi, wa_lo = _split_bf16(wa2_ref[...])
    for n in range(n_chunks):
        rows = slice(n * c, (n + 1) * c)
        a_hi, a_lo = _split_bf16(a_ref[rows, :])
        logit = _dot(a_hi, wa_hi) + _dot(a_lo, wa_hi) + _dot(a_hi, wa_lo) + ba2_ref[...]
        g = jax.nn.log_sigmoid(logit) * (1.0 / GLA_TAU)
        g1, g2, g3 = _split3_bf16(g)
        bcum = _dot(tril, g1) + _dot(tril, g2) + _dot(tril, g3)
        for h in range(GLA_HEADS):
            ks = slice(h * dk, (h + 1) * dk)
            vs = slice(h * dv, (h + 1) * dv)
            b = bcum[:, ks]
            qh = q_ref[rows, ks].astype(F32) * scale
            kh = k_ref[rows, ks].astype(F32)
            vh = v_ref[rows, vs]
            state = s0_ref[n, h] if not carry else st_scr[h]
            qe = (qh * jnp.exp(b)).astype(BF16)
            ke = (kh * jnp.exp(-b)).astype(BF16)
            att = jnp.where(causal, _dot_nt(qe, ke), 0.0)
            o = _dot(att.astype(BF16), vh) + _dot(qe, state.astype(BF16))
            blast = b[c - 1:c, :]
            kd = (kh * jnp.exp(blast - b)).astype(BF16)
            decay = jnp.sum(jnp.where(eye_dk, jnp.exp(blast), 0.0), axis=1, keepdims=True)
            new_state = decay * state + _dot_tn(kd, vh)
            if carry:
                st_scr[h] = new_state
            else:
                sout_ref[n, h] = new_state
            on = _rms(o, gh_ref[...])
            rr = r_ref[rows, vs].astype(F32)
            o_ref[rows, vs] = (on * (rr * jax.nn.sigmoid(rr))).astype(BF16)
    if carry:
        sout_ref[0] = st_scr[...]


def _gla_mixer(q, k, v, a, r, wa2, ba2, g_head, s0, t_start, t, carry, n_chunks):
    dqk = q.shape[1]
    dvt = v.shape[1]
    hh, dk, dv = s0.shape[1:]
    rows = n_chunks * CHUNK
    assert t % rows == 0 and t_start % rows == 0
    off = t_start // rows
    row = lambda i: (i, 0)
    row_in = lambda i: (i + off, 0)
    if carry:
        sblk, smap = (1, hh, dk, dv), (lambda i: (0, 0, 0, 0))
    else:
        sblk, smap = (n_chunks, hh, dk, dv), (lambda i: (i, 0, 0, 0))
    kern = functools.partial(_gla_kernel, carry=carry, n_chunks=n_chunks)
    return pl.pallas_call(
        kern,
        grid=(t // rows,),
        in_specs=[pl.BlockSpec((rows, dqk), row_in), pl.BlockSpec((rows, dqk), row_in),
                  pl.BlockSpec((rows, dvt), row_in), pl.BlockSpec((rows, LANES), row_in),
                  pl.BlockSpec((rows, dvt), row_in),
                  _const_spec(wa2.shape), _const_spec((1, dqk)), _const_spec((1, dv)),
                  pl.BlockSpec(sblk, smap)],
        out_specs=[pl.BlockSpec((rows, dvt), row), pl.BlockSpec(sblk, smap)],
        out_shape=[jax.ShapeDtypeStruct((t, dvt), BF16), jax.ShapeDtypeStruct(s0.shape, F32)],
        scratch_shapes=[pltpu.VMEM((hh, dk, dv), F32)],
        compiler_params=_cparams(("arbitrary",)),
        name="gla_mixer",
    )(q, k, v, a, r, wa2, ba2.reshape(1, dqk), g_head.reshape(1, dv), s0)


def _mixout_kernel(yp_ref, ys_ref, gp_ref, gs_ref, xp_ref, xs_ref, wglu_ref, bglu_ref, gs5_ref,
                   wout_ref, gx_ref, wq_ref, x1_ref, q_ref, *, n_first):
    first = pl.program_id(0) < n_first
    yb = _pick(first, yp_ref, ys_ref)
    y = jnp.concatenate([yb[j] for j in range(yb.shape[0])], axis=1)
    z = jax.nn.gelu(y)
    gate = jax.nn.sigmoid(_dot(z.astype(BF16), wglu_ref[...]) + bglu_ref[...])
    s5 = _rms(z * gate, gs5_ref[...])
    cat = jnp.concatenate([s5.astype(BF16), _pick(first, gp_ref, gs_ref)], axis=1)
    x1 = _pick(first, xp_ref, xs_ref) + _dot(cat, wout_ref[...])
    x1_ref[...] = x1
    q_ref[...] = _dot(_rms(x1, gx_ref[...]).astype(BF16), wq_ref[...]).astype(BF16)


def _mix_out(y_p, y_s, gla_p, gla_s, xp, xs, w_glu, b_glu, g_s5, w_out, g_x, w_q):
    d = xp.shape[1]
    t = xp.shape[0] + xs.shape[0]
    ns = y_p.shape[0]
    ds5 = ns * LANES
    dg = gla_p.shape[1]
    tm = TOKEN_TILE // 2
    n_first = xp.shape[0] // tm
    row = lambda i: (i, 0)
    return pl.pallas_call(
        functools.partial(_mixout_kernel, n_first=n_first),
        grid=(t // tm,),
        in_specs=(_pair_specs((ns, tm, LANES), 1, n_first) + _pair_specs((tm, dg), 0, n_first)
                  + _pair_specs((tm, d), 0, n_first)
                  + [_const_spec((ds5, ds5)), _const_spec((1, ds5)), _const_spec((1, ds5)),
                     _const_spec((ds5 + dg, d)), _const_spec((1, d)), _const_spec((d, d))]),
        out_specs=[pl.BlockSpec((tm, d), row), pl.BlockSpec((tm, d), row)],
        out_shape=[jax.ShapeDtypeStruct((t, d), F32), jax.ShapeDtypeStruct((t, d), BF16)],
        compiler_params=_cparams(("arbitrary",)),
        name="mix_out",
    )(y_p, y_s, gla_p, gla_s, xp, xs, w_glu.astype(BF16), b_glu.reshape(1, ds5),
      g_s5.reshape(1, ds5), w_out.astype(BF16), g_x.reshape(1, d), w_q.astype(BF16))


def _xattn_kernel(q_ref, mk_ref, mv_ref, x1_ref, wo_ref, gf_ref, wrh_ref, wrl_ref,
                  x2_ref, xn_ref, lg_ref):
    nb = mk_ref.shape[0]
    d = q_ref.shape[1]
    lt = q_ref.shape[0] // nb
    hd = d // X_HEADS
    scale = hd ** -0.5
    for b in range(nb):
        rows = slice(b * lt, (b + 1) * lt)
        mk = mk_ref[b].astype(BF16)
        mv = mv_ref[b].astype(BF16)
        outs = []
        for h in range(X_HEADS):
            hs = slice(h * hd, (h + 1) * hd)
            s = _dot_nt(q_ref[rows, hs], mk[:, hs]) * scale
            p = jnp.exp(s - jnp.max(s, axis=-1, keepdims=True))
            denom = jnp.sum(p, axis=-1, keepdims=True)
            outs.append((_dot(p.astype(BF16), mv[:, hs]) / denom).astype(BF16))
        o = jnp.concatenate(outs, axis=1)
        x2 = x1_ref[rows, :] + _dot(o, wo_ref[...])
        x2_ref[rows, :] = x2
        xn = _rms(x2, gf_ref[...])
        xn_ref[rows, :] = xn
        hi, lo = _split_bf16(xn)
        lg_ref[:, rows] = (_dot_nt(wrh_ref[...], hi) + _dot_nt(wrh_ref[...], lo)
                           + _dot_nt(wrl_ref[...], hi))


def _xattn(q, x1, mk, mv, w_o, g_ffn, wr_hi, wr_lo, t_start, t, nb, lt):
    d = q.shape[1]
    nm = mk.shape[1]
    ne = wr_hi.shape[0]
    rows = nb * lt
    assert t % rows == 0 and t_start % rows == 0
    off = t_start // rows
    shared = mk.shape[0] == 1
    assert nb == 1 if shared else mk.shape[0] * lt == t
    row = lambda i: (i, 0)
    row_in = lambda i: (i + off, 0)
    mblk = (lambda i: (0, 0, 0)) if shared else (lambda i: (i, 0, 0))
    return pl.pallas_call(
        _xattn_kernel,
        grid=(t // rows,),
        in_specs=[pl.BlockSpec((rows, d), row_in),
                  pl.BlockSpec((nb, nm, d), mblk), pl.BlockSpec((nb, nm, d), mblk),
                  pl.BlockSpec((rows, d), row_in),
                  _const_spec((d, d)), _const_spec((1, d)),
                  _const_spec((ne, d)), _const_spec((ne, d))],
        out_specs=[pl.BlockSpec((rows, d), row), pl.BlockSpec((rows, d), row),
                   pl.BlockSpec((ne, rows), lambda i: (0, i))],
        out_shape=[jax.ShapeDtypeStruct((t, d), F32), jax.ShapeDtypeStruct((t, d), F32),
                   jax.ShapeDtypeStruct((ne, t), F32)],
        compiler_params=_cparams(("arbitrary",)),
        name="mem_xattn",
    )(q, mk, mv, x1, w_o, g_ffn.reshape(1, d), wr_hi, wr_lo)


def _route_kernel(lgp_ref, lgs_ref, xnp_ref, xns_ref, bias_ref, eidx_ref, gate_ref, pos_ref, cnt_ref,
                  xn_ref, run_scr, *, n_first):
    ne, tn = lgp_ref.shape
    first = pl.program_id(0) < n_first
    xn_ref[...] = _pick(first, xnp_ref, xns_ref)

    @pl.when(pl.program_id(0) == 0)
    def _():
        run_scr[...] = jnp.zeros_like(run_scr)

    gsz = ne // N_GROUPS
    sc = jax.nn.sigmoid(_pick(first, lgp_ref, lgs_ref))
    sel = sc + bias_ref[...]
    s3 = sel.reshape(N_GROUPS, gsz, tn)
    ie = lax.broadcasted_iota(jnp.int32, s3.shape, 1).astype(F32)
    m1 = jnp.max(s3, axis=1, keepdims=True)
    first = jnp.min(jnp.where(s3 == m1, ie, float(gsz)), axis=1, keepdims=True)
    m2 = jnp.max(jnp.where(ie == first, -jnp.inf, s3), axis=1, keepdims=True)
    gs = m1 + m2
    ig = lax.broadcasted_iota(jnp.int32, gs.shape, 0)
    grank = jnp.zeros(gs.shape, jnp.int32)
    for g in range(N_GROUPS):
        other = gs[g:g + 1]
        ahead = jnp.where(other > gs, 1, jnp.where(other == gs, (ig > g).astype(jnp.int32), 0))
        grank = grank + ahead
    gkeep = jnp.broadcast_to(grank < TOPK_GROUPS, s3.shape)
    v = jnp.where(gkeep, s3, -jnp.inf).reshape(ne, tn)
    iv = lax.broadcasted_iota(jnp.int32, v.shape, 0)
    rank = jnp.zeros(v.shape, jnp.int32)
    for e in range(ne):
        other = v[e:e + 1]
        ahead = jnp.where(other > v, 1, jnp.where(other == v, (iv > e).astype(jnp.int32), 0))
        rank = rank + ahead
    chosen = rank < TOP_K
    gate = jnp.where(chosen, sc, 0.0)
    gate = gate / jnp.sum(gate, axis=0, keepdims=True) * ROUTE_SCALE
    ti = lax.broadcasted_iota(jnp.int32, (tn, tn), 0)
    tj = lax.broadcasted_iota(jnp.int32, (tn, tn), 1)
    incl = _dot(jnp.where(chosen, 1.0, 0.0).astype(BF16), (ti <= tj).astype(BF16))
    pos = run_scr[...] + incl - 1.0
    run_scr[...] = run_scr[...] + incl[:, tn - 1:tn]
    cnt_ref[...] = run_scr[...].astype(jnp.int32)
    ef = iv.astype(F32)
    for k in range(TOP_K):
        hit = rank == k
        pick = lambda a: jnp.sum(jnp.where(hit, a, 0.0), axis=0, keepdims=True)
        eidx_ref[k:k + 1, :] = pick(ef).astype(jnp.int32)
        gate_ref[k:k + 1, :] = pick(gate)
        pos_ref[k:k + 1, :] = pick(pos).astype(jnp.int32)


def _route(lg_p, lg_s, xn_p, xn_s, router_bias):
    ne = lg_p.shape[0]
    d = xn_p.shape[1]
    t = lg_p.shape[1] + lg_s.shape[1]
    tn = TOKEN_TILE
    n_first = lg_p.shape[1] // tn
    kt = lambda i: (0, i)
    return pl.pallas_call(
        functools.partial(_route_kernel, n_first=n_first),
        grid=(t // tn,),
        in_specs=(_pair_specs((ne, tn), 1, n_first) + _pair_specs((tn, d), 0, n_first)
                  + [_const_spec((ne, 1))]),
        out_specs=[pl.BlockSpec((TOP_K, tn), kt), pl.BlockSpec((TOP_K, tn), kt),
                   pl.BlockSpec((TOP_K, tn), kt), pl.BlockSpec((ne, 1), lambda i: (0, 0)),
                   pl.BlockSpec((tn, d), lambda i: (i, 0))],
        out_shape=[jax.ShapeDtypeStruct((TOP_K, t), jnp.int32),
                   jax.ShapeDtypeStruct((TOP_K, t), F32),
                   jax.ShapeDtypeStruct((TOP_K, t), jnp.int32),
                   jax.ShapeDtypeStruct((ne, 1), jnp.int32),
                   jax.ShapeDtypeStruct((t, d), F32)],
        scratch_shapes=[pltpu.VMEM((ne, 1), F32)],
        compiler_params=_cparams(("arbitrary",)),
        name="router",
    )(lg_p, lg_s, xn_p, xn_s, router_bias.astype(F32).reshape(ne, 1))


def _moe_plan(eidx, posk, counts, n_tiles_max):
    ne = counts.shape[0]
    ntile = (counts + EXPERT_TILE - 1) // EXPERT_TILE
    cum = jnp.cumsum(ntile)
    base = (cum - ntile) * EXPERT_TILE
    n_tiles = cum[-1:]
    experts = jnp.arange(ne, dtype=jnp.int32)
    last_used = jnp.max(jnp.where(ntile > 0, experts, 0))
    tiles = jnp.arange(n_tiles_max, dtype=jnp.int32)
    tile_expert = jnp.sum(tiles[:, None] >= cum[None, :], axis=1)
    tile_expert = jnp.minimum(tile_expert, last_used).astype(jnp.int32)
    slot = (jnp.sum(jnp.where(eidx[..., None] == experts, base, 0), axis=-1) + posk)
    slot = slot.astype(jnp.int32)
    n_pairs = slot.size
    pair_slot = jnp.transpose(slot).reshape(n_pairs)
    _, order = lax.sort_key_val(pair_slot, jnp.arange(n_pairs, dtype=jnp.int32))
    tok_sorted = jnp.concatenate([order // TOP_K, jnp.zeros((EXPERT_TILE,), jnp.int32)])
    first_pair = jnp.cumsum(counts) - counts
    first_tile = cum - ntile
    pair0 = (jnp.take(first_pair, tile_expert)
             + (tiles - jnp.take(first_tile, tile_expert)) * EXPERT_TILE)
    pair0 = jnp.clip(pair0, 0, n_pairs).astype(jnp.int32)
    return slot, tile_expert, n_tiles.astype(jnp.int32), tok_sorted, pair0


def _row_copy_wait(ref, n_rows, sem):
    pltpu.make_async_copy(ref.at[pl.ds(0, n_rows)], ref.at[pl.ds(0, n_rows)], sem).wait()


def _expert_kernel(te_ref, nt_ref, pair0_ref, tok_ref, xn_hbm, wg_ref, wu_ref, wd_ref, ys_ref,
                   xbuf, wg_scr, wu_scr, wd_scr, sem):
    i = pl.program_id(0)
    n_tiles = nt_ref[0]
    unroll = 8

    def gather(tile, buf):
        def body(rb, _):
            for u in range(unroll):
                r = rb * unroll + u
                t = tok_ref[pair0_ref[tile] + r]
                pltpu.make_async_copy(xn_hbm.at[pl.ds(t, 1)], xbuf.at[buf, pl.ds(r, 1)],
                                      sem.at[buf]).start(priority=u % 2)
            return 0
        lax.fori_loop(0, EXPERT_TILE // unroll, body, 0)

    @pl.when(i == 0)
    def _():
        gather(0, 0)

    @pl.when(i + 1 < n_tiles)
    def _():
        gather(i + 1, (i + 1) % 2)

    @pl.when(i < n_tiles)
    def _():
        buf = i % 2
        _row_copy_wait(xn_hbm, EXPERT_TILE, sem.at[buf])

        @pl.when((i == 0) | (te_ref[i] != te_ref[jnp.maximum(i - 1, 0)]))
        def _():
            wg_scr[...] = wg_ref[0].astype(BF16)
            wu_scr[...] = wu_ref[0].astype(BF16)
            wd_scr[...] = wd_ref[0].astype(BF16)

        x = xbuf[buf].astype(BF16)
        h = jax.nn.silu(_dot(x, wg_scr[...])) * _dot(x, wu_scr[...])
        ys_ref[...] = _dot(h.astype(BF16), wd_scr[...])

    @pl.when(i >= n_tiles)
    def _():
        ys_ref[...] = jnp.zeros_like(ys_ref)


def _experts(xn, tile_expert, n_tiles, pair0, tok_sorted, w_gate, w_up, w_down):
    d = xn.shape[1]
    ne, _, de = w_gate.shape
    n_tiles_max = tile_expert.shape[0]
    out_tile = lambda i, *_: (i, 0)
    wsel = lambda i, te, *_: (te[i], 0, 0)
    return pl.pallas_call(
        _expert_kernel,
        grid_spec=pltpu.PrefetchScalarGridSpec(
            num_scalar_prefetch=4,
            grid=(n_tiles_max,),
            in_specs=[pl.BlockSpec(memory_space=pl.ANY),
                      pl.BlockSpec((1, d, de), wsel), pl.BlockSpec((1, d, de), wsel),
                      pl.BlockSpec((1, de, d), wsel)],
            out_specs=pl.BlockSpec((EXPERT_TILE, d), out_tile),
            scratch_shapes=[pltpu.VMEM((2, EXPERT_TILE, d), F32),
                            pltpu.VMEM((d, de), BF16), pltpu.VMEM((d, de), BF16),
                            pltpu.VMEM((de, d), BF16), pltpu.SemaphoreType.DMA((2,))]),
        out_shape=jax.ShapeDtypeStruct((n_tiles_max * EXPERT_TILE, d), F32),
        compiler_params=_cparams(("arbitrary",)),
        name="moe_experts",
    )(tile_expert, n_tiles, pair0, tok_sorted, xn, w_gate, w_up, w_down)


def _combine_kernel(slot_ref, gate_ref, xn_ref, x2p_ref, x2s_ref, sg_ref, su_ref, sd_ref,
                    gf_ref, ys_hbm, yp_ref, ysamp_ref, buf, sem, *, n_first):
    tm = xn_ref.shape[0]
    i = pl.program_id(0)
    first = i < n_first

    def body(r, _):
        for k in range(TOP_K):
            s = slot_ref[0, 0, r * TOP_K + k]
            pltpu.make_async_copy(ys_hbm.at[pl.ds(s, 1)], buf.at[k, pl.ds(r, 1)],
                                  sem).start(priority=k % 2)
        return 0

    lax.fori_loop(0, tm, body, 0)
    x = xn_ref[...].astype(BF16)
    hs = jax.nn.silu(_dot(x, sg_ref[...])) * _dot(x, su_ref[...])
    acc = _pick(first, x2p_ref, x2s_ref) + _dot(hs.astype(BF16), sd_ref[...])
    _row_copy_wait(ys_hbm, tm * TOP_K, sem)
    g = gate_ref[...]
    for k in range(TOP_K):
        acc = acc + g[:, k:k + 1] * buf[k]
    y = _rms(acc, gf_ref[...])

    @pl.when(first)
    def _():
        yp_ref[...] = y

    @pl.when(jnp.logical_not(first))
    def _():
        ysamp_ref[...] = y


def _combine(slot_tiles, gates, xn, x2_p, x2_s, ws_gate, ws_up, ws_down, g_final, ys):
    d = x2_p.shape[1]
    tp, tsamp = x2_p.shape[0], x2_s.shape[0]
    dsh = ws_gate.shape[1]
    tm = MOE_TOKEN_TILE
    n_first = tp // tm
    row = lambda i: (i, 0)
    pair = lambda: _pair_specs((tm, d), 0, n_first)
    return pl.pallas_call(
        functools.partial(_combine_kernel, n_first=n_first),
        grid=((tp + tsamp) // tm,),
        in_specs=[pl.BlockSpec((1, 1, tm * TOP_K), lambda i: (i, 0, 0), memory_space=pltpu.SMEM),
                  pl.BlockSpec((tm, TOP_K), row), pl.BlockSpec((tm, d), row)] + pair()
                 + [_const_spec((d, dsh)), _const_spec((d, dsh)), _const_spec((dsh, d)),
                    _const_spec((1, d)), pl.BlockSpec(memory_space=pl.ANY)],
        out_specs=pair(),
        out_shape=[jax.ShapeDtypeStruct((tp, d), F32), jax.ShapeDtypeStruct((tsamp, d), F32)],
        scratch_shapes=[pltpu.VMEM((TOP_K, tm, d), F32), pltpu.SemaphoreType.DMA],
        compiler_params=_cparams(("arbitrary",)),
        name="moe_combine",
    )(slot_tiles, gates, xn, x2_p, x2_s, ws_gate.astype(BF16), ws_up.astype(BF16),
      ws_down.astype(BF16), g_final.reshape(1, d), ys)


def _moe(xn, x2_p, x2_s, eidx, gate, posk, counts, w_gate, w_up, w_down, ws_gate, ws_up,
         ws_down, g_final):
    t = xn.shape[0]
    ne = w_gate.shape[0]
    n_tiles_max = t * TOP_K // EXPERT_TILE + ne
    slot, tile_expert, n_tiles, tok_sorted, pair0 = _moe_plan(eidx, posk, counts[:, 0],
                                                              n_tiles_max)
    slot_tiles = jnp.transpose(slot).reshape(t // MOE_TOKEN_TILE, 1, MOE_TOKEN_TILE * TOP_K)
    ys = _experts(xn, tile_expert, n_tiles, pair0, tok_sorted, w_gate, w_up, w_down)
    return _combine(slot_tiles, jnp.transpose(gate), xn, x2_p, x2_s, ws_gate, ws_up, ws_down,
                    g_final, ys)


def kernel(x_prompt, x_sample, cache_mem_k, cache_mem_v, state_s5_re, state_s5_im, state_gla, mem_prompt, w_in, w_a2, b_a2, lam_re, lam_im, log_dt, b_re, b_im, c_re, c_im, d_skip, w_glu, b_glu, g_s5_out, g_gla_head, w_out, g_mix, g_xattn, g_mem, w_q, w_k, w_v, w_o, g_ffn, w_router, router_bias, w_gate, w_up, w_down, ws_gate, ws_up, ws_down, g_final):
    depth = w_in.shape[0]
    assert depth == 1, "one trunk layer"
    bp, lp, d = x_prompt.shape
    bs, ls, _ = x_sample.shape
    assert bp == 1 and ls == CHUNK and lp % CHUNK == 0
    n_mem = mem_prompt.shape[1]
    g, p = lam_re.shape[1:]
    d_s5 = g * S5_GROUP
    d_qk = GLA_HEADS * GLA_DK
    d_gla = w_out.shape[1] - d_s5
    ns = d_s5 // LANES
    ne = w_router.shape[2]
    tp, tsamp = bp * lp, bs * ls
    xp = x_prompt.reshape(tp, d)
    xs = x_sample.reshape(tsamp, d)
    sq = lambda a: a.reshape(a.shape[1:])

    mk_p, mv_p = _memory_kv(mem_prompt.reshape(n_mem, d), sq(g_mem), sq(w_k).astype(BF16),
                            sq(w_v).astype(BF16))

    u, q, k, v, a, r = _in_proj(xp, xs, sq(g_mix), sq(w_in), d_s5, d_qk, d_gla)

    ksm, psm, qsm, step_m, carry_m, a8 = _s5_coeffs(
        sq(lam_re), sq(lam_im), sq(log_dt), sq(b_re), sq(b_im), sq(c_re), sq(c_im), sq(d_skip))
    tables = tuple(_s5_expand(ksm, psm, qsm)) + (step_m, carry_m, a8)
    zero_h = jnp.zeros((ns, bp, 2 * GROUPS_PER_SLAB * p), F32)
    rows_p = lp // S5_BLOCK
    y_p, h_p = _s5_mixer(u, tables, zero_h, 0, tp, rows_per_seq=rows_p,
                         tile_rows=math.gcd(rows_p, 512))
    h0_s = _pack_s5_state(sq(state_s5_re), sq(state_s5_im), ns)
    y_s, h_s = _s5_mixer(u, tables, h0_s, tp, tsamp, rows_per_seq=ls // S5_BLOCK,
                         tile_rows=tsamp // S5_BLOCK)

    wa2 = jnp.pad(sq(w_a2), ((0, LANES - GLA_RANK), (0, 0)))
    zero_s = jnp.zeros((bp,) + state_gla.shape[2:], F32)
    gla_p, sg_p = _gla_mixer(q, k, v, a, r, wa2, sq(b_a2), sq(g_gla_head), zero_s, 0, tp,
                             carry=True, n_chunks=4)
    gla_s, sg_s = _gla_mixer(q, k, v, a, r, wa2, sq(b_a2), sq(g_gla_head), sq(state_gla), tp,
                             tsamp, carry=False, n_chunks=4)

    x1, xq = _mix_out(y_p, y_s, gla_p, gla_s, xp, xs, sq(w_glu), sq(b_glu), sq(g_s5_out),
                      sq(w_out), sq(g_xattn), sq(w_q))

    wr_hi, wr_lo = _split_bf16(jnp.transpose(sq(w_router)))
    wo = sq(w_o).astype(BF16)
    x2_p, xn_p, lg_p = _xattn(xq, x1, mk_p[None], mv_p[None], wo, sq(g_ffn), wr_hi, wr_lo,
                              0, tp, nb=1, lt=TOKEN_TILE)
    x2_s, xn_s, lg_s = _xattn(xq, x1, cache_mem_k.reshape(bs, n_mem, d),
                              cache_mem_v.reshape(bs, n_mem, d), wo, sq(g_ffn), wr_hi, wr_lo,
                              tp, tsamp, nb=2, lt=ls)

    eidx, gate, posk, counts, xn = _route(lg_p, lg_s, xn_p, xn_s, sq(router_bias))
    y_p2, y_s2 = _moe(xn, x2_p, x2_s, eidx, gate, posk, counts, sq(w_gate), sq(w_up),
                      sq(w_down), sq(ws_gate), sq(ws_up), sq(ws_down), g_final)

    xh = d // X_HEADS
    re_p, im_p = _unpack_s5_state(h_p, g, p)
    re_s, im_s = _unpack_s5_state(h_s, g, p)
    return (y_p2.reshape(bp, lp, d), y_s2.reshape(bs, ls, d),
            mk_p.reshape(1, bp, n_mem, X_HEADS, xh), mv_p.reshape(1, bp, n_mem, X_HEADS, xh),
            re_p[None], im_p[None], sg_p[None], re_s[None], im_s[None], sg_s[None])
```

```python
import functools
import math

import jax
import jax.numpy as jnp
from jax import lax
from jax.experimental import pallas as pl
from jax.experimental.pallas import tpu as pltpu

F32 = jnp.float32
BF16 = jnp.bfloat16

EPS = 1e-6
CHUNK = 64
S5_GROUP = 16
GLA_HEADS = 4
GLA_DK = 128
GLA_RANK = 16
GLA_TAU = 16.0
X_HEADS = 4
N_GROUPS = 8
TOPK_GROUPS = 4
TOP_K = 8
ROUTE_SCALE = 2.5

LANES = 128
SUBLANES = 8
S5_BLOCK = SUBLANES
GROUPS_PER_SLAB = LANES // S5_GROUP
TOKEN_TILE = 512
EXPERT_TILE = 256
MOE_TOKEN_TILE = 256
COMBINE_TILE = 128
VMEM_LIMIT = 56 * 1024 * 1024


def _cparams(sem):
    return pltpu.CompilerParams(dimension_semantics=sem, vmem_limit_bytes=VMEM_LIMIT)


def _const_spec(shape):
    nd = len(shape)
    return pl.BlockSpec(shape, lambda *_: (0,) * nd, pipeline_mode=pl.Buffered(1))


def _pair_specs(block, axis, n_first):
    def at(f):
        return lambda i, *_: tuple(f(i) if a == axis else 0 for a in range(len(block)))
    return [pl.BlockSpec(block, at(lambda i: jnp.minimum(i, n_first - 1))),
            pl.BlockSpec(block, at(lambda i: jnp.maximum(i - n_first, 0)))]


def _pick(first, a_ref, b_ref):
    return jnp.where(first, a_ref[...], b_ref[...])


def _rms(x, g):
    return x * lax.rsqrt(jnp.mean(x * x, axis=-1, keepdims=True) + EPS) * g


def _split_bf16(x):
    hi = x.astype(BF16)
    lo = (x - hi.astype(F32)).astype(BF16)
    return hi, lo


def _split3_bf16(x):
    a = x.astype(BF16)
    r = x - a.astype(F32)
    b = r.astype(BF16)
    c = (r - b.astype(F32)).astype(BF16)
    return a, b, c


def _dot(a, b):
    return jnp.dot(a, b, preferred_element_type=F32)


def _dot_nt(a, b):
    return lax.dot_general(a, b, (((1,), (1,)), ((), ())), preferred_element_type=F32)


def _dot_tn(a, b):
    return lax.dot_general(a, b, (((0,), (0,)), ((), ())), preferred_element_type=F32)


def _memkv_kernel(mem_ref, g_ref, wk_ref, wv_ref, mk_ref, mv_ref):
    m = _rms(mem_ref[...], g_ref[...]).astype(BF16)
    mk_ref[...] = _dot(m, wk_ref[...])
    mv_ref[...] = _dot(m, wv_ref[...])


def _memory_kv(mem, g_mem, wk, wv):
    n, d = mem.shape
    tn = 512
    return pl.pallas_call(
        _memkv_kernel,
        grid=(d // tn,),
        in_specs=[_const_spec((n, d)), _const_spec((1, d)),
                  pl.BlockSpec((d, tn), lambda j: (0, j)),
                  pl.BlockSpec((d, tn), lambda j: (0, j))],
        out_specs=[pl.BlockSpec((n, tn), lambda j: (0, j))] * 2,
        out_shape=[jax.ShapeDtypeStruct((n, d), F32)] * 2,
        compiler_params=_cparams(("arbitrary",)),
        name="memory_kv",
    )(mem, g_mem.reshape(1, d), wk, wv)


def _inproj_kernel(xp_ref, xs_ref, g_ref, wuh_ref, wul_ref, wqkv_ref, wa_ref, wr_ref,
                   u_ref, q_ref, k_ref, v_ref, a_ref, r_ref, *, n_first):
    first = pl.program_id(0) < n_first
    x = _pick(first, xp_ref, xs_ref)
    nx = _rms(x, g_ref[...])
    hi, lo = _split_bf16(nx)
    u = _dot(hi, wuh_ref[...])
    for j in range(u_ref.shape[0]):
        u_ref[j] = u[:, j * LANES:(j + 1) * LANES]

    @pl.when(jnp.logical_not(first))
    def _():
        fix = _dot(lo, wuh_ref[...]) + _dot(hi, wul_ref[...])
        for j in range(u_ref.shape[0]):
            u_ref[j] += fix[:, j * LANES:(j + 1) * LANES]

    qkv = _dot(hi, wqkv_ref[...])
    dq = q_ref.shape[1]
    q_ref[...] = qkv[:, :dq].astype(BF16)
    k_ref[...] = qkv[:, dq:2 * dq].astype(BF16)
    v_ref[...] = qkv[:, 2 * dq:].astype(BF16)
    a_ref[...] = _dot(hi, wa_ref[...]) + _dot(lo, wa_ref[...])
    r_ref[...] = _dot(hi, wr_ref[...]).astype(BF16)


def _in_proj(xp, xs, g_mix, w_in, d_s5, d_qk, d_gla):
    d = xp.shape[1]
    t = xp.shape[0] + xs.shape[0]
    tm = TOKEN_TILE
    n_first = xp.shape[0] // tm
    o1 = d_s5
    o2 = o1 + 2 * d_qk + d_gla
    o3 = o2 + GLA_RANK
    wu_hi, wu_lo = _split_bf16(w_in[:, :o1])
    wqkv = w_in[:, o1:o2].astype(BF16)
    wa = jnp.pad(w_in[:, o2:o3], ((0, 0), (0, LANES - GLA_RANK))).astype(BF16)
    wr = w_in[:, o3:].astype(BF16)
    n_slab = d_s5 // LANES
    row = lambda i: (i, 0)
    return pl.pallas_call(
        functools.partial(_inproj_kernel, n_first=n_first),
        grid=(t // tm,),
        in_specs=_pair_specs((tm, d), 0, n_first) + [
            _const_spec((1, d)), _const_spec(wu_hi.shape), _const_spec(wu_lo.shape),
            _const_spec(wqkv.shape), _const_spec(wa.shape), _const_spec(wr.shape)],
        out_specs=[pl.BlockSpec((n_slab, tm, LANES), lambda i: (0, i, 0)),
                   pl.BlockSpec((tm, d_qk), row), pl.BlockSpec((tm, d_qk), row),
                   pl.BlockSpec((tm, d_gla), row), pl.BlockSpec((tm, LANES), row),
                   pl.BlockSpec((tm, d_gla), row)],
        out_shape=[jax.ShapeDtypeStruct((n_slab, t, LANES), F32),
                   jax.ShapeDtypeStruct((t, d_qk), BF16), jax.ShapeDtypeStruct((t, d_qk), BF16),
                   jax.ShapeDtypeStruct((t, d_gla), BF16), jax.ShapeDtypeStruct((t, LANES), F32),
                   jax.ShapeDtypeStruct((t, d_gla), BF16)],
        compiler_params=_cparams(("arbitrary",)),
        name="in_proj",
    )(xp, xs, g_mix.reshape(1, d), wu_hi, wu_lo, wqkv, wa, wr)


def _s5_coeffs(lam_re, lam_im, log_dt, b_re, b_im, c_re, c_im, d_skip):
    g, p = lam_re.shape
    h = b_re.shape[-1]
    nb = S5_BLOCK
    ns = g // GROUPS_PER_SLAB
    gl = GROUPS_PER_SLAB
    dt = jnp.exp(log_dt.astype(F32))[:, None]
    lr, li = lam_re.astype(F32), lam_im.astype(F32)
    mag = jnp.exp(lr * dt)
    ar, ai = mag * jnp.cos(li * dt), mag * jnp.sin(li * dt)
    den = lr * lr + li * li
    nr = ar - 1.0
    cf_r = (nr * lr + ai * li) / den
    cf_i = (ai * lr - nr * li) / den
    bb_r = cf_r[..., None] * b_re - cf_i[..., None] * b_im
    bb_i = cf_r[..., None] * b_im + cf_i[..., None] * b_re

    def cpow(n):
        e = n[:, None, None] * dt[None]
        m = jnp.exp(lr[None] * e)
        return m * jnp.cos(li[None] * e), m * jnp.sin(li[None] * e)

    pr, pi = cpow(jnp.arange(nb + 1, dtype=F32))

    cb_r = jnp.einsum('gcp,gph->gpch', c_re, bb_r) - jnp.einsum('gcp,gph->gpch', c_im, bb_i)
    cb_i = jnp.einsum('gcp,gph->gpch', c_re, bb_i) + jnp.einsum('gcp,gph->gpch', c_im, bb_r)
    taps = (jnp.einsum('ngp,gpch->ngch', pr[:nb], cb_r)
            - jnp.einsum('ngp,gpch->ngch', pi[:nb], cb_i))
    taps = taps.at[0].add(d_skip[:, :, None] * jnp.eye(h, dtype=F32)[None])
    ksm = jnp.transpose(taps.reshape(nb, ns, gl, h, h), (1, 0, 2, 4, 3))
    ksm = ksm.reshape(ns, nb, LANES, h)

    wr, wi = pr[nb - 1::-1][:nb], pi[nb - 1::-1][:nb]
    inj_r = wr[..., None] * bb_r[None] - wi[..., None] * bb_i[None]
    inj_i = wr[..., None] * bb_i[None] + wi[..., None] * bb_r[None]
    inj = jnp.stack([inj_r, inj_i], axis=0).reshape(2, nb, ns, gl, p, h)
    psm = jnp.transpose(inj, (2, 1, 3, 5, 0, 4)).reshape(ns, nb, LANES, 2 * p)

    er, ei = pr[1:nb + 1], pi[1:nb + 1]
    q_r = c_re[None] * er[:, :, None, :] - c_im[None] * ei[:, :, None, :]
    q_i = -(c_re[None] * ei[:, :, None, :] + c_im[None] * er[:, :, None, :])
    qq = jnp.stack([q_r, q_i], axis=0).reshape(2, nb, ns, gl, h, p)
    qsm = jnp.transpose(qq, (2, 0, 3, 5, 1, 4)).reshape(ns, 2 * gl * p, nb * h)

    def slab(xr, xi):
        k = xr.shape[0]
        xr = jnp.transpose(xr.reshape(k, ns, gl * p), (1, 0, 2))
        xi = jnp.transpose(xi.reshape(k, ns, gl * p), (1, 0, 2))
        return jnp.concatenate([xr, xi], axis=-1)

    rows = jnp.arange(SUBLANES, dtype=F32)
    cr, ci = cpow(nb * rows)
    carry_m = slab(cr, ci)
    sr, si = cpow(nb * jnp.array([1.0, 2.0, 4.0], F32))
    keep = (rows[None, :] >= jnp.array([1.0, 2.0, 4.0], F32)[:, None]).astype(F32)
    step_m = slab((sr[:, None] * keep[:, :, None, None]).reshape(3 * SUBLANES, g, p),
                  (si[:, None] * keep[:, :, None, None]).reshape(3 * SUBLANES, g, p))
    step_m = step_m.reshape(ns, 3, SUBLANES, 2 * gl * p)
    a8 = carry_m[:, 1:2]
    return ksm, psm, qsm, step_m, carry_m, a8


def _s5_expand_kernel(ksm_ref, psm_ref, qsm_ref, tm_ref, ph_ref, pl_ref, qm_ref):
    nb = S5_BLOCK
    h = S5_GROUP
    w = ph_ref.shape[2]
    p = w // (2 * GROUPS_PER_SLAB)

    def spread(x, sel, keep):
        a, b, c = _split3_bf16(x)
        return jnp.where(keep, _dot(a, sel) + _dot(b, sel) + _dot(c, sel), 0.0)

    def iota(shape, axis):
        return lax.broadcasted_iota(jnp.int32, shape, axis)

    sel_c = (iota((h, LANES), 1) % h == iota((h, LANES), 0)).astype(BF16)
    keep_t = iota((LANES, LANES), 0) // h == iota((LANES, LANES), 1) // h
    tm_ref[...] = jnp.zeros_like(tm_ref)
    for tau in range(nb):
        blk = spread(ksm_ref[0, tau], sel_c, keep_t).astype(BF16)
        for s in range(nb - tau):
            t = s + tau
            tm_ref[0, s * LANES:(s + 1) * LANES, t * LANES:(t + 1) * LANES] = blk

    half = w // 2
    keep_p = iota((LANES, half), 0) // h == iota((LANES, half), 1) // p
    for ri in range(2):
        sel_p = (iota((2 * p, half), 0) == iota((2 * p, half), 1) % p + ri * p).astype(BF16)
        for s in range(nb):
            blk = spread(psm_ref[0, s], sel_p, keep_p)
            hi, lo = _split_bf16(blk)
            ph_ref[0, s * LANES:(s + 1) * LANES, ri * half:(ri + 1) * half] = hi
            pl_ref[0, s * LANES:(s + 1) * LANES, ri * half:(ri + 1) * half] = lo

    keep_q = (iota((w, LANES), 0) % half) // p == iota((w, LANES), 1) // h
    for t in range(nb):
        sel_q = (iota((nb * h, LANES), 0) == iota((nb * h, LANES), 1) % h + t * h).astype(BF16)
        qm_ref[0, :, t * LANES:(t + 1) * LANES] = spread(qsm_ref[0], sel_q, keep_q).astype(BF16)


def _s5_expand(ksm, psm, qsm):
    ns = ksm.shape[0]
    w = qsm.shape[1]
    k = S5_BLOCK * LANES
    blk = lambda shape: pl.BlockSpec((1,) + shape, lambda j: (j,) + (0,) * len(shape))
    return pl.pallas_call(
        _s5_expand_kernel,
        grid=(ns,),
        in_specs=[blk(ksm.shape[1:]), blk(psm.shape[1:]), blk(qsm.shape[1:])],
        out_specs=[blk((k, k)), blk((k, w)), blk((k, w)), blk((w, k))],
        out_shape=[jax.ShapeDtypeStruct((ns, k, k), BF16), jax.ShapeDtypeStruct((ns, k, w), BF16),
                   jax.ShapeDtypeStruct((ns, k, w), BF16), jax.ShapeDtypeStruct((ns, w, k), BF16)],
        compiler_params=_cparams(("arbitrary",)),
        name="s5_tables",
    )(ksm, psm, qsm)


def _cmul(xr, xi, mr, mi):
    return xr * mr - xi * mi, xr * mi + xi * mr


def _s5_kernel(u_ref, ph_ref, pl_ref, tm_ref, qm_ref, step_ref, carry_ref, a8_ref, h0_ref,
               y_ref, hout_ref, s_scr, hin_scr, c_scr, *, rows_per_seq):
    ti = pl.program_id(1)
    rows = s_scr.shape[0]
    half = s_scr.shape[1] // 2
    nb = S5_BLOCK
    u8 = jnp.concatenate([u_ref[0, pl.ds(s, rows, stride=nb), :] for s in range(nb)], axis=1)
    hi, lo = _split_bf16(u8)
    per_block_seq = rows_per_seq == SUBLANES
    if per_block_seq:
        s_scr[...] = _dot(hi, ph_ref[0]) + _dot(lo, ph_ref[0]) + _dot(hi, pl_ref[0])
    else:
        s_scr[...] = _dot(hi, ph_ref[0])

    if not per_block_seq:
        @pl.when(ti == 0)
        def _():
            c_scr[...] = h0_ref[0]

    not_first = (lax.broadcasted_iota(jnp.int32, (SUBLANES, 1), 0) >= 1).astype(F32)
    a8r, a8i = a8_ref[0, :, :half], a8_ref[0, :, half:]
    cmr, cmi = carry_ref[0, :, :half], carry_ref[0, :, half:]

    def body(rb, _):
        r0 = pl.multiple_of(rb * SUBLANES, SUBLANES)
        sb = s_scr[pl.ds(r0, SUBLANES), :]
        x = pltpu.roll(sb, 1, 0) * not_first
        xr, xi = x[:, :half], x[:, half:]
        for k, sh in enumerate((1, 2, 4)):
            m = step_ref[0, k]
            pr, pi = _cmul(pltpu.roll(xr, sh, 0), pltpu.roll(xi, sh, 0), m[:, :half], m[:, half:])
            xr, xi = xr + pr, xi + pi
        c = h0_ref[0, pl.ds(rb, 1), :] if per_block_seq else c_scr[...]
        cr, ci = c[:, :half], c[:, half:]
        pr, pi = _cmul(cr, ci, cmr, cmi)
        hr, hi_ = xr + pr, xi + pi
        hin_scr[pl.ds(r0, SUBLANES), :half] = hr
        hin_scr[pl.ds(r0, SUBLANES), half:] = hi_
        nr, ni = _cmul(hr[SUBLANES - 1:], hi_[SUBLANES - 1:], a8r, a8i)
        cn = jnp.concatenate([nr, ni], axis=1) + sb[SUBLANES - 1:]
        if per_block_seq:
            hout_ref[0, pl.ds(rb, 1), :] = cn
        else:
            c_scr[...] = cn
        return 0

    lax.fori_loop(0, rows // SUBLANES, body, 0)
    if not per_block_seq:
        hout_ref[0] = c_scr[...]

    y8 = _dot(hi, tm_ref[0]) + _dot(hin_scr[...].astype(BF16), qm_ref[0])
    for t in range(nb):
        y_ref[0, pl.ds(t, rows, stride=nb), :] = y8[:, t * LANES:(t + 1) * LANES]


def _s5_mixer(u_slabs, tables, h0, t_start, t, rows_per_seq, tile_rows):
    tm, pm_hi, pm_lo, qm, step_m, carry_m, a8 = tables
    ns = u_slabs.shape[0]
    rows = t // S5_BLOCK
    n_tiles = rows // tile_rows
    assert rows % tile_rows == 0 and t_start % (tile_rows * S5_BLOCK) == 0
    off = t_start // (tile_rows * S5_BLOCK)
    n_seq = h0.shape[1]
    w = pm_hi.shape[-1]
    slab3 = lambda j, i: (j, 0, 0)
    kern = functools.partial(_s5_kernel, rows_per_seq=rows_per_seq)
    return pl.pallas_call(
        kern,
        grid=(ns, n_tiles),
        in_specs=[pl.BlockSpec((1, tile_rows * S5_BLOCK, LANES), lambda j, i: (j, i + off, 0)),
                  pl.BlockSpec((1,) + pm_hi.shape[1:], slab3),
                  pl.BlockSpec((1,) + pm_lo.shape[1:], slab3),
                  pl.BlockSpec((1,) + tm.shape[1:], slab3),
                  pl.BlockSpec((1,) + qm.shape[1:], slab3),
                  pl.BlockSpec((1,) + step_m.shape[1:], lambda j, i: (j, 0, 0, 0)),
                  pl.BlockSpec((1,) + carry_m.shape[1:], slab3),
                  pl.BlockSpec((1,) + a8.shape[1:], slab3),
                  pl.BlockSpec((1, n_seq, w), slab3)],
        out_specs=[pl.BlockSpec((1, tile_rows * S5_BLOCK, LANES), lambda j, i: (j, i, 0)),
                   pl.BlockSpec((1, n_seq, w), slab3)],
        out_shape=[jax.ShapeDtypeStruct((ns, t, LANES), F32),
                   jax.ShapeDtypeStruct((ns, n_seq, w), F32)],
        scratch_shapes=[pltpu.VMEM((tile_rows, w), F32), pltpu.VMEM((tile_rows, w), F32),
                        pltpu.VMEM((1, w), F32)],
        compiler_params=_cparams(("arbitrary", "arbitrary")),
        name="s5_mixer",
    )(u_slabs, pm_hi, pm_lo, tm, qm, step_m, carry_m, a8, h0)


def _pack_s5_state(re, im, ns):
    b = re.shape[0]
    r = re.astype(F32).reshape(b, ns, -1)
    i = im.astype(F32).reshape(b, ns, -1)
    return jnp.transpose(jnp.concatenate([r, i], axis=-1), (1, 0, 2))


def _unpack_s5_state(hc, g, p):
    ns, b, w = hc.shape
    hc = jnp.transpose(hc, (1, 0, 2))
    re = hc[:, :, :w // 2].reshape(b, g, p)
    im = hc[:, :, w // 2:].reshape(b, g, p)
    return re, im


def _gla_kernel(q_ref, k_ref, v_ref, a_ref, r_ref, wa2_ref, ba2_ref, gh_ref, s0_ref,
                o_ref, sout_ref, st_scr, *, carry, n_chunks):
    step = pl.program_id(0)
    dk = GLA_DK
    dv = v_ref.shape[1] // GLA_HEADS
    c = CHUNK
    scale = dk ** -0.5
    ri = lax.broadcasted_iota(jnp.int32, (c, c), 0)
    ci = lax.broadcasted_iota(jnp.int32, (c, c), 1)
    causal = ri >= ci
    tril = causal.astype(BF16)
    eye_dk = (lax.broadcasted_iota(jnp.int32, (dk, dk), 0)
              == lax.broadcasted_iota(jnp.int32, (dk, dk), 1))

    if carry:
        @pl.when(step == 0)
        def _():
            st_scr[...] = s0_ref[0]

    wa_hi, wa_lo = _split_bf16(wa2_ref[...])
    for n in range(n_chunks):
        rows = slice(n * c, (n + 1) * c)
        a_hi, a_lo = _split_bf16(a_ref[rows, :])
        logit = _dot(a_hi, wa_hi) + _dot(a_lo, wa_hi) + _dot(a_hi, wa_lo) + ba2_ref[...]
        g = jax.nn.log_sigmoid(logit) * (1.0 / GLA_TAU)
        g1, g2, g3 = _split3_bf16(g)
        bcum = _dot(tril, g1) + _dot(tril, g2) + _dot(tril, g3)
        for h in range(GLA_HEADS):
            ks = slice(h * dk, (h + 1) * dk)
            vs = slice(h * dv, (h + 1) * dv)
            b = bcum[:, ks]
            qh = q_ref[rows, ks].astype(F32) * scale
            kh = k_ref[rows, ks].astype(F32)
            vh = v_ref[rows, vs]
            state = s0_ref[n, h] if not carry else st_scr[h]
            qe = (qh * jnp.exp(b)).astype(BF16)
            ke = (kh * jnp.exp(-b)).astype(BF16)
            att = jnp.where(causal, _dot_nt(qe, ke), 0.0)
            o = _dot(att.astype(BF16), vh) + _dot(qe, state.astype(BF16))
            blast = b[c - 1:c, :]
            kd = (kh * jnp.exp(blast - b)).astype(BF16)
            decay = jnp.sum(jnp.where(eye_dk, jnp.exp(blast), 0.0), axis=1, keepdims=True)
            new_state = decay * state + _dot_tn(kd, vh)
            if carry:
                st_scr[h] = new_state
            else:
                sout_ref[n, h] = new_state
            on = _rms(o, gh_ref[...])
            rr = r_ref[rows, vs].astype(F32)
            o_ref[rows, vs] = (on * (rr * jax.nn.sigmoid(rr))).astype(BF16)
    if carry:
        sout_ref[0] = st_scr[...]


def _gla_mixer(q, k, v, a, r, wa2, ba2, g_head, s0, t_start, t, carry, n_chunks):
    dqk = q.shape[1]
    dvt = v.shape[1]
    hh, dk, dv = s0.shape[1:]
    rows = n_chunks * CHUNK
    assert t % rows == 0 and t_start % rows == 0
    off = t_start // rows
    row = lambda i: (i, 0)
    row_in = lambda i: (i + off, 0)
    if carry:
        sblk, smap = (1, hh, dk, dv), (lambda i: (0, 0, 0, 0))
    else:
        sblk, smap = (n_chunks, hh, dk, dv), (lambda i: (i, 0, 0, 0))
    kern = functools.partial(_gla_kernel, carry=carry, n_chunks=n_chunks)
    return pl.pallas_call(
        kern,
        grid=(t // rows,),
        in_specs=[pl.BlockSpec((rows, dqk), row_in), pl.BlockSpec((rows, dqk), row_in),
                  pl.BlockSpec((rows, dvt), row_in), pl.BlockSpec((rows, LANES), row_in),
                  pl.BlockSpec((rows, dvt), row_in),
                  _const_spec(wa2.shape), _const_spec((1, dqk)), _const_spec((1, dv)),
                  pl.BlockSpec(sblk, smap)],
        out_specs=[pl.BlockSpec((rows, dvt), row), pl.BlockSpec(sblk, smap)],
        out_shape=[jax.ShapeDtypeStruct((t, dvt), BF16), jax.ShapeDtypeStruct(s0.shape, F32)],
        scratch_shapes=[pltpu.VMEM((hh, dk, dv), F32)],
        compiler_params=_cparams(("arbitrary",)),
        name="gla_mixer",
    )(q, k, v, a, r, wa2, ba2.reshape(1, dqk), g_head.reshape(1, dv), s0)


def _mixout_kernel(yp_ref, ys_ref, gp_ref, gs_ref, xp_ref, xs_ref, wglu_ref, bglu_ref, gs5_ref,
                   wout_ref, gx_ref, wq_ref, x1_ref, q_ref, *, n_first):
    first = pl.program_id(0) < n_first
    yb = _pick(first, yp_ref, ys_ref)
    y = jnp.concatenate([yb[j] for j in range(yb.shape[0])], axis=1)
    z = jax.nn.gelu(y)
    gate = jax.nn.sigmoid(_dot(z.astype(BF16), wglu_ref[...]) + bglu_ref[...])
    s5 = _rms(z * gate, gs5_ref[...])
    cat = jnp.concatenate([s5.astype(BF16), _pick(first, gp_ref, gs_ref)], axis=1)
    x1 = _pick(first, xp_ref, xs_ref) + _dot(cat, wout_ref[...])
    x1_ref[...] = x1
    q_ref[...] = _dot(_rms(x1, gx_ref[...]).astype(BF16), wq_ref[...]).astype(BF16)


def _mix_out(y_p, y_s, gla_p, gla_s, xp, xs, w_glu, b_glu, g_s5, w_out, g_x, w_q):
    d = xp.shape[1]
    t = xp.shape[0] + xs.shape[0]
    ns = y_p.shape[0]
    ds5 = ns * LANES
    dg = gla_p.shape[1]
    tm = TOKEN_TILE // 2
    n_first = xp.shape[0] // tm
    row = lambda i: (i, 0)
    return pl.pallas_call(
        functools.partial(_mixout_kernel, n_first=n_first),
        grid=(t // tm,),
        in_specs=(_pair_specs((ns, tm, LANES), 1, n_first) + _pair_specs((tm, dg), 0, n_first)
                  + _pair_specs((tm, d), 0, n_first)
                  + [_const_spec((ds5, ds5)), _const_spec((1, ds5)), _const_spec((1, ds5)),
                     _const_spec((ds5 + dg, d)), _const_spec((1, d)), _const_spec((d, d))]),
        out_specs=[pl.BlockSpec((tm, d), row), pl.BlockSpec((tm, d), row)],
        out_shape=[jax.ShapeDtypeStruct((t, d), F32), jax.ShapeDtypeStruct((t, d), BF16)],
        compiler_params=_cparams(("arbitrary",)),
        name="mix_out",
    )(y_p, y_s, gla_p, gla_s, xp, xs, w_glu.astype(BF16), b_glu.reshape(1, ds5),
      g_s5.reshape(1, ds5), w_out.astype(BF16), g_x.reshape(1, d), w_q.astype(BF16))


def _xattn_kernel(q_ref, mk_ref, mv_ref, x1_ref, wo_ref, gf_ref, wrh_ref, wrl_ref,
                  x2_ref, xn_ref, lg_ref):
    nb = mk_ref.shape[0]
    d = q_ref.shape[1]
    lt = q_ref.shape[0] // nb
    hd = d // X_HEADS
    scale = hd ** -0.5
    for b in range(nb):
        rows = slice(b * lt, (b + 1) * lt)
        mk = mk_ref[b].astype(BF16)
        mv = mv_ref[b].astype(BF16)
        outs = []
        for h in range(X_HEADS):
            hs = slice(h * hd, (h + 1) * hd)
            s = _dot_nt(q_ref[rows, hs], mk[:, hs]) * scale
            p = jnp.exp(s - jnp.max(s, axis=-1, keepdims=True))
            denom = jnp.sum(p, axis=-1, keepdims=True)
            outs.append((_dot(p.astype(BF16), mv[:, hs]) / denom).astype(BF16))
        o = jnp.concatenate(outs, axis=1)
        x2 = x1_ref[rows, :] + _dot(o, wo_ref[...])
        x2_ref[rows, :] = x2
        xn = _rms(x2, gf_ref[...])
        xn_ref[rows, :] = xn
        hi, lo = _split_bf16(xn)
        lg_ref[:, rows] = (_dot_nt(wrh_ref[...], hi) + _dot_nt(wrh_ref[...], lo)
                           + _dot_nt(wrl_ref[...], hi))


def _xattn(q, x1, mk, mv, w_o, g_ffn, wr_hi, wr_lo, t_start, t, nb, lt):
    d = q.shape[1]
    nm = mk.shape[1]
    ne = wr_hi.shape[0]
    rows = nb * lt
    assert t % rows == 0 and t_start % rows == 0
    off = t_start // rows
    shared = mk.shape[0] == 1
    assert nb == 1 if shared else mk.shape[0] * lt == t
    row = lambda i: (i, 0)
    row_in = lambda i: (i + off, 0)
    mblk = (lambda i: (0, 0, 0)) if shared else (lambda i: (i, 0, 0))
    return pl.pallas_call(
        _xattn_kernel,
        grid=(t // rows,),
        in_specs=[pl.BlockSpec((rows, d), row_in),
                  pl.BlockSpec((nb, nm, d), mblk), pl.BlockSpec((nb, nm, d), mblk),
                  pl.BlockSpec((rows, d), row_in),
                  _const_spec((d, d)), _const_spec((1, d)),
                  _const_spec((ne, d)), _const_spec((ne, d))],
        out_specs=[pl.BlockSpec((rows, d), row), pl.BlockSpec((rows, d), row),
                   pl.BlockSpec((ne, rows), lambda i: (0, i))],
        out_shape=[jax.ShapeDtypeStruct((t, d), F32), jax.ShapeDtypeStruct((t, d), F32),
                   jax.ShapeDtypeStruct((ne, t), F32)],
        compiler_params=_cparams(("arbitrary",)),
        name="mem_xattn",
    )(q, mk, mv, x1, w_o, g_ffn.reshape(1, d), wr_hi, wr_lo)


def _route_kernel(lgp_ref, lgs_ref, bias_ref, eidx_ref, gate_ref, pos_ref, cnt_ref, run_scr,
                  *, n_first):
    ne, tn = lgp_ref.shape

    @pl.when(pl.program_id(0) == 0)
    def _():
        run_scr[...] = jnp.zeros_like(run_scr)

    gsz = ne // N_GROUPS
    sc = jax.nn.sigmoid(_pick(pl.program_id(0) < n_first, lgp_ref, lgs_ref))
    sel = sc + bias_ref[...]
    s3 = sel.reshape(N_GROUPS, gsz, tn)
    ie = lax.broadcasted_iota(jnp.int32, s3.shape, 1).astype(F32)
    m1 = jnp.max(s3, axis=1, keepdims=True)
    first = jnp.min(jnp.where(s3 == m1, ie, float(gsz)), axis=1, keepdims=True)
    m2 = jnp.max(jnp.where(ie == first, -jnp.inf, s3), axis=1, keepdims=True)
    gs = m1 + m2
    ig = lax.broadcasted_iota(jnp.int32, gs.shape, 0)
    grank = jnp.zeros(gs.shape, jnp.int32)
    for g in range(N_GROUPS):
        other = gs[g:g + 1]
        ahead = jnp.where(other > gs, 1, jnp.where(other == gs, (ig > g).astype(jnp.int32), 0))
        grank = grank + ahead
    gkeep = jnp.broadcast_to(grank < TOPK_GROUPS, s3.shape)
    v = jnp.where(gkeep, s3, -jnp.inf).reshape(ne, tn)
    iv = lax.broadcasted_iota(jnp.int32, v.shape, 0)
    rank = jnp.zeros(v.shape, jnp.int32)
    for e in range(ne):
        other = v[e:e + 1]
        ahead = jnp.where(other > v, 1, jnp.where(other == v, (iv > e).astype(jnp.int32), 0))
        rank = rank + ahead
    chosen = rank < TOP_K
    gate = jnp.where(chosen, sc, 0.0)
    gate = gate / jnp.sum(gate, axis=0, keepdims=True) * ROUTE_SCALE
    ti = lax.broadcasted_iota(jnp.int32, (tn, tn), 0)
    tj = lax.broadcasted_iota(jnp.int32, (tn, tn), 1)
    incl = _dot(jnp.where(chosen, 1.0, 0.0).astype(BF16), (ti <= tj).astype(BF16))
    pos = run_scr[...] + incl - 1.0
    run_scr[...] = run_scr[...] + incl[:, tn - 1:tn]
    cnt_ref[...] = run_scr[...].astype(jnp.int32)
    ef = iv.astype(F32)
    for k in range(TOP_K):
        hit = rank == k
        pick = lambda a: jnp.sum(jnp.where(hit, a, 0.0), axis=0, keepdims=True)
        eidx_ref[k:k + 1, :] = pick(ef).astype(jnp.int32)
        gate_ref[k:k + 1, :] = pick(gate)
        pos_ref[k:k + 1, :] = pick(pos).astype(jnp.int32)


def _route(lg_p, lg_s, router_bias):
    ne = lg_p.shape[0]
    t = lg_p.shape[1] + lg_s.shape[1]
    tn = TOKEN_TILE
    n_first = lg_p.shape[1] // tn
    kt = lambda i: (0, i)
    return pl.pallas_call(
        functools.partial(_route_kernel, n_first=n_first),
        grid=(t // tn,),
        in_specs=_pair_specs((ne, tn), 1, n_first) + [_const_spec((ne, 1))],
        out_specs=[pl.BlockSpec((TOP_K, tn), kt), pl.BlockSpec((TOP_K, tn), kt),
                   pl.BlockSpec((TOP_K, tn), kt), pl.BlockSpec((ne, 1), lambda i: (0, 0))],
        out_shape=[jax.ShapeDtypeStruct((TOP_K, t), jnp.int32),
                   jax.ShapeDtypeStruct((TOP_K, t), F32),
                   jax.ShapeDtypeStruct((TOP_K, t), jnp.int32),
                   jax.ShapeDtypeStruct((ne, 1), jnp.int32)],
        scratch_shapes=[pltpu.VMEM((ne, 1), F32)],
        compiler_params=_cparams(("arbitrary",)),
        name="router",
    )(lg_p, lg_s, router_bias.astype(F32).reshape(ne, 1))


def _moe_plan(eidx, posk, counts, n_tiles_max):
    ne = counts.shape[0]
    ntile = (counts + EXPERT_TILE - 1) // EXPERT_TILE
    cum = jnp.cumsum(ntile)
    base = (cum - ntile) * EXPERT_TILE
    n_tiles = cum[-1:]
    experts = jnp.arange(ne, dtype=jnp.int32)
    last_used = jnp.max(jnp.where(ntile > 0, experts, 0))
    tiles = jnp.arange(n_tiles_max, dtype=jnp.int32)
    tile_expert = jnp.sum(tiles[:, None] >= cum[None, :], axis=1)
    tile_expert = jnp.minimum(tile_expert, last_used).astype(jnp.int32)
    partial = jnp.any((tiles[:, None] == cum[None, :] - 1) & (counts % EXPERT_TILE != 0)[None, :],
                      axis=1)
    zfill = (partial | (tiles >= n_tiles[0])).astype(jnp.int32)
    slot = jnp.sum(jnp.where(eidx[..., None] == experts, base, 0), axis=-1) + posk
    return slot.astype(jnp.int32), tile_expert, n_tiles.astype(jnp.int32), zfill


def _row_copy_wait(ref, n_rows, sem):
    pltpu.make_async_copy(ref.at[pl.ds(0, n_rows)], ref.at[pl.ds(0, n_rows)], sem).wait()


def _dispatch_kernel(zfill_ref, slot_ref, xp_ref, xs_ref, xs_hbm, zero_scr, sem, zsem, *, n_first):
    tm = xp_ref.shape[0]
    i = pl.program_id(0)

    @pl.when(i == 0)
    def _():
        zero_scr[...] = jnp.zeros_like(zero_scr)

        def zero_tile(n):
            row0 = pl.multiple_of(n * EXPERT_TILE, EXPERT_TILE)
            return pltpu.make_async_copy(zero_scr, xs_hbm.at[pl.ds(row0, EXPERT_TILE)], zsem)

        def fill(n, _):
            @pl.when(zfill_ref[n] != 0)
            def _():
                zero_tile(n).start()
            return 0

        def drain(n, _):
            @pl.when(zfill_ref[n] != 0)
            def _():
                zero_tile(n).wait()
            return 0

        lax.fori_loop(0, zfill_ref.shape[0], fill, 0)
        lax.fori_loop(0, zfill_ref.shape[0], drain, 0)

    def scatter_rows(x_ref):
        def body(r, _):
            for k in range(TOP_K):
                s = slot_ref[0, 0, r * TOP_K + k]
                pltpu.make_async_copy(x_ref.at[pl.ds(r, 1)], xs_hbm.at[pl.ds(s, 1)],
                                      sem).start(priority=k % 2)
            return 0
        lax.fori_loop(0, tm, body, 0)

    @pl.when(i < n_first)
    def _():
        scatter_rows(xp_ref)

    @pl.when(i >= n_first)
    def _():
        scatter_rows(xs_ref)

    _row_copy_wait(xs_hbm, tm * TOP_K, sem)


def _dispatch(xn_p, xn_s, slot_tiles, zfill, n_rows):
    w = xn_p.shape[1]
    t = xn_p.shape[0] + xn_s.shape[0]
    tm = MOE_TOKEN_TILE
    n_first = xn_p.shape[0] // tm
    return pl.pallas_call(
        functools.partial(_dispatch_kernel, n_first=n_first),
        grid_spec=pltpu.PrefetchScalarGridSpec(
            num_scalar_prefetch=1,
            grid=(t // tm,),
            in_specs=[pl.BlockSpec((1, 1, tm * TOP_K), lambda i, *_: (i, 0, 0),
                                   memory_space=pltpu.SMEM)] + _pair_specs((tm, w), 0, n_first),
            out_specs=pl.BlockSpec(memory_space=pl.ANY),
            scratch_shapes=[pltpu.VMEM((EXPERT_TILE, w), F32),
                            pltpu.SemaphoreType.DMA, pltpu.SemaphoreType.DMA]),
        out_shape=jax.ShapeDtypeStruct((n_rows, w), F32),
        compiler_params=_cparams(("arbitrary",)),
        name="moe_dispatch",
    )(zfill, slot_tiles, xn_p, xn_s)


def _expert_kernel(te_ref, nt_ref, xs_ref, wg_ref, wu_ref, wd_ref, ys_ref, wg_scr, wu_scr, wd_scr):
    i = pl.program_id(0)

    @pl.when(i < nt_ref[0])
    def _():
        @pl.when((i == 0) | (te_ref[i] != te_ref[jnp.maximum(i - 1, 0)]))
        def _():
            wg_scr[...] = wg_ref[0].astype(BF16)
            wu_scr[...] = wu_ref[0].astype(BF16)
            wd_scr[...] = wd_ref[0].astype(BF16)

        x = xs_ref[...].astype(BF16)
        h = jax.nn.silu(_dot(x, wg_scr[...])) * _dot(x, wu_scr[...])
        ys_ref[...] = _dot(h.astype(BF16), wd_scr[...])

    @pl.when(i >= nt_ref[0])
    def _():
        ys_ref[...] = jnp.zeros_like(ys_ref)


def _experts(xs, tile_expert, n_tiles, w_gate, w_up, w_down):
    n_rows, w = xs.shape
    ne, d, de = w_gate.shape
    tile = lambda i, te, nt: (jnp.minimum(i, nt[0] - 1), 0)
    out_tile = lambda i, te, nt: (i, 0)
    wsel = lambda i, te, nt: (te[i], 0, 0)
    return pl.pallas_call(
        _expert_kernel,
        grid_spec=pltpu.PrefetchScalarGridSpec(
            num_scalar_prefetch=2,
            grid=(n_rows // EXPERT_TILE,),
            in_specs=[pl.BlockSpec((EXPERT_TILE, w), tile),
                      pl.BlockSpec((1, d, de), wsel), pl.BlockSpec((1, d, de), wsel),
                      pl.BlockSpec((1, de, d), wsel)],
            out_specs=pl.BlockSpec((EXPERT_TILE, w), out_tile),
            scratch_shapes=[pltpu.VMEM((d, de), BF16), pltpu.VMEM((d, de), BF16),
                            pltpu.VMEM((de, d), BF16)]),
        out_shape=jax.ShapeDtypeStruct((n_rows, w), F32),
        compiler_params=_cparams(("arbitrary",)),
        name="moe_experts",
    )(tile_expert, n_tiles, xs, w_gate, w_up, w_down)


def _combine_kernel(slot_ref, gate_ref, xnp_ref, xns_ref, x2p_ref, x2s_ref, sg_ref, su_ref, sd_ref,
                    gf_ref, ys_hbm, yp_ref, ysamp_ref, buf_a, buf_b, base_scr, y_scr, sem_a, sem_b,
                    *, n_first):
    tm = xnp_ref.shape[0]
    i = pl.program_id(0)
    n_steps = pl.num_programs(0)
    first = i < n_first
    rb_rows = SUBLANES

    def issue_rows(tile, buf, sem, r0, n):
        for rr in range(n):
            for k in range(TOP_K):
                s = slot_ref[(tile * tm + r0 + rr) * TOP_K + k]
                pltpu.make_async_copy(ys_hbm.at[pl.ds(s, 1)], buf.at[k, pl.ds(r0 + rr, 1)],
                                      sem).start(priority=k % 2)

    @pl.when(i == 0)
    def _():
        def body(r, _):
            issue_rows(0, buf_a, sem_a, r, 1)
            return 0
        lax.fori_loop(0, tm, body, 0)

    x = _pick(first, xnp_ref, xns_ref).astype(BF16)
    hs = jax.nn.silu(_dot(x, sg_ref[...])) * _dot(x, su_ref[...])
    base_scr[...] = _pick(first, x2p_ref, x2s_ref) + _dot(hs.astype(BF16), sd_ref[...])
    nxt_tile = jnp.where(i + 1 < n_steps, i + 1, 0)

    def run(cur, sem_cur, nxt, sem_nxt):
        _row_copy_wait(ys_hbm, tm * TOP_K, sem_cur)

        for rb in range(tm // rb_rows):
            r0 = rb * rb_rows
            issue_rows(nxt_tile, nxt, sem_nxt, r0, rb_rows)
            acc = base_scr[pl.ds(r0, rb_rows), :]
            g = gate_ref[pl.ds(r0, rb_rows), :]
            for k in range(TOP_K):
                acc = acc + g[:, k:k + 1] * cur[k, pl.ds(r0, rb_rows), :]
            y_scr[pl.ds(r0, rb_rows), :] = _rms(acc, gf_ref[...])

        @pl.when(i == n_steps - 1)
        def _():
            _row_copy_wait(ys_hbm, tm * TOP_K, sem_nxt)

    @pl.when(i % 2 == 0)
    def _():
        run(buf_a, sem_a, buf_b, sem_b)

    @pl.when(i % 2 == 1)
    def _():
        run(buf_b, sem_b, buf_a, sem_a)

    @pl.when(first)
    def _():
        yp_ref[...] = y_scr[...]

    @pl.when(jnp.logical_not(first))
    def _():
        ysamp_ref[...] = y_scr[...]


def _combine(slot_flat, gates, xn_p, xn_s, x2_p, x2_s, ws_gate, ws_up, ws_down, g_final, ys):
    d = x2_p.shape[1]
    tp, tsamp = x2_p.shape[0], x2_s.shape[0]
    dsh = ws_gate.shape[1]
    tm = COMBINE_TILE
    n_first = tp // tm
    row = lambda i, *_: (i, 0)
    pair = lambda: _pair_specs((tm, d), 0, n_first)
    return pl.pallas_call(
        functools.partial(_combine_kernel, n_first=n_first),
        grid_spec=pltpu.PrefetchScalarGridSpec(
            num_scalar_prefetch=1,
            grid=((tp + tsamp) // tm,),
            in_specs=[pl.BlockSpec((tm, TOP_K), row)] + pair() + pair()
                     + [_const_spec((d, dsh)), _const_spec((d, dsh)), _const_spec((dsh, d)),
                        _const_spec((1, d)), pl.BlockSpec(memory_space=pl.ANY)],
            out_specs=pair(),
            scratch_shapes=[pltpu.VMEM((TOP_K, tm, d), F32), pltpu.VMEM((TOP_K, tm, d), F32),
                            pltpu.VMEM((tm, d), F32), pltpu.VMEM((tm, d), F32),
                            pltpu.SemaphoreType.DMA, pltpu.SemaphoreType.DMA]),
        out_shape=[jax.ShapeDtypeStruct((tp, d), F32), jax.ShapeDtypeStruct((tsamp, d), F32)],
        compiler_params=_cparams(("arbitrary",)),
        name="moe_combine",
    )(slot_flat, gates, xn_p, xn_s, x2_p, x2_s, ws_gate.astype(BF16), ws_up.astype(BF16),
      ws_down.astype(BF16), g_final.reshape(1, d), ys)


def _moe(xn_p, xn_s, x2_p, x2_s, eidx, gate, posk, counts, w_gate, w_up, w_down, ws_gate, ws_up,
         ws_down, g_final):
    t = xn_p.shape[0] + xn_s.shape[0]
    ne = w_gate.shape[0]
    n_tiles_max = t * TOP_K // EXPERT_TILE + ne
    slot, tile_expert, n_tiles, zfill = _moe_plan(eidx, posk, counts[:, 0], n_tiles_max)
    slot_tm = jnp.transpose(slot)
    slot_tiles = slot_tm.reshape(t // MOE_TOKEN_TILE, 1, MOE_TOKEN_TILE * TOP_K)
    xs = _dispatch(xn_p, xn_s, slot_tiles, zfill, n_tiles_max * EXPERT_TILE)
    ys = _experts(xs, tile_expert, n_tiles, w_gate, w_up, w_down)
    return _combine(slot_tm.reshape(t * TOP_K), jnp.transpose(gate), xn_p, xn_s, x2_p, x2_s,
                    ws_gate, ws_up, ws_down, g_final, ys)


def kernel(x_prompt, x_sample, cache_mem_k, cache_mem_v, state_s5_re, state_s5_im, state_gla, mem_prompt, w_in, w_a2, b_a2, lam_re, lam_im, log_dt, b_re, b_im, c_re, c_im, d_skip, w_glu, b_glu, g_s5_out, g_gla_head, w_out, g_mix, g_xattn, g_mem, w_q, w_k, w_v, w_o, g_ffn, w_router, router_bias, w_gate, w_up, w_down, ws_gate, ws_up, ws_down, g_final):
    depth = w_in.shape[0]
    assert depth == 1, "one trunk layer"
    bp, lp, d = x_prompt.shape
    bs, ls, _ = x_sample.shape
    assert bp == 1 and ls == CHUNK and lp % CHUNK == 0
    n_mem = mem_prompt.shape[1]
    g, p = lam_re.shape[1:]
    d_s5 = g * S5_GROUP
    d_qk = GLA_HEADS * GLA_DK
    d_gla = w_out.shape[1] - d_s5
    ns = d_s5 // LANES
    ne = w_router.shape[2]
    tp, tsamp = bp * lp, bs * ls
    xp = x_prompt.reshape(tp, d)
    xs = x_sample.reshape(tsamp, d)
    sq = lambda a: a.reshape(a.shape[1:])

    mk_p, mv_p = _memory_kv(mem_prompt.reshape(n_mem, d), sq(g_mem), sq(w_k).astype(BF16),
                            sq(w_v).astype(BF16))

    u, q, k, v, a, r = _in_proj(xp, xs, sq(g_mix), sq(w_in), d_s5, d_qk, d_gla)

    ksm, psm, qsm, step_m, carry_m, a8 = _s5_coeffs(
        sq(lam_re), sq(lam_im), sq(log_dt), sq(b_re), sq(b_im), sq(c_re), sq(c_im), sq(d_skip))
    tables = tuple(_s5_expand(ksm, psm, qsm)) + (step_m, carry_m, a8)
    zero_h = jnp.zeros((ns, bp, 2 * GROUPS_PER_SLAB * p), F32)
    rows_p = lp // S5_BLOCK
    y_p, h_p = _s5_mixer(u, tables, zero_h, 0, tp, rows_per_seq=rows_p,
                         tile_rows=math.gcd(rows_p, 512))
    h0_s = _pack_s5_state(sq(state_s5_re), sq(state_s5_im), ns)
    y_s, h_s = _s5_mixer(u, tables, h0_s, tp, tsamp, rows_per_seq=ls // S5_BLOCK,
                         tile_rows=tsamp // S5_BLOCK)

    wa2 = jnp.pad(sq(w_a2), ((0, LANES - GLA_RANK), (0, 0)))
    zero_s = jnp.zeros((bp,) + state_gla.shape[2:], F32)
    gla_p, sg_p = _gla_mixer(q, k, v, a, r, wa2, sq(b_a2), sq(g_gla_head), zero_s, 0, tp,
                             carry=True, n_chunks=4)
    gla_s, sg_s = _gla_mixer(q, k, v, a, r, wa2, sq(b_a2), sq(g_gla_head), sq(state_gla), tp,
                             tsamp, carry=False, n_chunks=4)

    x1, xq = _mix_out(y_p, y_s, gla_p, gla_s, xp, xs, sq(w_glu), sq(b_glu), sq(g_s5_out),
                      sq(w_out), sq(g_xattn), sq(w_q))

    wr_hi, wr_lo = _split_bf16(jnp.transpose(sq(w_router)))
    wo = sq(w_o).astype(BF16)
    x2_p, xn_p, lg_p = _xattn(xq, x1, mk_p[None], mv_p[None], wo, sq(g_ffn), wr_hi, wr_lo,
                              0, tp, nb=1, lt=TOKEN_TILE)
    x2_s, xn_s, lg_s = _xattn(xq, x1, cache_mem_k.reshape(bs, n_mem, d),
                              cache_mem_v.reshape(bs, n_mem, d), wo, sq(g_ffn), wr_hi, wr_lo,
                              tp, tsamp, nb=2, lt=ls)

    eidx, gate, posk, counts = _route(lg_p, lg_s, sq(router_bias))
    y_p2, y_s2 = _moe(xn_p, xn_s, x2_p, x2_s, eidx, gate, posk, counts, sq(w_gate), sq(w_up),
                      sq(w_down), sq(ws_gate), sq(ws_up), sq(ws_down), g_final)

    xh = d // X_HEADS
    re_p, im_p = _unpack_s5_state(h_p, g, p)
    re_s, im_s = _unpack_s5_state(h_s, g, p)
    return (y_p2.reshape(bp, lp, d), y_s2.reshape(bs, ls, d),
            mk_p.reshape(1, bp, n_mem, X_HEADS, xh), mv_p.reshape(1, bp, n_mem, X_HEADS, xh),
            re_p[None], im_p[None], sg_p[None], re_s[None], im_s[None], sg_s[None])
```

```python
import functools
import math

import jax
import jax.numpy as jnp
from jax import lax
from jax.experimental import pallas as pl
from jax.experimental.pallas import tpu as pltpu

F32 = jnp.float32
BF16 = jnp.bfloat16

EPS = 1e-6
CHUNK = 64
S5_GROUP = 16
GLA_HEADS = 4
GLA_DK = 128
GLA_RANK = 16
GLA_TAU = 16.0
X_HEADS = 4
N_GROUPS = 8
TOPK_GROUPS = 4
TOP_K = 8
ROUTE_SCALE = 2.5

LANES = 128
SUBLANES = 8
S5_BLOCK = SUBLANES
GROUPS_PER_SLAB = LANES // S5_GROUP
TOKEN_TILE = 512
EXPERT_TILE = 256
MOE_TOKEN_TILE = 256
COMBINE_TILE = 128
VMEM_LIMIT = 56 * 1024 * 1024


def _cparams(sem):
    return pltpu.CompilerParams(dimension_semantics=sem, vmem_limit_bytes=VMEM_LIMIT)


def _const_spec(shape):
    nd = len(shape)
    return pl.BlockSpec(shape, lambda *_: (0,) * nd, pipeline_mode=pl.Buffered(1))


def _pair_specs(block, axis, n_first):
    def at(f):
        return lambda i, *_: tuple(f(i) if a == axis else 0 for a in range(len(block)))
    return [pl.BlockSpec(block, at(lambda i: jnp.minimum(i, n_first - 1))),
            pl.BlockSpec(block, at(lambda i: jnp.maximum(i - n_first, 0)))]


def _pick(first, a_ref, b_ref):
    return jnp.where(first, a_ref[...], b_ref[...])


def _rms(x, g):
    return x * lax.rsqrt(jnp.mean(x * x, axis=-1, keepdims=True) + EPS) * g


def _split_bf16(x):
    hi = x.astype(BF16)
    lo = (x - hi.astype(F32)).astype(BF16)
    return hi, lo


def _split3_bf16(x):
    a = x.astype(BF16)
    r = x - a.astype(F32)
    b = r.astype(BF16)
    c = (r - b.astype(F32)).astype(BF16)
    return a, b, c


def _dot(a, b):
    return jnp.dot(a, b, preferred_element_type=F32)


def _dot_nt(a, b):
    return lax.dot_general(a, b, (((1,), (1,)), ((), ())), preferred_element_type=F32)


def _dot_tn(a, b):
    return lax.dot_general(a, b, (((0,), (0,)), ((), ())), preferred_element_type=F32)


def _memkv_kernel(mem_ref, g_ref, wk_ref, wv_ref, mk_ref, mv_ref):
    m = _rms(mem_ref[...], g_ref[...]).astype(BF16)
    mk_ref[...] = _dot(m, wk_ref[...])
    mv_ref[...] = _dot(m, wv_ref[...])


def _memory_kv(mem, g_mem, wk, wv):
    n, d = mem.shape
    tn = 512
    return pl.pallas_call(
        _memkv_kernel,
        grid=(d // tn,),
        in_specs=[_const_spec((n, d)), _const_spec((1, d)),
                  pl.BlockSpec((d, tn), lambda j: (0, j)),
                  pl.BlockSpec((d, tn), lambda j: (0, j))],
        out_specs=[pl.BlockSpec((n, tn), lambda j: (0, j))] * 2,
        out_shape=[jax.ShapeDtypeStruct((n, d), F32)] * 2,
        compiler_params=_cparams(("arbitrary",)),
        name="memory_kv",
    )(mem, g_mem.reshape(1, d), wk, wv)


def _inproj_kernel(xp_ref, xs_ref, g_ref, wuh_ref, wul_ref, wqkv_ref, wa_ref, wr_ref,
                   u_ref, q_ref, k_ref, v_ref, a_ref, r_ref, *, n_first):
    first = pl.program_id(0) < n_first
    x = _pick(first, xp_ref, xs_ref)
    nx = _rms(x, g_ref[...])
    hi, lo = _split_bf16(nx)
    u = _dot(hi, wuh_ref[...])
    for j in range(u_ref.shape[0]):
        u_ref[j] = u[:, j * LANES:(j + 1) * LANES]

    @pl.when(jnp.logical_not(first))
    def _():
        fix = _dot(lo, wuh_ref[...]) + _dot(hi, wul_ref[...])
        for j in range(u_ref.shape[0]):
            u_ref[j] += fix[:, j * LANES:(j + 1) * LANES]

    qkv = _dot(hi, wqkv_ref[...])
    dq = q_ref.shape[1]
    q_ref[...] = qkv[:, :dq].astype(BF16)
    k_ref[...] = qkv[:, dq:2 * dq].astype(BF16)
    v_ref[...] = qkv[:, 2 * dq:].astype(BF16)
    a_ref[...] = _dot(hi, wa_ref[...]) + _dot(lo, wa_ref[...])
    r_ref[...] = _dot(hi, wr_ref[...]).astype(BF16)


def _in_proj(xp, xs, g_mix, w_in, d_s5, d_qk, d_gla):
    d = xp.shape[1]
    t = xp.shape[0] + xs.shape[0]
    tm = TOKEN_TILE
    n_first = xp.shape[0] // tm
    o1 = d_s5
    o2 = o1 + 2 * d_qk + d_gla
    o3 = o2 + GLA_RANK
    wu_hi, wu_lo = _split_bf16(w_in[:, :o1])
    wqkv = w_in[:, o1:o2].astype(BF16)
    wa = jnp.pad(w_in[:, o2:o3], ((0, 0), (0, LANES - GLA_RANK))).astype(BF16)
    wr = w_in[:, o3:].astype(BF16)
    n_slab = d_s5 // LANES
    row = lambda i: (i, 0)
    return pl.pallas_call(
        functools.partial(_inproj_kernel, n_first=n_first),
        grid=(t // tm,),
        in_specs=_pair_specs((tm, d), 0, n_first) + [
            _const_spec((1, d)), _const_spec(wu_hi.shape), _const_spec(wu_lo.shape),
            _const_spec(wqkv.shape), _const_spec(wa.shape), _const_spec(wr.shape)],
        out_specs=[pl.BlockSpec((n_slab, tm, LANES), lambda i: (0, i, 0)),
                   pl.BlockSpec((tm, d_qk), row), pl.BlockSpec((tm, d_qk), row),
                   pl.BlockSpec((tm, d_gla), row), pl.BlockSpec((tm, LANES), row),
                   pl.BlockSpec((tm, d_gla), row)],
        out_shape=[jax.ShapeDtypeStruct((n_slab, t, LANES), F32),
                   jax.ShapeDtypeStruct((t, d_qk), BF16), jax.ShapeDtypeStruct((t, d_qk), BF16),
                   jax.ShapeDtypeStruct((t, d_gla), BF16), jax.ShapeDtypeStruct((t, LANES), F32),
                   jax.ShapeDtypeStruct((t, d_gla), BF16)],
        compiler_params=_cparams(("arbitrary",)),
        name="in_proj",
    )(xp, xs, g_mix.reshape(1, d), wu_hi, wu_lo, wqkv, wa, wr)


def _s5_coeffs(lam_re, lam_im, log_dt, b_re, b_im, c_re, c_im, d_skip):
    g, p = lam_re.shape
    h = b_re.shape[-1]
    nb = S5_BLOCK
    ns = g // GROUPS_PER_SLAB
    gl = GROUPS_PER_SLAB
    dt = jnp.exp(log_dt.astype(F32))[:, None]
    lr, li = lam_re.astype(F32), lam_im.astype(F32)
    mag = jnp.exp(lr * dt)
    ar, ai = mag * jnp.cos(li * dt), mag * jnp.sin(li * dt)
    den = lr * lr + li * li
    nr = ar - 1.0
    cf_r = (nr * lr + ai * li) / den
    cf_i = (ai * lr - nr * li) / den
    bb_r = cf_r[..., None] * b_re - cf_i[..., None] * b_im
    bb_i = cf_r[..., None] * b_im + cf_i[..., None] * b_re

    def cpow(n):
        e = n[:, None, None] * dt[None]
        m = jnp.exp(lr[None] * e)
        return m * jnp.cos(li[None] * e), m * jnp.sin(li[None] * e)

    pr, pi = cpow(jnp.arange(nb + 1, dtype=F32))

    bt_r, bt_i = jnp.swapaxes(bb_r, 1, 2), jnp.swapaxes(bb_i, 1, 2)
    ct_r, ct_i = jnp.swapaxes(c_re, 1, 2), jnp.swapaxes(c_im, 1, 2)

    cb_r = jnp.einsum('gcp,gph->gphc', c_re, bb_r) - jnp.einsum('gcp,gph->gphc', c_im, bb_i)
    cb_i = jnp.einsum('gcp,gph->gphc', c_re, bb_i) + jnp.einsum('gcp,gph->gphc', c_im, bb_r)
    taps = (jnp.einsum('ngp,gphc->nghc', pr[:nb], cb_r)
            - jnp.einsum('ngp,gphc->nghc', pi[:nb], cb_i))
    skip = d_skip[None, :, :, None] * jnp.eye(h, dtype=F32)[None, None]
    taps = taps + jnp.where(jnp.arange(nb)[:, None, None, None] == 0, skip, 0.0)
    ksm = jnp.swapaxes(taps.reshape(nb, ns, gl * h, h), 0, 1)

    wr, wi = pr[nb - 1::-1][:nb], pi[nb - 1::-1][:nb]
    wr, wi = wr[:, :, None, :], wi[:, :, None, :]
    inj = jnp.concatenate([wr * bt_r[None] - wi * bt_i[None],
                           wr * bt_i[None] + wi * bt_r[None]], axis=-1)
    psm = jnp.swapaxes(inj.reshape(nb, ns, gl * h, 2 * p), 0, 1)

    er = jnp.transpose(pr[1:nb + 1], (1, 2, 0))[..., None]
    ei = jnp.transpose(pi[1:nb + 1], (1, 2, 0))[..., None]
    q_r = ct_r[:, :, None, :] * er - ct_i[:, :, None, :] * ei
    q_i = -(ct_r[:, :, None, :] * ei + ct_i[:, :, None, :] * er)
    qq = jnp.stack([q_r.reshape(ns, gl * p, nb * h), q_i.reshape(ns, gl * p, nb * h)], axis=1)
    qsm = qq.reshape(ns, 2 * gl * p, nb * h)

    def slab(xr, xi):
        k = xr.shape[0]
        xr = jnp.transpose(xr.reshape(k, ns, gl * p), (1, 0, 2))
        xi = jnp.transpose(xi.reshape(k, ns, gl * p), (1, 0, 2))
        return jnp.concatenate([xr, xi], axis=-1)

    rows = jnp.arange(SUBLANES, dtype=F32)
    cr, ci = cpow(nb * rows)
    carry_m = slab(cr, ci)
    sr, si = cpow(nb * jnp.array([1.0, 2.0, 4.0], F32))
    keep = (rows[None, :] >= jnp.array([1.0, 2.0, 4.0], F32)[:, None]).astype(F32)
    step_m = slab((sr[:, None] * keep[:, :, None, None]).reshape(3 * SUBLANES, g, p),
                  (si[:, None] * keep[:, :, None, None]).reshape(3 * SUBLANES, g, p))
    step_m = step_m.reshape(ns, 3, SUBLANES, 2 * gl * p)
    a8 = carry_m[:, 1:2]
    return ksm, psm, qsm, step_m, carry_m, a8


def _s5_expand_kernel(ksm_ref, psm_ref, qsm_ref, tm_ref, ph_ref, pl_ref, qm_ref):
    nb = S5_BLOCK
    h = S5_GROUP
    w = ph_ref.shape[2]
    p = w // (2 * GROUPS_PER_SLAB)

    def spread(x, sel, keep):
        a, b, c = _split3_bf16(x)
        return jnp.where(keep, _dot(a, sel) + _dot(b, sel) + _dot(c, sel), 0.0)

    def iota(shape, axis):
        return lax.broadcasted_iota(jnp.int32, shape, axis)

    sel_c = (iota((h, LANES), 1) % h == iota((h, LANES), 0)).astype(BF16)
    keep_t = iota((LANES, LANES), 0) // h == iota((LANES, LANES), 1) // h
    tm_ref[...] = jnp.zeros_like(tm_ref)
    for tau in range(nb):
        blk = spread(ksm_ref[0, tau], sel_c, keep_t).astype(BF16)
        for s in range(nb - tau):
            t = s + tau
            tm_ref[0, s * LANES:(s + 1) * LANES, t * LANES:(t + 1) * LANES] = blk

    half = w // 2
    keep_p = iota((LANES, half), 0) // h == iota((LANES, half), 1) // p
    for ri in range(2):
        sel_p = (iota((2 * p, half), 0) == iota((2 * p, half), 1) % p + ri * p).astype(BF16)
        for s in range(nb):
            blk = spread(psm_ref[0, s], sel_p, keep_p)
            hi, lo = _split_bf16(blk)
            ph_ref[0, s * LANES:(s + 1) * LANES, ri * half:(ri + 1) * half] = hi
            pl_ref[0, s * LANES:(s + 1) * LANES, ri * half:(ri + 1) * half] = lo

    keep_q = (iota((w, LANES), 0) % half) // p == iota((w, LANES), 1) // h
    for t in range(nb):
        sel_q = (iota((nb * h, LANES), 0) == iota((nb * h, LANES), 1) % h + t * h).astype(BF16)
        qm_ref[0, :, t * LANES:(t + 1) * LANES] = spread(qsm_ref[0], sel_q, keep_q).astype(BF16)


def _s5_expand(ksm, psm, qsm):
    ns = ksm.shape[0]
    w = qsm.shape[1]
    k = S5_BLOCK * LANES
    blk = lambda shape: pl.BlockSpec((1,) + shape, lambda j: (j,) + (0,) * len(shape))
    return pl.pallas_call(
        _s5_expand_kernel,
        grid=(ns,),
        in_specs=[blk(ksm.shape[1:]), blk(psm.shape[1:]), blk(qsm.shape[1:])],
        out_specs=[blk((k, k)), blk((k, w)), blk((k, w)), blk((w, k))],
        out_shape=[jax.ShapeDtypeStruct((ns, k, k), BF16), jax.ShapeDtypeStruct((ns, k, w), BF16),
                   jax.ShapeDtypeStruct((ns, k, w), BF16), jax.ShapeDtypeStruct((ns, w, k), BF16)],
        compiler_params=_cparams(("arbitrary",)),
        name="s5_tables",
    )(ksm, psm, qsm)


def _cmul(xr, xi, mr, mi):
    return xr * mr - xi * mi, xr * mi + xi * mr


def _s5_kernel(u_ref, ph_ref, pl_ref, tm_ref, qm_ref, step_ref, carry_ref, a8_ref, h0_ref,
               y_ref, hout_ref, s_scr, hin_scr, c_scr, *, rows_per_seq):
    ti = pl.program_id(1)
    rows = s_scr.shape[0]
    half = s_scr.shape[1] // 2
    nb = S5_BLOCK
    u8 = jnp.concatenate([u_ref[0, pl.ds(s, rows, stride=nb), :] for s in range(nb)], axis=1)
    hi, lo = _split_bf16(u8)
    per_block_seq = rows_per_seq == SUBLANES
    if per_block_seq:
        s_scr[...] = _dot(hi, ph_ref[0]) + _dot(lo, ph_ref[0]) + _dot(hi, pl_ref[0])
    else:
        s_scr[...] = _dot(hi, ph_ref[0])

    if not per_block_seq:
        @pl.when(ti == 0)
        def _():
            c_scr[...] = h0_ref[0]

    not_first = (lax.broadcasted_iota(jnp.int32, (SUBLANES, 1), 0) >= 1).astype(F32)
    a8r, a8i = a8_ref[0, :, :half], a8_ref[0, :, half:]
    cmr, cmi = carry_ref[0, :, :half], carry_ref[0, :, half:]

    def body(rb, _):
        r0 = pl.multiple_of(rb * SUBLANES, SUBLANES)
        sb = s_scr[pl.ds(r0, SUBLANES), :]
        x = pltpu.roll(sb, 1, 0) * not_first
        xr, xi = x[:, :half], x[:, half:]
        for k, sh in enumerate((1, 2, 4)):
            m = step_ref[0, k]
            pr, pi = _cmul(pltpu.roll(xr, sh, 0), pltpu.roll(xi, sh, 0), m[:, :half], m[:, half:])
            xr, xi = xr + pr, xi + pi
        c = h0_ref[0, pl.ds(rb, 1), :] if per_block_seq else c_scr[...]
        cr, ci = c[:, :half], c[:, half:]
        pr, pi = _cmul(cr, ci, cmr, cmi)
        hr, hi_ = xr + pr, xi + pi
        hin_scr[pl.ds(r0, SUBLANES), :half] = hr
        hin_scr[pl.ds(r0, SUBLANES), half:] = hi_
        nr, ni = _cmul(hr[SUBLANES - 1:], hi_[SUBLANES - 1:], a8r, a8i)
        cn = jnp.concatenate([nr, ni], axis=1) + sb[SUBLANES - 1:]
        if per_block_seq:
            hout_ref[0, pl.ds(rb, 1), :] = cn
        else:
            c_scr[...] = cn
        return 0

    lax.fori_loop(0, rows // SUBLANES, body, 0)
    if not per_block_seq:
        hout_ref[0] = c_scr[...]

    y8 = _dot(hi, tm_ref[0]) + _dot(hin_scr[...].astype(BF16), qm_ref[0])
    for t in range(nb):
        y_ref[0, pl.ds(t, rows, stride=nb), :] = y8[:, t * LANES:(t + 1) * LANES]


def _s5_mixer(u_slabs, tables, h0, t_start, t, rows_per_seq, tile_rows):
    tm, pm_hi, pm_lo, qm, step_m, carry_m, a8 = tables
    ns = u_slabs.shape[0]
    rows = t // S5_BLOCK
    n_tiles = rows // tile_rows
    assert rows % tile_rows == 0 and t_start % (tile_rows * S5_BLOCK) == 0
    off = t_start // (tile_rows * S5_BLOCK)
    n_seq = h0.shape[1]
    w = pm_hi.shape[-1]
    slab3 = lambda j, i: (j, 0, 0)
    kern = functools.partial(_s5_kernel, rows_per_seq=rows_per_seq)
    return pl.pallas_call(
        kern,
        grid=(ns, n_tiles),
        in_specs=[pl.BlockSpec((1, tile_rows * S5_BLOCK, LANES), lambda j, i: (j, i + off, 0)),
                  pl.BlockSpec((1,) + pm_hi.shape[1:], slab3),
                  pl.BlockSpec((1,) + pm_lo.shape[1:], slab3),
                  pl.BlockSpec((1,) + tm.shape[1:], slab3),
                  pl.BlockSpec((1,) + qm.shape[1:], slab3),
                  pl.BlockSpec((1,) + step_m.shape[1:], lambda j, i: (j, 0, 0, 0)),
                  pl.BlockSpec((1,) + carry_m.shape[1:], slab3),
                  pl.BlockSpec((1,) + a8.shape[1:], slab3),
                  pl.BlockSpec((1, n_seq, w), slab3)],
        out_specs=[pl.BlockSpec((1, tile_rows * S5_BLOCK, LANES), lambda j, i: (j, i, 0)),
                   pl.BlockSpec((1, n_seq, w), slab3)],
        out_shape=[jax.ShapeDtypeStruct((ns, t, LANES), F32),
                   jax.ShapeDtypeStruct((ns, n_seq, w), F32)],
        scratch_shapes=[pltpu.VMEM((tile_rows, w), F32), pltpu.VMEM((tile_rows, w), F32),
                        pltpu.VMEM((1, w), F32)],
        compiler_params=_cparams(("arbitrary", "arbitrary")),
        name="s5_mixer",
    )(u_slabs, pm_hi, pm_lo, tm, qm, step_m, carry_m, a8, h0)


def _pack_s5_state(re, im, ns):
    b = re.shape[0]
    r = re.astype(F32).reshape(b, ns, -1)
    i = im.astype(F32).reshape(b, ns, -1)
    return jnp.transpose(jnp.concatenate([r, i], axis=-1), (1, 0, 2))


def _unpack_s5_state(hc, g, p):
    ns, b, w = hc.shape
    hc = jnp.transpose(hc, (1, 0, 2))
    re = hc[:, :, :w // 2].reshape(b, g, p)
    im = hc[:, :, w // 2:].reshape(b, g, p)
    return re, im


def _gla_kernel(q_ref, k_ref, v_ref, a_ref, r_ref, wa2_ref, ba2_ref, gh_ref, s0_ref,
                o_ref, sout_ref, st_scr, *, carry, n_chunks):
    step = pl.program_id(0)
    dk = GLA_DK
    dv = v_ref.shape[1] // GLA_HEADS
    c = CHUNK
    scale = dk ** -0.5
    ri = lax.broadcasted_iota(jnp.int32, (c, c), 0)
    ci = lax.broadcasted_iota(jnp.int32, (c, c), 1)
    causal = ri >= ci
    tril = causal.astype(BF16)
    eye_dk = (lax.broadcasted_iota(jnp.int32, (dk, dk), 0)
              == lax.broadcasted_iota(jnp.int32, (dk, dk), 1))

    if carry:
        @pl.when(step == 0)
        def _():
            st_scr[...] = s0_ref[0]

    wa_hi, wa_lo = _split_bf16(wa2_ref[...])
    for n in range(n_chunks):
        rows = slice(n * c, (n + 1) * c)
        a_hi, a_lo = _split_bf16(a_ref[rows, :])
        logit = _dot(a_hi, wa_hi) + _dot(a_lo, wa_hi) + _dot(a_hi, wa_lo) + ba2_ref[...]
        g = jax.nn.log_sigmoid(logit) * (1.0 / GLA_TAU)
        g1, g2, g3 = _split3_bf16(g)
        bcum = _dot(tril, g1) + _dot(tril, g2) + _dot(tril, g3)
        for h in range(GLA_HEADS):
            ks = slice(h * dk, (h + 1) * dk)
            vs = slice(h * dv, (h + 1) * dv)
            b = bcum[:, ks]
            qh = q_ref[rows, ks].astype(F32) * scale
            kh = k_ref[rows, ks].astype(F32)
            vh = v_ref[rows, vs]
            state = s0_ref[n, h] if not carry else st_scr[h]
            qe = (qh * jnp.exp(b)).astype(BF16)
            ke = (kh * jnp.exp(-b)).astype(BF16)
            att = jnp.where(causal, _dot_nt(qe, ke), 0.0)
            o = _dot(att.astype(BF16), vh) + _dot(qe, state.astype(BF16))
            blast = b[c - 1:c, :]
            kd = (kh * jnp.exp(blast - b)).astype(BF16)
            decay = jnp.sum(jnp.where(eye_dk, jnp.exp(blast), 0.0), axis=1, keepdims=True)
            new_state = decay * state + _dot_tn(kd, vh)
            if carry:
                st_scr[h] = new_state
            else:
                sout_ref[n, h] = new_state
            on = _rms(o, gh_ref[...])
            rr = r_ref[rows, vs].astype(F32)
            o_ref[rows, vs] = (on * (rr * jax.nn.sigmoid(rr))).astype(BF16)
    if carry:
        sout_ref[0] = st_scr[...]


def _gla_mixer(q, k, v, a, r, wa2, ba2, g_head, s0, t_start, t, carry, n_chunks):
    dqk = q.shape[1]
    dvt = v.shape[1]
    hh, dk, dv = s0.shape[1:]
    rows = n_chunks * CHUNK
    assert t % rows == 0 and t_start % rows == 0
    off = t_start // rows
    row = lambda i: (i, 0)
    row_in = lambda i: (i + off, 0)
    if carry:
        sblk, smap = (1, hh, dk, dv), (lambda i: (0, 0, 0, 0))
    else:
        sblk, smap = (n_chunks, hh, dk, dv), (lambda i: (i, 0, 0, 0))
    kern = functools.partial(_gla_kernel, carry=carry, n_chunks=n_chunks)
    return pl.pallas_call(
        kern,
        grid=(t // rows,),
        in_specs=[pl.BlockSpec((rows, dqk), row_in), pl.BlockSpec((rows, dqk), row_in),
                  pl.BlockSpec((rows, dvt), row_in), pl.BlockSpec((rows, LANES), row_in),
                  pl.BlockSpec((rows, dvt), row_in),
                  _const_spec(wa2.shape), _const_spec((1, dqk)), _const_spec((1, dv)),
                  pl.BlockSpec(sblk, smap)],
        out_specs=[pl.BlockSpec((rows, dvt), row), pl.BlockSpec(sblk, smap)],
        out_shape=[jax.ShapeDtypeStruct((t, dvt), BF16), jax.ShapeDtypeStruct(s0.shape, F32)],
        scratch_shapes=[pltpu.VMEM((hh, dk, dv), F32)],
        compiler_params=_cparams(("arbitrary",)),
        name="gla_mixer",
    )(q, k, v, a, r, wa2, ba2.reshape(1, dqk), g_head.reshape(1, dv), s0)


def _mixout_kernel(yp_ref, ys_ref, gp_ref, gs_ref, xp_ref, xs_ref, wglu_ref, bglu_ref, gs5_ref,
                   wout_ref, gx_ref, wq_ref, x1_ref, q_ref, *, n_first):
    first = pl.program_id(0) < n_first
    yb = _pick(first, yp_ref, ys_ref)
    y = jnp.concatenate([yb[j] for j in range(yb.shape[0])], axis=1)
    z = jax.nn.gelu(y)
    gate = jax.nn.sigmoid(_dot(z.astype(BF16), wglu_ref[...]) + bglu_ref[...])
    s5 = _rms(z * gate, gs5_ref[...])
    cat = jnp.concatenate([s5.astype(BF16), _pick(first, gp_ref, gs_ref)], axis=1)
    x1 = _pick(first, xp_ref, xs_ref) + _dot(cat, wout_ref[...])
    x1_ref[...] = x1
    q_ref[...] = _dot(_rms(x1, gx_ref[...]).astype(BF16), wq_ref[...]).astype(BF16)


def _mix_out(y_p, y_s, gla_p, gla_s, xp, xs, w_glu, b_glu, g_s5, w_out, g_x, w_q):
    d = xp.shape[1]
    t = xp.shape[0] + xs.shape[0]
    ns = y_p.shape[0]
    ds5 = ns * LANES
    dg = gla_p.shape[1]
    tm = TOKEN_TILE // 2
    n_first = xp.shape[0] // tm
    row = lambda i: (i, 0)
    return pl.pallas_call(
        functools.partial(_mixout_kernel, n_first=n_first),
        grid=(t // tm,),
        in_specs=(_pair_specs((ns, tm, LANES), 1, n_first) + _pair_specs((tm, dg), 0, n_first)
                  + _pair_specs((tm, d), 0, n_first)
                  + [_const_spec((ds5, ds5)), _const_spec((1, ds5)), _const_spec((1, ds5)),
                     _const_spec((ds5 + dg, d)), _const_spec((1, d)), _const_spec((d, d))]),
        out_specs=[pl.BlockSpec((tm, d), row), pl.BlockSpec((tm, d), row)],
        out_shape=[jax.ShapeDtypeStruct((t, d), F32), jax.ShapeDtypeStruct((t, d), BF16)],
        compiler_params=_cparams(("arbitrary",)),
        name="mix_out",
    )(y_p, y_s, gla_p, gla_s, xp, xs, w_glu.astype(BF16), b_glu.reshape(1, ds5),
      g_s5.reshape(1, ds5), w_out.astype(BF16), g_x.reshape(1, d), w_q.astype(BF16))


def _xattn_kernel(q_ref, mk_ref, mv_ref, x1_ref, wo_ref, gf_ref, wrh_ref, wrl_ref,
                  x2_ref, xn_ref, lg_ref, *cache_scratch, nb):
    d = q_ref.shape[1]
    lt = q_ref.shape[0] // nb
    hd = d // X_HEADS
    scale = hd ** -0.5
    if cache_scratch:
        kbuf, vbuf, sem = cache_scratch
        i = pl.program_id(0)

        def fetch(step, slot):
            for b in range(nb):
                for h in range(X_HEADS):
                    for src, dst in ((mk_ref, kbuf), (mv_ref, vbuf)):
                        pltpu.make_async_copy(src.at[step * nb + b, :, h, :],
                                              dst.at[slot, b, :, pl.ds(h * hd, hd)],
                                              sem.at[slot]).start()

        @pl.when(i == 0)
        def _():
            fetch(0, 0)

        @pl.when(i + 1 < pl.num_programs(0))
        def _():
            fetch(i + 1, (i + 1) % 2)

        slot = i % 2
        for buf in (kbuf, vbuf):
            pltpu.make_async_copy(buf.at[slot], buf.at[slot], sem.at[slot]).wait()
        memory = lambda b: (kbuf[slot, b], vbuf[slot, b])
    else:
        memory = lambda b: (mk_ref[b], mv_ref[b])
    for b in range(nb):
        rows = slice(b * lt, (b + 1) * lt)
        mk, mv = (m.astype(BF16) for m in memory(b))
        outs = []
        for h in range(X_HEADS):
            hs = slice(h * hd, (h + 1) * hd)
            s = _dot_nt(q_ref[rows, hs], mk[:, hs]) * scale
            p = jnp.exp(s - jnp.max(s, axis=-1, keepdims=True))
            denom = jnp.sum(p, axis=-1, keepdims=True)
            outs.append((_dot(p.astype(BF16), mv[:, hs]) / denom).astype(BF16))
        o = jnp.concatenate(outs, axis=1)
        x2 = x1_ref[rows, :] + _dot(o, wo_ref[...])
        x2_ref[rows, :] = x2
        xn = _rms(x2, gf_ref[...])
        xn_ref[rows, :] = xn
        hi, lo = _split_bf16(xn)
        lg_ref[:, rows] = (_dot_nt(wrh_ref[...], hi) + _dot_nt(wrh_ref[...], lo)
                           + _dot_nt(wrl_ref[...], hi))


def _xattn(q, x1, mk, mv, w_o, g_ffn, wr_hi, wr_lo, t_start, t, nb, lt):
    d = q.shape[1]
    nm = mk.shape[1]
    ne = wr_hi.shape[0]
    rows = nb * lt
    assert t % rows == 0 and t_start % rows == 0
    off = t_start // rows
    row = lambda i: (i, 0)
    row_in = lambda i: (i + off, 0)
    if mk.ndim == 3:
        assert nb == 1 and mk.shape[0] == 1
        mem_spec = pl.BlockSpec((1, nm, d), lambda i: (0, 0, 0))
        scratch = []
    else:
        assert mk.shape[0] * lt == t
        mem_spec = pl.BlockSpec(memory_space=pl.ANY)
        scratch = [pltpu.VMEM((2, nb, nm, d), F32), pltpu.VMEM((2, nb, nm, d), F32),
                   pltpu.SemaphoreType.DMA((2,))]
    return pl.pallas_call(
        functools.partial(_xattn_kernel, nb=nb),
        grid=(t // rows,),
        in_specs=[pl.BlockSpec((rows, d), row_in), mem_spec, mem_spec,
                  pl.BlockSpec((rows, d), row_in),
                  _const_spec((d, d)), _const_spec((1, d)),
                  _const_spec((ne, d)), _const_spec((ne, d))],
        scratch_shapes=scratch,
        out_specs=[pl.BlockSpec((rows, d), row), pl.BlockSpec((rows, d), row),
                   pl.BlockSpec((ne, rows), lambda i: (0, i))],
        out_shape=[jax.ShapeDtypeStruct((t, d), F32), jax.ShapeDtypeStruct((t, d), F32),
                   jax.ShapeDtypeStruct((ne, t), F32)],
        compiler_params=_cparams(("arbitrary",)),
        name="mem_xattn",
    )(q, mk, mv, x1, w_o, g_ffn.reshape(1, d), wr_hi, wr_lo)


def _route_kernel(lgp_ref, lgs_ref, bias_ref, eidx_ref, gate_ref, pos_ref, cnt_ref, run_scr,
                  *, n_first):
    ne, tn = lgp_ref.shape

    @pl.when(pl.program_id(0) == 0)
    def _():
        run_scr[...] = jnp.zeros_like(run_scr)

    gsz = ne // N_GROUPS
    sc = jax.nn.sigmoid(_pick(pl.program_id(0) < n_first, lgp_ref, lgs_ref))
    sel = sc + bias_ref[...]
    s3 = sel.reshape(N_GROUPS, gsz, tn)
    ie = lax.broadcasted_iota(jnp.int32, s3.shape, 1).astype(F32)
    m1 = jnp.max(s3, axis=1, keepdims=True)
    first = jnp.min(jnp.where(s3 == m1, ie, float(gsz)), axis=1, keepdims=True)
    m2 = jnp.max(jnp.where(ie == first, -jnp.inf, s3), axis=1, keepdims=True)
    gs = m1 + m2
    ig = lax.broadcasted_iota(jnp.int32, gs.shape, 0)
    grank = jnp.zeros(gs.shape, jnp.int32)
    for g in range(N_GROUPS):
        other = gs[g:g + 1]
        ahead = jnp.where(other > gs, 1, jnp.where(other == gs, (ig > g).astype(jnp.int32), 0))
        grank = grank + ahead
    gkeep = jnp.broadcast_to(grank < TOPK_GROUPS, s3.shape)
    v = jnp.where(gkeep, s3, -jnp.inf).reshape(ne, tn)
    iv = lax.broadcasted_iota(jnp.int32, v.shape, 0)
    rank = jnp.zeros(v.shape, jnp.int32)
    for e in range(ne):
        other = v[e:e + 1]
        ahead = jnp.where(other > v, 1, jnp.where(other == v, (iv > e).astype(jnp.int32), 0))
        rank = rank + ahead
    chosen = rank < TOP_K
    gate = jnp.where(chosen, sc, 0.0)
    gate = gate / jnp.sum(gate, axis=0, keepdims=True) * ROUTE_SCALE
    ti = lax.broadcasted_iota(jnp.int32, (tn, tn), 0)
    tj = lax.broadcasted_iota(jnp.int32, (tn, tn), 1)
    incl = _dot(jnp.where(chosen, 1.0, 0.0).astype(BF16), (ti <= tj).astype(BF16))
    pos = run_scr[...] + incl - 1.0
    run_scr[...] = run_scr[...] + incl[:, tn - 1:tn]
    cnt_ref[...] = run_scr[...].astype(jnp.int32)
    ef = iv.astype(F32)
    for k in range(TOP_K):
        hit = rank == k
        pick = lambda a: jnp.sum(jnp.where(hit, a, 0.0), axis=0, keepdims=True)
        eidx_ref[k:k + 1, :] = pick(ef).astype(jnp.int32)
        gate_ref[k:k + 1, :] = pick(gate)
        pos_ref[k:k + 1, :] = pick(pos).astype(jnp.int32)


def _route(lg_p, lg_s, router_bias):
    ne = lg_p.shape[0]
    t = lg_p.shape[1] + lg_s.shape[1]
    tn = TOKEN_TILE
    n_first = lg_p.shape[1] // tn
    kt = lambda i: (0, i)
    return pl.pallas_call(
        functools.partial(_route_kernel, n_first=n_first),
        grid=(t // tn,),
        in_specs=_pair_specs((ne, tn), 1, n_first) + [_const_spec((ne, 1))],
        out_specs=[pl.BlockSpec((TOP_K, tn), kt), pl.BlockSpec((TOP_K, tn), kt),
                   pl.BlockSpec((TOP_K, tn), kt), pl.BlockSpec((ne, 1), lambda i: (0, 0))],
        out_shape=[jax.ShapeDtypeStruct((TOP_K, t), jnp.int32),
                   jax.ShapeDtypeStruct((TOP_K, t), F32),
                   jax.ShapeDtypeStruct((TOP_K, t), jnp.int32),
                   jax.ShapeDtypeStruct((ne, 1), jnp.int32)],
        scratch_shapes=[pltpu.VMEM((ne, 1), F32)],
        compiler_params=_cparams(("arbitrary",)),
        name="router",
    )(lg_p, lg_s, router_bias.astype(F32).reshape(ne, 1))


def _moe_plan(eidx, posk, counts, n_tiles_max):
    ne = counts.shape[0]
    ntile = (counts + EXPERT_TILE - 1) // EXPERT_TILE
    cum = jnp.cumsum(ntile)
    base = (cum - ntile) * EXPERT_TILE
    n_tiles = cum[-1:]
    experts = jnp.arange(ne, dtype=jnp.int32)
    last_used = jnp.max(jnp.where(ntile > 0, experts, 0))
    tiles = jnp.arange(n_tiles_max, dtype=jnp.int32)
    tile_expert = jnp.sum(tiles[:, None] >= cum[None, :], axis=1)
    tile_expert = jnp.minimum(tile_expert, last_used).astype(jnp.int32)
    partial = jnp.any((tiles[:, None] == cum[None, :] - 1) & (counts % EXPERT_TILE != 0)[None, :],
                      axis=1)
    zfill = (partial | (tiles >= n_tiles[0])).astype(jnp.int32)
    slot = jnp.sum(jnp.where(eidx[..., None] == experts, base, 0), axis=-1) + posk
    return slot.astype(jnp.int32), tile_expert, n_tiles.astype(jnp.int32), zfill


def _row_copy_wait(ref, n_rows, sem):
    pltpu.make_async_copy(ref.at[pl.ds(0, n_rows)], ref.at[pl.ds(0, n_rows)], sem).wait()


def _dispatch_kernel(zfill_ref, slot_ref, xp_ref, xs_ref, xs_hbm, zero_scr, sem, zsem, *, n_first):
    tm = xp_ref.shape[0]
    i = pl.program_id(0)

    @pl.when(i == 0)
    def _():
        zero_scr[...] = jnp.zeros_like(zero_scr)

        def zero_tile(n):
            row0 = pl.multiple_of(n * EXPERT_TILE, EXPERT_TILE)
            return pltpu.make_async_copy(zero_scr, xs_hbm.at[pl.ds(row0, EXPERT_TILE)], zsem)

        def fill(n, _):
            @pl.when(zfill_ref[n] != 0)
            def _():
                zero_tile(n).start()
            return 0

        def drain(n, _):
            @pl.when(zfill_ref[n] != 0)
            def _():
                zero_tile(n).wait()
            return 0

        lax.fori_loop(0, zfill_ref.shape[0], fill, 0)
        lax.fori_loop(0, zfill_ref.shape[0], drain, 0)

    def scatter_rows(x_ref):
        def body(r, _):
            for k in range(TOP_K):
                s = slot_ref[0, 0, r * TOP_K + k]
                pltpu.make_async_copy(x_ref.at[pl.ds(r, 1)], xs_hbm.at[pl.ds(s, 1)],
                                      sem).start(priority=k % 2)
            return 0
        lax.fori_loop(0, tm, body, 0)

    @pl.when(i < n_first)
    def _():
        scatter_rows(xp_ref)

    @pl.when(i >= n_first)
    def _():
        scatter_rows(xs_ref)

    _row_copy_wait(xs_hbm, tm * TOP_K, sem)


def _dispatch(xn_p, xn_s, slot_tiles, zfill, n_rows):
    w = xn_p.shape[1]
    t = xn_p.shape[0] + xn_s.shape[0]
    tm = MOE_TOKEN_TILE
    n_first = xn_p.shape[0] // tm
    return pl.pallas_call(
        functools.partial(_dispatch_kernel, n_first=n_first),
        grid_spec=pltpu.PrefetchScalarGridSpec(
            num_scalar_prefetch=1,
            grid=(t // tm,),
            in_specs=[pl.BlockSpec((1, 1, tm * TOP_K), lambda i, *_: (i, 0, 0),
                                   memory_space=pltpu.SMEM)] + _pair_specs((tm, w), 0, n_first),
            out_specs=pl.BlockSpec(memory_space=pl.ANY),
            scratch_shapes=[pltpu.VMEM((EXPERT_TILE, w), F32),
                            pltpu.SemaphoreType.DMA, pltpu.SemaphoreType.DMA]),
        out_shape=jax.ShapeDtypeStruct((n_rows, w), F32),
        compiler_params=_cparams(("arbitrary",)),
        name="moe_dispatch",
    )(zfill, slot_tiles, xn_p, xn_s)


def _expert_kernel(te_ref, nt_ref, xs_ref, wg_ref, wu_ref, wd_ref, ys_ref, wg_scr, wu_scr, wd_scr):
    i = pl.program_id(0)

    @pl.when(i < nt_ref[0])
    def _():
        @pl.when((i == 0) | (te_ref[i] != te_ref[jnp.maximum(i - 1, 0)]))
        def _():
            wg_scr[...] = wg_ref[0].astype(BF16)
            wu_scr[...] = wu_ref[0].astype(BF16)
            wd_scr[...] = wd_ref[0].astype(BF16)

        x = xs_ref[...].astype(BF16)
        h = jax.nn.silu(_dot(x, wg_scr[...])) * _dot(x, wu_scr[...])
        ys_ref[...] = _dot(h.astype(BF16), wd_scr[...])

    @pl.when(i >= nt_ref[0])
    def _():
        ys_ref[...] = jnp.zeros_like(ys_ref)


def _experts(xs, tile_expert, n_tiles, w_gate, w_up, w_down):
    n_rows, w = xs.shape
    ne, d, de = w_gate.shape
    tile = lambda i, te, nt: (jnp.minimum(i, nt[0] - 1), 0)
    out_tile = lambda i, te, nt: (i, 0)
    wsel = lambda i, te, nt: (te[i], 0, 0)
    return pl.pallas_call(
        _expert_kernel,
        grid_spec=pltpu.PrefetchScalarGridSpec(
            num_scalar_prefetch=2,
            grid=(n_rows // EXPERT_TILE,),
            in_specs=[pl.BlockSpec((EXPERT_TILE, w), tile),
                      pl.BlockSpec((1, d, de), wsel), pl.BlockSpec((1, d, de), wsel),
                      pl.BlockSpec((1, de, d), wsel)],
            out_specs=pl.BlockSpec((EXPERT_TILE, w), out_tile),
            scratch_shapes=[pltpu.VMEM((d, de), BF16), pltpu.VMEM((d, de), BF16),
                            pltpu.VMEM((de, d), BF16)]),
        out_shape=jax.ShapeDtypeStruct((n_rows, w), F32),
        compiler_params=_cparams(("arbitrary",)),
        name="moe_experts",
    )(tile_expert, n_tiles, xs, w_gate, w_up, w_down)


def _combine_kernel(slot_ref, gate_ref, xnp_ref, xns_ref, x2p_ref, x2s_ref, sg_ref, su_ref, sd_ref,
                    gf_ref, ys_hbm, yp_ref, ysamp_ref, buf_a, buf_b, base_scr, y_scr, sem_a, sem_b,
                    *, n_first):
    tm = xnp_ref.shape[0]
    i = pl.program_id(0)
    n_steps = pl.num_programs(0)
    first = i < n_first
    rb_rows = SUBLANES

    def issue_rows(tile, buf, sem, r0, n):
        for rr in range(n):
            for k in range(TOP_K):
                s = slot_ref[(tile * tm + r0 + rr) * TOP_K + k]
                pltpu.make_async_copy(ys_hbm.at[pl.ds(s, 1)], buf.at[k, pl.ds(r0 + rr, 1)],
                                      sem).start(priority=k % 2)

    @pl.when(i == 0)
    def _():
        def body(r, _):
            issue_rows(0, buf_a, sem_a, r, 1)
            return 0
        lax.fori_loop(0, tm, body, 0)

    x = _pick(first, xnp_ref, xns_ref).astype(BF16)
    hs = jax.nn.silu(_dot(x, sg_ref[...])) * _dot(x, su_ref[...])
    base_scr[...] = _pick(first, x2p_ref, x2s_ref) + _dot(hs.astype(BF16), sd_ref[...])
    nxt_tile = jnp.where(i + 1 < n_steps, i + 1, 0)

    def run(cur, sem_cur, nxt, sem_nxt):
        _row_copy_wait(ys_hbm, tm * TOP_K, sem_cur)

        for rb in range(tm // rb_rows):
            r0 = rb * rb_rows
            issue_rows(nxt_tile, nxt, sem_nxt, r0, rb_rows)
            acc = base_scr[pl.ds(r0, rb_rows), :]
            g = gate_ref[pl.ds(r0, rb_rows), :]
            for k in range(TOP_K):
                acc = acc + g[:, k:k + 1] * cur[k, pl.ds(r0, rb_rows), :]
            y_scr[pl.ds(r0, rb_rows), :] = _rms(acc, gf_ref[...])

        @pl.when(i == n_steps - 1)
        def _():
            _row_copy_wait(ys_hbm, tm * TOP_K, sem_nxt)

    @pl.when(i % 2 == 0)
    def _():
        run(buf_a, sem_a, buf_b, sem_b)

    @pl.when(i % 2 == 1)
    def _():
        run(buf_b, sem_b, buf_a, sem_a)

    @pl.when(first)
    def _():
        yp_ref[...] = y_scr[...]

    @pl.when(jnp.logical_not(first))
    def _():
        ysamp_ref[...] = y_scr[...]


def _combine(slot_flat, gates, xn_p, xn_s, x2_p, x2_s, ws_gate, ws_up, ws_down, g_final, ys):
    d = x2_p.shape[1]
    tp, tsamp = x2_p.shape[0], x2_s.shape[0]
    dsh = ws_gate.shape[1]
    tm = COMBINE_TILE
    n_first = tp // tm
    row = lambda i, *_: (i, 0)
    pair = lambda: _pair_specs((tm, d), 0, n_first)
    return pl.pallas_call(
        functools.partial(_combine_kernel, n_first=n_first),
        grid_spec=pltpu.PrefetchScalarGridSpec(
            num_scalar_prefetch=1,
            grid=((tp + tsamp) // tm,),
            in_specs=[pl.BlockSpec((tm, TOP_K), row)] + pair() + pair()
                     + [_const_spec((d, dsh)), _const_spec((d, dsh)), _const_spec((dsh, d)),
                        _const_spec((1, d)), pl.BlockSpec(memory_space=pl.ANY)],
            out_specs=pair(),
            scratch_shapes=[pltpu.VMEM((TOP_K, tm, d), F32), pltpu.VMEM((TOP_K, tm, d), F32),
                            pltpu.VMEM((tm, d), F32), pltpu.VMEM((tm, d), F32),
                            pltpu.SemaphoreType.DMA, pltpu.SemaphoreType.DMA]),
        out_shape=[jax.ShapeDtypeStruct((tp, d), F32), jax.ShapeDtypeStruct((tsamp, d), F32)],
        compiler_params=_cparams(("arbitrary",)),
        name="moe_combine",
    )(slot_flat, gates, xn_p, xn_s, x2_p, x2_s, ws_gate.astype(BF16), ws_up.astype(BF16),
      ws_down.astype(BF16), g_final.reshape(1, d), ys)


def _moe(xn_p, xn_s, x2_p, x2_s, eidx, gate, posk, counts, w_gate, w_up, w_down, ws_gate, ws_up,
         ws_down, g_final):
    t = xn_p.shape[0] + xn_s.shape[0]
    ne = w_gate.shape[0]
    n_tiles_max = t * TOP_K // EXPERT_TILE + ne
    slot, tile_expert, n_tiles, zfill = _moe_plan(eidx, posk, counts[:, 0], n_tiles_max)
    slot_tm = jnp.transpose(slot)
    slot_tiles = slot_tm.reshape(t // MOE_TOKEN_TILE, 1, MOE_TOKEN_TILE * TOP_K)
    xs = _dispatch(xn_p, xn_s, slot_tiles, zfill, n_tiles_max * EXPERT_TILE)
    ys = _experts(xs, tile_expert, n_tiles, w_gate, w_up, w_down)
    return _combine(slot_tm.reshape(t * TOP_K), jnp.transpose(gate), xn_p, xn_s, x2_p, x2_s,
                    ws_gate, ws_up, ws_down, g_final, ys)


def kernel(x_prompt, x_sample, cache_mem_k, cache_mem_v, state_s5_re, state_s5_im, state_gla, mem_prompt, w_in, w_a2, b_a2, lam_re, lam_im, log_dt, b_re, b_im, c_re, c_im, d_skip, w_glu, b_glu, g_s5_out, g_gla_head, w_out, g_mix, g_xattn, g_mem, w_q, w_k, w_v, w_o, g_ffn, w_router, router_bias, w_gate, w_up, w_down, ws_gate, ws_up, ws_down, g_final):
    depth = w_in.shape[0]
    assert depth == 1, "one trunk layer"
    bp, lp, d = x_prompt.shape
    bs, ls, _ = x_sample.shape
    assert bp == 1 and ls == CHUNK and lp % CHUNK == 0
    n_mem = mem_prompt.shape[1]
    g, p = lam_re.shape[1:]
    d_s5 = g * S5_GROUP
    d_qk = GLA_HEADS * GLA_DK
    d_gla = w_out.shape[1] - d_s5
    ns = d_s5 // LANES
    ne = w_router.shape[2]
    tp, tsamp = bp * lp, bs * ls
    xp = x_prompt.reshape(tp, d)
    xs = x_sample.reshape(tsamp, d)
    sq = lambda a: a.reshape(a.shape[1:])

    mk_p, mv_p = _memory_kv(mem_prompt.reshape(n_mem, d), sq(g_mem), sq(w_k).astype(BF16),
                            sq(w_v).astype(BF16))

    u, q, k, v, a, r = _in_proj(xp, xs, sq(g_mix), sq(w_in), d_s5, d_qk, d_gla)

    ksm, psm, qsm, step_m, carry_m, a8 = _s5_coeffs(
        sq(lam_re), sq(lam_im), sq(log_dt), sq(b_re), sq(b_im), sq(c_re), sq(c_im), sq(d_skip))
    tables = tuple(_s5_expand(ksm, psm, qsm)) + (step_m, carry_m, a8)
    zero_h = jnp.zeros((ns, bp, 2 * GROUPS_PER_SLAB * p), F32)
    rows_p = lp // S5_BLOCK
    y_p, h_p = _s5_mixer(u, tables, zero_h, 0, tp, rows_per_seq=rows_p,
                         tile_rows=math.gcd(rows_p, 512))
    h0_s = _pack_s5_state(sq(state_s5_re), sq(state_s5_im), ns)
    y_s, h_s = _s5_mixer(u, tables, h0_s, tp, tsamp, rows_per_seq=ls // S5_BLOCK,
                         tile_rows=tsamp // S5_BLOCK)

    wa2 = jnp.pad(sq(w_a2), ((0, LANES - GLA_RANK), (0, 0)))
    zero_s = jnp.zeros((bp,) + state_gla.shape[2:], F32)
    gla_p, sg_p = _gla_mixer(q, k, v, a, r, wa2, sq(b_a2), sq(g_gla_head), zero_s, 0, tp,
                             carry=True, n_chunks=4)
    gla_s, sg_s = _gla_mixer(q, k, v, a, r, wa2, sq(b_a2), sq(g_gla_head), sq(state_gla), tp,
                             tsamp, carry=False, n_chunks=4)

    x1, xq = _mix_out(y_p, y_s, gla_p, gla_s, xp, xs, sq(w_glu), sq(b_glu), sq(g_s5_out),
                      sq(w_out), sq(g_xattn), sq(w_q))

    wr_hi, wr_lo = _split_bf16(jnp.transpose(sq(w_router)))
    wo = sq(w_o).astype(BF16)
    x2_p, xn_p, lg_p = _xattn(xq, x1, mk_p[None], mv_p[None], wo, sq(g_ffn), wr_hi, wr_lo,
                              0, tp, nb=1, lt=TOKEN_TILE)
    x2_s, xn_s, lg_s = _xattn(xq, x1, sq(cache_mem_k), sq(cache_mem_v), wo, sq(g_ffn), wr_hi,
                              wr_lo, tp, tsamp, nb=2, lt=ls)

    eidx, gate, posk, counts = _route(lg_p, lg_s, sq(router_bias))
    y_p2, y_s2 = _moe(xn_p, xn_s, x2_p, x2_s, eidx, gate, posk, counts, sq(w_gate), sq(w_up),
                      sq(w_down), sq(ws_gate), sq(ws_up), sq(ws_down), g_final)

    xh = d // X_HEADS
    re_p, im_p = _unpack_s5_state(h_p, g, p)
    re_s, im_s = _unpack_s5_state(h_s, g, p)
    return (y_p2.reshape(bp, lp, d), y_s2.reshape(bs, ls, d),
            mk_p.reshape(1, bp, n_mem, X_HEADS, xh), mv_p.reshape(1, bp, n_mem, X_HEADS, xh),
            re_p[None], im_p[None], sg_p[None], re_s[None], im_s[None], sg_s[None])
```

```python
import functools
import math

import jax
import jax.numpy as jnp
from jax import lax
from jax.experimental import pallas as pl
from jax.experimental.pallas import tpu as pltpu

F32 = jnp.float32
BF16 = jnp.bfloat16

EPS = 1e-6
CHUNK = 64
S5_GROUP = 16
GLA_HEADS = 4
GLA_DK = 128
GLA_RANK = 16
GLA_TAU = 16.0
X_HEADS = 4
N_GROUPS = 8
TOPK_GROUPS = 4
TOP_K = 8
ROUTE_SCALE = 2.5

LANES = 128
SUBLANES = 8
S5_BLOCK = SUBLANES
GROUPS_PER_SLAB = LANES // S5_GROUP
TOKEN_TILE = 512
EXPERT_TILE = 256
MOE_TOKEN_TILE = 256
COMBINE_TILE = 128
VMEM_LIMIT = 56 * 1024 * 1024


def _cparams(sem):
    return pltpu.CompilerParams(dimension_semantics=sem, vmem_limit_bytes=VMEM_LIMIT)


def _const_spec(shape):
    nd = len(shape)
    return pl.BlockSpec(shape, lambda *_: (0,) * nd, pipeline_mode=pl.Buffered(1))


def _pair_specs(block, axis, n_first):
    def at(f):
        return lambda i, *_: tuple(f(i) if a == axis else 0 for a in range(len(block)))
    return [pl.BlockSpec(block, at(lambda i: jnp.minimum(i, n_first - 1))),
            pl.BlockSpec(block, at(lambda i: jnp.maximum(i - n_first, 0)))]


def _pick(first, a_ref, b_ref):
    return jnp.where(first, a_ref[...], b_ref[...])


def _rms(x, g):
    return x * lax.rsqrt(jnp.mean(x * x, axis=-1, keepdims=True) + EPS) * g


def _split_bf16(x):
    hi = x.astype(BF16)
    lo = (x - hi.astype(F32)).astype(BF16)
    return hi, lo


def _split3_bf16(x):
    a = x.astype(BF16)
    r = x - a.astype(F32)
    b = r.astype(BF16)
    c = (r - b.astype(F32)).astype(BF16)
    return a, b, c


def _dot(a, b):
    return jnp.dot(a, b, preferred_element_type=F32)


def _dot_nt(a, b):
    return lax.dot_general(a, b, (((1,), (1,)), ((), ())), preferred_element_type=F32)


def _dot_tn(a, b):
    return lax.dot_general(a, b, (((0,), (0,)), ((), ())), preferred_element_type=F32)


def _pair_rows(x, scr):
    n, d = x.shape
    h = d // 2
    slabs = range(h // LANES)
    for l in slabs:
        scr[l, pl.ds(0, n, stride=2), :] = x[:, l * LANES:(l + 1) * LANES]
        scr[l, pl.ds(1, n, stride=2), :] = x[:, h + l * LANES:h + (l + 1) * LANES]
    return jnp.concatenate([scr[l, 0:2 * n, :] for l in slabs], axis=1).astype(BF16)


def _pair_at(ref, r):
    return ref.at[pl.ds(pl.multiple_of(r * 2, 2), 2)]


def _unpair_rows(p, scr):
    n2, h = p.shape
    n = n2 // 2
    slabs = range(h // LANES)
    for l in slabs:
        scr[l, 0:n2, :] = p[:, l * LANES:(l + 1) * LANES].astype(F32)
    first = [scr[l, pl.ds(0, n, stride=2), :] for l in slabs]
    second = [scr[l, pl.ds(1, n, stride=2), :] for l in slabs]
    return jnp.concatenate(first + second, axis=1)


def _memkv_kernel(mem_ref, g_ref, wk_ref, wv_ref, mk_ref, mv_ref):
    m = _rms(mem_ref[...], g_ref[...]).astype(BF16)
    mk_ref[...] = _dot(m, wk_ref[...])
    mv_ref[...] = _dot(m, wv_ref[...])


def _memory_kv(mem, g_mem, wk, wv):
    n, d = mem.shape
    tn = 512
    return pl.pallas_call(
        _memkv_kernel,
        grid=(d // tn,),
        in_specs=[_const_spec((n, d)), _const_spec((1, d)),
                  pl.BlockSpec((d, tn), lambda j: (0, j)),
                  pl.BlockSpec((d, tn), lambda j: (0, j))],
        out_specs=[pl.BlockSpec((n, tn), lambda j: (0, j))] * 2,
        out_shape=[jax.ShapeDtypeStruct((n, d), F32)] * 2,
        compiler_params=_cparams(("arbitrary",)),
        name="memory_kv",
    )(mem, g_mem.reshape(1, d), wk, wv)


def _inproj_kernel(xp_ref, xs_ref, g_ref, wuh_ref, wul_ref, wqkv_ref, wa_ref, wr_ref,
                   u_ref, q_ref, k_ref, v_ref, a_ref, r_ref, *, n_first):
    first = pl.program_id(0) < n_first
    x = _pick(first, xp_ref, xs_ref)
    nx = _rms(x, g_ref[...])
    hi, lo = _split_bf16(nx)
    u = _dot(hi, wuh_ref[...])
    for j in range(u_ref.shape[0]):
        u_ref[j] = u[:, j * LANES:(j + 1) * LANES]

    @pl.when(jnp.logical_not(first))
    def _():
        fix = _dot(lo, wuh_ref[...]) + _dot(hi, wul_ref[...])
        for j in range(u_ref.shape[0]):
            u_ref[j] += fix[:, j * LANES:(j + 1) * LANES]

    qkv = _dot(hi, wqkv_ref[...])
    dq = q_ref.shape[1]
    q_ref[...] = qkv[:, :dq].astype(BF16)
    k_ref[...] = qkv[:, dq:2 * dq].astype(BF16)
    v_ref[...] = qkv[:, 2 * dq:].astype(BF16)
    a_ref[...] = _dot(hi, wa_ref[...]) + _dot(lo, wa_ref[...])
    r_ref[...] = _dot(hi, wr_ref[...]).astype(BF16)


def _in_proj(xp, xs, g_mix, w_in, d_s5, d_qk, d_gla):
    d = xp.shape[1]
    t = xp.shape[0] + xs.shape[0]
    tm = TOKEN_TILE
    n_first = xp.shape[0] // tm
    o1 = d_s5
    o2 = o1 + 2 * d_qk + d_gla
    o3 = o2 + GLA_RANK
    wu_hi, wu_lo = _split_bf16(w_in[:, :o1])
    wqkv = w_in[:, o1:o2].astype(BF16)
    wa = jnp.pad(w_in[:, o2:o3], ((0, 0), (0, LANES - GLA_RANK))).astype(BF16)
    wr = w_in[:, o3:].astype(BF16)
    n_slab = d_s5 // LANES
    row = lambda i: (i, 0)
    return pl.pallas_call(
        functools.partial(_inproj_kernel, n_first=n_first),
        grid=(t // tm,),
        in_specs=_pair_specs((tm, d), 0, n_first) + [
            _const_spec((1, d)), _const_spec(wu_hi.shape), _const_spec(wu_lo.shape),
            _const_spec(wqkv.shape), _const_spec(wa.shape), _const_spec(wr.shape)],
        out_specs=[pl.BlockSpec((n_slab, tm, LANES), lambda i: (0, i, 0)),
                   pl.BlockSpec((tm, d_qk), row), pl.BlockSpec((tm, d_qk), row),
                   pl.BlockSpec((tm, d_gla), row), pl.BlockSpec((tm, LANES), row),
                   pl.BlockSpec((tm, d_gla), row)],
        out_shape=[jax.ShapeDtypeStruct((n_slab, t, LANES), F32),
                   jax.ShapeDtypeStruct((t, d_qk), BF16), jax.ShapeDtypeStruct((t, d_qk), BF16),
                   jax.ShapeDtypeStruct((t, d_gla), BF16), jax.ShapeDtypeStruct((t, LANES), F32),
                   jax.ShapeDtypeStruct((t, d_gla), BF16)],
        compiler_params=_cparams(("arbitrary",)),
        name="in_proj",
    )(xp, xs, g_mix.reshape(1, d), wu_hi, wu_lo, wqkv, wa, wr)


def _s5_coeffs(lam_re, lam_im, log_dt, b_re, b_im, c_re, c_im, d_skip):
    g, p = lam_re.shape
    h = b_re.shape[-1]
    nb = S5_BLOCK
    ns = g // GROUPS_PER_SLAB
    gl = GROUPS_PER_SLAB
    dt = jnp.exp(log_dt.astype(F32))[:, None]
    lr, li = lam_re.astype(F32), lam_im.astype(F32)
    mag = jnp.exp(lr * dt)
    ar, ai = mag * jnp.cos(li * dt), mag * jnp.sin(li * dt)
    den = lr * lr + li * li
    nr = ar - 1.0
    cf_r = (nr * lr + ai * li) / den
    cf_i = (ai * lr - nr * li) / den
    bb_r = cf_r[..., None] * b_re - cf_i[..., None] * b_im
    bb_i = cf_r[..., None] * b_im + cf_i[..., None] * b_re

    def cpow(n):
        e = n[:, None, None] * dt[None]
        m = jnp.exp(lr[None] * e)
        return m * jnp.cos(li[None] * e), m * jnp.sin(li[None] * e)

    pr, pi = cpow(jnp.arange(nb + 1, dtype=F32))

    bt_r, bt_i = jnp.swapaxes(bb_r, 1, 2), jnp.swapaxes(bb_i, 1, 2)
    ct_r, ct_i = jnp.swapaxes(c_re, 1, 2), jnp.swapaxes(c_im, 1, 2)

    cb_r = jnp.einsum('gcp,gph->gphc', c_re, bb_r) - jnp.einsum('gcp,gph->gphc', c_im, bb_i)
    cb_i = jnp.einsum('gcp,gph->gphc', c_re, bb_i) + jnp.einsum('gcp,gph->gphc', c_im, bb_r)
    taps = (jnp.einsum('ngp,gphc->nghc', pr[:nb], cb_r)
            - jnp.einsum('ngp,gphc->nghc', pi[:nb], cb_i))
    skip = d_skip[None, :, :, None] * jnp.eye(h, dtype=F32)[None, None]
    taps = taps + jnp.where(jnp.arange(nb)[:, None, None, None] == 0, skip, 0.0)
    ksm = jnp.swapaxes(taps.reshape(nb, ns, gl * h, h), 0, 1)

    wr, wi = pr[nb - 1::-1][:nb], pi[nb - 1::-1][:nb]
    wr, wi = wr[:, :, None, :], wi[:, :, None, :]
    inj = jnp.concatenate([wr * bt_r[None] - wi * bt_i[None],
                           wr * bt_i[None] + wi * bt_r[None]], axis=-1)
    psm = jnp.swapaxes(inj.reshape(nb, ns, gl * h, 2 * p), 0, 1)

    er = jnp.transpose(pr[1:nb + 1], (1, 2, 0))[..., None]
    ei = jnp.transpose(pi[1:nb + 1], (1, 2, 0))[..., None]
    q_r = ct_r[:, :, None, :] * er - ct_i[:, :, None, :] * ei
    q_i = -(ct_r[:, :, None, :] * ei + ct_i[:, :, None, :] * er)
    qq = jnp.stack([q_r.reshape(ns, gl * p, nb * h), q_i.reshape(ns, gl * p, nb * h)], axis=1)
    qsm = qq.reshape(ns, 2 * gl * p, nb * h)

    def slab(xr, xi):
        k = xr.shape[0]
        xr = jnp.transpose(xr.reshape(k, ns, gl * p), (1, 0, 2))
        xi = jnp.transpose(xi.reshape(k, ns, gl * p), (1, 0, 2))
        return jnp.concatenate([xr, xi], axis=-1)

    rows = jnp.arange(SUBLANES, dtype=F32)
    cr, ci = cpow(nb * rows)
    carry_m = slab(cr, ci)
    sr, si = cpow(nb * jnp.array([1.0, 2.0, 4.0], F32))
    keep = (rows[None, :] >= jnp.array([1.0, 2.0, 4.0], F32)[:, None]).astype(F32)
    step_m = slab((sr[:, None] * keep[:, :, None, None]).reshape(3 * SUBLANES, g, p),
                  (si[:, None] * keep[:, :, None, None]).reshape(3 * SUBLANES, g, p))
    step_m = step_m.reshape(ns, 3, SUBLANES, 2 * gl * p)
    a8 = carry_m[:, 1:2]
    return ksm, psm, qsm, step_m, carry_m, a8


def _s5_expand_kernel(ksm_ref, psm_ref, qsm_ref, tm_ref, ph_ref, pl_ref, qm_ref):
    nb = S5_BLOCK
    h = S5_GROUP
    w = ph_ref.shape[2]
    p = w // (2 * GROUPS_PER_SLAB)

    def spread(x, sel, keep):
        a, b, c = _split3_bf16(x)
        return jnp.where(keep, _dot(a, sel) + _dot(b, sel) + _dot(c, sel), 0.0)

    def iota(shape, axis):
        return lax.broadcasted_iota(jnp.int32, shape, axis)

    sel_c = (iota((h, LANES), 1) % h == iota((h, LANES), 0)).astype(BF16)
    keep_t = iota((LANES, LANES), 0) // h == iota((LANES, LANES), 1) // h
    tm_ref[...] = jnp.zeros_like(tm_ref)
    for tau in range(nb):
        blk = spread(ksm_ref[0, tau], sel_c, keep_t).astype(BF16)
        for s in range(nb - tau):
            t = s + tau
            tm_ref[0, s * LANES:(s + 1) * LANES, t * LANES:(t + 1) * LANES] = blk

    half = w // 2
    keep_p = iota((LANES, half), 0) // h == iota((LANES, half), 1) // p
    for ri in range(2):
        sel_p = (iota((2 * p, half), 0) == iota((2 * p, half), 1) % p + ri * p).astype(BF16)
        for s in range(nb):
            blk = spread(psm_ref[0, s], sel_p, keep_p)
            hi, lo = _split_bf16(blk)
            ph_ref[0, s * LANES:(s + 1) * LANES, ri * half:(ri + 1) * half] = hi
            pl_ref[0, s * LANES:(s + 1) * LANES, ri * half:(ri + 1) * half] = lo

    keep_q = (iota((w, LANES), 0) % half) // p == iota((w, LANES), 1) // h
    for t in range(nb):
        sel_q = (iota((nb * h, LANES), 0) == iota((nb * h, LANES), 1) % h + t * h).astype(BF16)
        qm_ref[0, :, t * LANES:(t + 1) * LANES] = spread(qsm_ref[0], sel_q, keep_q).astype(BF16)


def _s5_expand(ksm, psm, qsm):
    ns = ksm.shape[0]
    w = qsm.shape[1]
    k = S5_BLOCK * LANES
    blk = lambda shape: pl.BlockSpec((1,) + shape, lambda j: (j,) + (0,) * len(shape))
    return pl.pallas_call(
        _s5_expand_kernel,
        grid=(ns,),
        in_specs=[blk(ksm.shape[1:]), blk(psm.shape[1:]), blk(qsm.shape[1:])],
        out_specs=[blk((k, k)), blk((k, w)), blk((k, w)), blk((w, k))],
        out_shape=[jax.ShapeDtypeStruct((ns, k, k), BF16), jax.ShapeDtypeStruct((ns, k, w), BF16),
                   jax.ShapeDtypeStruct((ns, k, w), BF16), jax.ShapeDtypeStruct((ns, w, k), BF16)],
        compiler_params=_cparams(("arbitrary",)),
        name="s5_tables",
    )(ksm, psm, qsm)


def _cmul(xr, xi, mr, mi):
    return xr * mr - xi * mi, xr * mi + xi * mr


def _s5_kernel(u_ref, ph_ref, pl_ref, tm_ref, qm_ref, step_ref, carry_ref, a8_ref, h0_ref,
               y_ref, hout_ref, s_scr, hin_scr, c_scr, *, rows_per_seq):
    ti = pl.program_id(1)
    rows = s_scr.shape[0]
    half = s_scr.shape[1] // 2
    nb = S5_BLOCK
    u8 = jnp.concatenate([u_ref[0, pl.ds(s, rows, stride=nb), :] for s in range(nb)], axis=1)
    hi, lo = _split_bf16(u8)
    per_block_seq = rows_per_seq == SUBLANES
    if per_block_seq:
        s_scr[...] = _dot(hi, ph_ref[0]) + _dot(lo, ph_ref[0]) + _dot(hi, pl_ref[0])
    else:
        s_scr[...] = _dot(hi, ph_ref[0])

    if not per_block_seq:
        @pl.when(ti == 0)
        def _():
            c_scr[...] = h0_ref[0]

    not_first = (lax.broadcasted_iota(jnp.int32, (SUBLANES, 1), 0) >= 1).astype(F32)
    a8r, a8i = a8_ref[0, :, :half], a8_ref[0, :, half:]
    cmr, cmi = carry_ref[0, :, :half], carry_ref[0, :, half:]

    def body(rb, _):
        r0 = pl.multiple_of(rb * SUBLANES, SUBLANES)
        sb = s_scr[pl.ds(r0, SUBLANES), :]
        x = pltpu.roll(sb, 1, 0) * not_first
        xr, xi = x[:, :half], x[:, half:]
        for k, sh in enumerate((1, 2, 4)):
            m = step_ref[0, k]
            pr, pi = _cmul(pltpu.roll(xr, sh, 0), pltpu.roll(xi, sh, 0), m[:, :half], m[:, half:])
            xr, xi = xr + pr, xi + pi
        c = h0_ref[0, pl.ds(rb, 1), :] if per_block_seq else c_scr[...]
        cr, ci = c[:, :half], c[:, half:]
        pr, pi = _cmul(cr, ci, cmr, cmi)
        hr, hi_ = xr + pr, xi + pi
        hin_scr[pl.ds(r0, SUBLANES), :half] = hr
        hin_scr[pl.ds(r0, SUBLANES), half:] = hi_
        nr, ni = _cmul(hr[SUBLANES - 1:], hi_[SUBLANES - 1:], a8r, a8i)
        cn = jnp.concatenate([nr, ni], axis=1) + sb[SUBLANES - 1:]
        if per_block_seq:
            hout_ref[0, pl.ds(rb, 1), :] = cn
        else:
            c_scr[...] = cn
        return 0

    lax.fori_loop(0, rows // SUBLANES, body, 0)
    if not per_block_seq:
        hout_ref[0] = c_scr[...]

    y8 = _dot(hi, tm_ref[0]) + _dot(hin_scr[...].astype(BF16), qm_ref[0])
    for t in range(nb):
        y_ref[0, pl.ds(t, rows, stride=nb), :] = y8[:, t * LANES:(t + 1) * LANES]


def _s5_mixer(u_slabs, tables, h0, t_start, t, rows_per_seq, tile_rows):
    tm, pm_hi, pm_lo, qm, step_m, carry_m, a8 = tables
    ns = u_slabs.shape[0]
    rows = t // S5_BLOCK
    n_tiles = rows // tile_rows
    assert rows % tile_rows == 0 and t_start % (tile_rows * S5_BLOCK) == 0
    off = t_start // (tile_rows * S5_BLOCK)
    n_seq = h0.shape[1]
    w = pm_hi.shape[-1]
    slab3 = lambda j, i: (j, 0, 0)
    kern = functools.partial(_s5_kernel, rows_per_seq=rows_per_seq)
    return pl.pallas_call(
        kern,
        grid=(ns, n_tiles),
        in_specs=[pl.BlockSpec((1, tile_rows * S5_BLOCK, LANES), lambda j, i: (j, i + off, 0)),
                  pl.BlockSpec((1,) + pm_hi.shape[1:], slab3),
                  pl.BlockSpec((1,) + pm_lo.shape[1:], slab3),
                  pl.BlockSpec((1,) + tm.shape[1:], slab3),
                  pl.BlockSpec((1,) + qm.shape[1:], slab3),
                  pl.BlockSpec((1,) + step_m.shape[1:], lambda j, i: (j, 0, 0, 0)),
                  pl.BlockSpec((1,) + carry_m.shape[1:], slab3),
                  pl.BlockSpec((1,) + a8.shape[1:], slab3),
                  pl.BlockSpec((1, n_seq, w), slab3)],
        out_specs=[pl.BlockSpec((1, tile_rows * S5_BLOCK, LANES), lambda j, i: (j, i, 0)),
                   pl.BlockSpec((1, n_seq, w), slab3)],
        out_shape=[jax.ShapeDtypeStruct((ns, t, LANES), F32),
                   jax.ShapeDtypeStruct((ns, n_seq, w), F32)],
        scratch_shapes=[pltpu.VMEM((tile_rows, w), F32), pltpu.VMEM((tile_rows, w), F32),
                        pltpu.VMEM((1, w), F32)],
        compiler_params=_cparams(("arbitrary", "arbitrary")),
        name="s5_mixer",
    )(u_slabs, pm_hi, pm_lo, tm, qm, step_m, carry_m, a8, h0)


def _pack_s5_state(re, im, ns):
    b = re.shape[0]
    r = re.astype(F32).reshape(b, ns, -1)
    i = im.astype(F32).reshape(b, ns, -1)
    return jnp.transpose(jnp.concatenate([r, i], axis=-1), (1, 0, 2))


def _unpack_s5_state(hc, g, p):
    ns, b, w = hc.shape
    hc = jnp.transpose(hc, (1, 0, 2))
    re = hc[:, :, :w // 2].reshape(b, g, p)
    im = hc[:, :, w // 2:].reshape(b, g, p)
    return re, im


def _gla_kernel(q_ref, k_ref, v_ref, a_ref, r_ref, wa2_ref, ba2_ref, gh_ref, s0_ref,
                o_ref, sout_ref, st_scr, *, carry, n_chunks):
    step = pl.program_id(0)
    dk = GLA_DK
    dv = v_ref.shape[1] // GLA_HEADS
    c = CHUNK
    scale = dk ** -0.5
    ri = lax.broadcasted_iota(jnp.int32, (c, c), 0)
    ci = lax.broadcasted_iota(jnp.int32, (c, c), 1)
    causal = ri >= ci
    tril = causal.astype(BF16)
    eye_dk = (lax.broadcasted_iota(jnp.int32, (dk, dk), 0)
              == lax.broadcasted_iota(jnp.int32, (dk, dk), 1))

    if carry:
        @pl.when(step == 0)
        def _():
            st_scr[...] = s0_ref[0]

    wa_hi, wa_lo = _split_bf16(wa2_ref[...])
    for n in range(n_chunks):
        rows = slice(n * c, (n + 1) * c)
        a_hi, a_lo = _split_bf16(a_ref[rows, :])
        logit = _dot(a_hi, wa_hi) + _dot(a_lo, wa_hi) + _dot(a_hi, wa_lo) + ba2_ref[...]
        g = jax.nn.log_sigmoid(logit) * (1.0 / GLA_TAU)
        g1, g2, g3 = _split3_bf16(g)
        bcum = _dot(tril, g1) + _dot(tril, g2) + _dot(tril, g3)
        for h in range(GLA_HEADS):
            ks = slice(h * dk, (h + 1) * dk)
            vs = slice(h * dv, (h + 1) * dv)
            b = bcum[:, ks]
            qh = q_ref[rows, ks].astype(F32) * scale
            kh = k_ref[rows, ks].astype(F32)
            vh = v_ref[rows, vs]
            state = s0_ref[n, h] if not carry else st_scr[h]
            qe = (qh * jnp.exp(b)).astype(BF16)
            ke = (kh * jnp.exp(-b)).astype(BF16)
            att = jnp.where(causal, _dot_nt(qe, ke), 0.0)
            o = _dot(att.astype(BF16), vh) + _dot(qe, state.astype(BF16))
            blast = b[c - 1:c, :]
            kd = (kh * jnp.exp(blast - b)).astype(BF16)
            decay = jnp.sum(jnp.where(eye_dk, jnp.exp(blast), 0.0), axis=1, keepdims=True)
            new_state = decay * state + _dot_tn(kd, vh)
            if carry:
                st_scr[h] = new_state
            else:
                sout_ref[n, h] = new_state
            on = _rms(o, gh_ref[...])
            rr = r_ref[rows, vs].astype(F32)
            o_ref[rows, vs] = (on * (rr * jax.nn.sigmoid(rr))).astype(BF16)
    if carry:
        sout_ref[0] = st_scr[...]


def _gla_mixer(q, k, v, a, r, wa2, ba2, g_head, s0, t_start, t, carry, n_chunks):
    dqk = q.shape[1]
    dvt = v.shape[1]
    hh, dk, dv = s0.shape[1:]
    rows = n_chunks * CHUNK
    assert t % rows == 0 and t_start % rows == 0
    off = t_start // rows
    row = lambda i: (i, 0)
    row_in = lambda i: (i + off, 0)
    if carry:
        sblk, smap = (1, hh, dk, dv), (lambda i: (0, 0, 0, 0))
    else:
        sblk, smap = (n_chunks, hh, dk, dv), (lambda i: (i, 0, 0, 0))
    kern = functools.partial(_gla_kernel, carry=carry, n_chunks=n_chunks)
    return pl.pallas_call(
        kern,
        grid=(t // rows,),
        in_specs=[pl.BlockSpec((rows, dqk), row_in), pl.BlockSpec((rows, dqk), row_in),
                  pl.BlockSpec((rows, dvt), row_in), pl.BlockSpec((rows, LANES), row_in),
                  pl.BlockSpec((rows, dvt), row_in),
                  _const_spec(wa2.shape), _const_spec((1, dqk)), _const_spec((1, dv)),
                  pl.BlockSpec(sblk, smap)],
        out_specs=[pl.BlockSpec((rows, dvt), row), pl.BlockSpec(sblk, smap)],
        out_shape=[jax.ShapeDtypeStruct((t, dvt), BF16), jax.ShapeDtypeStruct(s0.shape, F32)],
        scratch_shapes=[pltpu.VMEM((hh, dk, dv), F32)],
        compiler_params=_cparams(("arbitrary",)),
        name="gla_mixer",
    )(q, k, v, a, r, wa2, ba2.reshape(1, dqk), g_head.reshape(1, dv), s0)


def _mixout_kernel(yp_ref, ys_ref, gp_ref, gs_ref, xp_ref, xs_ref, wglu_ref, bglu_ref, gs5_ref,
                   wout_ref, gx_ref, wq_ref, x1_ref, q_ref, *, n_first):
    first = pl.program_id(0) < n_first
    yb = _pick(first, yp_ref, ys_ref)
    y = jnp.concatenate([yb[j] for j in range(yb.shape[0])], axis=1)
    z = jax.nn.gelu(y)
    gate = jax.nn.sigmoid(_dot(z.astype(BF16), wglu_ref[...]) + bglu_ref[...])
    s5 = _rms(z * gate, gs5_ref[...])
    cat = jnp.concatenate([s5.astype(BF16), _pick(first, gp_ref, gs_ref)], axis=1)
    x1 = _pick(first, xp_ref, xs_ref) + _dot(cat, wout_ref[...])
    x1_ref[...] = x1
    q_ref[...] = _dot(_rms(x1, gx_ref[...]).astype(BF16), wq_ref[...]).astype(BF16)


def _mix_out(y_p, y_s, gla_p, gla_s, xp, xs, w_glu, b_glu, g_s5, w_out, g_x, w_q):
    d = xp.shape[1]
    t = xp.shape[0] + xs.shape[0]
    ns = y_p.shape[0]
    ds5 = ns * LANES
    dg = gla_p.shape[1]
    tm = TOKEN_TILE // 2
    n_first = xp.shape[0] // tm
    row = lambda i: (i, 0)
    return pl.pallas_call(
        functools.partial(_mixout_kernel, n_first=n_first),
        grid=(t // tm,),
        in_specs=(_pair_specs((ns, tm, LANES), 1, n_first) + _pair_specs((tm, dg), 0, n_first)
                  + _pair_specs((tm, d), 0, n_first)
                  + [_const_spec((ds5, ds5)), _const_spec((1, ds5)), _const_spec((1, ds5)),
                     _const_spec((ds5 + dg, d)), _const_spec((1, d)), _const_spec((d, d))]),
        out_specs=[pl.BlockSpec((tm, d), row), pl.BlockSpec((tm, d), row)],
        out_shape=[jax.ShapeDtypeStruct((t, d), F32), jax.ShapeDtypeStruct((t, d), BF16)],
        compiler_params=_cparams(("arbitrary",)),
        name="mix_out",
    )(y_p, y_s, gla_p, gla_s, xp, xs, w_glu.astype(BF16), b_glu.reshape(1, ds5),
      g_s5.reshape(1, ds5), w_out.astype(BF16), g_x.reshape(1, d), w_q.astype(BF16))


def _xattn_kernel(q_ref, mk_ref, mv_ref, x1_ref, wo_ref, gf_ref, wrh_ref, wrl_ref,
                  x2_ref, xn_ref, lg_ref, xpair_ref, pair_scr, *cache_scratch, nb):
    d = q_ref.shape[1]
    lt = q_ref.shape[0] // nb
    hd = d // X_HEADS
    scale = hd ** -0.5
    if cache_scratch:
        kbuf, vbuf, sem = cache_scratch
        i = pl.program_id(0)

        def fetch(step, slot):
            for b in range(nb):
                for h in range(X_HEADS):
                    for src, dst in ((mk_ref, kbuf), (mv_ref, vbuf)):
                        pltpu.make_async_copy(src.at[step * nb + b, :, h, :],
                                              dst.at[slot, b, :, pl.ds(h * hd, hd)],
                                              sem.at[slot]).start()

        @pl.when(i == 0)
        def _():
            fetch(0, 0)

        @pl.when(i + 1 < pl.num_programs(0))
        def _():
            fetch(i + 1, (i + 1) % 2)

        slot = i % 2
        for buf in (kbuf, vbuf):
            pltpu.make_async_copy(buf.at[slot], buf.at[slot], sem.at[slot]).wait()
        memory = lambda b: (kbuf[slot, b], vbuf[slot, b])
    else:
        memory = lambda b: (mk_ref[b], mv_ref[b])
    for b in range(nb):
        rows = slice(b * lt, (b + 1) * lt)
        mk, mv = (m.astype(BF16) for m in memory(b))
        outs = []
        for h in range(X_HEADS):
            hs = slice(h * hd, (h + 1) * hd)
            s = _dot_nt(q_ref[rows, hs], mk[:, hs]) * scale
            p = jnp.exp(s - jnp.max(s, axis=-1, keepdims=True))
            denom = jnp.sum(p, axis=-1, keepdims=True)
            outs.append((_dot(p.astype(BF16), mv[:, hs]) / denom).astype(BF16))
        o = jnp.concatenate(outs, axis=1)
        x2 = x1_ref[rows, :] + _dot(o, wo_ref[...])
        x2_ref[rows, :] = x2
        xn = _rms(x2, gf_ref[...])
        xn_ref[rows, :] = xn
        xpair_ref[2 * b * lt:2 * (b + 1) * lt, :] = _pair_rows(xn, pair_scr)
        hi, lo = _split_bf16(xn)
        lg_ref[:, rows] = (_dot_nt(wrh_ref[...], hi) + _dot_nt(wrh_ref[...], lo)
                           + _dot_nt(wrl_ref[...], hi))


def _xattn(q, x1, mk, mv, w_o, g_ffn, wr_hi, wr_lo, t_start, t, nb, lt):
    d = q.shape[1]
    nm = mk.shape[1]
    ne = wr_hi.shape[0]
    rows = nb * lt
    assert t % rows == 0 and t_start % rows == 0
    off = t_start // rows
    row = lambda i: (i, 0)
    row_in = lambda i: (i + off, 0)
    if mk.ndim == 3:
        assert nb == 1 and mk.shape[0] == 1
        mem_spec = pl.BlockSpec((1, nm, d), lambda i: (0, 0, 0))
        scratch = []
    else:
        assert mk.shape[0] * lt == t
        mem_spec = pl.BlockSpec(memory_space=pl.ANY)
        scratch = [pltpu.VMEM((2, nb, nm, d), F32), pltpu.VMEM((2, nb, nm, d), F32),
                   pltpu.SemaphoreType.DMA((2,))]
    return pl.pallas_call(
        functools.partial(_xattn_kernel, nb=nb),
        grid=(t // rows,),
        in_specs=[pl.BlockSpec((rows, d), row_in), mem_spec, mem_spec,
                  pl.BlockSpec((rows, d), row_in),
                  _const_spec((d, d)), _const_spec((1, d)),
                  _const_spec((ne, d)), _const_spec((ne, d))],
        scratch_shapes=[pltpu.VMEM((d // (2 * LANES), 2 * lt, LANES), F32)] + scratch,
        out_specs=[pl.BlockSpec((rows, d), row), pl.BlockSpec((rows, d), row),
                   pl.BlockSpec((ne, rows), lambda i: (0, i)),
                   pl.BlockSpec((2 * rows, d // 2), row)],
        out_shape=[jax.ShapeDtypeStruct((t, d), F32), jax.ShapeDtypeStruct((t, d), F32),
                   jax.ShapeDtypeStruct((ne, t), F32),
                   jax.ShapeDtypeStruct((2 * t, d // 2), BF16)],
        compiler_params=_cparams(("arbitrary",)),
        name="mem_xattn",
    )(q, mk, mv, x1, w_o, g_ffn.reshape(1, d), wr_hi, wr_lo)


def _route_kernel(lgp_ref, lgs_ref, bias_ref, eidx_ref, gate_ref, pos_ref, cnt_ref, run_scr,
                  *, n_first):
    ne, tn = lgp_ref.shape

    @pl.when(pl.program_id(0) == 0)
    def _():
        run_scr[...] = jnp.zeros_like(run_scr)

    gsz = ne // N_GROUPS
    sc = jax.nn.sigmoid(_pick(pl.program_id(0) < n_first, lgp_ref, lgs_ref))
    sel = sc + bias_ref[...]
    s3 = sel.reshape(N_GROUPS, gsz, tn)
    ie = lax.broadcasted_iota(jnp.int32, s3.shape, 1).astype(F32)
    m1 = jnp.max(s3, axis=1, keepdims=True)
    first = jnp.min(jnp.where(s3 == m1, ie, float(gsz)), axis=1, keepdims=True)
    m2 = jnp.max(jnp.where(ie == first, -jnp.inf, s3), axis=1, keepdims=True)
    gs = m1 + m2
    ig = lax.broadcasted_iota(jnp.int32, gs.shape, 0)
    grank = jnp.zeros(gs.shape, jnp.int32)
    for g in range(N_GROUPS):
        other = gs[g:g + 1]
        ahead = jnp.where(other > gs, 1, jnp.where(other == gs, (ig > g).astype(jnp.int32), 0))
        grank = grank + ahead
    gkeep = jnp.broadcast_to(grank < TOPK_GROUPS, s3.shape)
    v = jnp.where(gkeep, s3, -jnp.inf).reshape(ne, tn)
    iv = lax.broadcasted_iota(jnp.int32, v.shape, 0)
    rank = jnp.zeros(v.shape, jnp.int32)
    for e in range(ne):
        other = v[e:e + 1]
        ahead = jnp.where(other > v, 1, jnp.where(other == v, (iv > e).astype(jnp.int32), 0))
        rank = rank + ahead
    chosen = rank < TOP_K
    gate = jnp.where(chosen, sc, 0.0)
    gate = gate / jnp.sum(gate, axis=0, keepdims=True) * ROUTE_SCALE
    ti = lax.broadcasted_iota(jnp.int32, (tn, tn), 0)
    tj = lax.broadcasted_iota(jnp.int32, (tn, tn), 1)
    incl = _dot(jnp.where(chosen, 1.0, 0.0).astype(BF16), (ti <= tj).astype(BF16))
    pos = run_scr[...] + incl - 1.0
    run_scr[...] = run_scr[...] + incl[:, tn - 1:tn]
    cnt_ref[...] = run_scr[...].astype(jnp.int32)
    ef = iv.astype(F32)
    for k in range(TOP_K):
        hit = rank == k
        pick = lambda a: jnp.sum(jnp.where(hit, a, 0.0), axis=0, keepdims=True)
        eidx_ref[k:k + 1, :] = pick(ef).astype(jnp.int32)
        gate_ref[k:k + 1, :] = pick(gate)
        pos_ref[k:k + 1, :] = pick(pos).astype(jnp.int32)


def _route(lg_p, lg_s, router_bias):
    ne = lg_p.shape[0]
    t = lg_p.shape[1] + lg_s.shape[1]
    tn = TOKEN_TILE
    n_first = lg_p.shape[1] // tn
    kt = lambda i: (0, i)
    return pl.pallas_call(
        functools.partial(_route_kernel, n_first=n_first),
        grid=(t // tn,),
        in_specs=_pair_specs((ne, tn), 1, n_first) + [_const_spec((ne, 1))],
        out_specs=[pl.BlockSpec((TOP_K, tn), kt), pl.BlockSpec((TOP_K, tn), kt),
                   pl.BlockSpec((TOP_K, tn), kt), pl.BlockSpec((ne, 1), lambda i: (0, 0))],
        out_shape=[jax.ShapeDtypeStruct((TOP_K, t), jnp.int32),
                   jax.ShapeDtypeStruct((TOP_K, t), F32),
                   jax.ShapeDtypeStruct((TOP_K, t), jnp.int32),
                   jax.ShapeDtypeStruct((ne, 1), jnp.int32)],
        scratch_shapes=[pltpu.VMEM((ne, 1), F32)],
        compiler_params=_cparams(("arbitrary",)),
        name="router",
    )(lg_p, lg_s, router_bias.astype(F32).reshape(ne, 1))


def _moe_plan(eidx, posk, counts, n_tiles_max):
    ne = counts.shape[0]
    ntile = (counts + EXPERT_TILE - 1) // EXPERT_TILE
    cum = jnp.cumsum(ntile)
    base = (cum - ntile) * EXPERT_TILE
    n_tiles = cum[-1:]
    experts = jnp.arange(ne, dtype=jnp.int32)
    last_used = jnp.max(jnp.where(ntile > 0, experts, 0))
    tiles = jnp.arange(n_tiles_max, dtype=jnp.int32)
    tile_expert = jnp.sum(tiles[:, None] >= cum[None, :], axis=1)
    tile_expert = jnp.minimum(tile_expert, last_used).astype(jnp.int32)
    partial = jnp.any((tiles[:, None] == cum[None, :] - 1) & (counts % EXPERT_TILE != 0)[None, :],
                      axis=1)
    zfill = (partial | (tiles >= n_tiles[0])).astype(jnp.int32)
    slot = jnp.sum(jnp.where(eidx[..., None] == experts, base, 0), axis=-1) + posk
    return slot.astype(jnp.int32), tile_expert, n_tiles.astype(jnp.int32), zfill


def _row_copy_wait(ref, n_rows, sem):
    pltpu.make_async_copy(ref.at[pl.ds(0, n_rows)], ref.at[pl.ds(0, n_rows)], sem).wait()


def _dispatch_kernel(zfill_ref, slot_ref, xp_ref, xs_ref, xs_hbm, zero_scr, sem, zsem, *, n_first):
    tm = xp_ref.shape[0] // 2
    i = pl.program_id(0)

    @pl.when(i == 0)
    def _():
        zero_scr[...] = jnp.zeros_like(zero_scr)

        def zero_tile(n):
            row0 = pl.multiple_of(n * 2 * EXPERT_TILE, 2 * EXPERT_TILE)
            return pltpu.make_async_copy(zero_scr, xs_hbm.at[pl.ds(row0, 2 * EXPERT_TILE)], zsem)

        def fill(n, _):
            @pl.when(zfill_ref[n] != 0)
            def _():
                zero_tile(n).start()
            return 0

        def drain(n, _):
            @pl.when(zfill_ref[n] != 0)
            def _():
                zero_tile(n).wait()
            return 0

        lax.fori_loop(0, zfill_ref.shape[0], fill, 0)
        lax.fori_loop(0, zfill_ref.shape[0], drain, 0)

    def scatter_rows(x_ref):
        def body(r, _):
            for k in range(TOP_K):
                s = slot_ref[0, 0, r * TOP_K + k]
                pltpu.make_async_copy(_pair_at(x_ref, r), _pair_at(xs_hbm, s),
                                      sem).start(priority=k % 2)
            return 0
        lax.fori_loop(0, tm, body, 0)

    @pl.when(i < n_first)
    def _():
        scatter_rows(xp_ref)

    @pl.when(i >= n_first)
    def _():
        scatter_rows(xs_ref)

    _row_copy_wait(xs_hbm, 2 * tm * TOP_K, sem)


def _dispatch(xpair_p, xpair_s, slot_tiles, zfill, n_rows):
    w = xpair_p.shape[1]
    t = (xpair_p.shape[0] + xpair_s.shape[0]) // 2
    tm = MOE_TOKEN_TILE
    n_first = xpair_p.shape[0] // (2 * tm)
    return pl.pallas_call(
        functools.partial(_dispatch_kernel, n_first=n_first),
        grid_spec=pltpu.PrefetchScalarGridSpec(
            num_scalar_prefetch=1,
            grid=(t // tm,),
            in_specs=[pl.BlockSpec((1, 1, tm * TOP_K), lambda i, *_: (i, 0, 0),
                                   memory_space=pltpu.SMEM)]
                     + _pair_specs((2 * tm, w), 0, n_first),
            out_specs=pl.BlockSpec(memory_space=pl.ANY),
            scratch_shapes=[pltpu.VMEM((2 * EXPERT_TILE, w), BF16),
                            pltpu.SemaphoreType.DMA, pltpu.SemaphoreType.DMA]),
        out_shape=jax.ShapeDtypeStruct((2 * n_rows, w), BF16),
        compiler_params=_cparams(("arbitrary",)),
        name="moe_dispatch",
    )(zfill, slot_tiles, xpair_p, xpair_s)


def _expert_kernel(te_ref, nt_ref, xs_ref, wg_ref, wu_ref, wd_ref, ys_ref, wg_scr, wu_scr, wd_scr,
                   pair_scr):
    i = pl.program_id(0)

    @pl.when(i < nt_ref[0])
    def _():
        @pl.when((i == 0) | (te_ref[i] != te_ref[jnp.maximum(i - 1, 0)]))
        def _():
            wg_scr[...] = wg_ref[0].astype(BF16)
            wu_scr[...] = wu_ref[0].astype(BF16)
            wd_scr[...] = wd_ref[0].astype(BF16)

        x = _unpair_rows(xs_ref[...], pair_scr).astype(BF16)
        h = jax.nn.silu(_dot(x, wg_scr[...])) * _dot(x, wu_scr[...])
        ys_ref[...] = _pair_rows(_dot(h.astype(BF16), wd_scr[...]), pair_scr)

    @pl.when(i >= nt_ref[0])
    def _():
        ys_ref[...] = jnp.zeros_like(ys_ref)


def _experts(xs, tile_expert, n_tiles, w_gate, w_up, w_down):
    n_pair_rows, w = xs.shape
    ne, d, de = w_gate.shape
    blk = (2 * EXPERT_TILE, w)
    tile = lambda i, te, nt: (jnp.minimum(i, nt[0] - 1), 0)
    out_tile = lambda i, te, nt: (i, 0)
    wsel = lambda i, te, nt: (te[i], 0, 0)
    return pl.pallas_call(
        _expert_kernel,
        grid_spec=pltpu.PrefetchScalarGridSpec(
            num_scalar_prefetch=2,
            grid=(n_pair_rows // blk[0],),
            in_specs=[pl.BlockSpec(blk, tile),
                      pl.BlockSpec((1, d, de), wsel), pl.BlockSpec((1, d, de), wsel),
                      pl.BlockSpec((1, de, d), wsel)],
            out_specs=pl.BlockSpec(blk, out_tile),
            scratch_shapes=[pltpu.VMEM((d, de), BF16), pltpu.VMEM((d, de), BF16),
                            pltpu.VMEM((de, d), BF16),
                            pltpu.VMEM((w // LANES, blk[0], LANES), F32)]),
        out_shape=jax.ShapeDtypeStruct((n_pair_rows, w), BF16),
        compiler_params=_cparams(("arbitrary",)),
        name="moe_experts",
    )(tile_expert, n_tiles, xs, w_gate, w_up, w_down)


def _combine_kernel(slot_ref, gate_ref, xnp_ref, xns_ref, x2p_ref, x2s_ref, sg_ref, su_ref, sd_ref,
                    gf_ref, ys_hbm, yp_ref, ysamp_ref, buf_a, buf_b, base_scr, y_scr, pair_scr,
                    sem_a, sem_b, *, n_first):
    tm = xnp_ref.shape[0]
    i = pl.program_id(0)
    n_steps = pl.num_programs(0)
    first = i < n_first
    rb_rows = SUBLANES

    def issue_rows(tile, buf, sem, r0, n):
        for rr in range(n):
            for k in range(TOP_K):
                s = slot_ref[(tile * tm + r0 + rr) * TOP_K + k]
                pltpu.make_async_copy(_pair_at(ys_hbm, s), _pair_at(buf.at[k], r0 + rr),
                                      sem).start(priority=k % 2)

    @pl.when(i == 0)
    def _():
        def body(r, _):
            issue_rows(0, buf_a, sem_a, r, 1)
            return 0
        lax.fori_loop(0, tm, body, 0)

    x = _pick(first, xnp_ref, xns_ref).astype(BF16)
    hs = jax.nn.silu(_dot(x, sg_ref[...])) * _dot(x, su_ref[...])
    base_scr[...] = _pick(first, x2p_ref, x2s_ref) + _dot(hs.astype(BF16), sd_ref[...])
    nxt_tile = jnp.where(i + 1 < n_steps, i + 1, 0)

    def run(cur, sem_cur, nxt, sem_nxt):
        _row_copy_wait(ys_hbm, 2 * tm * TOP_K, sem_cur)

        for rb in range(tm // rb_rows):
            r0 = rb * rb_rows
            issue_rows(nxt_tile, nxt, sem_nxt, r0, rb_rows)
            pairs = pl.ds(2 * r0, 2 * rb_rows)
            g = gate_ref[pairs, :]
            acc = g[:, 0:1] * cur[0, pairs, :].astype(F32)
            for k in range(1, TOP_K):
                acc = acc + g[:, k:k + 1] * cur[k, pairs, :].astype(F32)
            rows = base_scr[pl.ds(r0, rb_rows), :] + _unpair_rows(acc, pair_scr)
            y_scr[pl.ds(r0, rb_rows), :] = _rms(rows, gf_ref[...])

        @pl.when(i == n_steps - 1)
        def _():
            _row_copy_wait(ys_hbm, 2 * tm * TOP_K, sem_nxt)

    @pl.when(i % 2 == 0)
    def _():
        run(buf_a, sem_a, buf_b, sem_b)

    @pl.when(i % 2 == 1)
    def _():
        run(buf_b, sem_b, buf_a, sem_a)

    @pl.when(first)
    def _():
        yp_ref[...] = y_scr[...]

    @pl.when(jnp.logical_not(first))
    def _():
        ysamp_ref[...] = y_scr[...]


def _combine(slot_flat, gates, xn_p, xn_s, x2_p, x2_s, ws_gate, ws_up, ws_down, g_final, ys):
    d = x2_p.shape[1]
    tp, tsamp = x2_p.shape[0], x2_s.shape[0]
    dsh = ws_gate.shape[1]
    tm = COMBINE_TILE
    n_first = tp // tm
    row = lambda i, *_: (i, 0)
    pair = lambda: _pair_specs((tm, d), 0, n_first)
    return pl.pallas_call(
        functools.partial(_combine_kernel, n_first=n_first),
        grid_spec=pltpu.PrefetchScalarGridSpec(
            num_scalar_prefetch=1,
            grid=((tp + tsamp) // tm,),
            in_specs=[pl.BlockSpec((2 * tm, TOP_K), row)] + pair() + pair()
                     + [_const_spec((d, dsh)), _const_spec((d, dsh)), _const_spec((dsh, d)),
                        _const_spec((1, d)), pl.BlockSpec(memory_space=pl.ANY)],
            out_specs=pair(),
            scratch_shapes=[pltpu.VMEM((TOP_K, 2 * tm, d // 2), BF16),
                            pltpu.VMEM((TOP_K, 2 * tm, d // 2), BF16),
                            pltpu.VMEM((tm, d), F32), pltpu.VMEM((tm, d), F32),
                            pltpu.VMEM((d // (2 * LANES), 2 * SUBLANES, LANES), F32),
                            pltpu.SemaphoreType.DMA, pltpu.SemaphoreType.DMA]),
        out_shape=[jax.ShapeDtypeStruct((tp, d), F32), jax.ShapeDtypeStruct((tsamp, d), F32)],
        compiler_params=_cparams(("arbitrary",)),
        name="moe_combine",
    )(slot_flat, gates, xn_p, xn_s, x2_p, x2_s, ws_gate.astype(BF16), ws_up.astype(BF16),
      ws_down.astype(BF16), g_final.reshape(1, d), ys)


def _moe(xn_p, xn_s, xpair_p, xpair_s, x2_p, x2_s, eidx, gate, posk, counts, w_gate, w_up, w_down,
         ws_gate, ws_up, ws_down, g_final):
    t = xn_p.shape[0] + xn_s.shape[0]
    ne = w_gate.shape[0]
    n_tiles_max = t * TOP_K // EXPERT_TILE + ne
    slot, tile_expert, n_tiles, zfill = _moe_plan(eidx, posk, counts[:, 0], n_tiles_max)
    slot_tm = jnp.transpose(slot)
    slot_tiles = slot_tm.reshape(t // MOE_TOKEN_TILE, 1, MOE_TOKEN_TILE * TOP_K)
    xs = _dispatch(xpair_p, xpair_s, slot_tiles, zfill, n_tiles_max * EXPERT_TILE)
    ys = _experts(xs, tile_expert, n_tiles, w_gate, w_up, w_down)
    gates2 = jnp.repeat(jnp.transpose(gate), 2, axis=0)
    return _combine(slot_tm.reshape(t * TOP_K), gates2, xn_p, xn_s, x2_p, x2_s,
                    ws_gate, ws_up, ws_down, g_final, ys)


def kernel(x_prompt, x_sample, cache_mem_k, cache_mem_v, state_s5_re, state_s5_im, state_gla, mem_prompt, w_in, w_a2, b_a2, lam_re, lam_im, log_dt, b_re, b_im, c_re, c_im, d_skip, w_glu, b_glu, g_s5_out, g_gla_head, w_out, g_mix, g_xattn, g_mem, w_q, w_k, w_v, w_o, g_ffn, w_router, router_bias, w_gate, w_up, w_down, ws_gate, ws_up, ws_down, g_final):
    depth = w_in.shape[0]
    assert depth == 1, "one trunk layer"
    bp, lp, d = x_prompt.shape
    bs, ls, _ = x_sample.shape
    assert bp == 1 and ls == CHUNK and lp % CHUNK == 0
    n_mem = mem_prompt.shape[1]
    g, p = lam_re.shape[1:]
    d_s5 = g * S5_GROUP
    d_qk = GLA_HEADS * GLA_DK
    d_gla = w_out.shape[1] - d_s5
    ns = d_s5 // LANES
    ne = w_router.shape[2]
    tp, tsamp = bp * lp, bs * ls
    xp = x_prompt.reshape(tp, d)
    xs = x_sample.reshape(tsamp, d)
    sq = lambda a: a.reshape(a.shape[1:])

    mk_p, mv_p = _memory_kv(mem_prompt.reshape(n_mem, d), sq(g_mem), sq(w_k).astype(BF16),
                            sq(w_v).astype(BF16))

    u, q, k, v, a, r = _in_proj(xp, xs, sq(g_mix), sq(w_in), d_s5, d_qk, d_gla)

    ksm, psm, qsm, step_m, carry_m, a8 = _s5_coeffs(
        sq(lam_re), sq(lam_im), sq(log_dt), sq(b_re), sq(b_im), sq(c_re), sq(c_im), sq(d_skip))
    tables = tuple(_s5_expand(ksm, psm, qsm)) + (step_m, carry_m, a8)
    zero_h = jnp.zeros((ns, bp, 2 * GROUPS_PER_SLAB * p), F32)
    rows_p = lp // S5_BLOCK
    y_p, h_p = _s5_mixer(u, tables, zero_h, 0, tp, rows_per_seq=rows_p,
                         tile_rows=math.gcd(rows_p, 512))
    h0_s = _pack_s5_state(sq(state_s5_re), sq(state_s5_im), ns)
    y_s, h_s = _s5_mixer(u, tables, h0_s, tp, tsamp, rows_per_seq=ls // S5_BLOCK,
                         tile_rows=tsamp // S5_BLOCK)

    wa2 = jnp.pad(sq(w_a2), ((0, LANES - GLA_RANK), (0, 0)))
    zero_s = jnp.zeros((bp,) + state_gla.shape[2:], F32)
    gla_p, sg_p = _gla_mixer(q, k, v, a, r, wa2, sq(b_a2), sq(g_gla_head), zero_s, 0, tp,
                             carry=True, n_chunks=4)
    gla_s, sg_s = _gla_mixer(q, k, v, a, r, wa2, sq(b_a2), sq(g_gla_head), sq(state_gla), tp,
                             tsamp, carry=False, n_chunks=4)

    x1, xq = _mix_out(y_p, y_s, gla_p, gla_s, xp, xs, sq(w_glu), sq(b_glu), sq(g_s5_out),
                      sq(w_out), sq(g_xattn), sq(w_q))

    wr_hi, wr_lo = _split_bf16(jnp.transpose(sq(w_router)))
    wo = sq(w_o).astype(BF16)
    x2_p, xn_p, lg_p, xpair_p = _xattn(xq, x1, mk_p[None], mv_p[None], wo, sq(g_ffn), wr_hi,
                                       wr_lo, 0, tp, nb=1, lt=TOKEN_TILE)
    x2_s, xn_s, lg_s, xpair_s = _xattn(xq, x1, sq(cache_mem_k), sq(cache_mem_v), wo, sq(g_ffn),
                                       wr_hi, wr_lo, tp, tsamp, nb=2, lt=ls)

    eidx, gate, posk, counts = _route(lg_p, lg_s, sq(router_bias))
    y_p2, y_s2 = _moe(xn_p, xn_s, xpair_p, xpair_s, x2_p, x2_s, eidx, gate, posk, counts,
                      sq(w_gate), sq(w_up), sq(w_down), sq(ws_gate), sq(ws_up), sq(ws_down),
                      g_final)

    xh = d // X_HEADS
    re_p, im_p = _unpack_s5_state(h_p, g, p)
    re_s, im_s = _unpack_s5_state(h_s, g, p)
    return (y_p2.reshape(bp, lp, d), y_s2.reshape(bs, ls, d),
            mk_p.reshape(1, bp, n_mem, X_HEADS, xh), mv_p.reshape(1, bp, n_mem, X_HEADS, xh),
            re_p[None], im_p[None], sg_p[None], re_s[None], im_s[None], sg_s[None])
```

```python
import functools
import math

import jax
import jax.numpy as jnp
from jax import lax
from jax.experimental import pallas as pl
from jax.experimental.pallas import tpu as pltpu

F32 = jnp.float32
BF16 = jnp.bfloat16

EPS = 1e-6
CHUNK = 64
S5_GROUP = 16
GLA_HEADS = 4
GLA_DK = 128
GLA_RANK = 16
GLA_TAU = 16.0
X_HEADS = 4
N_GROUPS = 8
TOPK_GROUPS = 4
TOP_K = 8
ROUTE_SCALE = 2.5

LANES = 128
SUBLANES = 8
S5_BLOCK = SUBLANES
GROUPS_PER_SLAB = LANES // S5_GROUP
TOKEN_TILE = 512
EXPERT_TILE = 256
COMBINE_TILE = 128
VMEM_LIMIT = 56 * 1024 * 1024


def _cparams(sem):
    return pltpu.CompilerParams(dimension_semantics=sem, vmem_limit_bytes=VMEM_LIMIT)


def _const_spec(shape):
    nd = len(shape)
    return pl.BlockSpec(shape, lambda *_: (0,) * nd, pipeline_mode=pl.Buffered(1))


def _pair_specs(block, axis, n_first):
    def at(f):
        return lambda i, *_: tuple(f(i) if a == axis else 0 for a in range(len(block)))
    return [pl.BlockSpec(block, at(lambda i: jnp.minimum(i, n_first - 1))),
            pl.BlockSpec(block, at(lambda i: jnp.maximum(i - n_first, 0)))]


def _pick(first, a_ref, b_ref):
    return jnp.where(first, a_ref[...], b_ref[...])


def _rms(x, g):
    return x * lax.rsqrt(jnp.mean(x * x, axis=-1, keepdims=True) + EPS) * g


def _split_bf16(x):
    hi = x.astype(BF16)
    lo = (x - hi.astype(F32)).astype(BF16)
    return hi, lo


def _split3_bf16(x):
    a = x.astype(BF16)
    r = x - a.astype(F32)
    b = r.astype(BF16)
    c = (r - b.astype(F32)).astype(BF16)
    return a, b, c


def _dot(a, b):
    return jnp.dot(a, b, preferred_element_type=F32)


def _dot_nt(a, b):
    return lax.dot_general(a, b, (((1,), (1,)), ((), ())), preferred_element_type=F32)


def _dot_tn(a, b):
    return lax.dot_general(a, b, (((0,), (0,)), ((), ())), preferred_element_type=F32)


def _memkv_kernel(mem_ref, g_ref, wk_ref, wv_ref, mk_ref, mv_ref):
    m = _rms(mem_ref[...], g_ref[...]).astype(BF16)
    mk_ref[...] = _dot(m, wk_ref[...])
    mv_ref[...] = _dot(m, wv_ref[...])


def _memory_kv(mem, g_mem, wk, wv):
    n, d = mem.shape
    tn = 512
    return pl.pallas_call(
        _memkv_kernel,
        grid=(d // tn,),
        in_specs=[_const_spec((n, d)), _const_spec((1, d)),
                  pl.BlockSpec((d, tn), lambda j: (0, j)),
                  pl.BlockSpec((d, tn), lambda j: (0, j))],
        out_specs=[pl.BlockSpec((n, tn), lambda j: (0, j))] * 2,
        out_shape=[jax.ShapeDtypeStruct((n, d), F32)] * 2,
        compiler_params=_cparams(("arbitrary",)),
        name="memory_kv",
    )(mem, g_mem.reshape(1, d), wk, wv)


def _inproj_kernel(xp_ref, xs_ref, g_ref, wuh_ref, wul_ref, wqkv_ref, wa_ref, wr_ref,
                   u_ref, q_ref, k_ref, v_ref, a_ref, r_ref, *, n_first):
    first = pl.program_id(0) < n_first
    x = _pick(first, xp_ref, xs_ref)
    nx = _rms(x, g_ref[...])
    hi, lo = _split_bf16(nx)
    u = _dot(hi, wuh_ref[...])
    for j in range(u_ref.shape[0]):
        u_ref[j] = u[:, j * LANES:(j + 1) * LANES]

    @pl.when(jnp.logical_not(first))
    def _():
        fix = _dot(lo, wuh_ref[...]) + _dot(hi, wul_ref[...])
        for j in range(u_ref.shape[0]):
            u_ref[j] += fix[:, j * LANES:(j + 1) * LANES]

    qkv = _dot(hi, wqkv_ref[...])
    dq = q_ref.shape[1]
    q_ref[...] = qkv[:, :dq].astype(BF16)
    k_ref[...] = qkv[:, dq:2 * dq].astype(BF16)
    v_ref[...] = qkv[:, 2 * dq:].astype(BF16)
    a_ref[...] = _dot(hi, wa_ref[...]) + _dot(lo, wa_ref[...])
    r_ref[...] = _dot(hi, wr_ref[...]).astype(BF16)


def _in_proj(xp, xs, g_mix, w_in, d_s5, d_qk, d_gla):
    d = xp.shape[1]
    t = xp.shape[0] + xs.shape[0]
    tm = TOKEN_TILE
    n_first = xp.shape[0] // tm
    o1 = d_s5
    o2 = o1 + 2 * d_qk + d_gla
    o3 = o2 + GLA_RANK
    wu_hi, wu_lo = _split_bf16(w_in[:, :o1])
    wqkv = w_in[:, o1:o2].astype(BF16)
    wa = jnp.pad(w_in[:, o2:o3], ((0, 0), (0, LANES - GLA_RANK))).astype(BF16)
    wr = w_in[:, o3:].astype(BF16)
    n_slab = d_s5 // LANES
    row = lambda i: (i, 0)
    return pl.pallas_call(
        functools.partial(_inproj_kernel, n_first=n_first),
        grid=(t // tm,),
        in_specs=_pair_specs((tm, d), 0, n_first) + [
            _const_spec((1, d)), _const_spec(wu_hi.shape), _const_spec(wu_lo.shape),
            _const_spec(wqkv.shape), _const_spec(wa.shape), _const_spec(wr.shape)],
        out_specs=[pl.BlockSpec((n_slab, tm, LANES), lambda i: (0, i, 0)),
                   pl.BlockSpec((tm, d_qk), row), pl.BlockSpec((tm, d_qk), row),
                   pl.BlockSpec((tm, d_gla), row), pl.BlockSpec((tm, LANES), row),
                   pl.BlockSpec((tm, d_gla), row)],
        out_shape=[jax.ShapeDtypeStruct((n_slab, t, LANES), F32),
                   jax.ShapeDtypeStruct((t, d_qk), BF16), jax.ShapeDtypeStruct((t, d_qk), BF16),
                   jax.ShapeDtypeStruct((t, d_gla), BF16), jax.ShapeDtypeStruct((t, LANES), F32),
                   jax.ShapeDtypeStruct((t, d_gla), BF16)],
        compiler_params=_cparams(("arbitrary",)),
        name="in_proj",
    )(xp, xs, g_mix.reshape(1, d), wu_hi, wu_lo, wqkv, wa, wr)


def _s5_coeffs(lam_re, lam_im, log_dt, b_re, b_im, c_re, c_im, d_skip):
    g, p = lam_re.shape
    h = b_re.shape[-1]
    nb = S5_BLOCK
    ns = g // GROUPS_PER_SLAB
    gl = GROUPS_PER_SLAB
    dt = jnp.exp(log_dt.astype(F32))[:, None]
    lr, li = lam_re.astype(F32), lam_im.astype(F32)
    mag = jnp.exp(lr * dt)
    ar, ai = mag * jnp.cos(li * dt), mag * jnp.sin(li * dt)
    den = lr * lr + li * li
    nr = ar - 1.0
    cf_r = (nr * lr + ai * li) / den
    cf_i = (ai * lr - nr * li) / den
    bb_r = cf_r[..., None] * b_re - cf_i[..., None] * b_im
    bb_i = cf_r[..., None] * b_im + cf_i[..., None] * b_re

    def cpow(n):
        e = n[:, None, None] * dt[None]
        m = jnp.exp(lr[None] * e)
        return m * jnp.cos(li[None] * e), m * jnp.sin(li[None] * e)

    pr, pi = cpow(jnp.arange(nb + 1, dtype=F32))

    bt_r, bt_i = jnp.swapaxes(bb_r, 1, 2), jnp.swapaxes(bb_i, 1, 2)
    ct_r, ct_i = jnp.swapaxes(c_re, 1, 2), jnp.swapaxes(c_im, 1, 2)

    cb_r = jnp.einsum('gcp,gph->gphc', c_re, bb_r) - jnp.einsum('gcp,gph->gphc', c_im, bb_i)
    cb_i = jnp.einsum('gcp,gph->gphc', c_re, bb_i) + jnp.einsum('gcp,gph->gphc', c_im, bb_r)
    taps = (jnp.einsum('ngp,gphc->nghc', pr[:nb], cb_r)
            - jnp.einsum('ngp,gphc->nghc', pi[:nb], cb_i))
    skip = d_skip[None, :, :, None] * jnp.eye(h, dtype=F32)[None, None]
    taps = taps + jnp.where(jnp.arange(nb)[:, None, None, None] == 0, skip, 0.0)
    ksm = jnp.swapaxes(taps.reshape(nb, ns, gl * h, h), 0, 1)

    wr, wi = pr[nb - 1::-1][:nb], pi[nb - 1::-1][:nb]
    wr, wi = wr[:, :, None, :], wi[:, :, None, :]
    inj = jnp.concatenate([wr * bt_r[None] - wi * bt_i[None],
                           wr * bt_i[None] + wi * bt_r[None]], axis=-1)
    psm = jnp.swapaxes(inj.reshape(nb, ns, gl * h, 2 * p), 0, 1)

    er = jnp.transpose(pr[1:nb + 1], (1, 2, 0))[..., None]
    ei = jnp.transpose(pi[1:nb + 1], (1, 2, 0))[..., None]
    q_r = ct_r[:, :, None, :] * er - ct_i[:, :, None, :] * ei
    q_i = -(ct_r[:, :, None, :] * ei + ct_i[:, :, None, :] * er)
    qq = jnp.stack([q_r.reshape(ns, gl * p, nb * h), q_i.reshape(ns, gl * p, nb * h)], axis=1)
    qsm = qq.reshape(ns, 2 * gl * p, nb * h)

    def slab(xr, xi):
        k = xr.shape[0]
        xr = jnp.transpose(xr.reshape(k, ns, gl * p), (1, 0, 2))
        xi = jnp.transpose(xi.reshape(k, ns, gl * p), (1, 0, 2))
        return jnp.concatenate([xr, xi], axis=-1)

    rows = jnp.arange(SUBLANES, dtype=F32)
    cr, ci = cpow(nb * rows)
    carry_m = slab(cr, ci)
    sr, si = cpow(nb * jnp.array([1.0, 2.0, 4.0], F32))
    keep = (rows[None, :] >= jnp.array([1.0, 2.0, 4.0], F32)[:, None]).astype(F32)
    step_m = slab((sr[:, None] * keep[:, :, None, None]).reshape(3 * SUBLANES, g, p),
                  (si[:, None] * keep[:, :, None, None]).reshape(3 * SUBLANES, g, p))
    step_m = step_m.reshape(ns, 3, SUBLANES, 2 * gl * p)
    a8 = carry_m[:, 1:2]
    return ksm, psm, qsm, step_m, carry_m, a8


def _s5_expand_kernel(ksm_ref, psm_ref, qsm_ref, tm_ref, ph_ref, pl_ref, qm_ref):
    nb = S5_BLOCK
    h = S5_GROUP
    w = ph_ref.shape[2]
    p = w // (2 * GROUPS_PER_SLAB)

    def spread(x, sel, keep):
        a, b, c = _split3_bf16(x)
        return jnp.where(keep, _dot(a, sel) + _dot(b, sel) + _dot(c, sel), 0.0)

    def iota(shape, axis):
        return lax.broadcasted_iota(jnp.int32, shape, axis)

    sel_c = (iota((h, LANES), 1) % h == iota((h, LANES), 0)).astype(BF16)
    keep_t = iota((LANES, LANES), 0) // h == iota((LANES, LANES), 1) // h
    tm_ref[...] = jnp.zeros_like(tm_ref)
    for tau in range(nb):
        blk = spread(ksm_ref[0, tau], sel_c, keep_t).astype(BF16)
        for s in range(nb - tau):
            t = s + tau
            tm_ref[0, s * LANES:(s + 1) * LANES, t * LANES:(t + 1) * LANES] = blk

    half = w // 2
    keep_p = iota((LANES, half), 0) // h == iota((LANES, half), 1) // p
    for ri in range(2):
        sel_p = (iota((2 * p, half), 0) == iota((2 * p, half), 1) % p + ri * p).astype(BF16)
        for s in range(nb):
            blk = spread(psm_ref[0, s], sel_p, keep_p)
            hi, lo = _split_bf16(blk)
            ph_ref[0, s * LANES:(s + 1) * LANES, ri * half:(ri + 1) * half] = hi
            pl_ref[0, s * LANES:(s + 1) * LANES, ri * half:(ri + 1) * half] = lo

    keep_q = (iota((w, LANES), 0) % half) // p == iota((w, LANES), 1) // h
    for t in range(nb):
        sel_q = (iota((nb * h, LANES), 0) == iota((nb * h, LANES), 1) % h + t * h).astype(BF16)
        qm_ref[0, :, t * LANES:(t + 1) * LANES] = spread(qsm_ref[0], sel_q, keep_q).astype(BF16)


def _s5_expand(ksm, psm, qsm):
    ns = ksm.shape[0]
    w = qsm.shape[1]
    k = S5_BLOCK * LANES
    blk = lambda shape: pl.BlockSpec((1,) + shape, lambda j: (j,) + (0,) * len(shape))
    return pl.pallas_call(
        _s5_expand_kernel,
        grid=(ns,),
        in_specs=[blk(ksm.shape[1:]), blk(psm.shape[1:]), blk(qsm.shape[1:])],
        out_specs=[blk((k, k)), blk((k, w)), blk((k, w)), blk((w, k))],
        out_shape=[jax.ShapeDtypeStruct((ns, k, k), BF16), jax.ShapeDtypeStruct((ns, k, w), BF16),
                   jax.ShapeDtypeStruct((ns, k, w), BF16), jax.ShapeDtypeStruct((ns, w, k), BF16)],
        compiler_params=_cparams(("arbitrary",)),
        name="s5_tables",
    )(ksm, psm, qsm)


def _cmul(xr, xi, mr, mi):
    return xr * mr - xi * mi, xr * mi + xi * mr


def _s5_kernel(u_ref, ph_ref, pl_ref, tm_ref, qm_ref, step_ref, carry_ref, a8_ref, h0_ref,
               y_ref, hout_ref, s_scr, hin_scr, c_scr, *, rows_per_seq):
    ti = pl.program_id(1)
    rows = s_scr.shape[0]
    half = s_scr.shape[1] // 2
    nb = S5_BLOCK
    u8 = jnp.concatenate([u_ref[0, pl.ds(s, rows, stride=nb), :] for s in range(nb)], axis=1)
    hi, lo = _split_bf16(u8)
    per_block_seq = rows_per_seq == SUBLANES
    if per_block_seq:
        s_scr[...] = _dot(hi, ph_ref[0]) + _dot(lo, ph_ref[0]) + _dot(hi, pl_ref[0])
    else:
        s_scr[...] = _dot(hi, ph_ref[0])

    if not per_block_seq:
        @pl.when(ti == 0)
        def _():
            c_scr[...] = h0_ref[0]

    not_first = (lax.broadcasted_iota(jnp.int32, (SUBLANES, 1), 0) >= 1).astype(F32)
    a8r, a8i = a8_ref[0, :, :half], a8_ref[0, :, half:]
    cmr, cmi = carry_ref[0, :, :half], carry_ref[0, :, half:]

    def body(rb, _):
        r0 = pl.multiple_of(rb * SUBLANES, SUBLANES)
        sb = s_scr[pl.ds(r0, SUBLANES), :]
        x = pltpu.roll(sb, 1, 0) * not_first
        xr, xi = x[:, :half], x[:, half:]
        for k, sh in enumerate((1, 2, 4)):
            m = step_ref[0, k]
            pr, pi = _cmul(pltpu.roll(xr, sh, 0), pltpu.roll(xi, sh, 0), m[:, :half], m[:, half:])
            xr, xi = xr + pr, xi + pi
        c = h0_ref[0, pl.ds(rb, 1), :] if per_block_seq else c_scr[...]
        cr, ci = c[:, :half], c[:, half:]
        pr, pi = _cmul(cr, ci, cmr, cmi)
        hr, hi_ = xr + pr, xi + pi
        hin_scr[pl.ds(r0, SUBLANES), :half] = hr
        hin_scr[pl.ds(r0, SUBLANES), half:] = hi_
        nr, ni = _cmul(hr[SUBLANES - 1:], hi_[SUBLANES - 1:], a8r, a8i)
        cn = jnp.concatenate([nr, ni], axis=1) + sb[SUBLANES - 1:]
        if per_block_seq:
            hout_ref[0, pl.ds(rb, 1), :] = cn
        else:
            c_scr[...] = cn
        return 0

    lax.fori_loop(0, rows // SUBLANES, body, 0)
    if not per_block_seq:
        hout_ref[0] = c_scr[...]

    y8 = _dot(hi, tm_ref[0]) + _dot(hin_scr[...].astype(BF16), qm_ref[0])
    for t in range(nb):
        y_ref[0, pl.ds(t, rows, stride=nb), :] = y8[:, t * LANES:(t + 1) * LANES]


def _s5_mixer(u_slabs, tables, h0, t_start, t, rows_per_seq, tile_rows):
    tm, pm_hi, pm_lo, qm, step_m, carry_m, a8 = tables
    ns = u_slabs.shape[0]
    rows = t // S5_BLOCK
    n_tiles = rows // tile_rows
    assert rows % tile_rows == 0 and t_start % (tile_rows * S5_BLOCK) == 0
    off = t_start // (tile_rows * S5_BLOCK)
    n_seq = h0.shape[1]
    w = pm_hi.shape[-1]
    slab3 = lambda j, i: (j, 0, 0)
    kern = functools.partial(_s5_kernel, rows_per_seq=rows_per_seq)
    return pl.pallas_call(
        kern,
        grid=(ns, n_tiles),
        in_specs=[pl.BlockSpec((1, tile_rows * S5_BLOCK, LANES), lambda j, i: (j, i + off, 0)),
                  pl.BlockSpec((1,) + pm_hi.shape[1:], slab3),
                  pl.BlockSpec((1,) + pm_lo.shape[1:], slab3),
                  pl.BlockSpec((1,) + tm.shape[1:], slab3),
                  pl.BlockSpec((1,) + qm.shape[1:], slab3),
                  pl.BlockSpec((1,) + step_m.shape[1:], lambda j, i: (j, 0, 0, 0)),
                  pl.BlockSpec((1,) + carry_m.shape[1:], slab3),
                  pl.BlockSpec((1,) + a8.shape[1:], slab3),
                  pl.BlockSpec((1, n_seq, w), slab3)],
        out_specs=[pl.BlockSpec((1, tile_rows * S5_BLOCK, LANES), lambda j, i: (j, i, 0)),
                   pl.BlockSpec((1, n_seq, w), slab3)],
        out_shape=[jax.ShapeDtypeStruct((ns, t, LANES), F32),
                   jax.ShapeDtypeStruct((ns, n_seq, w), F32)],
        scratch_shapes=[pltpu.VMEM((tile_rows, w), F32), pltpu.VMEM((tile_rows, w), F32),
                        pltpu.VMEM((1, w), F32)],
        compiler_params=_cparams(("arbitrary", "arbitrary")),
        name="s5_mixer",
    )(u_slabs, pm_hi, pm_lo, tm, qm, step_m, carry_m, a8, h0)


def _pack_s5_state(re, im, ns):
    b = re.shape[0]
    r = re.astype(F32).reshape(b, ns, -1)
    i = im.astype(F32).reshape(b, ns, -1)
    return jnp.transpose(jnp.concatenate([r, i], axis=-1), (1, 0, 2))


def _unpack_s5_state(hc, g, p):
    ns, b, w = hc.shape
    hc = jnp.transpose(hc, (1, 0, 2))
    re = hc[:, :, :w // 2].reshape(b, g, p)
    im = hc[:, :, w // 2:].reshape(b, g, p)
    return re, im


def _gla_kernel(q_ref, k_ref, v_ref, a_ref, r_ref, wa2_ref, ba2_ref, gh_ref, s0_ref,
                o_ref, sout_ref, st_scr, *, carry, n_chunks):
    step = pl.program_id(0)
    dk = GLA_DK
    dv = v_ref.shape[1] // GLA_HEADS
    c = CHUNK
    scale = dk ** -0.5
    ri = lax.broadcasted_iota(jnp.int32, (c, c), 0)
    ci = lax.broadcasted_iota(jnp.int32, (c, c), 1)
    causal = ri >= ci
    tril = causal.astype(BF16)
    eye_dk = (lax.broadcasted_iota(jnp.int32, (dk, dk), 0)
              == lax.broadcasted_iota(jnp.int32, (dk, dk), 1))

    if carry:
        @pl.when(step == 0)
        def _():
            st_scr[...] = s0_ref[0]

    wa_hi, wa_lo = _split_bf16(wa2_ref[...])
    for n in range(n_chunks):
        rows = slice(n * c, (n + 1) * c)
        a_hi, a_lo = _split_bf16(a_ref[rows, :])
        logit = _dot(a_hi, wa_hi) + _dot(a_lo, wa_hi) + _dot(a_hi, wa_lo) + ba2_ref[...]
        g = jax.nn.log_sigmoid(logit) * (1.0 / GLA_TAU)
        g1, g2, g3 = _split3_bf16(g)
        bcum = _dot(tril, g1) + _dot(tril, g2) + _dot(tril, g3)
        for h in range(GLA_HEADS):
            ks = slice(h * dk, (h + 1) * dk)
            vs = slice(h * dv, (h + 1) * dv)
            b = bcum[:, ks]
            qh = q_ref[rows, ks].astype(F32) * scale
            kh = k_ref[rows, ks].astype(F32)
            vh = v_ref[rows, vs]
            state = s0_ref[n, h] if not carry else st_scr[h]
            qe = (qh * jnp.exp(b)).astype(BF16)
            ke = (kh * jnp.exp(-b)).astype(BF16)
            att = jnp.where(causal, _dot_nt(qe, ke), 0.0)
            o = _dot(att.astype(BF16), vh) + _dot(qe, state.astype(BF16))
            blast = b[c - 1:c, :]
            kd = (kh * jnp.exp(blast - b)).astype(BF16)
            decay = jnp.sum(jnp.where(eye_dk, jnp.exp(blast), 0.0), axis=1, keepdims=True)
            new_state = decay * state + _dot_tn(kd, vh)
            if carry:
                st_scr[h] = new_state
            else:
                sout_ref[n, h] = new_state
            on = _rms(o, gh_ref[...])
            rr = r_ref[rows, vs].astype(F32)
            o_ref[rows, vs] = (on * (rr * jax.nn.sigmoid(rr))).astype(BF16)
    if carry:
        sout_ref[0] = st_scr[...]


def _gla_mixer(q, k, v, a, r, wa2, ba2, g_head, s0, t_start, t, carry, n_chunks):
    dqk = q.shape[1]
    dvt = v.shape[1]
    hh, dk, dv = s0.shape[1:]
    rows = n_chunks * CHUNK
    assert t % rows == 0 and t_start % rows == 0
    off = t_start // rows
    row = lambda i: (i, 0)
    row_in = lambda i: (i + off, 0)
    if carry:
        sblk, smap = (1, hh, dk, dv), (lambda i: (0, 0, 0, 0))
    else:
        sblk, smap = (n_chunks, hh, dk, dv), (lambda i: (i, 0, 0, 0))
    kern = functools.partial(_gla_kernel, carry=carry, n_chunks=n_chunks)
    return pl.pallas_call(
        kern,
        grid=(t // rows,),
        in_specs=[pl.BlockSpec((rows, dqk), row_in), pl.BlockSpec((rows, dqk), row_in),
                  pl.BlockSpec((rows, dvt), row_in), pl.BlockSpec((rows, LANES), row_in),
                  pl.BlockSpec((rows, dvt), row_in),
                  _const_spec(wa2.shape), _const_spec((1, dqk)), _const_spec((1, dv)),
                  pl.BlockSpec(sblk, smap)],
        out_specs=[pl.BlockSpec((rows, dvt), row), pl.BlockSpec(sblk, smap)],
        out_shape=[jax.ShapeDtypeStruct((t, dvt), BF16), jax.ShapeDtypeStruct(s0.shape, F32)],
        scratch_shapes=[pltpu.VMEM((hh, dk, dv), F32)],
        compiler_params=_cparams(("arbitrary",)),
        name="gla_mixer",
    )(q, k, v, a, r, wa2, ba2.reshape(1, dqk), g_head.reshape(1, dv), s0)


def _mixout_kernel(yp_ref, ys_ref, gp_ref, gs_ref, xp_ref, xs_ref, wglu_ref, bglu_ref, gs5_ref,
                   wout_ref, gx_ref, wq_ref, x1_ref, q_ref, *, n_first):
    first = pl.program_id(0) < n_first
    yb = _pick(first, yp_ref, ys_ref)
    y = jnp.concatenate([yb[j] for j in range(yb.shape[0])], axis=1)
    z = jax.nn.gelu(y)
    gate = jax.nn.sigmoid(_dot(z.astype(BF16), wglu_ref[...]) + bglu_ref[...])
    s5 = _rms(z * gate, gs5_ref[...])
    cat = jnp.concatenate([s5.astype(BF16), _pick(first, gp_ref, gs_ref)], axis=1)
    x1 = _pick(first, xp_ref, xs_ref) + _dot(cat, wout_ref[...])
    x1_ref[...] = x1
    q_ref[...] = _dot(_rms(x1, gx_ref[...]).astype(BF16), wq_ref[...]).astype(BF16)


def _mix_out(y_p, y_s, gla_p, gla_s, xp, xs, w_glu, b_glu, g_s5, w_out, g_x, w_q):
    d = xp.shape[1]
    t = xp.shape[0] + xs.shape[0]
    ns = y_p.shape[0]
    ds5 = ns * LANES
    dg = gla_p.shape[1]
    tm = TOKEN_TILE // 2
    n_first = xp.shape[0] // tm
    row = lambda i: (i, 0)
    return pl.pallas_call(
        functools.partial(_mixout_kernel, n_first=n_first),
        grid=(t // tm,),
        in_specs=(_pair_specs((ns, tm, LANES), 1, n_first) + _pair_specs((tm, dg), 0, n_first)
                  + _pair_specs((tm, d), 0, n_first)
                  + [_const_spec((ds5, ds5)), _const_spec((1, ds5)), _const_spec((1, ds5)),
                     _const_spec((ds5 + dg, d)), _const_spec((1, d)), _const_spec((d, d))]),
        out_specs=[pl.BlockSpec((tm, d), row), pl.BlockSpec((tm, d), row)],
        out_shape=[jax.ShapeDtypeStruct((t, d), F32), jax.ShapeDtypeStruct((t, d), BF16)],
        compiler_params=_cparams(("arbitrary",)),
        name="mix_out",
    )(y_p, y_s, gla_p, gla_s, xp, xs, w_glu.astype(BF16), b_glu.reshape(1, ds5),
      g_s5.reshape(1, ds5), w_out.astype(BF16), g_x.reshape(1, d), w_q.astype(BF16))


def _xattn_kernel(q_ref, mk_ref, mv_ref, x1_ref, wo_ref, gf_ref, wrh_ref, wrl_ref,
                  x2_ref, xn_ref, lg_ref, *cache_scratch, nb):
    d = q_ref.shape[1]
    lt = q_ref.shape[0] // nb
    hd = d // X_HEADS
    scale = hd ** -0.5
    if cache_scratch:
        kbuf, vbuf, sem = cache_scratch
        i = pl.program_id(0)

        def fetch(step, slot):
            for b in range(nb):
                for h in range(X_HEADS):
                    for src, dst in ((mk_ref, kbuf), (mv_ref, vbuf)):
                        pltpu.make_async_copy(src.at[step * nb + b, :, h, :],
                                              dst.at[slot, b, :, pl.ds(h * hd, hd)],
                                              sem.at[slot]).start()

        @pl.when(i == 0)
        def _():
            fetch(0, 0)

        @pl.when(i + 1 < pl.num_programs(0))
        def _():
            fetch(i + 1, (i + 1) % 2)

        slot = i % 2
        for buf in (kbuf, vbuf):
            pltpu.make_async_copy(buf.at[slot], buf.at[slot], sem.at[slot]).wait()
        memory = lambda b: (kbuf[slot, b], vbuf[slot, b])
    else:
        memory = lambda b: (mk_ref[b], mv_ref[b])
    for b in range(nb):
        rows = slice(b * lt, (b + 1) * lt)
        mk, mv = (m.astype(BF16) for m in memory(b))
        outs = []
        for h in range(X_HEADS):
            hs = slice(h * hd, (h + 1) * hd)
            s = _dot_nt(q_ref[rows, hs], mk[:, hs]) * scale
            p = jnp.exp(s - jnp.max(s, axis=-1, keepdims=True))
            denom = jnp.sum(p, axis=-1, keepdims=True)
            outs.append((_dot(p.astype(BF16), mv[:, hs]) / denom).astype(BF16))
        o = jnp.concatenate(outs, axis=1)
        x2 = x1_ref[rows, :] + _dot(o, wo_ref[...])
        x2_ref[rows, :] = x2
        xn = _rms(x2, gf_ref[...])
        xn_ref[rows, :] = xn
        hi, lo = _split_bf16(xn)
        lg_ref[:, rows] = (_dot_nt(wrh_ref[...], hi) + _dot_nt(wrh_ref[...], lo)
                           + _dot_nt(wrl_ref[...], hi))


def _xattn(q, x1, mk, mv, w_o, g_ffn, wr_hi, wr_lo, t_start, t, nb, lt):
    d = q.shape[1]
    nm = mk.shape[1]
    ne = wr_hi.shape[0]
    rows = nb * lt
    assert t % rows == 0 and t_start % rows == 0
    off = t_start // rows
    row = lambda i: (i, 0)
    row_in = lambda i: (i + off, 0)
    if mk.ndim == 3:
        assert nb == 1 and mk.shape[0] == 1
        mem_spec = pl.BlockSpec((1, nm, d), lambda i: (0, 0, 0))
        scratch = []
    else:
        assert mk.shape[0] * lt == t
        mem_spec = pl.BlockSpec(memory_space=pl.ANY)
        scratch = [pltpu.VMEM((2, nb, nm, d), F32), pltpu.VMEM((2, nb, nm, d), F32),
                   pltpu.SemaphoreType.DMA((2,))]
    return pl.pallas_call(
        functools.partial(_xattn_kernel, nb=nb),
        grid=(t // rows,),
        in_specs=[pl.BlockSpec((rows, d), row_in), mem_spec, mem_spec,
                  pl.BlockSpec((rows, d), row_in),
                  _const_spec((d, d)), _const_spec((1, d)),
                  _const_spec((ne, d)), _const_spec((ne, d))],
        scratch_shapes=scratch,
        out_specs=[pl.BlockSpec((rows, d), row), pl.BlockSpec((rows, d), row),
                   pl.BlockSpec((ne, rows), lambda i: (0, i))],
        out_shape=[jax.ShapeDtypeStruct((t, d), F32), jax.ShapeDtypeStruct((t, d), F32),
                   jax.ShapeDtypeStruct((ne, t), F32)],
        compiler_params=_cparams(("arbitrary",)),
        name="mem_xattn",
    )(q, mk, mv, x1, w_o, g_ffn.reshape(1, d), wr_hi, wr_lo)


def _route_kernel(lgp_ref, lgs_ref, xnp_ref, xns_ref, bias_ref, eidx_ref, gate_ref, pos_ref, cnt_ref,
                  xn_ref, run_scr, *, n_first):
    ne, tn = lgp_ref.shape
    first = pl.program_id(0) < n_first
    xn_ref[...] = _pick(first, xnp_ref, xns_ref)

    @pl.when(pl.program_id(0) == 0)
    def _():
        run_scr[...] = jnp.zeros_like(run_scr)

    gsz = ne // N_GROUPS
    sc = jax.nn.sigmoid(_pick(first, lgp_ref, lgs_ref))
    sel = sc + bias_ref[...]
    s3 = sel.reshape(N_GROUPS, gsz, tn)
    ie = lax.broadcasted_iota(jnp.int32, s3.shape, 1).astype(F32)
    m1 = jnp.max(s3, axis=1, keepdims=True)
    first = jnp.min(jnp.where(s3 == m1, ie, float(gsz)), axis=1, keepdims=True)
    m2 = jnp.max(jnp.where(ie == first, -jnp.inf, s3), axis=1, keepdims=True)
    gs = m1 + m2
    ig = lax.broadcasted_iota(jnp.int32, gs.shape, 0)
    grank = jnp.zeros(gs.shape, jnp.int32)
    for g in range(N_GROUPS):
        other = gs[g:g + 1]
        ahead = jnp.where(other > gs, 1, jnp.where(other == gs, (ig > g).astype(jnp.int32), 0))
        grank = grank + ahead
    gkeep = jnp.broadcast_to(grank < TOPK_GROUPS, s3.shape)
    v = jnp.where(gkeep, s3, -jnp.inf).reshape(ne, tn)
    iv = lax.broadcasted_iota(jnp.int32, v.shape, 0)
    rank = jnp.zeros(v.shape, jnp.int32)
    for e in range(ne):
        other = v[e:e + 1]
        ahead = jnp.where(other > v, 1, jnp.where(other == v, (iv > e).astype(jnp.int32), 0))
        rank = rank + ahead
    chosen = rank < TOP_K
    gate = jnp.where(chosen, sc, 0.0)
    gate = gate / jnp.sum(gate, axis=0, keepdims=True) * ROUTE_SCALE
    ti = lax.broadcasted_iota(jnp.int32, (tn, tn), 0)
    tj = lax.broadcasted_iota(jnp.int32, (tn, tn), 1)
    incl = _dot(jnp.where(chosen, 1.0, 0.0).astype(BF16), (ti <= tj).astype(BF16))
    pos = run_scr[...] + incl - 1.0
    run_scr[...] = run_scr[...] + incl[:, tn - 1:tn]
    cnt_ref[...] = run_scr[...].astype(jnp.int32)
    ef = iv.astype(F32)
    for k in range(TOP_K):
        hit = rank == k
        pick = lambda a: jnp.sum(jnp.where(hit, a, 0.0), axis=0, keepdims=True)
        eidx_ref[k:k + 1, :] = pick(ef).astype(jnp.int32)
        gate_ref[k:k + 1, :] = pick(gate)
        pos_ref[k:k + 1, :] = pick(pos).astype(jnp.int32)


def _route(lg_p, lg_s, xn_p, xn_s, router_bias):
    ne = lg_p.shape[0]
    d = xn_p.shape[1]
    t = lg_p.shape[1] + lg_s.shape[1]
    tn = TOKEN_TILE
    n_first = lg_p.shape[1] // tn
    kt = lambda i: (0, i)
    return pl.pallas_call(
        functools.partial(_route_kernel, n_first=n_first),
        grid=(t // tn,),
        in_specs=(_pair_specs((ne, tn), 1, n_first) + _pair_specs((tn, d), 0, n_first)
                  + [_const_spec((ne, 1))]),
        out_specs=[pl.BlockSpec((TOP_K, tn), kt), pl.BlockSpec((TOP_K, tn), kt),
                   pl.BlockSpec((TOP_K, tn), kt), pl.BlockSpec((ne, 1), lambda i: (0, 0)),
                   pl.BlockSpec((tn, d), lambda i: (i, 0))],
        out_shape=[jax.ShapeDtypeStruct((TOP_K, t), jnp.int32),
                   jax.ShapeDtypeStruct((TOP_K, t), F32),
                   jax.ShapeDtypeStruct((TOP_K, t), jnp.int32),
                   jax.ShapeDtypeStruct((ne, 1), jnp.int32),
                   jax.ShapeDtypeStruct((t, d), F32)],
        scratch_shapes=[pltpu.VMEM((ne, 1), F32)],
        compiler_params=_cparams(("arbitrary",)),
        name="router",
    )(lg_p, lg_s, xn_p, xn_s, router_bias.astype(F32).reshape(ne, 1))


def _moe_plan(eidx, posk, counts, n_tiles_max):
    ne = counts.shape[0]
    ntile = (counts + EXPERT_TILE - 1) // EXPERT_TILE
    cum = jnp.cumsum(ntile)
    base = (cum - ntile) * EXPERT_TILE
    n_tiles = cum[-1:]
    experts = jnp.arange(ne, dtype=jnp.int32)
    last_used = jnp.max(jnp.where(ntile > 0, experts, 0))
    tiles = jnp.arange(n_tiles_max, dtype=jnp.int32)
    tile_expert = jnp.sum(tiles[:, None] >= cum[None, :], axis=1)
    tile_expert = jnp.minimum(tile_expert, last_used).astype(jnp.int32)
    slot = (jnp.sum(jnp.where(eidx[..., None] == experts, base, 0), axis=-1) + posk)
    slot = slot.astype(jnp.int32)
    n_k, t = slot.shape
    tok = lax.broadcasted_iota(jnp.int32, slot.shape, 1)
    _, tok_sorted = lax.sort_key_val(slot.reshape(n_k * t), tok.reshape(n_k * t))
    tok_sorted = jnp.concatenate([tok_sorted, jnp.zeros((EXPERT_TILE,), jnp.int32)])
    first_pair = jnp.cumsum(counts) - counts
    first_tile = cum - ntile
    pair0 = (jnp.take(first_pair, tile_expert)
             + (tiles - jnp.take(first_tile, tile_expert)) * EXPERT_TILE)
    pair0 = jnp.clip(pair0, 0, n_k * t).astype(jnp.int32)
    return slot, tile_expert, n_tiles.astype(jnp.int32), tok_sorted, pair0


def _row_copy_wait(ref, n_rows, sem):
    pltpu.make_async_copy(ref.at[pl.ds(0, n_rows)], ref.at[pl.ds(0, n_rows)], sem).wait()


def _expert_kernel(te_ref, nt_ref, pair0_ref, tok_ref, xn_hbm, wg_ref, wu_ref, wd_ref, ys_ref,
                   x_a, x_b, wg_scr, wu_scr, wd_scr, sem_a, sem_b):
    i = pl.program_id(0)
    n_tiles = nt_ref[0]

    def issue_row(tile, buf, sem, r, queue):
        t = tok_ref[pair0_ref[tile] + r]
        pltpu.make_async_copy(xn_hbm.at[pl.ds(t, 1)], buf.at[pl.ds(r, 1)],
                              sem).start(priority=queue)

    @pl.when(i == 0)
    def _():
        def body(rb, _):
            for u in range(SUBLANES):
                issue_row(0, x_a, sem_a, rb * SUBLANES + u, u % 2)
            return 0
        lax.fori_loop(0, EXPERT_TILE // SUBLANES, body, 0)

    nxt_tile = jnp.where(i + 1 < n_tiles, i + 1, 0)

    def run(cur, sem_cur, nxt, sem_nxt):
        _row_copy_wait(xn_hbm, EXPERT_TILE, sem_cur)

        @pl.when((i == 0) | (te_ref[i] != te_ref[jnp.maximum(i - 1, 0)]))
        def _():
            wg_scr[...] = wg_ref[0].astype(BF16)
            wu_scr[...] = wu_ref[0].astype(BF16)
            wd_scr[...] = wd_ref[0].astype(BF16)

        for r in range(EXPERT_TILE):
            issue_row(nxt_tile, nxt, sem_nxt, r, r % 2)
        x = cur[...].astype(BF16)
        h = jax.nn.silu(_dot(x, wg_scr[...])) * _dot(x, wu_scr[...])
        ys_ref[...] = _dot(h.astype(BF16), wd_scr[...])

        @pl.when(i == n_tiles - 1)
        def _():
            _row_copy_wait(xn_hbm, EXPERT_TILE, sem_nxt)

    @pl.when((i < n_tiles) & (i % 2 == 0))
    def _():
        run(x_a, sem_a, x_b, sem_b)

    @pl.when((i < n_tiles) & (i % 2 == 1))
    def _():
        run(x_b, sem_b, x_a, sem_a)

    @pl.when(i >= n_tiles)
    def _():
        ys_ref[...] = jnp.zeros_like(ys_ref)


def _experts(xn, tile_expert, n_tiles, pair0, tok_sorted, w_gate, w_up, w_down):
    d = xn.shape[1]
    ne, _, de = w_gate.shape
    n_tiles_max = tile_expert.shape[0]
    out_tile = lambda i, *_: (i, 0)
    wsel = lambda i, te, *_: (te[i], 0, 0)
    return pl.pallas_call(
        _expert_kernel,
        grid_spec=pltpu.PrefetchScalarGridSpec(
            num_scalar_prefetch=4,
            grid=(n_tiles_max,),
            in_specs=[pl.BlockSpec(memory_space=pl.ANY),
                      pl.BlockSpec((1, d, de), wsel), pl.BlockSpec((1, d, de), wsel),
                      pl.BlockSpec((1, de, d), wsel)],
            out_specs=pl.BlockSpec((EXPERT_TILE, d), out_tile),
            scratch_shapes=[pltpu.VMEM((EXPERT_TILE, d), F32), pltpu.VMEM((EXPERT_TILE, d), F32),
                            pltpu.VMEM((d, de), BF16), pltpu.VMEM((d, de), BF16),
                            pltpu.VMEM((de, d), BF16),
                            pltpu.SemaphoreType.DMA, pltpu.SemaphoreType.DMA]),
        out_shape=jax.ShapeDtypeStruct((n_tiles_max * EXPERT_TILE, d), F32),
        compiler_params=_cparams(("arbitrary",)),
        name="moe_experts",
    )(tile_expert, n_tiles, pair0, tok_sorted, xn, w_gate, w_up, w_down)


def _combine_kernel(slot_ref, gate_ref, xn_ref, x2p_ref, x2s_ref, sg_ref, su_ref, sd_ref,
                    gf_ref, ys_hbm, yp_ref, ysamp_ref, buf_a, buf_b, base_scr, y_scr, sem_a, sem_b,
                    *, n_first):
    tm = xn_ref.shape[0]
    i = pl.program_id(0)
    n_steps = pl.num_programs(0)
    first = i < n_first
    rb_rows = SUBLANES

    def issue_rows(tile, buf, sem, r0, n):
        for rr in range(n):
            for k in range(TOP_K):
                s = slot_ref[(tile * tm + r0 + rr) * TOP_K + k]
                pltpu.make_async_copy(ys_hbm.at[pl.ds(s, 1)], buf.at[k, pl.ds(r0 + rr, 1)],
                                      sem).start(priority=k % 2)

    @pl.when(i == 0)
    def _():
        def body(r, _):
            issue_rows(0, buf_a, sem_a, r, 1)
            return 0
        lax.fori_loop(0, tm, body, 0)

    x = xn_ref[...].astype(BF16)
    hs = jax.nn.silu(_dot(x, sg_ref[...])) * _dot(x, su_ref[...])
    base_scr[...] = _pick(first, x2p_ref, x2s_ref) + _dot(hs.astype(BF16), sd_ref[...])
    nxt_tile = jnp.where(i + 1 < n_steps, i + 1, 0)

    def run(cur, sem_cur, nxt, sem_nxt):
        _row_copy_wait(ys_hbm, tm * TOP_K, sem_cur)

        for rb in range(tm // rb_rows):
            r0 = rb * rb_rows
            issue_rows(nxt_tile, nxt, sem_nxt, r0, rb_rows)
            acc = base_scr[pl.ds(r0, rb_rows), :]
            g = gate_ref[pl.ds(r0, rb_rows), :]
            for k in range(TOP_K):
                acc = acc + g[:, k:k + 1] * cur[k, pl.ds(r0, rb_rows), :]
            y_scr[pl.ds(r0, rb_rows), :] = _rms(acc, gf_ref[...])

        @pl.when(i == n_steps - 1)
        def _():
            _row_copy_wait(ys_hbm, tm * TOP_K, sem_nxt)

    @pl.when(i % 2 == 0)
    def _():
        run(buf_a, sem_a, buf_b, sem_b)

    @pl.when(i % 2 == 1)
    def _():
        run(buf_b, sem_b, buf_a, sem_a)

    @pl.when(first)
    def _():
        yp_ref[...] = y_scr[...]

    @pl.when(jnp.logical_not(first))
    def _():
        ysamp_ref[...] = y_scr[...]


def _combine(slot_flat, gates, xn, x2_p, x2_s, ws_gate, ws_up, ws_down, g_final, ys):
    d = x2_p.shape[1]
    tp, tsamp = x2_p.shape[0], x2_s.shape[0]
    dsh = ws_gate.shape[1]
    tm = COMBINE_TILE
    n_first = tp // tm
    row = lambda i, *_: (i, 0)
    pair = lambda: _pair_specs((tm, d), 0, n_first)
    return pl.pallas_call(
        functools.partial(_combine_kernel, n_first=n_first),
        grid_spec=pltpu.PrefetchScalarGridSpec(
            num_scalar_prefetch=1,
            grid=((tp + tsamp) // tm,),
            in_specs=[pl.BlockSpec((tm, TOP_K), row), pl.BlockSpec((tm, d), row)] + pair()
                     + [_const_spec((d, dsh)), _const_spec((d, dsh)), _const_spec((dsh, d)),
                        _const_spec((1, d)), pl.BlockSpec(memory_space=pl.ANY)],
            out_specs=pair(),
            scratch_shapes=[pltpu.VMEM((TOP_K, tm, d), F32), pltpu.VMEM((TOP_K, tm, d), F32),
                            pltpu.VMEM((tm, d), F32), pltpu.VMEM((tm, d), F32),
                            pltpu.SemaphoreType.DMA, pltpu.SemaphoreType.DMA]),
        out_shape=[jax.ShapeDtypeStruct((tp, d), F32), jax.ShapeDtypeStruct((tsamp, d), F32)],
        compiler_params=_cparams(("arbitrary",)),
        name="moe_combine",
    )(slot_flat, gates, xn, x2_p, x2_s, ws_gate.astype(BF16), ws_up.astype(BF16),
      ws_down.astype(BF16), g_final.reshape(1, d), ys)


def _moe(xn, x2_p, x2_s, eidx, gate, posk, counts, w_gate, w_up, w_down, ws_gate, ws_up,
         ws_down, g_final):
    t = xn.shape[0]
    ne = w_gate.shape[0]
    n_tiles_max = t * TOP_K // EXPERT_TILE + ne
    slot, tile_expert, n_tiles, tok_sorted, pair0 = _moe_plan(eidx, posk, counts[:, 0],
                                                              n_tiles_max)
    ys = _experts(xn, tile_expert, n_tiles, pair0, tok_sorted, w_gate, w_up, w_down)
    return _combine(jnp.transpose(slot).reshape(t * TOP_K), jnp.transpose(gate), xn, x2_p, x2_s,
                    ws_gate, ws_up, ws_down, g_final, ys)


def kernel(x_prompt, x_sample, cache_mem_k, cache_mem_v, state_s5_re, state_s5_im, state_gla, mem_prompt, w_in, w_a2, b_a2, lam_re, lam_im, log_dt, b_re, b_im, c_re, c_im, d_skip, w_glu, b_glu, g_s5_out, g_gla_head, w_out, g_mix, g_xattn, g_mem, w_q, w_k, w_v, w_o, g_ffn, w_router, router_bias, w_gate, w_up, w_down, ws_gate, ws_up, ws_down, g_final):
    depth = w_in.shape[0]
    assert depth == 1, "one trunk layer"
    bp, lp, d = x_prompt.shape
    bs, ls, _ = x_sample.shape
    assert bp == 1 and ls == CHUNK and lp % CHUNK == 0
    n_mem = mem_prompt.shape[1]
    g, p = lam_re.shape[1:]
    d_s5 = g * S5_GROUP
    d_qk = GLA_HEADS * GLA_DK
    d_gla = w_out.shape[1] - d_s5
    ns = d_s5 // LANES
    ne = w_router.shape[2]
    tp, tsamp = bp * lp, bs * ls
    xp = x_prompt.reshape(tp, d)
    xs = x_sample.reshape(tsamp, d)
    sq = lambda a: a.reshape(a.shape[1:])

    mk_p, mv_p = _memory_kv(mem_prompt.reshape(n_mem, d), sq(g_mem), sq(w_k).astype(BF16),
                            sq(w_v).astype(BF16))

    u, q, k, v, a, r = _in_proj(xp, xs, sq(g_mix), sq(w_in), d_s5, d_qk, d_gla)

    ksm, psm, qsm, step_m, carry_m, a8 = _s5_coeffs(
        sq(lam_re), sq(lam_im), sq(log_dt), sq(b_re), sq(b_im), sq(c_re), sq(c_im), sq(d_skip))
    tables = tuple(_s5_expand(ksm, psm, qsm)) + (step_m, carry_m, a8)
    zero_h = jnp.zeros((ns, bp, 2 * GROUPS_PER_SLAB * p), F32)
    rows_p = lp // S5_BLOCK
    y_p, h_p = _s5_mixer(u, tables, zero_h, 0, tp, rows_per_seq=rows_p,
                         tile_rows=math.gcd(rows_p, 512))
    h0_s = _pack_s5_state(sq(state_s5_re), sq(state_s5_im), ns)
    y_s, h_s = _s5_mixer(u, tables, h0_s, tp, tsamp, rows_per_seq=ls // S5_BLOCK,
                         tile_rows=tsamp // S5_BLOCK)

    wa2 = jnp.pad(sq(w_a2), ((0, LANES - GLA_RANK), (0, 0)))
    zero_s = jnp.zeros((bp,) + state_gla.shape[2:], F32)
    gla_p, sg_p = _gla_mixer(q, k, v, a, r, wa2, sq(b_a2), sq(g_gla_head), zero_s, 0, tp,
                             carry=True, n_chunks=4)
    gla_s, sg_s = _gla_mixer(q, k, v, a, r, wa2, sq(b_a2), sq(g_gla_head), sq(state_gla), tp,
                             tsamp, carry=False, n_chunks=4)

    x1, xq = _mix_out(y_p, y_s, gla_p, gla_s, xp, xs, sq(w_glu), sq(b_glu), sq(g_s5_out),
                      sq(w_out), sq(g_xattn), sq(w_q))

    wr_hi, wr_lo = _split_bf16(jnp.transpose(sq(w_router)))
    wo = sq(w_o).astype(BF16)
    x2_p, xn_p, lg_p = _xattn(xq, x1, mk_p[None], mv_p[None], wo, sq(g_ffn), wr_hi, wr_lo,
                              0, tp, nb=1, lt=TOKEN_TILE)
    x2_s, xn_s, lg_s = _xattn(xq, x1, sq(cache_mem_k), sq(cache_mem_v), wo, sq(g_ffn), wr_hi,
                              wr_lo, tp, tsamp, nb=2, lt=ls)

    eidx, gate, posk, counts, xn = _route(lg_p, lg_s, xn_p, xn_s, sq(router_bias))
    y_p2, y_s2 = _moe(xn, x2_p, x2_s, eidx, gate, posk, counts, sq(w_gate), sq(w_up),
                      sq(w_down), sq(ws_gate), sq(ws_up), sq(ws_down), g_final)

    xh = d // X_HEADS
    re_p, im_p = _unpack_s5_state(h_p, g, p)
    re_s, im_s = _unpack_s5_state(h_s, g, p)
    return (y_p2.reshape(bp, lp, d), y_s2.reshape(bs, ls, d),
            mk_p.reshape(1, bp, n_mem, X_HEADS, xh), mv_p.reshape(1, bp, n_mem, X_HEADS, xh),
            re_p[None], im_p[None], sg_p[None], re_s[None], im_s[None], sg_s[None])
```

```python
import functools
import math

import jax
import jax.numpy as jnp
from jax import lax
from jax.experimental import pallas as pl
from jax.experimental.pallas import tpu as pltpu

F32 = jnp.float32
BF16 = jnp.bfloat16

EPS = 1e-6
CHUNK = 64
S5_GROUP = 16
GLA_HEADS = 4
GLA_DK = 128
GLA_RANK = 16
GLA_TAU = 16.0
X_HEADS = 4
N_GROUPS = 8
TOPK_GROUPS = 4
TOP_K = 8
ROUTE_SCALE = 2.5

LANES = 128
SUBLANES = 8
S5_BLOCK = SUBLANES
GROUPS_PER_SLAB = LANES // S5_GROUP
TOKEN_TILE = 512
EXPERT_TILE = 256
COMBINE_TILE = 128
VMEM_LIMIT = 56 * 1024 * 1024


def _cparams(sem):
    return pltpu.CompilerParams(dimension_semantics=sem, vmem_limit_bytes=VMEM_LIMIT)


def _const_spec(shape):
    nd = len(shape)
    return pl.BlockSpec(shape, lambda *_: (0,) * nd, pipeline_mode=pl.Buffered(1))


def _pair_specs(block, axis, n_first):
    def at(f):
        return lambda i, *_: tuple(f(i) if a == axis else 0 for a in range(len(block)))
    return [pl.BlockSpec(block, at(lambda i: jnp.minimum(i, n_first - 1))),
            pl.BlockSpec(block, at(lambda i: jnp.maximum(i - n_first, 0)))]


def _pick(first, a_ref, b_ref):
    return jnp.where(first, a_ref[...], b_ref[...])


def _rms(x, g):
    return x * lax.rsqrt(jnp.mean(x * x, axis=-1, keepdims=True) + EPS) * g


def _split_bf16(x):
    hi = x.astype(BF16)
    lo = (x - hi.astype(F32)).astype(BF16)
    return hi, lo


def _split3_bf16(x):
    a = x.astype(BF16)
    r = x - a.astype(F32)
    b = r.astype(BF16)
    c = (r - b.astype(F32)).astype(BF16)
    return a, b, c


def _dot(a, b):
    return jnp.dot(a, b, preferred_element_type=F32)


def _dot_nt(a, b):
    return lax.dot_general(a, b, (((1,), (1,)), ((), ())), preferred_element_type=F32)


def _dot_tn(a, b):
    return lax.dot_general(a, b, (((0,), (0,)), ((), ())), preferred_element_type=F32)


def _memkv_kernel(mem_ref, g_ref, wk_ref, wv_ref, mk_ref, mv_ref):
    m = _rms(mem_ref[...], g_ref[...]).astype(BF16)
    mk_ref[...] = _dot(m, wk_ref[...])
    mv_ref[...] = _dot(m, wv_ref[...])


def _memory_kv(mem, g_mem, wk, wv):
    n, d = mem.shape
    tn = 512
    return pl.pallas_call(
        _memkv_kernel,
        grid=(d // tn,),
        in_specs=[_const_spec((n, d)), _const_spec((1, d)),
                  pl.BlockSpec((d, tn), lambda j: (0, j)),
                  pl.BlockSpec((d, tn), lambda j: (0, j))],
        out_specs=[pl.BlockSpec((n, tn), lambda j: (0, j))] * 2,
        out_shape=[jax.ShapeDtypeStruct((n, d), F32)] * 2,
        compiler_params=_cparams(("arbitrary",)),
        name="memory_kv",
    )(mem, g_mem.reshape(1, d), wk, wv)


def _inproj_kernel(xp_ref, xs_ref, g_ref, wuh_ref, wul_ref, wqkv_ref, wa_ref, wr_ref,
                   u_ref, q_ref, k_ref, v_ref, a_ref, r_ref, *, n_first):
    first = pl.program_id(0) < n_first
    x = _pick(first, xp_ref, xs_ref)
    nx = _rms(x, g_ref[...])
    hi, lo = _split_bf16(nx)
    u = _dot(hi, wuh_ref[...])
    for j in range(u_ref.shape[0]):
        u_ref[j] = u[:, j * LANES:(j + 1) * LANES]

    @pl.when(jnp.logical_not(first))
    def _():
        fix = _dot(lo, wuh_ref[...]) + _dot(hi, wul_ref[...])
        for j in range(u_ref.shape[0]):
            u_ref[j] += fix[:, j * LANES:(j + 1) * LANES]

    qkv = _dot(hi, wqkv_ref[...])
    dq = q_ref.shape[1]
    q_ref[...] = qkv[:, :dq].astype(BF16)
    k_ref[...] = qkv[:, dq:2 * dq].astype(BF16)
    v_ref[...] = qkv[:, 2 * dq:].astype(BF16)
    a_ref[...] = _dot(hi, wa_ref[...]) + _dot(lo, wa_ref[...])
    r_ref[...] = _dot(hi, wr_ref[...]).astype(BF16)


def _in_proj(xp, xs, g_mix, w_in, d_s5, d_qk, d_gla):
    d = xp.shape[1]
    t = xp.shape[0] + xs.shape[0]
    tm = TOKEN_TILE
    n_first = xp.shape[0] // tm
    o1 = d_s5
    o2 = o1 + 2 * d_qk + d_gla
    o3 = o2 + GLA_RANK
    wu_hi, wu_lo = _split_bf16(w_in[:, :o1])
    wqkv = w_in[:, o1:o2].astype(BF16)
    wa = jnp.pad(w_in[:, o2:o3], ((0, 0), (0, LANES - GLA_RANK))).astype(BF16)
    wr = w_in[:, o3:].astype(BF16)
    n_slab = d_s5 // LANES
    row = lambda i: (i, 0)
    return pl.pallas_call(
        functools.partial(_inproj_kernel, n_first=n_first),
        grid=(t // tm,),
        in_specs=_pair_specs((tm, d), 0, n_first) + [
            _const_spec((1, d)), _const_spec(wu_hi.shape), _const_spec(wu_lo.shape),
            _const_spec(wqkv.shape), _const_spec(wa.shape), _const_spec(wr.shape)],
        out_specs=[pl.BlockSpec((n_slab, tm, LANES), lambda i: (0, i, 0)),
                   pl.BlockSpec((tm, d_qk), row), pl.BlockSpec((tm, d_qk), row),
                   pl.BlockSpec((tm, d_gla), row), pl.BlockSpec((tm, LANES), row),
                   pl.BlockSpec((tm, d_gla), row)],
        out_shape=[jax.ShapeDtypeStruct((n_slab, t, LANES), F32),
                   jax.ShapeDtypeStruct((t, d_qk), BF16), jax.ShapeDtypeStruct((t, d_qk), BF16),
                   jax.ShapeDtypeStruct((t, d_gla), BF16), jax.ShapeDtypeStruct((t, LANES), F32),
                   jax.ShapeDtypeStruct((t, d_gla), BF16)],
        compiler_params=_cparams(("arbitrary",)),
        name="in_proj",
    )(xp, xs, g_mix.reshape(1, d), wu_hi, wu_lo, wqkv, wa, wr)


def _s5_coeffs(lam_re, lam_im, log_dt, b_re, b_im, c_re, c_im, d_skip):
    g, p = lam_re.shape
    h = b_re.shape[-1]
    nb = S5_BLOCK
    ns = g // GROUPS_PER_SLAB
    gl = GROUPS_PER_SLAB
    dt = jnp.exp(log_dt.astype(F32))[:, None]
    lr, li = lam_re.astype(F32), lam_im.astype(F32)
    mag = jnp.exp(lr * dt)
    ar, ai = mag * jnp.cos(li * dt), mag * jnp.sin(li * dt)
    den = lr * lr + li * li
    nr = ar - 1.0
    cf_r = (nr * lr + ai * li) / den
    cf_i = (ai * lr - nr * li) / den
    bb_r = cf_r[..., None] * b_re - cf_i[..., None] * b_im
    bb_i = cf_r[..., None] * b_im + cf_i[..., None] * b_re

    def cpow(n):
        e = n[:, None, None] * dt[None]
        m = jnp.exp(lr[None] * e)
        return m * jnp.cos(li[None] * e), m * jnp.sin(li[None] * e)

    pr, pi = cpow(jnp.arange(nb + 1, dtype=F32))

    bt_r, bt_i = jnp.swapaxes(bb_r, 1, 2), jnp.swapaxes(bb_i, 1, 2)
    ct_r, ct_i = jnp.swapaxes(c_re, 1, 2), jnp.swapaxes(c_im, 1, 2)

    cb_r = jnp.einsum('gcp,gph->gphc', c_re, bb_r) - jnp.einsum('gcp,gph->gphc', c_im, bb_i)
    cb_i = jnp.einsum('gcp,gph->gphc', c_re, bb_i) + jnp.einsum('gcp,gph->gphc', c_im, bb_r)
    taps = (jnp.einsum('ngp,gphc->nghc', pr[:nb], cb_r)
            - jnp.einsum('ngp,gphc->nghc', pi[:nb], cb_i))
    skip = d_skip[None, :, :, None] * jnp.eye(h, dtype=F32)[None, None]
    taps = taps + jnp.where(jnp.arange(nb)[:, None, None, None] == 0, skip, 0.0)
    ksm = jnp.swapaxes(taps.reshape(nb, ns, gl * h, h), 0, 1)

    wr, wi = pr[nb - 1::-1][:nb], pi[nb - 1::-1][:nb]
    wr, wi = wr[:, :, None, :], wi[:, :, None, :]
    inj = jnp.concatenate([wr * bt_r[None] - wi * bt_i[None],
                           wr * bt_i[None] + wi * bt_r[None]], axis=-1)
    psm = jnp.swapaxes(inj.reshape(nb, ns, gl * h, 2 * p), 0, 1)

    er = jnp.transpose(pr[1:nb + 1], (1, 2, 0))[..., None]
    ei = jnp.transpose(pi[1:nb + 1], (1, 2, 0))[..., None]
    q_r = ct_r[:, :, None, :] * er - ct_i[:, :, None, :] * ei
    q_i = -(ct_r[:, :, None, :] * ei + ct_i[:, :, None, :] * er)
    qq = jnp.stack([q_r.reshape(ns, gl * p, nb * h), q_i.reshape(ns, gl * p, nb * h)], axis=1)
    qsm = qq.reshape(ns, 2 * gl * p, nb * h)

    def slab(xr, xi):
        k = xr.shape[0]
        xr = jnp.transpose(xr.reshape(k, ns, gl * p), (1, 0, 2))
        xi = jnp.transpose(xi.reshape(k, ns, gl * p), (1, 0, 2))
        return jnp.concatenate([xr, xi], axis=-1)

    rows = jnp.arange(SUBLANES, dtype=F32)
    cr, ci = cpow(nb * rows)
    carry_m = slab(cr, ci)
    sr, si = cpow(nb * jnp.array([1.0, 2.0, 4.0], F32))
    keep = (rows[None, :] >= jnp.array([1.0, 2.0, 4.0], F32)[:, None]).astype(F32)
    step_m = slab((sr[:, None] * keep[:, :, None, None]).reshape(3 * SUBLANES, g, p),
                  (si[:, None] * keep[:, :, None, None]).reshape(3 * SUBLANES, g, p))
    step_m = step_m.reshape(ns, 3, SUBLANES, 2 * gl * p)
    a8 = carry_m[:, 1:2]
    return ksm, psm, qsm, step_m, carry_m, a8


def _s5_expand_kernel(ksm_ref, psm_ref, qsm_ref, tm_ref, ph_ref, pl_ref, qm_ref):
    nb = S5_BLOCK
    h = S5_GROUP
    w = ph_ref.shape[2]
    p = w // (2 * GROUPS_PER_SLAB)

    def spread(x, sel, keep):
        a, b, c = _split3_bf16(x)
        return jnp.where(keep, _dot(a, sel) + _dot(b, sel) + _dot(c, sel), 0.0)

    def iota(shape, axis):
        return lax.broadcasted_iota(jnp.int32, shape, axis)

    sel_c = (iota((h, LANES), 1) % h == iota((h, LANES), 0)).astype(BF16)
    keep_t = iota((LANES, LANES), 0) // h == iota((LANES, LANES), 1) // h
    tm_ref[...] = jnp.zeros_like(tm_ref)
    for tau in range(nb):
        blk = spread(ksm_ref[0, tau], sel_c, keep_t).astype(BF16)
        for s in range(nb - tau):
            t = s + tau
            tm_ref[0, s * LANES:(s + 1) * LANES, t * LANES:(t + 1) * LANES] = blk

    half = w // 2
    keep_p = iota((LANES, half), 0) // h == iota((LANES, half), 1) // p
    for ri in range(2):
        sel_p = (iota((2 * p, half), 0) == iota((2 * p, half), 1) % p + ri * p).astype(BF16)
        for s in range(nb):
            blk = spread(psm_ref[0, s], sel_p, keep_p)
            hi, lo = _split_bf16(blk)
            ph_ref[0, s * LANES:(s + 1) * LANES, ri * half:(ri + 1) * half] = hi
            pl_ref[0, s * LANES:(s + 1) * LANES, ri * half:(ri + 1) * half] = lo

    keep_q = (iota((w, LANES), 0) % half) // p == iota((w, LANES), 1) // h
    for t in range(nb):
        sel_q = (iota((nb * h, LANES), 0) == iota((nb * h, LANES), 1) % h + t * h).astype(BF16)
        qm_ref[0, :, t * LANES:(t + 1) * LANES] = spread(qsm_ref[0], sel_q, keep_q).astype(BF16)


def _s5_expand(ksm, psm, qsm):
    ns = ksm.shape[0]
    w = qsm.shape[1]
    k = S5_BLOCK * LANES
    blk = lambda shape: pl.BlockSpec((1,) + shape, lambda j: (j,) + (0,) * len(shape))
    return pl.pallas_call(
        _s5_expand_kernel,
        grid=(ns,),
        in_specs=[blk(ksm.shape[1:]), blk(psm.shape[1:]), blk(qsm.shape[1:])],
        out_specs=[blk((k, k)), blk((k, w)), blk((k, w)), blk((w, k))],
        out_shape=[jax.ShapeDtypeStruct((ns, k, k), BF16), jax.ShapeDtypeStruct((ns, k, w), BF16),
                   jax.ShapeDtypeStruct((ns, k, w), BF16), jax.ShapeDtypeStruct((ns, w, k), BF16)],
        compiler_params=_cparams(("arbitrary",)),
        name="s5_tables",
    )(ksm, psm, qsm)


def _cmul(xr, xi, mr, mi):
    return xr * mr - xi * mi, xr * mi + xi * mr


def _s5_kernel(u_ref, ph_ref, pl_ref, tm_ref, qm_ref, step_ref, carry_ref, a8_ref, h0_ref,
               y_ref, hout_ref, s_scr, hin_scr, c_scr, *, rows_per_seq):
    ti = pl.program_id(1)
    rows = s_scr.shape[0]
    half = s_scr.shape[1] // 2
    nb = S5_BLOCK
    u8 = jnp.concatenate([u_ref[0, pl.ds(s, rows, stride=nb), :] for s in range(nb)], axis=1)
    hi, lo = _split_bf16(u8)
    per_block_seq = rows_per_seq == SUBLANES
    if per_block_seq:
        s_scr[...] = _dot(hi, ph_ref[0]) + _dot(lo, ph_ref[0]) + _dot(hi, pl_ref[0])
    else:
        s_scr[...] = _dot(hi, ph_ref[0])

    if not per_block_seq:
        @pl.when(ti == 0)
        def _():
            c_scr[...] = h0_ref[0]

    not_first = (lax.broadcasted_iota(jnp.int32, (SUBLANES, 1), 0) >= 1).astype(F32)
    a8r, a8i = a8_ref[0, :, :half], a8_ref[0, :, half:]
    cmr, cmi = carry_ref[0, :, :half], carry_ref[0, :, half:]

    def body(rb, _):
        r0 = pl.multiple_of(rb * SUBLANES, SUBLANES)
        sb = s_scr[pl.ds(r0, SUBLANES), :]
        x = pltpu.roll(sb, 1, 0) * not_first
        xr, xi = x[:, :half], x[:, half:]
        for k, sh in enumerate((1, 2, 4)):
            m = step_ref[0, k]
            pr, pi = _cmul(pltpu.roll(xr, sh, 0), pltpu.roll(xi, sh, 0), m[:, :half], m[:, half:])
            xr, xi = xr + pr, xi + pi
        c = h0_ref[0, pl.ds(rb, 1), :] if per_block_seq else c_scr[...]
        cr, ci = c[:, :half], c[:, half:]
        pr, pi = _cmul(cr, ci, cmr, cmi)
        hr, hi_ = xr + pr, xi + pi
        hin_scr[pl.ds(r0, SUBLANES), :half] = hr
        hin_scr[pl.ds(r0, SUBLANES), half:] = hi_
        nr, ni = _cmul(hr[SUBLANES - 1:], hi_[SUBLANES - 1:], a8r, a8i)
        cn = jnp.concatenate([nr, ni], axis=1) + sb[SUBLANES - 1:]
        if per_block_seq:
            hout_ref[0, pl.ds(rb, 1), :] = cn
        else:
            c_scr[...] = cn
        return 0

    lax.fori_loop(0, rows // SUBLANES, body, 0)
    if not per_block_seq:
        hout_ref[0] = c_scr[...]

    y8 = _dot(hi, tm_ref[0]) + _dot(hin_scr[...].astype(BF16), qm_ref[0])
    for t in range(nb):
        y_ref[0, pl.ds(t, rows, stride=nb), :] = y8[:, t * LANES:(t + 1) * LANES]


def _s5_mixer(u_slabs, tables, h0, t_start, t, rows_per_seq, tile_rows):
    tm, pm_hi, pm_lo, qm, step_m, carry_m, a8 = tables
    ns = u_slabs.shape[0]
    rows = t // S5_BLOCK
    n_tiles = rows // tile_rows
    assert rows % tile_rows == 0 and t_start % (tile_rows * S5_BLOCK) == 0
    off = t_start // (tile_rows * S5_BLOCK)
    n_seq = h0.shape[1]
    w = pm_hi.shape[-1]
    slab3 = lambda j, i: (j, 0, 0)
    kern = functools.partial(_s5_kernel, rows_per_seq=rows_per_seq)
    return pl.pallas_call(
        kern,
        grid=(ns, n_tiles),
        in_specs=[pl.BlockSpec((1, tile_rows * S5_BLOCK, LANES), lambda j, i: (j, i + off, 0)),
                  pl.BlockSpec((1,) + pm_hi.shape[1:], slab3),
                  pl.BlockSpec((1,) + pm_lo.shape[1:], slab3),
                  pl.BlockSpec((1,) + tm.shape[1:], slab3),
                  pl.BlockSpec((1,) + qm.shape[1:], slab3),
                  pl.BlockSpec((1,) + step_m.shape[1:], lambda j, i: (j, 0, 0, 0)),
                  pl.BlockSpec((1,) + carry_m.shape[1:], slab3),
                  pl.BlockSpec((1,) + a8.shape[1:], slab3),
                  pl.BlockSpec((1, n_seq, w), slab3)],
        out_specs=[pl.BlockSpec((1, tile_rows * S5_BLOCK, LANES), lambda j, i: (j, i, 0)),
                   pl.BlockSpec((1, n_seq, w), slab3)],
        out_shape=[jax.ShapeDtypeStruct((ns, t, LANES), F32),
                   jax.ShapeDtypeStruct((ns, n_seq, w), F32)],
        scratch_shapes=[pltpu.VMEM((tile_rows, w), F32), pltpu.VMEM((tile_rows, w), F32),
                        pltpu.VMEM((1, w), F32)],
        compiler_params=_cparams(("arbitrary", "arbitrary")),
        name="s5_mixer",
    )(u_slabs, pm_hi, pm_lo, tm, qm, step_m, carry_m, a8, h0)


def _pack_s5_state(re, im, ns):
    b = re.shape[0]
    r = re.astype(F32).reshape(b, ns, -1)
    i = im.astype(F32).reshape(b, ns, -1)
    return jnp.transpose(jnp.concatenate([r, i], axis=-1), (1, 0, 2))


def _unpack_s5_state(hc, g, p):
    ns, b, w = hc.shape
    hc = jnp.transpose(hc, (1, 0, 2))
    re = hc[:, :, :w // 2].reshape(b, g, p)
    im = hc[:, :, w // 2:].reshape(b, g, p)
    return re, im


def _gla_kernel(q_ref, k_ref, v_ref, a_ref, r_ref, wa2_ref, ba2_ref, gh_ref, s0_ref,
                o_ref, sout_ref, st_scr, *, carry, n_chunks):
    step = pl.program_id(0)
    dk = GLA_DK
    dv = v_ref.shape[1] // GLA_HEADS
    c = CHUNK
    scale = dk ** -0.5
    ri = lax.broadcasted_iota(jnp.int32, (c, c), 0)
    ci = lax.broadcasted_iota(jnp.int32, (c, c), 1)
    causal = ri >= ci
    tril = causal.astype(BF16)
    eye_dk = (lax.broadcasted_iota(jnp.int32, (dk, dk), 0)
              == lax.broadcasted_iota(jnp.int32, (dk, dk), 1))

    if carry:
        @pl.when(step == 0)
        def _():
            st_scr[...] = s0_ref[0]

    wa_hi, wa_lo = _split_bf16(wa2_ref[...])
    for n in range(n_chunks):
        rows = slice(n * c, (n + 1) * c)
        a_hi, a_lo = _split_bf16(a_ref[rows, :])
        logit = _dot(a_hi, wa_hi) + _dot(a_lo, wa_hi) + _dot(a_hi, wa_lo) + ba2_ref[...]
        g = jax.nn.log_sigmoid(logit) * (1.0 / GLA_TAU)
        g1, g2, g3 = _split3_bf16(g)
        bcum = _dot(tril, g1) + _dot(tril, g2) + _dot(tril, g3)
        for h in range(GLA_HEADS):
            ks = slice(h * dk, (h + 1) * dk)
            vs = slice(h * dv, (h + 1) * dv)
            b = bcum[:, ks]
            qh = q_ref[rows, ks].astype(F32) * scale
            kh = k_ref[rows, ks].astype(F32)
            vh = v_ref[rows, vs]
            state = s0_ref[n, h] if not carry else st_scr[h]
            qe = (qh * jnp.exp(b)).astype(BF16)
            ke = (kh * jnp.exp(-b)).astype(BF16)
            att = jnp.where(causal, _dot_nt(qe, ke), 0.0)
            o = _dot(att.astype(BF16), vh) + _dot(qe, state.astype(BF16))
            blast = b[c - 1:c, :]
            kd = (kh * jnp.exp(blast - b)).astype(BF16)
            decay = jnp.sum(jnp.where(eye_dk, jnp.exp(blast), 0.0), axis=1, keepdims=True)
            new_state = decay * state + _dot_tn(kd, vh)
            if carry:
                st_scr[h] = new_state
            else:
                sout_ref[n, h] = new_state
            on = _rms(o, gh_ref[...])
            rr = r_ref[rows, vs].astype(F32)
            o_ref[rows, vs] = (on * (rr * jax.nn.sigmoid(rr))).astype(BF16)
    if carry:
        sout_ref[0] = st_scr[...]


def _gla_mixer(q, k, v, a, r, wa2, ba2, g_head, s0, t_start, t, carry, n_chunks):
    dqk = q.shape[1]
    dvt = v.shape[1]
    hh, dk, dv = s0.shape[1:]
    rows = n_chunks * CHUNK
    assert t % rows == 0 and t_start % rows == 0
    off = t_start // rows
    row = lambda i: (i, 0)
    row_in = lambda i: (i + off, 0)
    if carry:
        sblk, smap = (1, hh, dk, dv), (lambda i: (0, 0, 0, 0))
    else:
        sblk, smap = (n_chunks, hh, dk, dv), (lambda i: (i, 0, 0, 0))
    kern = functools.partial(_gla_kernel, carry=carry, n_chunks=n_chunks)
    return pl.pallas_call(
        kern,
        grid=(t // rows,),
        in_specs=[pl.BlockSpec((rows, dqk), row_in), pl.BlockSpec((rows, dqk), row_in),
                  pl.BlockSpec((rows, dvt), row_in), pl.BlockSpec((rows, LANES), row_in),
                  pl.BlockSpec((rows, dvt), row_in),
                  _const_spec(wa2.shape), _const_spec((1, dqk)), _const_spec((1, dv)),
                  pl.BlockSpec(sblk, smap)],
        out_specs=[pl.BlockSpec((rows, dvt), row), pl.BlockSpec(sblk, smap)],
        out_shape=[jax.ShapeDtypeStruct((t, dvt), BF16), jax.ShapeDtypeStruct(s0.shape, F32)],
        scratch_shapes=[pltpu.VMEM((hh, dk, dv), F32)],
        compiler_params=_cparams(("arbitrary",)),
        name="gla_mixer",
    )(q, k, v, a, r, wa2, ba2.reshape(1, dqk), g_head.reshape(1, dv), s0)


def _mixout_kernel(yp_ref, ys_ref, gp_ref, gs_ref, xp_ref, xs_ref, wglu_ref, bglu_ref, gs5_ref,
                   wout_ref, gx_ref, wq_ref, x1_ref, q_ref, *, n_first):
    first = pl.program_id(0) < n_first
    yb = _pick(first, yp_ref, ys_ref)
    y = jnp.concatenate([yb[j] for j in range(yb.shape[0])], axis=1)
    z = jax.nn.gelu(y)
    gate = jax.nn.sigmoid(_dot(z.astype(BF16), wglu_ref[...]) + bglu_ref[...])
    s5 = _rms(z * gate, gs5_ref[...])
    cat = jnp.concatenate([s5.astype(BF16), _pick(first, gp_ref, gs_ref)], axis=1)
    x1 = _pick(first, xp_ref, xs_ref) + _dot(cat, wout_ref[...])
    x1_ref[...] = x1
    q_ref[...] = _dot(_rms(x1, gx_ref[...]).astype(BF16), wq_ref[...]).astype(BF16)


def _mix_out(y_p, y_s, gla_p, gla_s, xp, xs, w_glu, b_glu, g_s5, w_out, g_x, w_q):
    d = xp.shape[1]
    t = xp.shape[0] + xs.shape[0]
    ns = y_p.shape[0]
    ds5 = ns * LANES
    dg = gla_p.shape[1]
    tm = TOKEN_TILE // 2
    n_first = xp.shape[0] // tm
    row = lambda i: (i, 0)
    return pl.pallas_call(
        functools.partial(_mixout_kernel, n_first=n_first),
        grid=(t // tm,),
        in_specs=(_pair_specs((ns, tm, LANES), 1, n_first) + _pair_specs((tm, dg), 0, n_first)
                  + _pair_specs((tm, d), 0, n_first)
                  + [_const_spec((ds5, ds5)), _const_spec((1, ds5)), _const_spec((1, ds5)),
                     _const_spec((ds5 + dg, d)), _const_spec((1, d)), _const_spec((d, d))]),
        out_specs=[pl.BlockSpec((tm, d), row), pl.BlockSpec((tm, d), row)],
        out_shape=[jax.ShapeDtypeStruct((t, d), F32), jax.ShapeDtypeStruct((t, d), BF16)],
        compiler_params=_cparams(("arbitrary",)),
        name="mix_out",
    )(y_p, y_s, gla_p, gla_s, xp, xs, w_glu.astype(BF16), b_glu.reshape(1, ds5),
      g_s5.reshape(1, ds5), w_out.astype(BF16), g_x.reshape(1, d), w_q.astype(BF16))


def _xattn_kernel(q_ref, mk_ref, mv_ref, x1_ref, wo_ref, gf_ref, wrh_ref, wrl_ref,
                  x2_ref, xn_ref, lg_ref, *cache_scratch, nb):
    d = q_ref.shape[1]
    lt = q_ref.shape[0] // nb
    hd = d // X_HEADS
    scale = hd ** -0.5
    if cache_scratch:
        kbuf, vbuf, sem = cache_scratch
        i = pl.program_id(0)

        def fetch(step, slot):
            for b in range(nb):
                for h in range(X_HEADS):
                    for src, dst in ((mk_ref, kbuf), (mv_ref, vbuf)):
                        pltpu.make_async_copy(src.at[step * nb + b, :, h, :],
                                              dst.at[slot, b, :, pl.ds(h * hd, hd)],
                                              sem.at[slot]).start()

        @pl.when(i == 0)
        def _():
            fetch(0, 0)

        @pl.when(i + 1 < pl.num_programs(0))
        def _():
            fetch(i + 1, (i + 1) % 2)

        slot = i % 2
        for buf in (kbuf, vbuf):
            pltpu.make_async_copy(buf.at[slot], buf.at[slot], sem.at[slot]).wait()
        memory = lambda b: (kbuf[slot, b], vbuf[slot, b])
    else:
        memory = lambda b: (mk_ref[b], mv_ref[b])
    for b in range(nb):
        rows = slice(b * lt, (b + 1) * lt)
        mk, mv = (m.astype(BF16) for m in memory(b))
        outs = []
        for h in range(X_HEADS):
            hs = slice(h * hd, (h + 1) * hd)
            s = _dot_nt(q_ref[rows, hs], mk[:, hs]) * scale
            p = jnp.exp(s - jnp.max(s, axis=-1, keepdims=True))
            denom = jnp.sum(p, axis=-1, keepdims=True)
            outs.append((_dot(p.astype(BF16), mv[:, hs]) / denom).astype(BF16))
        o = jnp.concatenate(outs, axis=1)
        x2 = x1_ref[rows, :] + _dot(o, wo_ref[...])
        x2_ref[rows, :] = x2
        xn = _rms(x2, gf_ref[...])
        xn_ref[rows, :] = xn
        hi, lo = _split_bf16(xn)
        lg_ref[:, rows] = (_dot_nt(wrh_ref[...], hi) + _dot_nt(wrh_ref[...], lo)
                           + _dot_nt(wrl_ref[...], hi))


def _xattn(q, x1, mk, mv, w_o, g_ffn, wr_hi, wr_lo, t_start, t, nb, lt):
    d = q.shape[1]
    nm = mk.shape[1]
    ne = wr_hi.shape[0]
    rows = nb * lt
    assert t % rows == 0 and t_start % rows == 0
    off = t_start // rows
    row = lambda i: (i, 0)
    row_in = lambda i: (i + off, 0)
    if mk.ndim == 3:
        assert nb == 1 and mk.shape[0] == 1
        mem_spec = pl.BlockSpec((1, nm, d), lambda i: (0, 0, 0))
        scratch = []
    else:
        assert mk.shape[0] * lt == t
        mem_spec = pl.BlockSpec(memory_space=pl.ANY)
        scratch = [pltpu.VMEM((2, nb, nm, d), F32), pltpu.VMEM((2, nb, nm, d), F32),
                   pltpu.SemaphoreType.DMA((2,))]
    return pl.pallas_call(
        functools.partial(_xattn_kernel, nb=nb),
        grid=(t // rows,),
        in_specs=[pl.BlockSpec((rows, d), row_in), mem_spec, mem_spec,
                  pl.BlockSpec((rows, d), row_in),
                  _const_spec((d, d)), _const_spec((1, d)),
                  _const_spec((ne, d)), _const_spec((ne, d))],
        scratch_shapes=scratch,
        out_specs=[pl.BlockSpec((rows, d), row), pl.BlockSpec((rows, d), row),
                   pl.BlockSpec((ne, rows), lambda i: (0, i))],
        out_shape=[jax.ShapeDtypeStruct((t, d), F32), jax.ShapeDtypeStruct((t, d), F32),
                   jax.ShapeDtypeStruct((ne, t), F32)],
        compiler_params=_cparams(("arbitrary",)),
        name="mem_xattn",
    )(q, mk, mv, x1, w_o, g_ffn.reshape(1, d), wr_hi, wr_lo)


def _route_kernel(lgp_ref, lgs_ref, xnp_ref, xns_ref, bias_ref, eidx_ref, gate_ref, pos_ref, cnt_ref,
                  xn_ref, run_scr, *, n_first):
    ne, tn = lgp_ref.shape
    first = pl.program_id(0) < n_first
    xn_ref[...] = _pick(first, xnp_ref, xns_ref)

    @pl.when(pl.program_id(0) == 0)
    def _():
        run_scr[...] = jnp.zeros_like(run_scr)

    gsz = ne // N_GROUPS
    sc = jax.nn.sigmoid(_pick(first, lgp_ref, lgs_ref))
    sel = sc + bias_ref[...]
    s3 = sel.reshape(N_GROUPS, gsz, tn)
    ie = lax.broadcasted_iota(jnp.int32, s3.shape, 1).astype(F32)
    m1 = jnp.max(s3, axis=1, keepdims=True)
    first = jnp.min(jnp.where(s3 == m1, ie, float(gsz)), axis=1, keepdims=True)
    m2 = jnp.max(jnp.where(ie == first, -jnp.inf, s3), axis=1, keepdims=True)
    gs = m1 + m2
    ig = lax.broadcasted_iota(jnp.int32, gs.shape, 0)
    grank = jnp.zeros(gs.shape, jnp.int32)
    for g in range(N_GROUPS):
        other = gs[g:g + 1]
        ahead = jnp.where(other > gs, 1, jnp.where(other == gs, (ig > g).astype(jnp.int32), 0))
        grank = grank + ahead
    gkeep = jnp.broadcast_to(grank < TOPK_GROUPS, s3.shape)
    v = jnp.where(gkeep, s3, -jnp.inf).reshape(ne, tn)
    iv = lax.broadcasted_iota(jnp.int32, v.shape, 0)
    rank = jnp.zeros(v.shape, jnp.int32)
    for e in range(ne):
        other = v[e:e + 1]
        ahead = jnp.where(other > v, 1, jnp.where(other == v, (iv > e).astype(jnp.int32), 0))
        rank = rank + ahead
    chosen = rank < TOP_K
    gate = jnp.where(chosen, sc, 0.0)
    gate = gate / jnp.sum(gate, axis=0, keepdims=True) * ROUTE_SCALE
    ti = lax.broadcasted_iota(jnp.int32, (tn, tn), 0)
    tj = lax.broadcasted_iota(jnp.int32, (tn, tn), 1)
    incl = _dot(jnp.where(chosen, 1.0, 0.0).astype(BF16), (ti <= tj).astype(BF16))
    pos = run_scr[...] + incl - 1.0
    run_scr[...] = run_scr[...] + incl[:, tn - 1:tn]
    cnt_ref[...] = run_scr[...].astype(jnp.int32)
    ef = iv.astype(F32)
    for k in range(TOP_K):
        hit = rank == k
        pick = lambda a: jnp.sum(jnp.where(hit, a, 0.0), axis=0, keepdims=True)
        eidx_ref[k:k + 1, :] = pick(ef).astype(jnp.int32)
        gate_ref[k:k + 1, :] = pick(gate)
        pos_ref[k:k + 1, :] = pick(pos).astype(jnp.int32)


def _route(lg_p, lg_s, xn_p, xn_s, router_bias):
    ne = lg_p.shape[0]
    d = xn_p.shape[1]
    t = lg_p.shape[1] + lg_s.shape[1]
    tn = TOKEN_TILE
    n_first = lg_p.shape[1] // tn
    kt = lambda i: (0, i)
    return pl.pallas_call(
        functools.partial(_route_kernel, n_first=n_first),
        grid=(t // tn,),
        in_specs=(_pair_specs((ne, tn), 1, n_first) + _pair_specs((tn, d), 0, n_first)
                  + [_const_spec((ne, 1))]),
        out_specs=[pl.BlockSpec((TOP_K, tn), kt), pl.BlockSpec((TOP_K, tn), kt),
                   pl.BlockSpec((TOP_K, tn), kt), pl.BlockSpec((ne, 1), lambda i: (0, 0)),
                   pl.BlockSpec((tn, d), lambda i: (i, 0))],
        out_shape=[jax.ShapeDtypeStruct((TOP_K, t), jnp.int32),
                   jax.ShapeDtypeStruct((TOP_K, t), F32),
                   jax.ShapeDtypeStruct((TOP_K, t), jnp.int32),
                   jax.ShapeDtypeStruct((ne, 1), jnp.int32),
                   jax.ShapeDtypeStruct((t, d), F32)],
        scratch_shapes=[pltpu.VMEM((ne, 1), F32)],
        compiler_params=_cparams(("arbitrary",)),
        name="router",
    )(lg_p, lg_s, xn_p, xn_s, router_bias.astype(F32).reshape(ne, 1))


def _moe_plan(eidx, posk, counts, n_tiles_max):
    ne = counts.shape[0]
    ntile = (counts + EXPERT_TILE - 1) // EXPERT_TILE
    cum = jnp.cumsum(ntile)
    base = (cum - ntile) * EXPERT_TILE
    n_tiles = cum[-1:]
    experts = jnp.arange(ne, dtype=jnp.int32)
    last_used = jnp.max(jnp.where(ntile > 0, experts, 0))
    tiles = jnp.arange(n_tiles_max, dtype=jnp.int32)
    tile_expert = jnp.sum(tiles[:, None] >= cum[None, :], axis=1)
    tile_expert = jnp.minimum(tile_expert, last_used).astype(jnp.int32)
    slot = (jnp.sum(jnp.where(eidx[..., None] == experts, base, 0), axis=-1) + posk)
    slot = slot.astype(jnp.int32)
    n_k, t = slot.shape
    tok = lax.broadcasted_iota(jnp.int32, slot.shape, 1)
    _, tok_sorted = lax.sort_key_val(slot.reshape(n_k * t), tok.reshape(n_k * t))
    tok_sorted = jnp.concatenate([tok_sorted, jnp.zeros((EXPERT_TILE,), jnp.int32)])
    first_pair = jnp.cumsum(counts) - counts
    first_tile = cum - ntile
    mine = tile_expert[:, None] == experts[None, :]
    lookup = lambda table: jnp.sum(jnp.where(mine, table[None, :], 0), axis=1)
    pair0 = lookup(first_pair) + (tiles - lookup(first_tile)) * EXPERT_TILE
    pair0 = jnp.clip(pair0, 0, n_k * t).astype(jnp.int32)
    return slot, tile_expert, n_tiles.astype(jnp.int32), tok_sorted, pair0


def _row_copy_wait(ref, n_rows, sem):
    pltpu.make_async_copy(ref.at[pl.ds(0, n_rows)], ref.at[pl.ds(0, n_rows)], sem).wait()


def _expert_kernel(te_ref, nt_ref, pair0_ref, tok_ref, xn_hbm, wg_ref, wu_ref, wd_ref, ys_ref,
                   x_a, x_b, wg_scr, wu_scr, wd_scr, sem_a, sem_b):
    i = pl.program_id(0)
    n_tiles = nt_ref[0]

    def issue_row(tile, buf, sem, r, queue):
        t = tok_ref[pair0_ref[tile] + r]
        pltpu.make_async_copy(xn_hbm.at[pl.ds(t, 1)], buf.at[pl.ds(r, 1)],
                              sem).start(priority=queue)

    @pl.when(i == 0)
    def _():
        def body(rb, _):
            for u in range(SUBLANES):
                issue_row(0, x_a, sem_a, rb * SUBLANES + u, u % 2)
            return 0
        lax.fori_loop(0, EXPERT_TILE // SUBLANES, body, 0)

    nxt_tile = jnp.where(i + 1 < n_tiles, i + 1, 0)

    def run(cur, sem_cur, nxt, sem_nxt):
        _row_copy_wait(xn_hbm, EXPERT_TILE, sem_cur)

        @pl.when((i == 0) | (te_ref[i] != te_ref[jnp.maximum(i - 1, 0)]))
        def _():
            wg_scr[...] = wg_ref[0].astype(BF16)
            wu_scr[...] = wu_ref[0].astype(BF16)
            wd_scr[...] = wd_ref[0].astype(BF16)

        for r in range(EXPERT_TILE):
            issue_row(nxt_tile, nxt, sem_nxt, r, r % 2)
        x = cur[...].astype(BF16)
        h = jax.nn.silu(_dot(x, wg_scr[...])) * _dot(x, wu_scr[...])
        ys_ref[...] = _dot(h.astype(BF16), wd_scr[...])

        @pl.when(i == n_tiles - 1)
        def _():
            _row_copy_wait(xn_hbm, EXPERT_TILE, sem_nxt)

    @pl.when((i < n_tiles) & (i % 2 == 0))
    def _():
        run(x_a, sem_a, x_b, sem_b)

    @pl.when((i < n_tiles) & (i % 2 == 1))
    def _():
        run(x_b, sem_b, x_a, sem_a)

    @pl.when(i >= n_tiles)
    def _():
        ys_ref[...] = jnp.zeros_like(ys_ref)


def _experts(xn, tile_expert, n_tiles, pair0, tok_sorted, w_gate, w_up, w_down):
    d = xn.shape[1]
    ne, _, de = w_gate.shape
    n_tiles_max = tile_expert.shape[0]
    out_tile = lambda i, *_: (i, 0)
    wsel = lambda i, te, *_: (te[i], 0, 0)
    return pl.pallas_call(
        _expert_kernel,
        grid_spec=pltpu.PrefetchScalarGridSpec(
            num_scalar_prefetch=4,
            grid=(n_tiles_max,),
            in_specs=[pl.BlockSpec(memory_space=pl.ANY),
                      pl.BlockSpec((1, d, de), wsel), pl.BlockSpec((1, d, de), wsel),
                      pl.BlockSpec((1, de, d), wsel)],
            out_specs=pl.BlockSpec((EXPERT_TILE, d), out_tile),
            scratch_shapes=[pltpu.VMEM((EXPERT_TILE, d), F32), pltpu.VMEM((EXPERT_TILE, d), F32),
                            pltpu.VMEM((d, de), BF16), pltpu.VMEM((d, de), BF16),
                            pltpu.VMEM((de, d), BF16),
                            pltpu.SemaphoreType.DMA, pltpu.SemaphoreType.DMA]),
        out_shape=jax.ShapeDtypeStruct((n_tiles_max * EXPERT_TILE, d), F32),
        compiler_params=_cparams(("arbitrary",)),
        name="moe_experts",
    )(tile_expert, n_tiles, pair0, tok_sorted, xn, w_gate, w_up, w_down)


def _combine_kernel(slot_ref, gate_ref, xn_ref, x2p_ref, x2s_ref, sg_ref, su_ref, sd_ref,
                    gf_ref, ys_hbm, yp_ref, ysamp_ref, buf_a, buf_b, base_scr, y_scr, sem_a, sem_b,
                    *, n_first):
    tm = xn_ref.shape[0]
    i = pl.program_id(0)
    n_steps = pl.num_programs(0)
    first = i < n_first
    rb_rows = SUBLANES

    def issue_rows(tile, buf, sem, r0, n):
        for rr in range(n):
            for k in range(TOP_K):
                s = slot_ref[(tile * tm + r0 + rr) * TOP_K + k]
                pltpu.make_async_copy(ys_hbm.at[pl.ds(s, 1)], buf.at[k, pl.ds(r0 + rr, 1)],
                                      sem).start(priority=k % 2)

    @pl.when(i == 0)
    def _():
        def body(r, _):
            issue_rows(0, buf_a, sem_a, r, 1)
            return 0
        lax.fori_loop(0, tm, body, 0)

    x = xn_ref[...].astype(BF16)
    hs = jax.nn.silu(_dot(x, sg_ref[...])) * _dot(x, su_ref[...])
    base_scr[...] = _pick(first, x2p_ref, x2s_ref) + _dot(hs.astype(BF16), sd_ref[...])
    nxt_tile = jnp.where(i + 1 < n_steps, i + 1, 0)

    def run(cur, sem_cur, nxt, sem_nxt):
        _row_copy_wait(ys_hbm, tm * TOP_K, sem_cur)

        for rb in range(tm // rb_rows):
            r0 = rb * rb_rows
            issue_rows(nxt_tile, nxt, sem_nxt, r0, rb_rows)
            acc = base_scr[pl.ds(r0, rb_rows), :]
            g = gate_ref[pl.ds(r0, rb_rows), :]
            for k in range(TOP_K):
                acc = acc + g[:, k:k + 1] * cur[k, pl.ds(r0, rb_rows), :]
            y_scr[pl.ds(r0, rb_rows), :] = _rms(acc, gf_ref[...])

        @pl.when(i == n_steps - 1)
        def _():
            _row_copy_wait(ys_hbm, tm * TOP_K, sem_nxt)

    @pl.when(i % 2 == 0)
    def _():
        run(buf_a, sem_a, buf_b, sem_b)

    @pl.when(i % 2 == 1)
    def _():
        run(buf_b, sem_b, buf_a, sem_a)

    @pl.when(first)
    def _():
        yp_ref[...] = y_scr[...]

    @pl.when(jnp.logical_not(first))
    def _():
        ysamp_ref[...] = y_scr[...]


def _combine(slot_flat, gates, xn, x2_p, x2_s, ws_gate, ws_up, ws_down, g_final, ys):
    d = x2_p.shape[1]
    tp, tsamp = x2_p.shape[0], x2_s.shape[0]
    dsh = ws_gate.shape[1]
    tm = COMBINE_TILE
    n_first = tp // tm
    row = lambda i, *_: (i, 0)
    pair = lambda: _pair_specs((tm, d), 0, n_first)
    return pl.pallas_call(
        functools.partial(_combine_kernel, n_first=n_first),
        grid_spec=pltpu.PrefetchScalarGridSpec(
            num_scalar_prefetch=1,
            grid=((tp + tsamp) // tm,),
            in_specs=[pl.BlockSpec((tm, TOP_K), row), pl.BlockSpec((tm, d), row)] + pair()
                     + [_const_spec((d, dsh)), _const_spec((d, dsh)), _const_spec((dsh, d)),
                        _const_spec((1, d)), pl.BlockSpec(memory_space=pl.ANY)],
            out_specs=pair(),
            scratch_shapes=[pltpu.VMEM((TOP_K, tm, d), F32), pltpu.VMEM((TOP_K, tm, d), F32),
                            pltpu.VMEM((tm, d), F32), pltpu.VMEM((tm, d), F32),
                            pltpu.SemaphoreType.DMA, pltpu.SemaphoreType.DMA]),
        out_shape=[jax.ShapeDtypeStruct((tp, d), F32), jax.ShapeDtypeStruct((tsamp, d), F32)],
        compiler_params=_cparams(("arbitrary",)),
        name="moe_combine",
    )(slot_flat, gates, xn, x2_p, x2_s, ws_gate.astype(BF16), ws_up.astype(BF16),
      ws_down.astype(BF16), g_final.reshape(1, d), ys)


def _moe(xn, x2_p, x2_s, eidx, gate, posk, counts, w_gate, w_up, w_down, ws_gate, ws_up,
         ws_down, g_final):
    t = xn.shape[0]
    ne = w_gate.shape[0]
    n_tiles_max = t * TOP_K // EXPERT_TILE + ne
    slot, tile_expert, n_tiles, tok_sorted, pair0 = _moe_plan(eidx, posk, counts[:, 0],
                                                              n_tiles_max)
    ys = _experts(xn, tile_expert, n_tiles, pair0, tok_sorted, w_gate, w_up, w_down)
    return _combine(jnp.transpose(slot).reshape(t * TOP_K), jnp.transpose(gate), xn, x2_p, x2_s,
                    ws_gate, ws_up, ws_down, g_final, ys)


def kernel(x_prompt, x_sample, cache_mem_k, cache_mem_v, state_s5_re, state_s5_im, state_gla, mem_prompt, w_in, w_a2, b_a2, lam_re, lam_im, log_dt, b_re, b_im, c_re, c_im, d_skip, w_glu, b_glu, g_s5_out, g_gla_head, w_out, g_mix, g_xattn, g_mem, w_q, w_k, w_v, w_o, g_ffn, w_router, router_bias, w_gate, w_up, w_down, ws_gate, ws_up, ws_down, g_final):
    depth = w_in.shape[0]
    assert depth == 1, "one trunk layer"
    bp, lp, d = x_prompt.shape
    bs, ls, _ = x_sample.shape
    assert bp == 1 and ls == CHUNK and lp % CHUNK == 0
    n_mem = mem_prompt.shape[1]
    g, p = lam_re.shape[1:]
    d_s5 = g * S5_GROUP
    d_qk = GLA_HEADS * GLA_DK
    d_gla = w_out.shape[1] - d_s5
    ns = d_s5 // LANES
    ne = w_router.shape[2]
    tp, tsamp = bp * lp, bs * ls
    xp = x_prompt.reshape(tp, d)
    xs = x_sample.reshape(tsamp, d)
    sq = lambda a: a.reshape(a.shape[1:])

    mk_p, mv_p = _memory_kv(mem_prompt.reshape(n_mem, d), sq(g_mem), sq(w_k).astype(BF16),
                            sq(w_v).astype(BF16))

    u, q, k, v, a, r = _in_proj(xp, xs, sq(g_mix), sq(w_in), d_s5, d_qk, d_gla)

    ksm, psm, qsm, step_m, carry_m, a8 = _s5_coeffs(
        sq(lam_re), sq(lam_im), sq(log_dt), sq(b_re), sq(b_im), sq(c_re), sq(c_im), sq(d_skip))
    tables = tuple(_s5_expand(ksm, psm, qsm)) + (step_m, carry_m, a8)
    zero_h = jnp.zeros((ns, bp, 2 * GROUPS_PER_SLAB * p), F32)
    rows_p = lp // S5_BLOCK
    y_p, h_p = _s5_mixer(u, tables, zero_h, 0, tp, rows_per_seq=rows_p,
                         tile_rows=math.gcd(rows_p, 512))
    h0_s = _pack_s5_state(sq(state_s5_re), sq(state_s5_im), ns)
    y_s, h_s = _s5_mixer(u, tables, h0_s, tp, tsamp, rows_per_seq=ls // S5_BLOCK,
                         tile_rows=tsamp // S5_BLOCK)

    wa2 = jnp.pad(sq(w_a2), ((0, LANES - GLA_RANK), (0, 0)))
    zero_s = jnp.zeros((bp,) + state_gla.shape[2:], F32)
    gla_p, sg_p = _gla_mixer(q, k, v, a, r, wa2, sq(b_a2), sq(g_gla_head), zero_s, 0, tp,
                             carry=True, n_chunks=4)
    gla_s, sg_s = _gla_mixer(q, k, v, a, r, wa2, sq(b_a2), sq(g_gla_head), sq(state_gla), tp,
                             tsamp, carry=False, n_chunks=4)

    x1, xq = _mix_out(y_p, y_s, gla_p, gla_s, xp, xs, sq(w_glu), sq(b_glu), sq(g_s5_out),
                      sq(w_out), sq(g_xattn), sq(w_q))

    wr_hi, wr_lo = _split_bf16(jnp.transpose(sq(w_router)))
    wo = sq(w_o).astype(BF16)
    x2_p, xn_p, lg_p = _xattn(xq, x1, mk_p[None], mv_p[None], wo, sq(g_ffn), wr_hi, wr_lo,
                              0, tp, nb=1, lt=TOKEN_TILE)
    x2_s, xn_s, lg_s = _xattn(xq, x1, sq(cache_mem_k), sq(cache_mem_v), wo, sq(g_ffn), wr_hi,
                              wr_lo, tp, tsamp, nb=2, lt=ls)

    eidx, gate, posk, counts, xn = _route(lg_p, lg_s, xn_p, xn_s, sq(router_bias))
    y_p2, y_s2 = _moe(xn, x2_p, x2_s, eidx, gate, posk, counts, sq(w_gate), sq(w_up),
                      sq(w_down), sq(ws_gate), sq(ws_up), sq(ws_down), g_final)

    xh = d // X_HEADS
    re_p, im_p = _unpack_s5_state(h_p, g, p)
    re_s, im_s = _unpack_s5_state(h_s, g, p)
    return (y_p2.reshape(bp, lp, d), y_s2.reshape(bs, ls, d),
            mk_p.reshape(1, bp, n_mem, X_HEADS, xh), mv_p.reshape(1, bp, n_mem, X_HEADS, xh),
            re_p[None], im_p[None], sg_p[None], re_s[None], im_s[None], sg_s[None])
```

```python
import functools
import math

import jax
import jax.numpy as jnp
from jax import lax
from jax.experimental import pallas as pl
from jax.experimental.pallas import tpu as pltpu

F32 = jnp.float32
BF16 = jnp.bfloat16

EPS = 1e-6
CHUNK = 64
S5_GROUP = 16
GLA_HEADS = 4
GLA_DK = 128
GLA_RANK = 16
GLA_TAU = 16.0
X_HEADS = 4
N_GROUPS = 8
TOPK_GROUPS = 4
TOP_K = 8
ROUTE_SCALE = 2.5

LANES = 128
SUBLANES = 8
S5_BLOCK = SUBLANES
GROUPS_PER_SLAB = LANES // S5_GROUP
TOKEN_TILE = 512
EXPERT_TILE = 256
COMBINE_TILE = 128
VMEM_LIMIT = 56 * 1024 * 1024


def _cparams(sem):
    return pltpu.CompilerParams(dimension_semantics=sem, vmem_limit_bytes=VMEM_LIMIT)


def _const_spec(shape):
    nd = len(shape)
    return pl.BlockSpec(shape, lambda *_: (0,) * nd, pipeline_mode=pl.Buffered(1))


def _pair_specs(block, axis, n_first):
    def at(f):
        return lambda i, *_: tuple(f(i) if a == axis else 0 for a in range(len(block)))
    return [pl.BlockSpec(block, at(lambda i: jnp.minimum(i, n_first - 1))),
            pl.BlockSpec(block, at(lambda i: jnp.maximum(i - n_first, 0)))]


def _pick(first, a_ref, b_ref):
    return jnp.where(first, a_ref[...], b_ref[...])


def _rms(x, g):
    return x * lax.rsqrt(jnp.mean(x * x, axis=-1, keepdims=True) + EPS) * g


def _split_bf16(x):
    hi = x.astype(BF16)
    lo = (x - hi.astype(F32)).astype(BF16)
    return hi, lo


def _split3_bf16(x):
    a = x.astype(BF16)
    r = x - a.astype(F32)
    b = r.astype(BF16)
    c = (r - b.astype(F32)).astype(BF16)
    return a, b, c


def _dot(a, b):
    return jnp.dot(a, b, preferred_element_type=F32)


def _dot_nt(a, b):
    return lax.dot_general(a, b, (((1,), (1,)), ((), ())), preferred_element_type=F32)


def _dot_tn(a, b):
    return lax.dot_general(a, b, (((0,), (0,)), ((), ())), preferred_element_type=F32)


def _memkv_kernel(mem_ref, g_ref, wk_ref, wv_ref, mk_ref, mv_ref):
    m = _rms(mem_ref[...], g_ref[...]).astype(BF16)
    mk_ref[...] = _dot(m, wk_ref[...].astype(BF16))
    mv_ref[...] = _dot(m, wv_ref[...].astype(BF16))


def _memory_kv(mem, g_mem, wk, wv):
    n, d = mem.shape
    tn = 512
    return pl.pallas_call(
        _memkv_kernel,
        grid=(d // tn,),
        in_specs=[_const_spec((n, d)), _const_spec((1, d)),
                  pl.BlockSpec((d, tn), lambda j: (0, j)),
                  pl.BlockSpec((d, tn), lambda j: (0, j))],
        out_specs=[pl.BlockSpec((n, tn), lambda j: (0, j))] * 2,
        out_shape=[jax.ShapeDtypeStruct((n, d), F32)] * 2,
        compiler_params=_cparams(("arbitrary",)),
        name="memory_kv",
    )(mem, g_mem.reshape(1, d), wk, wv)


def _inproj_kernel(xp_ref, xs_ref, g_ref, wuh_ref, wul_ref, wqkv_ref, wa_ref, wr_ref,
                   u_ref, q_ref, k_ref, v_ref, a_ref, r_ref, *, n_first):
    first = pl.program_id(0) < n_first
    x = _pick(first, xp_ref, xs_ref)
    nx = _rms(x, g_ref[...])
    hi, lo = _split_bf16(nx)
    u = _dot_nt(hi, wuh_ref[...])
    for j in range(u_ref.shape[0]):
        u_ref[j] = u[:, j * LANES:(j + 1) * LANES]

    @pl.when(jnp.logical_not(first))
    def _():
        fix = _dot_nt(lo, wuh_ref[...]) + _dot_nt(hi, wul_ref[...])
        for j in range(u_ref.shape[0]):
            u_ref[j] += fix[:, j * LANES:(j + 1) * LANES]

    qkv = _dot_nt(hi, wqkv_ref[...])
    dq = q_ref.shape[1]
    q_ref[...] = qkv[:, :dq].astype(BF16)
    k_ref[...] = qkv[:, dq:2 * dq].astype(BF16)
    v_ref[...] = qkv[:, 2 * dq:].astype(BF16)
    a_ref[...] = _dot_nt(hi, wa_ref[...]) + _dot_nt(lo, wa_ref[...])
    r_ref[...] = _dot_nt(hi, wr_ref[...]).astype(BF16)


def _in_proj(xp, xs, g_mix, w_in, d_s5, d_qk, d_gla):
    d = xp.shape[1]
    t = xp.shape[0] + xs.shape[0]
    tm = TOKEN_TILE
    n_first = xp.shape[0] // tm
    o1 = d_s5
    o2 = o1 + 2 * d_qk + d_gla
    o3 = o2 + GLA_RANK
    wt = jnp.transpose(w_in)
    wu_hi, wu_lo = _split_bf16(wt[:o1])
    wqkv = wt[o1:o2].astype(BF16)
    wa = jnp.pad(wt[o2:o3], ((0, LANES - GLA_RANK), (0, 0))).astype(BF16)
    wr = wt[o3:].astype(BF16)
    n_slab = d_s5 // LANES
    row = lambda i: (i, 0)
    return pl.pallas_call(
        functools.partial(_inproj_kernel, n_first=n_first),
        grid=(t // tm,),
        in_specs=_pair_specs((tm, d), 0, n_first) + [
            _const_spec((1, d)), _const_spec(wu_hi.shape), _const_spec(wu_lo.shape),
            _const_spec(wqkv.shape), _const_spec(wa.shape), _const_spec(wr.shape)],
        out_specs=[pl.BlockSpec((n_slab, tm, LANES), lambda i: (0, i, 0)),
                   pl.BlockSpec((tm, d_qk), row), pl.BlockSpec((tm, d_qk), row),
                   pl.BlockSpec((tm, d_gla), row), pl.BlockSpec((tm, LANES), row),
                   pl.BlockSpec((tm, d_gla), row)],
        out_shape=[jax.ShapeDtypeStruct((n_slab, t, LANES), F32),
                   jax.ShapeDtypeStruct((t, d_qk), BF16), jax.ShapeDtypeStruct((t, d_qk), BF16),
                   jax.ShapeDtypeStruct((t, d_gla), BF16), jax.ShapeDtypeStruct((t, LANES), F32),
                   jax.ShapeDtypeStruct((t, d_gla), BF16)],
        compiler_params=_cparams(("arbitrary",)),
        name="in_proj",
    )(xp, xs, g_mix.reshape(1, d), wu_hi, wu_lo, wqkv, wa, wr)


def _s5_coeffs(lam_re, lam_im, log_dt, b_re, b_im, c_re, c_im, d_skip):
    g, p = lam_re.shape
    h = b_re.shape[-1]
    nb = S5_BLOCK
    ns = g // GROUPS_PER_SLAB
    gl = GROUPS_PER_SLAB
    dt = jnp.exp(log_dt.astype(F32))[:, None]
    lr, li = lam_re.astype(F32), lam_im.astype(F32)
    mag = jnp.exp(lr * dt)
    ar, ai = mag * jnp.cos(li * dt), mag * jnp.sin(li * dt)
    den = lr * lr + li * li
    nr = ar - 1.0
    cf_r = (nr * lr + ai * li) / den
    cf_i = (ai * lr - nr * li) / den
    bb_r = cf_r[..., None] * b_re - cf_i[..., None] * b_im
    bb_i = cf_r[..., None] * b_im + cf_i[..., None] * b_re

    def cpow(n):
        e = n[:, None, None] * dt[None]
        m = jnp.exp(lr[None] * e)
        return m * jnp.cos(li[None] * e), m * jnp.sin(li[None] * e)

    pr, pi = cpow(jnp.arange(nb + 1, dtype=F32))

    bt_r, bt_i = jnp.swapaxes(bb_r, 1, 2), jnp.swapaxes(bb_i, 1, 2)
    ct_r, ct_i = jnp.swapaxes(c_re, 1, 2), jnp.swapaxes(c_im, 1, 2)

    cb_r = jnp.einsum('gcp,gph->gphc', c_re, bb_r) - jnp.einsum('gcp,gph->gphc', c_im, bb_i)
    cb_i = jnp.einsum('gcp,gph->gphc', c_re, bb_i) + jnp.einsum('gcp,gph->gphc', c_im, bb_r)
    taps = (jnp.einsum('ngp,gphc->nghc', pr[:nb], cb_r)
            - jnp.einsum('ngp,gphc->nghc', pi[:nb], cb_i))
    skip = d_skip[None, :, :, None] * jnp.eye(h, dtype=F32)[None, None]
    taps = taps + jnp.where(jnp.arange(nb)[:, None, None, None] == 0, skip, 0.0)
    ksm = jnp.swapaxes(taps.reshape(nb, ns, gl * h, h), 0, 1)

    wr, wi = pr[nb - 1::-1][:nb], pi[nb - 1::-1][:nb]
    wr, wi = wr[:, :, None, :], wi[:, :, None, :]
    inj = jnp.concatenate([wr * bt_r[None] - wi * bt_i[None],
                           wr * bt_i[None] + wi * bt_r[None]], axis=-1)
    psm = jnp.swapaxes(inj.reshape(nb, ns, gl * h, 2 * p), 0, 1)

    er = jnp.transpose(pr[1:nb + 1], (1, 2, 0))[..., None]
    ei = jnp.transpose(pi[1:nb + 1], (1, 2, 0))[..., None]
    q_r = ct_r[:, :, None, :] * er - ct_i[:, :, None, :] * ei
    q_i = -(ct_r[:, :, None, :] * ei + ct_i[:, :, None, :] * er)
    qq = jnp.stack([q_r.reshape(ns, gl * p, nb * h), q_i.reshape(ns, gl * p, nb * h)], axis=1)
    qsm = qq.reshape(ns, 2 * gl * p, nb * h)

    def slab(xr, xi):
        k = xr.shape[0]
        xr = jnp.transpose(xr.reshape(k, ns, gl * p), (1, 0, 2))
        xi = jnp.transpose(xi.reshape(k, ns, gl * p), (1, 0, 2))
        return jnp.concatenate([xr, xi], axis=-1)

    rows = jnp.arange(SUBLANES, dtype=F32)
    cr, ci = cpow(nb * rows)
    carry_m = slab(cr, ci)
    sr, si = cpow(nb * jnp.array([1.0, 2.0, 4.0], F32))
    keep = (rows[None, :] >= jnp.array([1.0, 2.0, 4.0], F32)[:, None]).astype(F32)
    step_m = slab((sr[:, None] * keep[:, :, None, None]).reshape(3 * SUBLANES, g, p),
                  (si[:, None] * keep[:, :, None, None]).reshape(3 * SUBLANES, g, p))
    step_m = step_m.reshape(ns, 3, SUBLANES, 2 * gl * p)
    a8 = carry_m[:, 1:2]
    return ksm, psm, qsm, step_m, carry_m, a8


def _s5_expand_kernel(ksm_ref, psm_ref, qsm_ref, tm_ref, ph_ref, pl_ref, qm_ref):
    nb = S5_BLOCK
    h = S5_GROUP
    w = ph_ref.shape[2]
    p = w // (2 * GROUPS_PER_SLAB)

    def spread(x, sel, keep):
        a, b, c = _split3_bf16(x)
        return jnp.where(keep, _dot(a, sel) + _dot(b, sel) + _dot(c, sel), 0.0)

    def iota(shape, axis):
        return lax.broadcasted_iota(jnp.int32, shape, axis)

    sel_c = (iota((h, LANES), 1) % h == iota((h, LANES), 0)).astype(BF16)
    keep_t = iota((LANES, LANES), 0) // h == iota((LANES, LANES), 1) // h
    tm_ref[...] = jnp.zeros_like(tm_ref)
    for tau in range(nb):
        blk = spread(ksm_ref[0, tau], sel_c, keep_t).astype(BF16)
        for s in range(nb - tau):
            t = s + tau
            tm_ref[0, s * LANES:(s + 1) * LANES, t * LANES:(t + 1) * LANES] = blk

    half = w // 2
    keep_p = iota((LANES, half), 0) // h == iota((LANES, half), 1) // p
    for ri in range(2):
        sel_p = (iota((2 * p, half), 0) == iota((2 * p, half), 1) % p + ri * p).astype(BF16)
        for s in range(nb):
            blk = spread(psm_ref[0, s], sel_p, keep_p)
            hi, lo = _split_bf16(blk)
            ph_ref[0, s * LANES:(s + 1) * LANES, ri * half:(ri + 1) * half] = hi
            pl_ref[0, s * LANES:(s + 1) * LANES, ri * half:(ri + 1) * half] = lo

    keep_q = (iota((w, LANES), 0) % half) // p == iota((w, LANES), 1) // h
    for t in range(nb):
        sel_q = (iota((nb * h, LANES), 0) == iota((nb * h, LANES), 1) % h + t * h).astype(BF16)
        qm_ref[0, :, t * LANES:(t + 1) * LANES] = spread(qsm_ref[0], sel_q, keep_q).astype(BF16)


def _s5_expand(ksm, psm, qsm):
    ns = ksm.shape[0]
    w = qsm.shape[1]
    k = S5_BLOCK * LANES
    blk = lambda shape: pl.BlockSpec((1,) + shape, lambda j: (j,) + (0,) * len(shape))
    return pl.pallas_call(
        _s5_expand_kernel,
        grid=(ns,),
        in_specs=[blk(ksm.shape[1:]), blk(psm.shape[1:]), blk(qsm.shape[1:])],
        out_specs=[blk((k, k)), blk((k, w)), blk((k, w)), blk((w, k))],
        out_shape=[jax.ShapeDtypeStruct((ns, k, k), BF16), jax.ShapeDtypeStruct((ns, k, w), BF16),
                   jax.ShapeDtypeStruct((ns, k, w), BF16), jax.ShapeDtypeStruct((ns, w, k), BF16)],
        compiler_params=_cparams(("arbitrary",)),
        name="s5_tables",
    )(ksm, psm, qsm)


def _cmul(xr, xi, mr, mi):
    return xr * mr - xi * mi, xr * mi + xi * mr


def _s5_kernel(u_ref, ph_ref, pl_ref, tm_ref, qm_ref, step_ref, carry_ref, a8_ref, h0_ref,
               y_ref, hout_ref, s_scr, hin_scr, c_scr, *, rows_per_seq):
    ti = pl.program_id(1)
    rows = s_scr.shape[0]
    half = s_scr.shape[1] // 2
    nb = S5_BLOCK
    u8 = jnp.concatenate([u_ref[0, pl.ds(s, rows, stride=nb), :] for s in range(nb)], axis=1)
    hi, lo = _split_bf16(u8)
    per_block_seq = rows_per_seq == SUBLANES
    if per_block_seq:
        s_scr[...] = _dot(hi, ph_ref[0]) + _dot(lo, ph_ref[0]) + _dot(hi, pl_ref[0])
    else:
        s_scr[...] = _dot(hi, ph_ref[0])

    if not per_block_seq:
        @pl.when(ti == 0)
        def _():
            c_scr[...] = h0_ref[0]

    not_first = (lax.broadcasted_iota(jnp.int32, (SUBLANES, 1), 0) >= 1).astype(F32)
    a8r, a8i = a8_ref[0, :, :half], a8_ref[0, :, half:]
    cmr, cmi = carry_ref[0, :, :half], carry_ref[0, :, half:]

    def body(rb, _):
        r0 = pl.multiple_of(rb * SUBLANES, SUBLANES)
        sb = s_scr[pl.ds(r0, SUBLANES), :]
        x = pltpu.roll(sb, 1, 0) * not_first
        xr, xi = x[:, :half], x[:, half:]
        for k, sh in enumerate((1, 2, 4)):
            m = step_ref[0, k]
            pr, pi = _cmul(pltpu.roll(xr, sh, 0), pltpu.roll(xi, sh, 0), m[:, :half], m[:, half:])
            xr, xi = xr + pr, xi + pi
        c = h0_ref[0, pl.ds(rb, 1), :] if per_block_seq else c_scr[...]
        cr, ci = c[:, :half], c[:, half:]
        pr, pi = _cmul(cr, ci, cmr, cmi)
        hr, hi_ = xr + pr, xi + pi
        hin_scr[pl.ds(r0, SUBLANES), :half] = hr
        hin_scr[pl.ds(r0, SUBLANES), half:] = hi_
        nr, ni = _cmul(hr[SUBLANES - 1:], hi_[SUBLANES - 1:], a8r, a8i)
        cn = jnp.concatenate([nr, ni], axis=1) + sb[SUBLANES - 1:]
        if per_block_seq:
            hout_ref[0, pl.ds(rb, 1), :] = cn
        else:
            c_scr[...] = cn
        return 0

    lax.fori_loop(0, rows // SUBLANES, body, 0)
    if not per_block_seq:
        hout_ref[0] = c_scr[...]

    y8 = _dot(hi, tm_ref[0]) + _dot(hin_scr[...].astype(BF16), qm_ref[0])
    for t in range(nb):
        y_ref[0, pl.ds(t, rows, stride=nb), :] = y8[:, t * LANES:(t + 1) * LANES]


def _s5_mixer(u_slabs, tables, h0, t_start, t, rows_per_seq, tile_rows):
    tm, pm_hi, pm_lo, qm, step_m, carry_m, a8 = tables
    ns = u_slabs.shape[0]
    rows = t // S5_BLOCK
    n_tiles = rows // tile_rows
    assert rows % tile_rows == 0 and t_start % (tile_rows * S5_BLOCK) == 0
    off = t_start // (tile_rows * S5_BLOCK)
    n_seq = h0.shape[1]
    w = pm_hi.shape[-1]
    slab3 = lambda j, i: (j, 0, 0)
    kern = functools.partial(_s5_kernel, rows_per_seq=rows_per_seq)
    return pl.pallas_call(
        kern,
        grid=(ns, n_tiles),
        in_specs=[pl.BlockSpec((1, tile_rows * S5_BLOCK, LANES), lambda j, i: (j, i + off, 0)),
                  pl.BlockSpec((1,) + pm_hi.shape[1:], slab3),
                  pl.BlockSpec((1,) + pm_lo.shape[1:], slab3),
                  pl.BlockSpec((1,) + tm.shape[1:], slab3),
                  pl.BlockSpec((1,) + qm.shape[1:], slab3),
                  pl.BlockSpec((1,) + step_m.shape[1:], lambda j, i: (j, 0, 0, 0)),
                  pl.BlockSpec((1,) + carry_m.shape[1:], slab3),
                  pl.BlockSpec((1,) + a8.shape[1:], slab3),
                  pl.BlockSpec((1, n_seq, w), slab3)],
        out_specs=[pl.BlockSpec((1, tile_rows * S5_BLOCK, LANES), lambda j, i: (j, i, 0)),
                   pl.BlockSpec((1, n_seq, w), slab3)],
        out_shape=[jax.ShapeDtypeStruct((ns, t, LANES), F32),
                   jax.ShapeDtypeStruct((ns, n_seq, w), F32)],
        scratch_shapes=[pltpu.VMEM((tile_rows, w), F32), pltpu.VMEM((tile_rows, w), F32),
                        pltpu.VMEM((1, w), F32)],
        compiler_params=_cparams(("arbitrary", "arbitrary")),
        name="s5_mixer",
    )(u_slabs, pm_hi, pm_lo, tm, qm, step_m, carry_m, a8, h0)


def _pack_s5_state(re, im, ns):
    b = re.shape[0]
    r = re.astype(F32).reshape(b, ns, -1)
    i = im.astype(F32).reshape(b, ns, -1)
    return jnp.transpose(jnp.concatenate([r, i], axis=-1), (1, 0, 2))


def _unpack_s5_state(hc, g, p):
    ns, b, w = hc.shape
    hc = jnp.transpose(hc, (1, 0, 2))
    re = hc[:, :, :w // 2].reshape(b, g, p)
    im = hc[:, :, w // 2:].reshape(b, g, p)
    return re, im


def _gla_kernel(q_ref, k_ref, v_ref, a_ref, r_ref, wa2_ref, ba2_ref, gh_ref, s0_ref,
                o_ref, sout_ref, st_scr, *, carry, n_chunks):
    step = pl.program_id(0)
    dk = GLA_DK
    dv = v_ref.shape[1] // GLA_HEADS
    c = CHUNK
    scale = dk ** -0.5
    ri = lax.broadcasted_iota(jnp.int32, (c, c), 0)
    ci = lax.broadcasted_iota(jnp.int32, (c, c), 1)
    causal = ri >= ci
    tril = causal.astype(BF16)
    eye_dk = (lax.broadcasted_iota(jnp.int32, (dk, dk), 0)
              == lax.broadcasted_iota(jnp.int32, (dk, dk), 1))

    if carry:
        @pl.when(step == 0)
        def _():
            st_scr[...] = s0_ref[0]

    wa_hi, wa_lo = _split_bf16(wa2_ref[...])
    for n in range(n_chunks):
        rows = slice(n * c, (n + 1) * c)
        a_hi, a_lo = _split_bf16(a_ref[rows, :])
        logit = _dot(a_hi, wa_hi) + _dot(a_lo, wa_hi) + _dot(a_hi, wa_lo) + ba2_ref[...]
        g = jax.nn.log_sigmoid(logit) * (1.0 / GLA_TAU)
        g1, g2, g3 = _split3_bf16(g)
        bcum = _dot(tril, g1) + _dot(tril, g2) + _dot(tril, g3)
        for h in range(GLA_HEADS):
            ks = slice(h * dk, (h + 1) * dk)
            vs = slice(h * dv, (h + 1) * dv)
            b = bcum[:, ks]
            qh = q_ref[rows, ks].astype(F32) * scale
            kh = k_ref[rows, ks].astype(F32)
            vh = v_ref[rows, vs]
            state = s0_ref[n, h] if not carry else st_scr[h]
            qe = (qh * jnp.exp(b)).astype(BF16)
            ke = (kh * jnp.exp(-b)).astype(BF16)
            att = jnp.where(causal, _dot_nt(qe, ke), 0.0)
            o = _dot(att.astype(BF16), vh) + _dot(qe, state.astype(BF16))
            blast = b[c - 1:c, :]
            kd = (kh * jnp.exp(blast - b)).astype(BF16)
            decay = jnp.sum(jnp.where(eye_dk, jnp.exp(blast), 0.0), axis=1, keepdims=True)
            new_state = decay * state + _dot_tn(kd, vh)
            if carry:
                st_scr[h] = new_state
            else:
                sout_ref[n, h] = new_state
            on = _rms(o, gh_ref[...])
            rr = r_ref[rows, vs].astype(F32)
            o_ref[rows, vs] = (on * (rr * jax.nn.sigmoid(rr))).astype(BF16)
    if carry:
        sout_ref[0] = st_scr[...]


def _gla_mixer(q, k, v, a, r, wa2, ba2, g_head, s0, t_start, t, carry, n_chunks):
    dqk = q.shape[1]
    dvt = v.shape[1]
    hh, dk, dv = s0.shape[1:]
    rows = n_chunks * CHUNK
    assert t % rows == 0 and t_start % rows == 0
    off = t_start // rows
    row = lambda i: (i, 0)
    row_in = lambda i: (i + off, 0)
    if carry:
        sblk, smap = (1, hh, dk, dv), (lambda i: (0, 0, 0, 0))
    else:
        sblk, smap = (n_chunks, hh, dk, dv), (lambda i: (i, 0, 0, 0))
    kern = functools.partial(_gla_kernel, carry=carry, n_chunks=n_chunks)
    return pl.pallas_call(
        kern,
        grid=(t // rows,),
        in_specs=[pl.BlockSpec((rows, dqk), row_in), pl.BlockSpec((rows, dqk), row_in),
                  pl.BlockSpec((rows, dvt), row_in), pl.BlockSpec((rows, LANES), row_in),
                  pl.BlockSpec((rows, dvt), row_in),
                  _const_spec(wa2.shape), _const_spec((1, dqk)), _const_spec((1, dv)),
                  pl.BlockSpec(sblk, smap)],
        out_specs=[pl.BlockSpec((rows, dvt), row), pl.BlockSpec(sblk, smap)],
        out_shape=[jax.ShapeDtypeStruct((t, dvt), BF16), jax.ShapeDtypeStruct(s0.shape, F32)],
        scratch_shapes=[pltpu.VMEM((hh, dk, dv), F32)],
        compiler_params=_cparams(("arbitrary",)),
        name="gla_mixer",
    )(q, k, v, a, r, wa2, ba2.reshape(1, dqk), g_head.reshape(1, dv), s0)


def _mixout_kernel(yp_ref, ys_ref, gp_ref, gs_ref, xp_ref, xs_ref, wglu_ref, bglu_ref, gs5_ref,
                   wout_ref, gx_ref, wq_ref, x1_ref, q_ref, *, n_first):
    first = pl.program_id(0) < n_first
    yb = _pick(first, yp_ref, ys_ref)
    y = jnp.concatenate([yb[j] for j in range(yb.shape[0])], axis=1)
    z = jax.nn.gelu(y)
    gate = jax.nn.sigmoid(_dot(z.astype(BF16), wglu_ref[...]) + bglu_ref[...])
    s5 = _rms(z * gate, gs5_ref[...])
    cat = jnp.concatenate([s5.astype(BF16), _pick(first, gp_ref, gs_ref)], axis=1)
    x1 = _pick(first, xp_ref, xs_ref) + _dot(cat, wout_ref[...])
    x1_ref[...] = x1
    q_ref[...] = _dot(_rms(x1, gx_ref[...]).astype(BF16), wq_ref[...]).astype(BF16)


def _mix_out(y_p, y_s, gla_p, gla_s, xp, xs, w_glu, b_glu, g_s5, w_out, g_x, w_q):
    d = xp.shape[1]
    t = xp.shape[0] + xs.shape[0]
    ns = y_p.shape[0]
    ds5 = ns * LANES
    dg = gla_p.shape[1]
    tm = TOKEN_TILE // 2
    n_first = xp.shape[0] // tm
    row = lambda i: (i, 0)
    return pl.pallas_call(
        functools.partial(_mixout_kernel, n_first=n_first),
        grid=(t // tm,),
        in_specs=(_pair_specs((ns, tm, LANES), 1, n_first) + _pair_specs((tm, dg), 0, n_first)
                  + _pair_specs((tm, d), 0, n_first)
                  + [_const_spec((ds5, ds5)), _const_spec((1, ds5)), _const_spec((1, ds5)),
                     _const_spec((ds5 + dg, d)), _const_spec((1, d)), _const_spec((d, d))]),
        out_specs=[pl.BlockSpec((tm, d), row), pl.BlockSpec((tm, d), row)],
        out_shape=[jax.ShapeDtypeStruct((t, d), F32), jax.ShapeDtypeStruct((t, d), BF16)],
        compiler_params=_cparams(("arbitrary",)),
        name="mix_out",
    )(y_p, y_s, gla_p, gla_s, xp, xs, w_glu.astype(BF16), b_glu.reshape(1, ds5),
      g_s5.reshape(1, ds5), w_out.astype(BF16), g_x.reshape(1, d), w_q.astype(BF16))


def _xattn_kernel(q_ref, mk_ref, mv_ref, x1_ref, wo_ref, gf_ref, wrh_ref, wrl_ref,
                  x2_ref, xn_ref, lg_ref, *cache_scratch, nb):
    d = q_ref.shape[1]
    lt = q_ref.shape[0] // nb
    hd = d // X_HEADS
    scale = hd ** -0.5
    if cache_scratch:
        kbuf, vbuf, sem = cache_scratch
        i = pl.program_id(0)

        def fetch(step, slot):
            for b in range(nb):
                for h in range(X_HEADS):
                    for src, dst in ((mk_ref, kbuf), (mv_ref, vbuf)):
                        pltpu.make_async_copy(src.at[step * nb + b, :, h, :],
                                              dst.at[slot, b, :, pl.ds(h * hd, hd)],
                                              sem.at[slot]).start()

        @pl.when(i == 0)
        def _():
            fetch(0, 0)

        @pl.when(i + 1 < pl.num_programs(0))
        def _():
            fetch(i + 1, (i + 1) % 2)

        slot = i % 2
        for buf in (kbuf, vbuf):
            pltpu.make_async_copy(buf.at[slot], buf.at[slot], sem.at[slot]).wait()
        memory = lambda b: (kbuf[slot, b], vbuf[slot, b])
    else:
        memory = lambda b: (mk_ref[b], mv_ref[b])
    for b in range(nb):
        rows = slice(b * lt, (b + 1) * lt)
        mk, mv = (m.astype(BF16) for m in memory(b))
        outs = []
        for h in range(X_HEADS):
            hs = slice(h * hd, (h + 1) * hd)
            s = _dot_nt(q_ref[rows, hs], mk[:, hs]) * scale
            p = jnp.exp(s - jnp.max(s, axis=-1, keepdims=True))
            denom = jnp.sum(p, axis=-1, keepdims=True)
            outs.append((_dot(p.astype(BF16), mv[:, hs]) / denom).astype(BF16))
        o = jnp.concatenate(outs, axis=1)
        x2 = x1_ref[rows, :] + _dot(o, wo_ref[...])
        x2_ref[rows, :] = x2
        xn = _rms(x2, gf_ref[...])
        xn_ref[rows, :] = xn
        hi, lo = _split_bf16(xn)
        lg_ref[:, rows] = (_dot_nt(wrh_ref[...], hi) + _dot_nt(wrh_ref[...], lo)
                           + _dot_nt(wrl_ref[...], hi))


def _xattn(q, x1, mk, mv, w_o, g_ffn, wr_hi, wr_lo, t_start, t, nb, lt):
    d = q.shape[1]
    nm = mk.shape[1]
    ne = wr_hi.shape[0]
    rows = nb * lt
    assert t % rows == 0 and t_start % rows == 0
    off = t_start // rows
    row = lambda i: (i, 0)
    row_in = lambda i: (i + off, 0)
    if mk.ndim == 3:
        assert nb == 1 and mk.shape[0] == 1
        mem_spec = pl.BlockSpec((1, nm, d), lambda i: (0, 0, 0))
        scratch = []
    else:
        assert mk.shape[0] * lt == t
        mem_spec = pl.BlockSpec(memory_space=pl.ANY)
        scratch = [pltpu.VMEM((2, nb, nm, d), F32), pltpu.VMEM((2, nb, nm, d), F32),
                   pltpu.SemaphoreType.DMA((2,))]
    return pl.pallas_call(
        functools.partial(_xattn_kernel, nb=nb),
        grid=(t // rows,),
        in_specs=[pl.BlockSpec((rows, d), row_in), mem_spec, mem_spec,
                  pl.BlockSpec((rows, d), row_in),
                  _const_spec((d, d)), _const_spec((1, d)),
                  _const_spec((ne, d)), _const_spec((ne, d))],
        scratch_shapes=scratch,
        out_specs=[pl.BlockSpec((rows, d), row), pl.BlockSpec((rows, d), row),
                   pl.BlockSpec((ne, rows), lambda i: (0, i))],
        out_shape=[jax.ShapeDtypeStruct((t, d), F32), jax.ShapeDtypeStruct((t, d), F32),
                   jax.ShapeDtypeStruct((ne, t), F32)],
        compiler_params=_cparams(("arbitrary",)),
        name="mem_xattn",
    )(q, mk, mv, x1, w_o, g_ffn.reshape(1, d), wr_hi, wr_lo)


def _route_kernel(lgp_ref, lgs_ref, xnp_ref, xns_ref, bias_ref, eidx_ref, gate_ref, pos_ref, cnt_ref,
                  xn_ref, run_scr, *, n_first):
    ne, tn = lgp_ref.shape
    first = pl.program_id(0) < n_first
    xn_ref[...] = _pick(first, xnp_ref, xns_ref)

    @pl.when(pl.program_id(0) == 0)
    def _():
        run_scr[...] = jnp.zeros_like(run_scr)

    gsz = ne // N_GROUPS
    sc = jax.nn.sigmoid(_pick(first, lgp_ref, lgs_ref))
    sel = sc + bias_ref[...]
    s3 = sel.reshape(N_GROUPS, gsz, tn)
    ie = lax.broadcasted_iota(jnp.int32, s3.shape, 1).astype(F32)
    m1 = jnp.max(s3, axis=1, keepdims=True)
    first = jnp.min(jnp.where(s3 == m1, ie, float(gsz)), axis=1, keepdims=True)
    m2 = jnp.max(jnp.where(ie == first, -jnp.inf, s3), axis=1, keepdims=True)
    gs = m1 + m2
    ig = lax.broadcasted_iota(jnp.int32, gs.shape, 0)
    grank = jnp.zeros(gs.shape, jnp.int32)
    for g in range(N_GROUPS):
        other = gs[g:g + 1]
        ahead = jnp.where(other > gs, 1, jnp.where(other == gs, (ig > g).astype(jnp.int32), 0))
        grank = grank + ahead
    gkeep = jnp.broadcast_to(grank < TOPK_GROUPS, s3.shape)
    v = jnp.where(gkeep, s3, -jnp.inf).reshape(ne, tn)
    iv = lax.broadcasted_iota(jnp.int32, v.shape, 0)
    rank = jnp.zeros(v.shape, jnp.int32)
    for e in range(ne):
        other = v[e:e + 1]
        ahead = jnp.where(other > v, 1, jnp.where(other == v, (iv > e).astype(jnp.int32), 0))
        rank = rank + ahead
    chosen = rank < TOP_K
    gate = jnp.where(chosen, sc, 0.0)
    gate = gate / jnp.sum(gate, axis=0, keepdims=True) * ROUTE_SCALE
    ti = lax.broadcasted_iota(jnp.int32, (tn, tn), 0)
    tj = lax.broadcasted_iota(jnp.int32, (tn, tn), 1)
    incl = _dot(jnp.where(chosen, 1.0, 0.0).astype(BF16), (ti <= tj).astype(BF16))
    pos = run_scr[...] + incl - 1.0
    run_scr[...] = run_scr[...] + incl[:, tn - 1:tn]
    cnt_ref[...] = run_scr[...].astype(jnp.int32)
    ef = iv.astype(F32)
    for k in range(TOP_K):
        hit = rank == k
        pick = lambda a: jnp.sum(jnp.where(hit, a, 0.0), axis=0, keepdims=True)
        eidx_ref[k:k + 1, :] = pick(ef).astype(jnp.int32)
        gate_ref[k:k + 1, :] = pick(gate)
        pos_ref[k:k + 1, :] = pick(pos).astype(jnp.int32)


def _route(lg_p, lg_s, xn_p, xn_s, router_bias):
    ne = lg_p.shape[0]
    d = xn_p.shape[1]
    t = lg_p.shape[1] + lg_s.shape[1]
    tn = TOKEN_TILE
    n_first = lg_p.shape[1] // tn
    kt = lambda i: (0, i)
    return pl.pallas_call(
        functools.partial(_route_kernel, n_first=n_first),
        grid=(t // tn,),
        in_specs=(_pair_specs((ne, tn), 1, n_first) + _pair_specs((tn, d), 0, n_first)
                  + [_const_spec((ne, 1))]),
        out_specs=[pl.BlockSpec((TOP_K, tn), kt), pl.BlockSpec((TOP_K, tn), kt),
                   pl.BlockSpec((TOP_K, tn), kt), pl.BlockSpec((ne, 1), lambda i: (0, 0)),
                   pl.BlockSpec((tn, d), lambda i: (i, 0))],
        out_shape=[jax.ShapeDtypeStruct((TOP_K, t), jnp.int32),
                   jax.ShapeDtypeStruct((TOP_K, t), F32),
                   jax.ShapeDtypeStruct((TOP_K, t), jnp.int32),
                   jax.ShapeDtypeStruct((ne, 1), jnp.int32),
                   jax.ShapeDtypeStruct((t, d), F32)],
        scratch_shapes=[pltpu.VMEM((ne, 1), F32)],
        compiler_params=_cparams(("arbitrary",)),
        name="router",
    )(lg_p, lg_s, xn_p, xn_s, router_bias.astype(F32).reshape(ne, 1))


def _moe_plan(eidx, posk, counts, n_tiles_max):
    ne = counts.shape[0]
    ntile = (counts + EXPERT_TILE - 1) // EXPERT_TILE
    cum = jnp.cumsum(ntile)
    base = (cum - ntile) * EXPERT_TILE
    n_tiles = cum[-1:]
    experts = jnp.arange(ne, dtype=jnp.int32)
    last_used = jnp.max(jnp.where(ntile > 0, experts, 0))
    tiles = jnp.arange(n_tiles_max, dtype=jnp.int32)
    tile_expert = jnp.sum(tiles[:, None] >= cum[None, :], axis=1)
    tile_expert = jnp.minimum(tile_expert, last_used).astype(jnp.int32)
    slot = (jnp.sum(jnp.where(eidx[..., None] == experts, base, 0), axis=-1) + posk)
    slot = slot.astype(jnp.int32)
    n_k, t = slot.shape
    tok = lax.broadcasted_iota(jnp.int32, slot.shape, 1)
    _, tok_sorted = lax.sort_key_val(slot.reshape(n_k * t), tok.reshape(n_k * t))
    tok_sorted = jnp.concatenate([tok_sorted, jnp.zeros((EXPERT_TILE,), jnp.int32)])
    first_pair = jnp.cumsum(counts) - counts
    first_tile = cum - ntile
    mine = tile_expert[:, None] == experts[None, :]
    lookup = lambda table: jnp.sum(jnp.where(mine, table[None, :], 0), axis=1)
    pair0 = lookup(first_pair) + (tiles - lookup(first_tile)) * EXPERT_TILE
    pair0 = jnp.clip(pair0, 0, n_k * t).astype(jnp.int32)
    return slot, tile_expert, n_tiles.astype(jnp.int32), tok_sorted, pair0


def _row_copy_wait(ref, n_rows, sem):
    pltpu.make_async_copy(ref.at[pl.ds(0, n_rows)], ref.at[pl.ds(0, n_rows)], sem).wait()


def _expert_kernel(te_ref, nt_ref, pair0_ref, tok_ref, xn_hbm, wg_ref, wu_ref, wd_ref, ys_ref,
                   x_a, x_b, wg_scr, wu_scr, wd_scr, sem_a, sem_b):
    i = pl.program_id(0)
    n_tiles = nt_ref[0]

    def issue_row(tile, buf, sem, r, queue):
        t = tok_ref[pair0_ref[tile] + r]
        pltpu.make_async_copy(xn_hbm.at[pl.ds(t, 1)], buf.at[pl.ds(r, 1)],
                              sem).start(priority=queue)

    @pl.when(i == 0)
    def _():
        def body(rb, _):
            for u in range(SUBLANES):
                issue_row(0, x_a, sem_a, rb * SUBLANES + u, u % 2)
            return 0
        lax.fori_loop(0, EXPERT_TILE // SUBLANES, body, 0)

    nxt_tile = jnp.where(i + 1 < n_tiles, i + 1, 0)

    def run(cur, sem_cur, nxt, sem_nxt):
        _row_copy_wait(xn_hbm, EXPERT_TILE, sem_cur)

        @pl.when((i == 0) | (te_ref[i] != te_ref[jnp.maximum(i - 1, 0)]))
        def _():
            wg_scr[...] = wg_ref[0].astype(BF16)
            wu_scr[...] = wu_ref[0].astype(BF16)
            wd_scr[...] = wd_ref[0].astype(BF16)

        for r in range(EXPERT_TILE):
            issue_row(nxt_tile, nxt, sem_nxt, r, r % 2)
        x = cur[...].astype(BF16)
        h = jax.nn.silu(_dot(x, wg_scr[...])) * _dot(x, wu_scr[...])
        ys_ref[...] = _dot(h.astype(BF16), wd_scr[...])

        @pl.when(i == n_tiles - 1)
        def _():
            _row_copy_wait(xn_hbm, EXPERT_TILE, sem_nxt)

    @pl.when((i < n_tiles) & (i % 2 == 0))
    def _():
        run(x_a, sem_a, x_b, sem_b)

    @pl.when((i < n_tiles) & (i % 2 == 1))
    def _():
        run(x_b, sem_b, x_a, sem_a)

    @pl.when(i >= n_tiles)
    def _():
        ys_ref[...] = jnp.zeros_like(ys_ref)


def _experts(xn, tile_expert, n_tiles, pair0, tok_sorted, w_gate, w_up, w_down):
    d = xn.shape[1]
    ne, _, de = w_gate.shape
    n_tiles_max = tile_expert.shape[0]
    out_tile = lambda i, *_: (i, 0)
    wsel = lambda i, te, *_: (te[i], 0, 0)
    return pl.pallas_call(
        _expert_kernel,
        grid_spec=pltpu.PrefetchScalarGridSpec(
            num_scalar_prefetch=4,
            grid=(n_tiles_max,),
            in_specs=[pl.BlockSpec(memory_space=pl.ANY),
                      pl.BlockSpec((1, d, de), wsel), pl.BlockSpec((1, d, de), wsel),
                      pl.BlockSpec((1, de, d), wsel)],
            out_specs=pl.BlockSpec((EXPERT_TILE, d), out_tile),
            scratch_shapes=[pltpu.VMEM((EXPERT_TILE, d), F32), pltpu.VMEM((EXPERT_TILE, d), F32),
                            pltpu.VMEM((d, de), BF16), pltpu.VMEM((d, de), BF16),
                            pltpu.VMEM((de, d), BF16),
                            pltpu.SemaphoreType.DMA, pltpu.SemaphoreType.DMA]),
        out_shape=jax.ShapeDtypeStruct((n_tiles_max * EXPERT_TILE, d), F32),
        compiler_params=_cparams(("arbitrary",)),
        name="moe_experts",
    )(tile_expert, n_tiles, pair0, tok_sorted, xn, w_gate, w_up, w_down)


def _combine_kernel(slot_ref, gate_ref, xn_ref, x2p_ref, x2s_ref, sg_ref, su_ref, sd_ref,
                    gf_ref, ys_hbm, yp_ref, ysamp_ref, buf_a, buf_b, base_scr, y_scr, sem_a, sem_b,
                    *, n_first):
    tm = xn_ref.shape[0]
    i = pl.program_id(0)
    n_steps = pl.num_programs(0)
    first = i < n_first
    rb_rows = SUBLANES

    def issue_rows(tile, buf, sem, r0, n):
        for rr in range(n):
            for k in range(TOP_K):
                s = slot_ref[(tile * tm + r0 + rr) * TOP_K + k]
                pltpu.make_async_copy(ys_hbm.at[pl.ds(s, 1)], buf.at[k, pl.ds(r0 + rr, 1)],
                                      sem).start(priority=k % 2)

    @pl.when(i == 0)
    def _():
        def body(r, _):
            issue_rows(0, buf_a, sem_a, r, 1)
            return 0
        lax.fori_loop(0, tm, body, 0)

    x = xn_ref[...].astype(BF16)
    hs = jax.nn.silu(_dot(x, sg_ref[...])) * _dot(x, su_ref[...])
    base_scr[...] = _pick(first, x2p_ref, x2s_ref) + _dot(hs.astype(BF16), sd_ref[...])
    nxt_tile = jnp.where(i + 1 < n_steps, i + 1, 0)

    def run(cur, sem_cur, nxt, sem_nxt):
        _row_copy_wait(ys_hbm, tm * TOP_K, sem_cur)

        for rb in range(tm // rb_rows):
            r0 = rb * rb_rows
            issue_rows(nxt_tile, nxt, sem_nxt, r0, rb_rows)
            acc = base_scr[pl.ds(r0, rb_rows), :]
            g = gate_ref[pl.ds(r0, rb_rows), :]
            for k in range(TOP_K):
                acc = acc + g[:, k:k + 1] * cur[k, pl.ds(r0, rb_rows), :]
            y_scr[pl.ds(r0, rb_rows), :] = _rms(acc, gf_ref[...])

        @pl.when(i == n_steps - 1)
        def _():
            _row_copy_wait(ys_hbm, tm * TOP_K, sem_nxt)

    @pl.when(i % 2 == 0)
    def _():
        run(buf_a, sem_a, buf_b, sem_b)

    @pl.when(i % 2 == 1)
    def _():
        run(buf_b, sem_b, buf_a, sem_a)

    @pl.when(first)
    def _():
        yp_ref[...] = y_scr[...]

    @pl.when(jnp.logical_not(first))
    def _():
        ysamp_ref[...] = y_scr[...]


def _combine(slot_flat, gates, xn, x2_p, x2_s, ws_gate, ws_up, ws_down, g_final, ys):
    d = x2_p.shape[1]
    tp, tsamp = x2_p.shape[0], x2_s.shape[0]
    dsh = ws_gate.shape[1]
    tm = COMBINE_TILE
    n_first = tp // tm
    row = lambda i, *_: (i, 0)
    pair = lambda: _pair_specs((tm, d), 0, n_first)
    return pl.pallas_call(
        functools.partial(_combine_kernel, n_first=n_first),
        grid_spec=pltpu.PrefetchScalarGridSpec(
            num_scalar_prefetch=1,
            grid=((tp + tsamp) // tm,),
            in_specs=[pl.BlockSpec((tm, TOP_K), row), pl.BlockSpec((tm, d), row)] + pair()
                     + [_const_spec((d, dsh)), _const_spec((d, dsh)), _const_spec((dsh, d)),
                        _const_spec((1, d)), pl.BlockSpec(memory_space=pl.ANY)],
            out_specs=pair(),
            scratch_shapes=[pltpu.VMEM((TOP_K, tm, d), F32), pltpu.VMEM((TOP_K, tm, d), F32),
                            pltpu.VMEM((tm, d), F32), pltpu.VMEM((tm, d), F32),
                            pltpu.SemaphoreType.DMA, pltpu.SemaphoreType.DMA]),
        out_shape=[jax.ShapeDtypeStruct((tp, d), F32), jax.ShapeDtypeStruct((tsamp, d), F32)],
        compiler_params=_cparams(("arbitrary",)),
        name="moe_combine",
    )(slot_flat, gates, xn, x2_p, x2_s, ws_gate.astype(BF16), ws_up.astype(BF16),
      ws_down.astype(BF16), g_final.reshape(1, d), ys)


def _moe(xn, x2_p, x2_s, eidx, gate, posk, counts, w_gate, w_up, w_down, ws_gate, ws_up,
         ws_down, g_final):
    t = xn.shape[0]
    ne = w_gate.shape[0]
    n_tiles_max = t * TOP_K // EXPERT_TILE + ne
    slot, tile_expert, n_tiles, tok_sorted, pair0 = _moe_plan(eidx, posk, counts[:, 0],
                                                              n_tiles_max)
    ys = _experts(xn, tile_expert, n_tiles, pair0, tok_sorted, w_gate, w_up, w_down)
    return _combine(jnp.transpose(slot).reshape(t * TOP_K), jnp.transpose(gate), xn, x2_p, x2_s,
                    ws_gate, ws_up, ws_down, g_final, ys)


def kernel(x_prompt, x_sample, cache_mem_k, cache_mem_v, state_s5_re, state_s5_im, state_gla, mem_prompt, w_in, w_a2, b_a2, lam_re, lam_im, log_dt, b_re, b_im, c_re, c_im, d_skip, w_glu, b_glu, g_s5_out, g_gla_head, w_out, g_mix, g_xattn, g_mem, w_q, w_k, w_v, w_o, g_ffn, w_router, router_bias, w_gate, w_up, w_down, ws_gate, ws_up, ws_down, g_final):
    depth = w_in.shape[0]
    assert depth == 1, "one trunk layer"
    bp, lp, d = x_prompt.shape
    bs, ls, _ = x_sample.shape
    assert bp == 1 and ls == CHUNK and lp % CHUNK == 0
    n_mem = mem_prompt.shape[1]
    g, p = lam_re.shape[1:]
    d_s5 = g * S5_GROUP
    d_qk = GLA_HEADS * GLA_DK
    d_gla = w_out.shape[1] - d_s5
    ns = d_s5 // LANES
    ne = w_router.shape[2]
    tp, tsamp = bp * lp, bs * ls
    xp = x_prompt.reshape(tp, d)
    xs = x_sample.reshape(tsamp, d)
    sq = lambda a: a.reshape(a.shape[1:])

    mk_p, mv_p = _memory_kv(mem_prompt.reshape(n_mem, d), sq(g_mem), sq(w_k), sq(w_v))

    u, q, k, v, a, r = _in_proj(xp, xs, sq(g_mix), sq(w_in), d_s5, d_qk, d_gla)

    ksm, psm, qsm, step_m, carry_m, a8 = _s5_coeffs(
        sq(lam_re), sq(lam_im), sq(log_dt), sq(b_re), sq(b_im), sq(c_re), sq(c_im), sq(d_skip))
    tables = tuple(_s5_expand(ksm, psm, qsm)) + (step_m, carry_m, a8)
    zero_h = jnp.zeros((ns, bp, 2 * GROUPS_PER_SLAB * p), F32)
    rows_p = lp // S5_BLOCK
    y_p, h_p = _s5_mixer(u, tables, zero_h, 0, tp, rows_per_seq=rows_p,
                         tile_rows=math.gcd(rows_p, 512))
    h0_s = _pack_s5_state(sq(state_s5_re), sq(state_s5_im), ns)
    y_s, h_s = _s5_mixer(u, tables, h0_s, tp, tsamp, rows_per_seq=ls // S5_BLOCK,
                         tile_rows=tsamp // S5_BLOCK)

    wa2 = jnp.pad(sq(w_a2), ((0, LANES - GLA_RANK), (0, 0)))
    zero_s = jnp.zeros((bp,) + state_gla.shape[2:], F32)
    gla_p, sg_p = _gla_mixer(q, k, v, a, r, wa2, sq(b_a2), sq(g_gla_head), zero_s, 0, tp,
                             carry=True, n_chunks=4)
    gla_s, sg_s = _gla_mixer(q, k, v, a, r, wa2, sq(b_a2), sq(g_gla_head), sq(state_gla), tp,
                             tsamp, carry=False, n_chunks=4)

    x1, xq = _mix_out(y_p, y_s, gla_p, gla_s, xp, xs, sq(w_glu), sq(b_glu), sq(g_s5_out),
                      sq(w_out), sq(g_xattn), sq(w_q))

    wr_hi, wr_lo = _split_bf16(jnp.transpose(sq(w_router)))
    wo = sq(w_o).astype(BF16)
    x2_p, xn_p, lg_p = _xattn(xq, x1, mk_p[None], mv_p[None], wo, sq(g_ffn), wr_hi, wr_lo,
                              0, tp, nb=1, lt=TOKEN_TILE)
    x2_s, xn_s, lg_s = _xattn(xq, x1, sq(cache_mem_k), sq(cache_mem_v), wo, sq(g_ffn), wr_hi,
                              wr_lo, tp, tsamp, nb=2, lt=ls)

    eidx, gate, posk, counts, xn = _route(lg_p, lg_s, xn_p, xn_s, sq(router_bias))
    y_p2, y_s2 = _moe(xn, x2_p, x2_s, eidx, gate, posk, counts, sq(w_gate), sq(w_up),
                      sq(w_down), sq(ws_gate), sq(ws_up), sq(ws_down), g_final)

    xh = d // X_HEADS
    re_p, im_p = _unpack_s5_state(h_p, g, p)
    re_s, im_s = _unpack_s5_state(h_s, g, p)
    return (y_p2.reshape(bp, lp, d), y_s2.reshape(bs, ls, d),
            mk_p.reshape(1, bp, n_mem, X_HEADS, xh), mv_p.reshape(1, bp, n_mem, X_HEADS, xh),
            re_p[None], im_p[None], sg_p[None], re_s[None], im_s[None], sg_s[None])
```

```python
import functools
import math

import jax
import jax.numpy as jnp
from jax import lax
from jax.experimental import pallas as pl
from jax.experimental.pallas import tpu as pltpu

F32 = jnp.float32
BF16 = jnp.bfloat16

EPS = 1e-6
CHUNK = 64
S5_GROUP = 16
GLA_HEADS = 4
GLA_DK = 128
GLA_RANK = 16
GLA_TAU = 16.0
X_HEADS = 4
N_GROUPS = 8
TOPK_GROUPS = 4
TOP_K = 8
ROUTE_SCALE = 2.5

LANES = 128
SUBLANES = 8
S5_BLOCK = SUBLANES
GROUPS_PER_SLAB = LANES // S5_GROUP
TOKEN_TILE = 512
EXPERT_TILE = 256
COMBINE_TILE = 128
VMEM_LIMIT = 56 * 1024 * 1024


def _cparams(sem):
    return pltpu.CompilerParams(dimension_semantics=sem, vmem_limit_bytes=VMEM_LIMIT)


def _const_spec(shape):
    nd = len(shape)
    return pl.BlockSpec(shape, lambda *_: (0,) * nd, pipeline_mode=pl.Buffered(1))


def _pair_specs(block, axis, n_first):
    def at(f):
        return lambda i, *_: tuple(f(i) if a == axis else 0 for a in range(len(block)))
    return [pl.BlockSpec(block, at(lambda i: jnp.minimum(i, n_first - 1))),
            pl.BlockSpec(block, at(lambda i: jnp.maximum(i - n_first, 0)))]


def _pick(first, a_ref, b_ref):
    return jnp.where(first, a_ref[...], b_ref[...])


def _rms(x, g):
    return x * lax.rsqrt(jnp.mean(x * x, axis=-1, keepdims=True) + EPS) * g


def _split_bf16(x):
    hi = x.astype(BF16)
    lo = (x - hi.astype(F32)).astype(BF16)
    return hi, lo


def _split3_bf16(x):
    a = x.astype(BF16)
    r = x - a.astype(F32)
    b = r.astype(BF16)
    c = (r - b.astype(F32)).astype(BF16)
    return a, b, c


def _dot(a, b):
    return jnp.dot(a, b, preferred_element_type=F32)


def _dot_nt(a, b):
    return lax.dot_general(a, b, (((1,), (1,)), ((), ())), preferred_element_type=F32)


def _dot_tn(a, b):
    return lax.dot_general(a, b, (((0,), (0,)), ((), ())), preferred_element_type=F32)


def _memkv_kernel(mem_ref, g_ref, wk_ref, wv_ref, mk_ref, mv_ref):
    m = _rms(mem_ref[...], g_ref[...]).astype(BF16)
    mk_ref[...] = _dot(m, wk_ref[...].astype(BF16))
    mv_ref[...] = _dot(m, wv_ref[...].astype(BF16))


def _memory_kv(mem, g_mem, wk, wv):
    n, d = mem.shape
    tn = 512
    return pl.pallas_call(
        _memkv_kernel,
        grid=(d // tn,),
        in_specs=[_const_spec((n, d)), _const_spec((1, d)),
                  pl.BlockSpec((d, tn), lambda j: (0, j)),
                  pl.BlockSpec((d, tn), lambda j: (0, j))],
        out_specs=[pl.BlockSpec((n, tn), lambda j: (0, j))] * 2,
        out_shape=[jax.ShapeDtypeStruct((n, d), F32)] * 2,
        compiler_params=_cparams(("arbitrary",)),
        name="memory_kv",
    )(mem, g_mem.reshape(1, d), wk, wv)


def _inproj_kernel(xp_ref, xs_ref, g_ref, wuh_ref, wul_ref, wqkv_ref, wa_ref, wr_ref,
                   u_ref, q_ref, k_ref, v_ref, a_ref, r_ref, *, n_first):
    first = pl.program_id(0) < n_first
    x = _pick(first, xp_ref, xs_ref)
    nx = _rms(x, g_ref[...])
    hi, lo = _split_bf16(nx)
    u = _dot_nt(hi, wuh_ref[...])
    for j in range(u_ref.shape[0]):
        u_ref[j] = u[:, j * LANES:(j + 1) * LANES]

    @pl.when(jnp.logical_not(first))
    def _():
        fix = _dot_nt(lo, wuh_ref[...]) + _dot_nt(hi, wul_ref[...])
        for j in range(u_ref.shape[0]):
            u_ref[j] += fix[:, j * LANES:(j + 1) * LANES]

    qkv = _dot_nt(hi, wqkv_ref[...])
    dq = q_ref.shape[1]
    q_ref[...] = qkv[:, :dq].astype(BF16)
    k_ref[...] = qkv[:, dq:2 * dq].astype(BF16)
    v_ref[...] = qkv[:, 2 * dq:].astype(BF16)
    a_ref[...] = _dot_nt(hi, wa_ref[...]) + _dot_nt(lo, wa_ref[...])
    r_ref[...] = _dot_nt(hi, wr_ref[...]).astype(BF16)


def _in_proj(xp, xs, g_mix, w_in, d_s5, d_qk, d_gla):
    d = xp.shape[1]
    t = xp.shape[0] + xs.shape[0]
    tm = TOKEN_TILE
    n_first = xp.shape[0] // tm
    o1 = d_s5
    o2 = o1 + 2 * d_qk + d_gla
    o3 = o2 + GLA_RANK
    wt = jnp.transpose(w_in)
    wu_hi, wu_lo = _split_bf16(wt[:o1])
    wqkv = wt[o1:o2].astype(BF16)
    wa = jnp.pad(wt[o2:o3], ((0, LANES - GLA_RANK), (0, 0))).astype(BF16)
    wr = wt[o3:].astype(BF16)
    n_slab = d_s5 // LANES
    row = lambda i: (i, 0)
    return pl.pallas_call(
        functools.partial(_inproj_kernel, n_first=n_first),
        grid=(t // tm,),
        in_specs=_pair_specs((tm, d), 0, n_first) + [
            _const_spec((1, d)), _const_spec(wu_hi.shape), _const_spec(wu_lo.shape),
            _const_spec(wqkv.shape), _const_spec(wa.shape), _const_spec(wr.shape)],
        out_specs=[pl.BlockSpec((n_slab, tm, LANES), lambda i: (0, i, 0)),
                   pl.BlockSpec((tm, d_qk), row), pl.BlockSpec((tm, d_qk), row),
                   pl.BlockSpec((tm, d_gla), row), pl.BlockSpec((tm, LANES), row),
                   pl.BlockSpec((tm, d_gla), row)],
        out_shape=[jax.ShapeDtypeStruct((n_slab, t, LANES), F32),
                   jax.ShapeDtypeStruct((t, d_qk), BF16), jax.ShapeDtypeStruct((t, d_qk), BF16),
                   jax.ShapeDtypeStruct((t, d_gla), BF16), jax.ShapeDtypeStruct((t, LANES), F32),
                   jax.ShapeDtypeStruct((t, d_gla), BF16)],
        compiler_params=_cparams(("arbitrary",)),
        name="in_proj",
    )(xp, xs, g_mix.reshape(1, d), wu_hi, wu_lo, wqkv, wa, wr)


def _s5_coeffs(lam_re, lam_im, log_dt, b_re, b_im, c_re, c_im, d_skip):
    g, p = lam_re.shape
    h = b_re.shape[-1]
    nb = S5_BLOCK
    ns = g // GROUPS_PER_SLAB
    gl = GROUPS_PER_SLAB
    dt = jnp.exp(log_dt.astype(F32))[:, None]
    lr, li = lam_re.astype(F32), lam_im.astype(F32)
    mag = jnp.exp(lr * dt)
    ar, ai = mag * jnp.cos(li * dt), mag * jnp.sin(li * dt)
    den = lr * lr + li * li
    nr = ar - 1.0
    cf_r = (nr * lr + ai * li) / den
    cf_i = (ai * lr - nr * li) / den
    bb_r = cf_r[..., None] * b_re - cf_i[..., None] * b_im
    bb_i = cf_r[..., None] * b_im + cf_i[..., None] * b_re

    def cpow(n):
        e = n[:, None, None] * dt[None]
        m = jnp.exp(lr[None] * e)
        return m * jnp.cos(li[None] * e), m * jnp.sin(li[None] * e)

    pr, pi = cpow(jnp.arange(nb + 1, dtype=F32))

    bt_r, bt_i = jnp.swapaxes(bb_r, 1, 2), jnp.swapaxes(bb_i, 1, 2)
    ct_r, ct_i = jnp.swapaxes(c_re, 1, 2), jnp.swapaxes(c_im, 1, 2)

    cb_r = jnp.einsum('gcp,gph->gphc', c_re, bb_r) - jnp.einsum('gcp,gph->gphc', c_im, bb_i)
    cb_i = jnp.einsum('gcp,gph->gphc', c_re, bb_i) + jnp.einsum('gcp,gph->gphc', c_im, bb_r)
    taps = (jnp.einsum('ngp,gphc->nghc', pr[:nb], cb_r)
            - jnp.einsum('ngp,gphc->nghc', pi[:nb], cb_i))
    skip = d_skip[None, :, :, None] * jnp.eye(h, dtype=F32)[None, None]
    taps = taps + jnp.where(jnp.arange(nb)[:, None, None, None] == 0, skip, 0.0)
    ksm = jnp.swapaxes(taps.reshape(nb, ns, gl * h, h), 0, 1)

    wr, wi = pr[nb - 1::-1][:nb], pi[nb - 1::-1][:nb]
    wr, wi = wr[:, :, None, :], wi[:, :, None, :]
    inj = jnp.concatenate([wr * bt_r[None] - wi * bt_i[None],
                           wr * bt_i[None] + wi * bt_r[None]], axis=-1)
    psm = jnp.swapaxes(inj.reshape(nb, ns, gl * h, 2 * p), 0, 1)

    er = jnp.transpose(pr[1:nb + 1], (1, 2, 0))[..., None]
    ei = jnp.transpose(pi[1:nb + 1], (1, 2, 0))[..., None]
    q_r = ct_r[:, :, None, :] * er - ct_i[:, :, None, :] * ei
    q_i = -(ct_r[:, :, None, :] * ei + ct_i[:, :, None, :] * er)
    qq = jnp.stack([q_r.reshape(ns, gl * p, nb * h), q_i.reshape(ns, gl * p, nb * h)], axis=1)
    qsm = qq.reshape(ns, 2 * gl * p, nb * h)

    def slab(xr, xi):
        k = xr.shape[0]
        xr = jnp.transpose(xr.reshape(k, ns, gl * p), (1, 0, 2))
        xi = jnp.transpose(xi.reshape(k, ns, gl * p), (1, 0, 2))
        return jnp.concatenate([xr, xi], axis=-1)

    rows = jnp.arange(SUBLANES, dtype=F32)
    cr, ci = cpow(nb * rows)
    carry_m = slab(cr, ci)
    sr, si = cpow(nb * jnp.array([1.0, 2.0, 4.0], F32))
    keep = (rows[None, :] >= jnp.array([1.0, 2.0, 4.0], F32)[:, None]).astype(F32)
    step_m = slab((sr[:, None] * keep[:, :, None, None]).reshape(3 * SUBLANES, g, p),
                  (si[:, None] * keep[:, :, None, None]).reshape(3 * SUBLANES, g, p))
    step_m = step_m.reshape(ns, 3, SUBLANES, 2 * gl * p)
    a8 = carry_m[:, 1:2]
    return ksm, psm, qsm, step_m, carry_m, a8


def _s5_expand_kernel(ksm_ref, psm_ref, qsm_ref, tm_ref, ph_ref, pl_ref, qm_ref):
    nb = S5_BLOCK
    h = S5_GROUP
    w = ph_ref.shape[2]
    p = w // (2 * GROUPS_PER_SLAB)

    def spread(x, sel, keep):
        a, b, c = _split3_bf16(x)
        return jnp.where(keep, _dot(a, sel) + _dot(b, sel) + _dot(c, sel), 0.0)

    def iota(shape, axis):
        return lax.broadcasted_iota(jnp.int32, shape, axis)

    sel_c = (iota((h, LANES), 1) % h == iota((h, LANES), 0)).astype(BF16)
    keep_t = iota((LANES, LANES), 0) // h == iota((LANES, LANES), 1) // h
    tm_ref[...] = jnp.zeros_like(tm_ref)
    for tau in range(nb):
        blk = spread(ksm_ref[0, tau], sel_c, keep_t).astype(BF16)
        for s in range(nb - tau):
            t = s + tau
            tm_ref[0, s * LANES:(s + 1) * LANES, t * LANES:(t + 1) * LANES] = blk

    half = w // 2
    keep_p = iota((LANES, half), 0) // h == iota((LANES, half), 1) // p
    for ri in range(2):
        sel_p = (iota((2 * p, half), 0) == iota((2 * p, half), 1) % p + ri * p).astype(BF16)
        for s in range(nb):
            blk = spread(psm_ref[0, s], sel_p, keep_p)
            hi, lo = _split_bf16(blk)
            ph_ref[0, s * LANES:(s + 1) * LANES, ri * half:(ri + 1) * half] = hi
            pl_ref[0, s * LANES:(s + 1) * LANES, ri * half:(ri + 1) * half] = lo

    keep_q = (iota((w, LANES), 0) % half) // p == iota((w, LANES), 1) // h
    for t in range(nb):
        sel_q = (iota((nb * h, LANES), 0) == iota((nb * h, LANES), 1) % h + t * h).astype(BF16)
        qm_ref[0, :, t * LANES:(t + 1) * LANES] = spread(qsm_ref[0], sel_q, keep_q).astype(BF16)


def _s5_expand(ksm, psm, qsm):
    ns = ksm.shape[0]
    w = qsm.shape[1]
    k = S5_BLOCK * LANES
    blk = lambda shape: pl.BlockSpec((1,) + shape, lambda j: (j,) + (0,) * len(shape))
    return pl.pallas_call(
        _s5_expand_kernel,
        grid=(ns,),
        in_specs=[blk(ksm.shape[1:]), blk(psm.shape[1:]), blk(qsm.shape[1:])],
        out_specs=[blk((k, k)), blk((k, w)), blk((k, w)), blk((w, k))],
        out_shape=[jax.ShapeDtypeStruct((ns, k, k), BF16), jax.ShapeDtypeStruct((ns, k, w), BF16),
                   jax.ShapeDtypeStruct((ns, k, w), BF16), jax.ShapeDtypeStruct((ns, w, k), BF16)],
        compiler_params=_cparams(("arbitrary",)),
        name="s5_tables",
    )(ksm, psm, qsm)


def _cmul(xr, xi, mr, mi):
    return xr * mr - xi * mi, xr * mi + xi * mr


def _s5_kernel(u_ref, ph_ref, pl_ref, tm_ref, qm_ref, step_ref, carry_ref, a8_ref, h0_ref,
               y_ref, hout_ref, s_scr, hin_scr, c_scr, *, rows_per_seq):
    ti = pl.program_id(1)
    rows = s_scr.shape[0]
    half = s_scr.shape[1] // 2
    nb = S5_BLOCK
    u8 = jnp.concatenate([u_ref[0, pl.ds(s, rows, stride=nb), :] for s in range(nb)], axis=1)
    hi, lo = _split_bf16(u8)
    per_block_seq = rows_per_seq == SUBLANES
    if per_block_seq:
        s_scr[...] = _dot(hi, ph_ref[0]) + _dot(lo, ph_ref[0]) + _dot(hi, pl_ref[0])
    else:
        s_scr[...] = _dot(hi, ph_ref[0])

    if not per_block_seq:
        @pl.when(ti == 0)
        def _():
            c_scr[...] = h0_ref[0]

    not_first = (lax.broadcasted_iota(jnp.int32, (SUBLANES, 1), 0) >= 1).astype(F32)
    a8r, a8i = a8_ref[0, :, :half], a8_ref[0, :, half:]
    cmr, cmi = carry_ref[0, :, :half], carry_ref[0, :, half:]

    def body(rb, _):
        r0 = pl.multiple_of(rb * SUBLANES, SUBLANES)
        sb = s_scr[pl.ds(r0, SUBLANES), :]
        x = pltpu.roll(sb, 1, 0) * not_first
        xr, xi = x[:, :half], x[:, half:]
        for k, sh in enumerate((1, 2, 4)):
            m = step_ref[0, k]
            pr, pi = _cmul(pltpu.roll(xr, sh, 0), pltpu.roll(xi, sh, 0), m[:, :half], m[:, half:])
            xr, xi = xr + pr, xi + pi
        c = h0_ref[0, pl.ds(rb, 1), :] if per_block_seq else c_scr[...]
        cr, ci = c[:, :half], c[:, half:]
        pr, pi = _cmul(cr, ci, cmr, cmi)
        hr, hi_ = xr + pr, xi + pi
        hin_scr[pl.ds(r0, SUBLANES), :half] = hr
        hin_scr[pl.ds(r0, SUBLANES), half:] = hi_
        nr, ni = _cmul(hr[SUBLANES - 1:], hi_[SUBLANES - 1:], a8r, a8i)
        cn = jnp.concatenate([nr, ni], axis=1) + sb[SUBLANES - 1:]
        if per_block_seq:
            hout_ref[0, pl.ds(rb, 1), :] = cn
        else:
            c_scr[...] = cn
        return 0

    lax.fori_loop(0, rows // SUBLANES, body, 0)
    if not per_block_seq:
        hout_ref[0] = c_scr[...]

    y8 = _dot(hi, tm_ref[0]) + _dot(hin_scr[...].astype(BF16), qm_ref[0])
    for t in range(nb):
        y_ref[0, pl.ds(t, rows, stride=nb), :] = y8[:, t * LANES:(t + 1) * LANES]


def _s5_mixer(u_slabs, tables, h0, t_start, t, rows_per_seq, tile_rows):
    tm, pm_hi, pm_lo, qm, step_m, carry_m, a8 = tables
    ns = u_slabs.shape[0]
    rows = t // S5_BLOCK
    n_tiles = rows // tile_rows
    assert rows % tile_rows == 0 and t_start % (tile_rows * S5_BLOCK) == 0
    off = t_start // (tile_rows * S5_BLOCK)
    n_seq = h0.shape[1]
    w = pm_hi.shape[-1]
    slab3 = lambda j, i: (j, 0, 0)
    kern = functools.partial(_s5_kernel, rows_per_seq=rows_per_seq)
    return pl.pallas_call(
        kern,
        grid=(ns, n_tiles),
        in_specs=[pl.BlockSpec((1, tile_rows * S5_BLOCK, LANES), lambda j, i: (j, i + off, 0)),
                  pl.BlockSpec((1,) + pm_hi.shape[1:], slab3),
                  pl.BlockSpec((1,) + pm_lo.shape[1:], slab3),
                  pl.BlockSpec((1,) + tm.shape[1:], slab3),
                  pl.BlockSpec((1,) + qm.shape[1:], slab3),
                  pl.BlockSpec((1,) + step_m.shape[1:], lambda j, i: (j, 0, 0, 0)),
                  pl.BlockSpec((1,) + carry_m.shape[1:], slab3),
                  pl.BlockSpec((1,) + a8.shape[1:], slab3),
                  pl.BlockSpec((1, n_seq, w), slab3)],
        out_specs=[pl.BlockSpec((1, tile_rows * S5_BLOCK, LANES), lambda j, i: (j, i, 0)),
                   pl.BlockSpec((1, n_seq, w), slab3)],
        out_shape=[jax.ShapeDtypeStruct((ns, t, LANES), F32),
                   jax.ShapeDtypeStruct((ns, n_seq, w), F32)],
        scratch_shapes=[pltpu.VMEM((tile_rows, w), F32), pltpu.VMEM((tile_rows, w), F32),
                        pltpu.VMEM((1, w), F32)],
        compiler_params=_cparams(("arbitrary", "arbitrary")),
        name="s5_mixer",
    )(u_slabs, pm_hi, pm_lo, tm, qm, step_m, carry_m, a8, h0)


def _pack_s5_state(re, im, ns):
    b = re.shape[0]
    r = re.astype(F32).reshape(b, ns, -1)
    i = im.astype(F32).reshape(b, ns, -1)
    return jnp.transpose(jnp.concatenate([r, i], axis=-1), (1, 0, 2))


def _unpack_s5_state(hc, g, p):
    ns, b, w = hc.shape
    hc = jnp.transpose(hc, (1, 0, 2))
    re = hc[:, :, :w // 2].reshape(b, g, p)
    im = hc[:, :, w // 2:].reshape(b, g, p)
    return re, im


def _gla_kernel(q_ref, k_ref, v_ref, a_ref, r_ref, wa2_ref, ba2_ref, gh_ref, s0_ref,
                o_ref, sout_ref, st_scr, *, carry, n_chunks):
    step = pl.program_id(0)
    dk = GLA_DK
    dv = v_ref.shape[1] // GLA_HEADS
    c = CHUNK
    scale = dk ** -0.5
    ri = lax.broadcasted_iota(jnp.int32, (c, c), 0)
    ci = lax.broadcasted_iota(jnp.int32, (c, c), 1)
    causal = ri >= ci
    tril = causal.astype(BF16)
    eye_dk = (lax.broadcasted_iota(jnp.int32, (dk, dk), 0)
              == lax.broadcasted_iota(jnp.int32, (dk, dk), 1))

    if carry:
        @pl.when(step == 0)
        def _():
            st_scr[...] = s0_ref[0]

    wa_hi, wa_lo = _split_bf16(wa2_ref[...])
    for n in range(n_chunks):
        rows = slice(n * c, (n + 1) * c)
        a_hi, a_lo = _split_bf16(a_ref[rows, :])
        logit = _dot(a_hi, wa_hi) + _dot(a_lo, wa_hi) + _dot(a_hi, wa_lo) + ba2_ref[...]
        g = jax.nn.log_sigmoid(logit) * (1.0 / GLA_TAU)
        g1, g2, g3 = _split3_bf16(g)
        bcum = _dot(tril, g1) + _dot(tril, g2) + _dot(tril, g3)
        for h in range(GLA_HEADS):
            ks = slice(h * dk, (h + 1) * dk)
            vs = slice(h * dv, (h + 1) * dv)
            b = bcum[:, ks]
            qh = q_ref[rows, ks].astype(F32) * scale
            kh = k_ref[rows, ks].astype(F32)
            vh = v_ref[rows, vs]
            state = s0_ref[n, h] if not carry else st_scr[h]
            qe = (qh * jnp.exp(b)).astype(BF16)
            ke = (kh * jnp.exp(-b)).astype(BF16)
            att = jnp.where(causal, _dot_nt(qe, ke), 0.0)
            o = _dot(att.astype(BF16), vh) + _dot(qe, state.astype(BF16))
            blast = b[c - 1:c, :]
            kd = (kh * jnp.exp(blast - b)).astype(BF16)
            decay = jnp.sum(jnp.where(eye_dk, jnp.exp(blast), 0.0), axis=1, keepdims=True)
            new_state = decay * state + _dot_tn(kd, vh)
            if carry:
                st_scr[h] = new_state
            else:
                sout_ref[n, h] = new_state
            on = _rms(o, gh_ref[...])
            rr = r_ref[rows, vs].astype(F32)
            o_ref[rows, vs] = (on * (rr * jax.nn.sigmoid(rr))).astype(BF16)
    if carry:
        sout_ref[0] = st_scr[...]


def _gla_mixer(q, k, v, a, r, wa2, ba2, g_head, s0, t_start, t, carry, n_chunks):
    dqk = q.shape[1]
    dvt = v.shape[1]
    hh, dk, dv = s0.shape[1:]
    rows = n_chunks * CHUNK
    assert t % rows == 0 and t_start % rows == 0
    off = t_start // rows
    row = lambda i: (i, 0)
    row_in = lambda i: (i + off, 0)
    if carry:
        sblk, smap = (1, hh, dk, dv), (lambda i: (0, 0, 0, 0))
    else:
        sblk, smap = (n_chunks, hh, dk, dv), (lambda i: (i, 0, 0, 0))
    kern = functools.partial(_gla_kernel, carry=carry, n_chunks=n_chunks)
    return pl.pallas_call(
        kern,
        grid=(t // rows,),
        in_specs=[pl.BlockSpec((rows, dqk), row_in), pl.BlockSpec((rows, dqk), row_in),
                  pl.BlockSpec((rows, dvt), row_in), pl.BlockSpec((rows, LANES), row_in),
                  pl.BlockSpec((rows, dvt), row_in),
                  _const_spec(wa2.shape), _const_spec((1, dqk)), _const_spec((1, dv)),
                  pl.BlockSpec(sblk, smap)],
        out_specs=[pl.BlockSpec((rows, dvt), row), pl.BlockSpec(sblk, smap)],
        out_shape=[jax.ShapeDtypeStruct((t, dvt), BF16), jax.ShapeDtypeStruct(s0.shape, F32)],
        scratch_shapes=[pltpu.VMEM((hh, dk, dv), F32)],
        compiler_params=_cparams(("arbitrary",)),
        name="gla_mixer",
    )(q, k, v, a, r, wa2, ba2.reshape(1, dqk), g_head.reshape(1, dv), s0)


def _mixout_kernel(yp_ref, ys_ref, gp_ref, gs_ref, xp_ref, xs_ref, wglu_ref, bglu_ref, gs5_ref,
                   wout_ref, gx_ref, wq_ref, x1_ref, q_ref, *, n_first):
    first = pl.program_id(0) < n_first
    yb = _pick(first, yp_ref, ys_ref)
    y = jnp.concatenate([yb[j] for j in range(yb.shape[0])], axis=1)
    z = jax.nn.gelu(y)
    gate = jax.nn.sigmoid(_dot(z.astype(BF16), wglu_ref[...]) + bglu_ref[...])
    s5 = _rms(z * gate, gs5_ref[...])
    cat = jnp.concatenate([s5.astype(BF16), _pick(first, gp_ref, gs_ref)], axis=1)
    x1 = _pick(first, xp_ref, xs_ref) + _dot(cat, wout_ref[...])
    x1_ref[...] = x1
    q_ref[...] = _dot(_rms(x1, gx_ref[...]).astype(BF16), wq_ref[...]).astype(BF16)


def _mix_out(y_p, y_s, gla_p, gla_s, xp, xs, w_glu, b_glu, g_s5, w_out, g_x, w_q):
    d = xp.shape[1]
    t = xp.shape[0] + xs.shape[0]
    ns = y_p.shape[0]
    ds5 = ns * LANES
    dg = gla_p.shape[1]
    tm = TOKEN_TILE // 2
    n_first = xp.shape[0] // tm
    row = lambda i: (i, 0)
    return pl.pallas_call(
        functools.partial(_mixout_kernel, n_first=n_first),
        grid=(t // tm,),
        in_specs=(_pair_specs((ns, tm, LANES), 1, n_first) + _pair_specs((tm, dg), 0, n_first)
                  + _pair_specs((tm, d), 0, n_first)
                  + [_const_spec((ds5, ds5)), _const_spec((1, ds5)), _const_spec((1, ds5)),
                     _const_spec((ds5 + dg, d)), _const_spec((1, d)), _const_spec((d, d))]),
        out_specs=[pl.BlockSpec((tm, d), row), pl.BlockSpec((tm, d), row)],
        out_shape=[jax.ShapeDtypeStruct((t, d), F32), jax.ShapeDtypeStruct((t, d), BF16)],
        compiler_params=_cparams(("arbitrary",)),
        name="mix_out",
    )(y_p, y_s, gla_p, gla_s, xp, xs, w_glu.astype(BF16), b_glu.reshape(1, ds5),
      g_s5.reshape(1, ds5), w_out.astype(BF16), g_x.reshape(1, d), w_q.astype(BF16))


def _xattn_kernel(q_ref, mk_ref, mv_ref, x1_ref, wo_ref, gf_ref, wrh_ref, wrl_ref,
                  x2_ref, xn_ref, lg_ref, *cache_scratch, nb):
    d = q_ref.shape[1]
    lt = q_ref.shape[0] // nb
    hd = d // X_HEADS
    scale = hd ** -0.5
    if cache_scratch:
        kbuf, vbuf, sem = cache_scratch
        i = pl.program_id(0)

        def fetch(step, slot):
            for b in range(nb):
                for h in range(X_HEADS):
                    for src, dst in ((mk_ref, kbuf), (mv_ref, vbuf)):
                        pltpu.make_async_copy(src.at[step * nb + b, :, h, :],
                                              dst.at[slot, b, :, pl.ds(h * hd, hd)],
                                              sem.at[slot]).start()

        @pl.when(i == 0)
        def _():
            fetch(0, 0)

        @pl.when(i + 1 < pl.num_programs(0))
        def _():
            fetch(i + 1, (i + 1) % 2)

        slot = i % 2
        for buf in (kbuf, vbuf):
            pltpu.make_async_copy(buf.at[slot], buf.at[slot], sem.at[slot]).wait()
        memory = lambda b: (kbuf[slot, b], vbuf[slot, b])
    else:
        memory = lambda b: (mk_ref[b], mv_ref[b])
    for b in range(nb):
        rows = slice(b * lt, (b + 1) * lt)
        mk, mv = (m.astype(BF16) for m in memory(b))
        outs = []
        for h in range(X_HEADS):
            hs = slice(h * hd, (h + 1) * hd)
            s = _dot_nt(q_ref[rows, hs], mk[:, hs]) * scale
            p = jnp.exp(s - jnp.max(s, axis=-1, keepdims=True))
            denom = jnp.sum(p, axis=-1, keepdims=True)
            outs.append((_dot(p.astype(BF16), mv[:, hs]) / denom).astype(BF16))
        o = jnp.concatenate(outs, axis=1)
        x2 = x1_ref[rows, :] + _dot(o, wo_ref[...])
        x2_ref[rows, :] = x2
        xn = _rms(x2, gf_ref[...])
        xn_ref[rows, :] = xn
        hi, lo = _split_bf16(xn)
        lg_ref[:, rows] = (_dot_nt(wrh_ref[...], hi) + _dot_nt(wrh_ref[...], lo)
                           + _dot_nt(wrl_ref[...], hi))


def _xattn(q, x1, mk, mv, w_o, g_ffn, wr_hi, wr_lo, t_start, t, nb, lt):
    d = q.shape[1]
    nm = mk.shape[1]
    ne = wr_hi.shape[0]
    rows = nb * lt
    assert t % rows == 0 and t_start % rows == 0
    off = t_start // rows
    row = lambda i: (i, 0)
    row_in = lambda i: (i + off, 0)
    if mk.ndim == 3:
        assert nb == 1 and mk.shape[0] == 1
        mem_spec = pl.BlockSpec((1, nm, d), lambda i: (0, 0, 0))
        scratch = []
    else:
        assert mk.shape[0] * lt == t
        mem_spec = pl.BlockSpec(memory_space=pl.ANY)
        scratch = [pltpu.VMEM((2, nb, nm, d), F32), pltpu.VMEM((2, nb, nm, d), F32),
                   pltpu.SemaphoreType.DMA((2,))]
    return pl.pallas_call(
        functools.partial(_xattn_kernel, nb=nb),
        grid=(t // rows,),
        in_specs=[pl.BlockSpec((rows, d), row_in), mem_spec, mem_spec,
                  pl.BlockSpec((rows, d), row_in),
                  _const_spec((d, d)), _const_spec((1, d)),
                  _const_spec((ne, d)), _const_spec((ne, d))],
        scratch_shapes=scratch,
        out_specs=[pl.BlockSpec((rows, d), row), pl.BlockSpec((rows, d), row),
                   pl.BlockSpec((ne, rows), lambda i: (0, i))],
        out_shape=[jax.ShapeDtypeStruct((t, d), F32), jax.ShapeDtypeStruct((t, d), F32),
                   jax.ShapeDtypeStruct((ne, t), F32)],
        compiler_params=_cparams(("arbitrary",)),
        name="mem_xattn",
    )(q, mk, mv, x1, w_o, g_ffn.reshape(1, d), wr_hi, wr_lo)


def _route_kernel(lgp_ref, lgs_ref, xnp_ref, xns_ref, bias_ref, eidx_ref, gate_ref, pos_ref, cnt_ref,
                  xn_ref, run_scr, *, n_first):
    ne, tn = lgp_ref.shape
    first = pl.program_id(0) < n_first
    xn_ref[...] = _pick(first, xnp_ref, xns_ref)

    @pl.when(pl.program_id(0) == 0)
    def _():
        run_scr[...] = jnp.zeros_like(run_scr)

    gsz = ne // N_GROUPS
    sc = jax.nn.sigmoid(_pick(first, lgp_ref, lgs_ref))
    sel = sc + bias_ref[...]
    s3 = sel.reshape(N_GROUPS, gsz, tn)
    ie = lax.broadcasted_iota(jnp.int32, s3.shape, 1).astype(F32)
    m1 = jnp.max(s3, axis=1, keepdims=True)
    first = jnp.min(jnp.where(s3 == m1, ie, float(gsz)), axis=1, keepdims=True)
    m2 = jnp.max(jnp.where(ie == first, -jnp.inf, s3), axis=1, keepdims=True)
    gs = m1 + m2
    ig = lax.broadcasted_iota(jnp.int32, gs.shape, 0)
    grank = jnp.zeros(gs.shape, jnp.int32)
    for g in range(N_GROUPS):
        other = gs[g:g + 1]
        ahead = jnp.where(other > gs, 1, jnp.where(other == gs, (ig > g).astype(jnp.int32), 0))
        grank = grank + ahead
    gkeep = jnp.broadcast_to(grank < TOPK_GROUPS, s3.shape)
    v = jnp.where(gkeep, s3, -jnp.inf).reshape(ne, tn)
    iv = lax.broadcasted_iota(jnp.int32, v.shape, 0)
    rank = jnp.zeros(v.shape, jnp.int32)
    for e in range(ne):
        other = v[e:e + 1]
        ahead = jnp.where(other > v, 1, jnp.where(other == v, (iv > e).astype(jnp.int32), 0))
        rank = rank + ahead
    chosen = rank < TOP_K
    gate = jnp.where(chosen, sc, 0.0)
    gate = gate / jnp.sum(gate, axis=0, keepdims=True) * ROUTE_SCALE
    ti = lax.broadcasted_iota(jnp.int32, (tn, tn), 0)
    tj = lax.broadcasted_iota(jnp.int32, (tn, tn), 1)
    incl = _dot(jnp.where(chosen, 1.0, 0.0).astype(BF16), (ti <= tj).astype(BF16))
    pos = run_scr[...] + incl - 1.0
    run_scr[...] = run_scr[...] + incl[:, tn - 1:tn]
    cnt_ref[...] = run_scr[...].astype(jnp.int32)
    ef = iv.astype(F32)
    for k in range(TOP_K):
        hit = rank == k
        pick = lambda a: jnp.sum(jnp.where(hit, a, 0.0), axis=0, keepdims=True)
        eidx_ref[k:k + 1, :] = pick(ef).astype(jnp.int32)
        gate_ref[k:k + 1, :] = pick(gate)
        pos_ref[k:k + 1, :] = pick(pos).astype(jnp.int32)


def _route(lg_p, lg_s, xn_p, xn_s, router_bias):
    ne = lg_p.shape[0]
    d = xn_p.shape[1]
    t = lg_p.shape[1] + lg_s.shape[1]
    tn = TOKEN_TILE
    n_first = lg_p.shape[1] // tn
    kt = lambda i: (0, i)
    return pl.pallas_call(
        functools.partial(_route_kernel, n_first=n_first),
        grid=(t // tn,),
        in_specs=(_pair_specs((ne, tn), 1, n_first) + _pair_specs((tn, d), 0, n_first)
                  + [_const_spec((ne, 1))]),
        out_specs=[pl.BlockSpec((TOP_K, tn), kt), pl.BlockSpec((TOP_K, tn), kt),
                   pl.BlockSpec((TOP_K, tn), kt), pl.BlockSpec((ne, 1), lambda i: (0, 0)),
                   pl.BlockSpec((tn, d), lambda i: (i, 0))],
        out_shape=[jax.ShapeDtypeStruct((TOP_K, t), jnp.int32),
                   jax.ShapeDtypeStruct((TOP_K, t), F32),
                   jax.ShapeDtypeStruct((TOP_K, t), jnp.int32),
                   jax.ShapeDtypeStruct((ne, 1), jnp.int32),
                   jax.ShapeDtypeStruct((t, d), F32)],
        scratch_shapes=[pltpu.VMEM((ne, 1), F32)],
        compiler_params=_cparams(("arbitrary",)),
        name="router",
    )(lg_p, lg_s, xn_p, xn_s, router_bias.astype(F32).reshape(ne, 1))


def _moe_plan(eidx, posk, counts, n_tiles_max):
    ne = counts.shape[0]
    ntile = (counts + EXPERT_TILE - 1) // EXPERT_TILE
    cum = jnp.cumsum(ntile)
    base = (cum - ntile) * EXPERT_TILE
    n_tiles = cum[-1:]
    experts = jnp.arange(ne, dtype=jnp.int32)
    last_used = jnp.max(jnp.where(ntile > 0, experts, 0))
    tiles = jnp.arange(n_tiles_max, dtype=jnp.int32)
    tile_expert = jnp.sum(tiles[:, None] >= cum[None, :], axis=1)
    tile_expert = jnp.minimum(tile_expert, last_used).astype(jnp.int32)
    slot = (jnp.sum(jnp.where(eidx[..., None] == experts, base, 0), axis=-1) + posk)
    slot = slot.astype(jnp.int32)
    n_k, t = slot.shape
    tok_bits = max(t - 1, 1).bit_length()
    assert (n_tiles_max * EXPERT_TILE) << tok_bits < 2 ** 31
    tok = lax.broadcasted_iota(jnp.int32, slot.shape, 1)
    keys = jnp.sort((slot * (1 << tok_bits) + tok).reshape(n_k * t))
    tok_sorted = keys & ((1 << tok_bits) - 1)
    tok_sorted = jnp.concatenate([tok_sorted, jnp.zeros((EXPERT_TILE,), jnp.int32)])
    first_pair = jnp.cumsum(counts) - counts
    first_tile = cum - ntile
    mine = tile_expert[:, None] == experts[None, :]
    lookup = lambda table: jnp.sum(jnp.where(mine, table[None, :], 0), axis=1)
    pair0 = lookup(first_pair) + (tiles - lookup(first_tile)) * EXPERT_TILE
    pair0 = jnp.clip(pair0, 0, n_k * t).astype(jnp.int32)
    return slot, tile_expert, n_tiles.astype(jnp.int32), tok_sorted, pair0


def _row_copy_wait(ref, n_rows, sem):
    pltpu.make_async_copy(ref.at[pl.ds(0, n_rows)], ref.at[pl.ds(0, n_rows)], sem).wait()


def _expert_kernel(te_ref, nt_ref, pair0_ref, tok_ref, xn_hbm, wg_ref, wu_ref, wd_ref, ys_ref,
                   x_a, x_b, wg_scr, wu_scr, wd_scr, sem_a, sem_b):
    i = pl.program_id(0)
    n_tiles = nt_ref[0]

    def issue_row(tile, buf, sem, r, queue):
        t = tok_ref[pair0_ref[tile] + r]
        pltpu.make_async_copy(xn_hbm.at[pl.ds(t, 1)], buf.at[pl.ds(r, 1)],
                              sem).start(priority=queue)

    @pl.when(i == 0)
    def _():
        def body(rb, _):
            for u in range(SUBLANES):
                issue_row(0, x_a, sem_a, rb * SUBLANES + u, u % 2)
            return 0
        lax.fori_loop(0, EXPERT_TILE // SUBLANES, body, 0)

    nxt_tile = jnp.where(i + 1 < n_tiles, i + 1, 0)

    def run(cur, sem_cur, nxt, sem_nxt):
        _row_copy_wait(xn_hbm, EXPERT_TILE, sem_cur)

        @pl.when((i == 0) | (te_ref[i] != te_ref[jnp.maximum(i - 1, 0)]))
        def _():
            wg_scr[...] = wg_ref[0].astype(BF16)
            wu_scr[...] = wu_ref[0].astype(BF16)
            wd_scr[...] = wd_ref[0].astype(BF16)

        for r in range(EXPERT_TILE):
            issue_row(nxt_tile, nxt, sem_nxt, r, r % 2)
        x = cur[...].astype(BF16)
        h = jax.nn.silu(_dot(x, wg_scr[...])) * _dot(x, wu_scr[...])
        ys_ref[...] = _dot(h.astype(BF16), wd_scr[...])

        @pl.when(i == n_tiles - 1)
        def _():
            _row_copy_wait(xn_hbm, EXPERT_TILE, sem_nxt)

    @pl.when((i < n_tiles) & (i % 2 == 0))
    def _():
        run(x_a, sem_a, x_b, sem_b)

    @pl.when((i < n_tiles) & (i % 2 == 1))
    def _():
        run(x_b, sem_b, x_a, sem_a)

    @pl.when(i >= n_tiles)
    def _():
        ys_ref[...] = jnp.zeros_like(ys_ref)


def _experts(xn, tile_expert, n_tiles, pair0, tok_sorted, w_gate, w_up, w_down):
    d = xn.shape[1]
    ne, _, de = w_gate.shape
    n_tiles_max = tile_expert.shape[0]
    out_tile = lambda i, *_: (i, 0)
    wsel = lambda i, te, *_: (te[i], 0, 0)
    return pl.pallas_call(
        _expert_kernel,
        grid_spec=pltpu.PrefetchScalarGridSpec(
            num_scalar_prefetch=4,
            grid=(n_tiles_max,),
            in_specs=[pl.BlockSpec(memory_space=pl.ANY),
                      pl.BlockSpec((1, d, de), wsel), pl.BlockSpec((1, d, de), wsel),
                      pl.BlockSpec((1, de, d), wsel)],
            out_specs=pl.BlockSpec((EXPERT_TILE, d), out_tile),
            scratch_shapes=[pltpu.VMEM((EXPERT_TILE, d), F32), pltpu.VMEM((EXPERT_TILE, d), F32),
                            pltpu.VMEM((d, de), BF16), pltpu.VMEM((d, de), BF16),
                            pltpu.VMEM((de, d), BF16),
                            pltpu.SemaphoreType.DMA, pltpu.SemaphoreType.DMA]),
        out_shape=jax.ShapeDtypeStruct((n_tiles_max * EXPERT_TILE, d), F32),
        compiler_params=_cparams(("arbitrary",)),
        name="moe_experts",
    )(tile_expert, n_tiles, pair0, tok_sorted, xn, w_gate, w_up, w_down)


def _combine_kernel(slot_ref, gate_ref, xn_ref, x2p_ref, x2s_ref, sg_ref, su_ref, sd_ref,
                    gf_ref, ys_hbm, yp_ref, ysamp_ref, buf_a, buf_b, base_scr, y_scr, sem_a, sem_b,
                    *, n_first):
    tm = xn_ref.shape[0]
    i = pl.program_id(0)
    n_steps = pl.num_programs(0)
    first = i < n_first
    rb_rows = SUBLANES

    def issue_rows(tile, buf, sem, r0, n):
        for rr in range(n):
            for k in range(TOP_K):
                s = slot_ref[(tile * tm + r0 + rr) * TOP_K + k]
                pltpu.make_async_copy(ys_hbm.at[pl.ds(s, 1)], buf.at[k, pl.ds(r0 + rr, 1)],
                                      sem).start(priority=k % 2)

    @pl.when(i == 0)
    def _():
        def body(r, _):
            issue_rows(0, buf_a, sem_a, r, 1)
            return 0
        lax.fori_loop(0, tm, body, 0)

    x = xn_ref[...].astype(BF16)
    hs = jax.nn.silu(_dot(x, sg_ref[...])) * _dot(x, su_ref[...])
    base_scr[...] = _pick(first, x2p_ref, x2s_ref) + _dot(hs.astype(BF16), sd_ref[...])
    nxt_tile = jnp.where(i + 1 < n_steps, i + 1, 0)

    def run(cur, sem_cur, nxt, sem_nxt):
        _row_copy_wait(ys_hbm, tm * TOP_K, sem_cur)

        for rb in range(tm // rb_rows):
            r0 = rb * rb_rows
            issue_rows(nxt_tile, nxt, sem_nxt, r0, rb_rows)
            acc = base_scr[pl.ds(r0, rb_rows), :]
            g = gate_ref[pl.ds(r0, rb_rows), :]
            for k in range(TOP_K):
                acc = acc + g[:, k:k + 1] * cur[k, pl.ds(r0, rb_rows), :]
            y_scr[pl.ds(r0, rb_rows), :] = _rms(acc, gf_ref[...])

        @pl.when(i == n_steps - 1)
        def _():
            _row_copy_wait(ys_hbm, tm * TOP_K, sem_nxt)

    @pl.when(i % 2 == 0)
    def _():
        run(buf_a, sem_a, buf_b, sem_b)

    @pl.when(i % 2 == 1)
    def _():
        run(buf_b, sem_b, buf_a, sem_a)

    @pl.when(first)
    def _():
        yp_ref[...] = y_scr[...]

    @pl.when(jnp.logical_not(first))
    def _():
        ysamp_ref[...] = y_scr[...]


def _combine(slot_flat, gates, xn, x2_p, x2_s, ws_gate, ws_up, ws_down, g_final, ys):
    d = x2_p.shape[1]
    tp, tsamp = x2_p.shape[0], x2_s.shape[0]
    dsh = ws_gate.shape[1]
    tm = COMBINE_TILE
    n_first = tp // tm
    row = lambda i, *_: (i, 0)
    pair = lambda: _pair_specs((tm, d), 0, n_first)
    return pl.pallas_call(
        functools.partial(_combine_kernel, n_first=n_first),
        grid_spec=pltpu.PrefetchScalarGridSpec(
            num_scalar_prefetch=1,
            grid=((tp + tsamp) // tm,),
            in_specs=[pl.BlockSpec((tm, TOP_K), row), pl.BlockSpec((tm, d), row)] + pair()
                     + [_const_spec((d, dsh)), _const_spec((d, dsh)), _const_spec((dsh, d)),
                        _const_spec((1, d)), pl.BlockSpec(memory_space=pl.ANY)],
            out_specs=pair(),
            scratch_shapes=[pltpu.VMEM((TOP_K, tm, d), F32), pltpu.VMEM((TOP_K, tm, d), F32),
                            pltpu.VMEM((tm, d), F32), pltpu.VMEM((tm, d), F32),
                            pltpu.SemaphoreType.DMA, pltpu.SemaphoreType.DMA]),
        out_shape=[jax.ShapeDtypeStruct((tp, d), F32), jax.ShapeDtypeStruct((tsamp, d), F32)],
        compiler_params=_cparams(("arbitrary",)),
        name="moe_combine",
    )(slot_flat, gates, xn, x2_p, x2_s, ws_gate.astype(BF16), ws_up.astype(BF16),
      ws_down.astype(BF16), g_final.reshape(1, d), ys)


def _moe(xn, x2_p, x2_s, eidx, gate, posk, counts, w_gate, w_up, w_down, ws_gate, ws_up,
         ws_down, g_final):
    t = xn.shape[0]
    ne = w_gate.shape[0]
    n_tiles_max = t * TOP_K // EXPERT_TILE + ne
    slot, tile_expert, n_tiles, tok_sorted, pair0 = _moe_plan(eidx, posk, counts[:, 0],
                                                              n_tiles_max)
    ys = _experts(xn, tile_expert, n_tiles, pair0, tok_sorted, w_gate, w_up, w_down)
    return _combine(jnp.transpose(slot).reshape(t * TOP_K), jnp.transpose(gate), xn, x2_p, x2_s,
                    ws_gate, ws_up, ws_down, g_final, ys)


def kernel(x_prompt, x_sample, cache_mem_k, cache_mem_v, state_s5_re, state_s5_im, state_gla, mem_prompt, w_in, w_a2, b_a2, lam_re, lam_im, log_dt, b_re, b_im, c_re, c_im, d_skip, w_glu, b_glu, g_s5_out, g_gla_head, w_out, g_mix, g_xattn, g_mem, w_q, w_k, w_v, w_o, g_ffn, w_router, router_bias, w_gate, w_up, w_down, ws_gate, ws_up, ws_down, g_final):
    depth = w_in.shape[0]
    assert depth == 1, "one trunk layer"
    bp, lp, d = x_prompt.shape
    bs, ls, _ = x_sample.shape
    assert bp == 1 and ls == CHUNK and lp % CHUNK == 0
    n_mem = mem_prompt.shape[1]
    g, p = lam_re.shape[1:]
    d_s5 = g * S5_GROUP
    d_qk = GLA_HEADS * GLA_DK
    d_gla = w_out.shape[1] - d_s5
    ns = d_s5 // LANES
    ne = w_router.shape[2]
    tp, tsamp = bp * lp, bs * ls
    xp = x_prompt.reshape(tp, d)
    xs = x_sample.reshape(tsamp, d)
    sq = lambda a: a.reshape(a.shape[1:])

    mk_p, mv_p = _memory_kv(mem_prompt.reshape(n_mem, d), sq(g_mem), sq(w_k), sq(w_v))

    u, q, k, v, a, r = _in_proj(xp, xs, sq(g_mix), sq(w_in), d_s5, d_qk, d_gla)

    ksm, psm, qsm, step_m, carry_m, a8 = _s5_coeffs(
        sq(lam_re), sq(lam_im), sq(log_dt), sq(b_re), sq(b_im), sq(c_re), sq(c_im), sq(d_skip))
    tables = tuple(_s5_expand(ksm, psm, qsm)) + (step_m, carry_m, a8)
    zero_h = jnp.zeros((ns, bp, 2 * GROUPS_PER_SLAB * p), F32)
    rows_p = lp // S5_BLOCK
    y_p, h_p = _s5_mixer(u, tables, zero_h, 0, tp, rows_per_seq=rows_p,
                         tile_rows=math.gcd(rows_p, 512))
    h0_s = _pack_s5_state(sq(state_s5_re), sq(state_s5_im), ns)
    y_s, h_s = _s5_mixer(u, tables, h0_s, tp, tsamp, rows_per_seq=ls // S5_BLOCK,
                         tile_rows=tsamp // S5_BLOCK)

    wa2 = jnp.pad(sq(w_a2), ((0, LANES - GLA_RANK), (0, 0)))
    zero_s = jnp.zeros((bp,) + state_gla.shape[2:], F32)
    gla_p, sg_p = _gla_mixer(q, k, v, a, r, wa2, sq(b_a2), sq(g_gla_head), zero_s, 0, tp,
                             carry=True, n_chunks=4)
    gla_s, sg_s = _gla_mixer(q, k, v, a, r, wa2, sq(b_a2), sq(g_gla_head), sq(state_gla), tp,
                             tsamp, carry=False, n_chunks=4)

    x1, xq = _mix_out(y_p, y_s, gla_p, gla_s, xp, xs, sq(w_glu), sq(b_glu), sq(g_s5_out),
                      sq(w_out), sq(g_xattn), sq(w_q))

    wr_hi, wr_lo = _split_bf16(jnp.transpose(sq(w_router)))
    wo = sq(w_o).astype(BF16)
    x2_p, xn_p, lg_p = _xattn(xq, x1, mk_p[None], mv_p[None], wo, sq(g_ffn), wr_hi, wr_lo,
                              0, tp, nb=1, lt=TOKEN_TILE)
    x2_s, xn_s, lg_s = _xattn(xq, x1, sq(cache_mem_k), sq(cache_mem_v), wo, sq(g_ffn), wr_hi,
                              wr_lo, tp, tsamp, nb=2, lt=ls)

    eidx, gate, posk, counts, xn = _route(lg_p, lg_s, xn_p, xn_s, sq(router_bias))
    y_p2, y_s2 = _moe(xn, x2_p, x2_s, eidx, gate, posk, counts, sq(w_gate), sq(w_up),
                      sq(w_down), sq(ws_gate), sq(ws_up), sq(ws_down), g_final)

    xh = d // X_HEADS
    re_p, im_p = _unpack_s5_state(h_p, g, p)
    re_s, im_s = _unpack_s5_state(h_s, g, p)
    return (y_p2.reshape(bp, lp, d), y_s2.reshape(bs, ls, d),
            mk_p.reshape(1, bp, n_mem, X_HEADS, xh), mv_p.reshape(1, bp, n_mem, X_HEADS, xh),
            re_p[None], im_p[None], sg_p[None], re_s[None], im_s[None], sg_s[None])
```

```python
import functools
import math

import jax
import jax.numpy as jnp
from jax import lax
from jax.experimental import pallas as pl
from jax.experimental.pallas import tpu as pltpu

F32 = jnp.float32
BF16 = jnp.bfloat16

EPS = 1e-6
CHUNK = 64
S5_GROUP = 16
GLA_HEADS = 4
GLA_DK = 128
GLA_RANK = 16
GLA_TAU = 16.0
X_HEADS = 4
N_GROUPS = 8
TOPK_GROUPS = 4
TOP_K = 8
ROUTE_SCALE = 2.5

LANES = 128
SUBLANES = 8
S5_BLOCK = SUBLANES
GROUPS_PER_SLAB = LANES // S5_GROUP
TOKEN_TILE = 512
EXPERT_TILE = 256
COMBINE_TILE = 128
VMEM_LIMIT = 56 * 1024 * 1024


def _cparams(sem):
    return pltpu.CompilerParams(dimension_semantics=sem, vmem_limit_bytes=VMEM_LIMIT)


def _const_spec(shape):
    nd = len(shape)
    return pl.BlockSpec(shape, lambda *_: (0,) * nd, pipeline_mode=pl.Buffered(1))


def _pair_specs(block, axis, n_first):
    def at(f):
        return lambda i, *_: tuple(f(i) if a == axis else 0 for a in range(len(block)))
    return [pl.BlockSpec(block, at(lambda i: jnp.minimum(i, n_first - 1))),
            pl.BlockSpec(block, at(lambda i: jnp.maximum(i - n_first, 0)))]


def _pick(first, a_ref, b_ref):
    return jnp.where(first, a_ref[...], b_ref[...])


def _rms(x, g):
    return x * lax.rsqrt(jnp.mean(x * x, axis=-1, keepdims=True) + EPS) * g


def _split_bf16(x):
    hi = x.astype(BF16)
    lo = (x - hi.astype(F32)).astype(BF16)
    return hi, lo


def _split3_bf16(x):
    a = x.astype(BF16)
    r = x - a.astype(F32)
    b = r.astype(BF16)
    c = (r - b.astype(F32)).astype(BF16)
    return a, b, c


def _dot(a, b):
    return jnp.dot(a, b, preferred_element_type=F32)


def _dot_nt(a, b):
    return lax.dot_general(a, b, (((1,), (1,)), ((), ())), preferred_element_type=F32)


def _dot_tn(a, b):
    return lax.dot_general(a, b, (((0,), (0,)), ((), ())), preferred_element_type=F32)


def _memkv_kernel(mem_ref, g_ref, wk_ref, wv_ref, mk_ref, mv_ref):
    m = _rms(mem_ref[...], g_ref[...]).astype(BF16)
    mk_ref[...] = _dot(m, wk_ref[...].astype(BF16))
    mv_ref[...] = _dot(m, wv_ref[...].astype(BF16))


def _memory_kv(mem, g_mem, wk, wv):
    n, d = mem.shape
    tn = 512
    return pl.pallas_call(
        _memkv_kernel,
        grid=(d // tn,),
        in_specs=[_const_spec((n, d)), _const_spec((1, d)),
                  pl.BlockSpec((d, tn), lambda j: (0, j)),
                  pl.BlockSpec((d, tn), lambda j: (0, j))],
        out_specs=[pl.BlockSpec((n, tn), lambda j: (0, j))] * 2,
        out_shape=[jax.ShapeDtypeStruct((n, d), F32)] * 2,
        compiler_params=_cparams(("arbitrary",)),
        name="memory_kv",
    )(mem, g_mem.reshape(1, d), wk, wv)


def _inproj_kernel(xp_ref, xs_ref, g_ref, wuh_ref, wul_ref, wqkv_ref, wa_ref, wr_ref,
                   u_ref, q_ref, k_ref, v_ref, a_ref, r_ref, *, n_first):
    first = pl.program_id(0) < n_first
    x = _pick(first, xp_ref, xs_ref)
    nx = _rms(x, g_ref[...])
    hi, lo = _split_bf16(nx)
    u = _dot_nt(hi, wuh_ref[...])
    for j in range(u_ref.shape[0]):
        u_ref[j] = u[:, j * LANES:(j + 1) * LANES]

    @pl.when(jnp.logical_not(first))
    def _():
        fix = _dot_nt(lo, wuh_ref[...]) + _dot_nt(hi, wul_ref[...])
        for j in range(u_ref.shape[0]):
            u_ref[j] += fix[:, j * LANES:(j + 1) * LANES]

    qkv = _dot_nt(hi, wqkv_ref[...])
    dq = q_ref.shape[1]
    q_ref[...] = qkv[:, :dq].astype(BF16)
    k_ref[...] = qkv[:, dq:2 * dq].astype(BF16)
    v_ref[...] = qkv[:, 2 * dq:].astype(BF16)
    a_ref[...] = _dot_nt(hi, wa_ref[...]) + _dot_nt(lo, wa_ref[...])
    r_ref[...] = _dot_nt(hi, wr_ref[...]).astype(BF16)


def _in_proj(xp, xs, g_mix, w_in, d_s5, d_qk, d_gla):
    d = xp.shape[1]
    t = xp.shape[0] + xs.shape[0]
    tm = TOKEN_TILE
    n_first = xp.shape[0] // tm
    o1 = d_s5
    o2 = o1 + 2 * d_qk + d_gla
    o3 = o2 + GLA_RANK
    wt = jnp.transpose(w_in)
    wu_hi, wu_lo = _split_bf16(wt[:o1])
    wqkv = wt[o1:o2].astype(BF16)
    wa = jnp.pad(wt[o2:o3], ((0, LANES - GLA_RANK), (0, 0))).astype(BF16)
    wr = wt[o3:].astype(BF16)
    n_slab = d_s5 // LANES
    row = lambda i: (i, 0)
    return pl.pallas_call(
        functools.partial(_inproj_kernel, n_first=n_first),
        grid=(t // tm,),
        in_specs=_pair_specs((tm, d), 0, n_first) + [
            _const_spec((1, d)), _const_spec(wu_hi.shape), _const_spec(wu_lo.shape),
            _const_spec(wqkv.shape), _const_spec(wa.shape), _const_spec(wr.shape)],
        out_specs=[pl.BlockSpec((n_slab, tm, LANES), lambda i: (0, i, 0)),
                   pl.BlockSpec((tm, d_qk), row), pl.BlockSpec((tm, d_qk), row),
                   pl.BlockSpec((tm, d_gla), row), pl.BlockSpec((tm, LANES), row),
                   pl.BlockSpec((tm, d_gla), row)],
        out_shape=[jax.ShapeDtypeStruct((n_slab, t, LANES), F32),
                   jax.ShapeDtypeStruct((t, d_qk), BF16), jax.ShapeDtypeStruct((t, d_qk), BF16),
                   jax.ShapeDtypeStruct((t, d_gla), BF16), jax.ShapeDtypeStruct((t, LANES), F32),
                   jax.ShapeDtypeStruct((t, d_gla), BF16)],
        compiler_params=_cparams(("arbitrary",)),
        name="in_proj",
    )(xp, xs, g_mix.reshape(1, d), wu_hi, wu_lo, wqkv, wa, wr)


def _s5_coeffs(lam_re, lam_im, log_dt, b_re, b_im, c_re, c_im, d_skip):
    g, p = lam_re.shape
    h = b_re.shape[-1]
    nb = S5_BLOCK
    ns = g // GROUPS_PER_SLAB
    gl = GROUPS_PER_SLAB
    dt = jnp.exp(log_dt.astype(F32))[:, None]
    lr, li = lam_re.astype(F32), lam_im.astype(F32)
    mag = jnp.exp(lr * dt)
    ar, ai = mag * jnp.cos(li * dt), mag * jnp.sin(li * dt)
    den = lr * lr + li * li
    nr = ar - 1.0
    cf_r = (nr * lr + ai * li) / den
    cf_i = (ai * lr - nr * li) / den
    bb_r = cf_r[..., None] * b_re - cf_i[..., None] * b_im
    bb_i = cf_r[..., None] * b_im + cf_i[..., None] * b_re

    def cpow(n):
        e = n[:, None, None] * dt[None]
        m = jnp.exp(lr[None] * e)
        return m * jnp.cos(li[None] * e), m * jnp.sin(li[None] * e)

    pr, pi = cpow(jnp.arange(nb + 1, dtype=F32))

    bt_r, bt_i = jnp.swapaxes(bb_r, 1, 2), jnp.swapaxes(bb_i, 1, 2)
    ct_r, ct_i = jnp.swapaxes(c_re, 1, 2), jnp.swapaxes(c_im, 1, 2)

    cb_r = jnp.einsum('gcp,gph->gphc', c_re, bb_r) - jnp.einsum('gcp,gph->gphc', c_im, bb_i)
    cb_i = jnp.einsum('gcp,gph->gphc', c_re, bb_i) + jnp.einsum('gcp,gph->gphc', c_im, bb_r)
    taps = (jnp.einsum('ngp,gphc->nghc', pr[:nb], cb_r)
            - jnp.einsum('ngp,gphc->nghc', pi[:nb], cb_i))
    skip = d_skip[None, :, :, None] * jnp.eye(h, dtype=F32)[None, None]
    taps = taps + jnp.where(jnp.arange(nb)[:, None, None, None] == 0, skip, 0.0)
    ksm = jnp.swapaxes(taps.reshape(nb, ns, gl * h, h), 0, 1)

    wr, wi = pr[nb - 1::-1][:nb], pi[nb - 1::-1][:nb]
    wr, wi = wr[:, :, None, :], wi[:, :, None, :]
    inj = jnp.concatenate([wr * bt_r[None] - wi * bt_i[None],
                           wr * bt_i[None] + wi * bt_r[None]], axis=-1)
    psm = jnp.swapaxes(inj.reshape(nb, ns, gl * h, 2 * p), 0, 1)

    er = jnp.transpose(pr[1:nb + 1], (1, 2, 0))[..., None]
    ei = jnp.transpose(pi[1:nb + 1], (1, 2, 0))[..., None]
    q_r = ct_r[:, :, None, :] * er - ct_i[:, :, None, :] * ei
    q_i = -(ct_r[:, :, None, :] * ei + ct_i[:, :, None, :] * er)
    qq = jnp.stack([q_r.reshape(ns, gl * p, nb * h), q_i.reshape(ns, gl * p, nb * h)], axis=1)
    qsm = qq.reshape(ns, 2 * gl * p, nb * h)

    def slab(xr, xi):
        k = xr.shape[0]
        xr = jnp.transpose(xr.reshape(k, ns, gl * p), (1, 0, 2))
        xi = jnp.transpose(xi.reshape(k, ns, gl * p), (1, 0, 2))
        return jnp.concatenate([xr, xi], axis=-1)

    rows = jnp.arange(SUBLANES, dtype=F32)
    cr, ci = cpow(nb * rows)
    carry_m = slab(cr, ci)
    sr, si = cpow(nb * jnp.array([1.0, 2.0, 4.0], F32))
    keep = (rows[None, :] >= jnp.array([1.0, 2.0, 4.0], F32)[:, None]).astype(F32)
    step_m = slab((sr[:, None] * keep[:, :, None, None]).reshape(3 * SUBLANES, g, p),
                  (si[:, None] * keep[:, :, None, None]).reshape(3 * SUBLANES, g, p))
    step_m = step_m.reshape(ns, 3, SUBLANES, 2 * gl * p)
    a8 = carry_m[:, 1:2]
    return ksm, psm, qsm, step_m, carry_m, a8


def _s5_expand_kernel(ksm_ref, psm_ref, qsm_ref, tm_ref, ph_ref, pl_ref, qm_ref):
    nb = S5_BLOCK
    h = S5_GROUP
    w = ph_ref.shape[2]
    p = w // (2 * GROUPS_PER_SLAB)

    def spread(x, sel, keep):
        a, b, c = _split3_bf16(x)
        return jnp.where(keep, _dot(a, sel) + _dot(b, sel) + _dot(c, sel), 0.0)

    def iota(shape, axis):
        return lax.broadcasted_iota(jnp.int32, shape, axis)

    sel_c = (iota((h, LANES), 1) % h == iota((h, LANES), 0)).astype(BF16)
    keep_t = iota((LANES, LANES), 0) // h == iota((LANES, LANES), 1) // h
    tm_ref[...] = jnp.zeros_like(tm_ref)
    for tau in range(nb):
        blk = spread(ksm_ref[0, tau], sel_c, keep_t).astype(BF16)
        for s in range(nb - tau):
            t = s + tau
            tm_ref[0, s * LANES:(s + 1) * LANES, t * LANES:(t + 1) * LANES] = blk

    half = w // 2
    keep_p = iota((LANES, half), 0) // h == iota((LANES, half), 1) // p
    for ri in range(2):
        sel_p = (iota((2 * p, half), 0) == iota((2 * p, half), 1) % p + ri * p).astype(BF16)
        for s in range(nb):
            blk = spread(psm_ref[0, s], sel_p, keep_p)
            hi, lo = _split_bf16(blk)
            ph_ref[0, s * LANES:(s + 1) * LANES, ri * half:(ri + 1) * half] = hi
            pl_ref[0, s * LANES:(s + 1) * LANES, ri * half:(ri + 1) * half] = lo

    keep_q = (iota((w, LANES), 0) % half) // p == iota((w, LANES), 1) // h
    for t in range(nb):
        sel_q = (iota((nb * h, LANES), 0) == iota((nb * h, LANES), 1) % h + t * h).astype(BF16)
        qm_ref[0, :, t * LANES:(t + 1) * LANES] = spread(qsm_ref[0], sel_q, keep_q).astype(BF16)


def _s5_expand(ksm, psm, qsm):
    ns = ksm.shape[0]
    w = qsm.shape[1]
    k = S5_BLOCK * LANES
    blk = lambda shape: pl.BlockSpec((1,) + shape, lambda j: (j,) + (0,) * len(shape))
    return pl.pallas_call(
        _s5_expand_kernel,
        grid=(ns,),
        in_specs=[blk(ksm.shape[1:]), blk(psm.shape[1:]), blk(qsm.shape[1:])],
        out_specs=[blk((k, k)), blk((k, w)), blk((k, w)), blk((w, k))],
        out_shape=[jax.ShapeDtypeStruct((ns, k, k), BF16), jax.ShapeDtypeStruct((ns, k, w), BF16),
                   jax.ShapeDtypeStruct((ns, k, w), BF16), jax.ShapeDtypeStruct((ns, w, k), BF16)],
        compiler_params=_cparams(("arbitrary",)),
        name="s5_tables",
    )(ksm, psm, qsm)


def _cmul(xr, xi, mr, mi):
    return xr * mr - xi * mi, xr * mi + xi * mr


def _s5_kernel(u_ref, ph_ref, pl_ref, tm_ref, qm_ref, step_ref, carry_ref, a8_ref, h0_ref,
               y_ref, hout_ref, s_scr, hin_scr, c_scr, *, rows_per_seq):
    ti = pl.program_id(1)
    rows = s_scr.shape[0]
    half = s_scr.shape[1] // 2
    nb = S5_BLOCK
    u8 = jnp.concatenate([u_ref[0, pl.ds(s, rows, stride=nb), :] for s in range(nb)], axis=1)
    hi, lo = _split_bf16(u8)
    per_block_seq = rows_per_seq == SUBLANES
    if per_block_seq:
        s_scr[...] = _dot(hi, ph_ref[0]) + _dot(lo, ph_ref[0]) + _dot(hi, pl_ref[0])
    else:
        s_scr[...] = _dot(hi, ph_ref[0])

    if not per_block_seq:
        @pl.when(ti == 0)
        def _():
            c_scr[...] = h0_ref[0]

    not_first = (lax.broadcasted_iota(jnp.int32, (SUBLANES, 1), 0) >= 1).astype(F32)
    a8r, a8i = a8_ref[0, :, :half], a8_ref[0, :, half:]
    cmr, cmi = carry_ref[0, :, :half], carry_ref[0, :, half:]

    def body(rb, _):
        r0 = pl.multiple_of(rb * SUBLANES, SUBLANES)
        sb = s_scr[pl.ds(r0, SUBLANES), :]
        x = pltpu.roll(sb, 1, 0) * not_first
        xr, xi = x[:, :half], x[:, half:]
        for k, sh in enumerate((1, 2, 4)):
            m = step_ref[0, k]
            pr, pi = _cmul(pltpu.roll(xr, sh, 0), pltpu.roll(xi, sh, 0), m[:, :half], m[:, half:])
            xr, xi = xr + pr, xi + pi
        c = h0_ref[0, pl.ds(rb, 1), :] if per_block_seq else c_scr[...]
        cr, ci = c[:, :half], c[:, half:]
        pr, pi = _cmul(cr, ci, cmr, cmi)
        hr, hi_ = xr + pr, xi + pi
        hin_scr[pl.ds(r0, SUBLANES), :half] = hr
        hin_scr[pl.ds(r0, SUBLANES), half:] = hi_
        nr, ni = _cmul(hr[SUBLANES - 1:], hi_[SUBLANES - 1:], a8r, a8i)
        cn = jnp.concatenate([nr, ni], axis=1) + sb[SUBLANES - 1:]
        if per_block_seq:
            hout_ref[0, pl.ds(rb, 1), :] = cn
        else:
            c_scr[...] = cn
        return 0

    lax.fori_loop(0, rows // SUBLANES, body, 0)
    if not per_block_seq:
        hout_ref[0] = c_scr[...]

    y8 = _dot(hi, tm_ref[0]) + _dot(hin_scr[...].astype(BF16), qm_ref[0])
    for t in range(nb):
        y_ref[0, pl.ds(t, rows, stride=nb), :] = y8[:, t * LANES:(t + 1) * LANES]


def _s5_mixer(u_slabs, tables, h0, t_start, t, rows_per_seq, tile_rows):
    tm, pm_hi, pm_lo, qm, step_m, carry_m, a8 = tables
    ns = u_slabs.shape[0]
    rows = t // S5_BLOCK
    n_tiles = rows // tile_rows
    assert rows % tile_rows == 0 and t_start % (tile_rows * S5_BLOCK) == 0
    off = t_start // (tile_rows * S5_BLOCK)
    n_seq = h0.shape[1]
    w = pm_hi.shape[-1]
    slab3 = lambda j, i: (j, 0, 0)
    kern = functools.partial(_s5_kernel, rows_per_seq=rows_per_seq)
    return pl.pallas_call(
        kern,
        grid=(ns, n_tiles),
        in_specs=[pl.BlockSpec((1, tile_rows * S5_BLOCK, LANES), lambda j, i: (j, i + off, 0)),
                  pl.BlockSpec((1,) + pm_hi.shape[1:], slab3),
                  pl.BlockSpec((1,) + pm_lo.shape[1:], slab3),
                  pl.BlockSpec((1,) + tm.shape[1:], slab3),
                  pl.BlockSpec((1,) + qm.shape[1:], slab3),
                  pl.BlockSpec((1,) + step_m.shape[1:], lambda j, i: (j, 0, 0, 0)),
                  pl.BlockSpec((1,) + carry_m.shape[1:], slab3),
                  pl.BlockSpec((1,) + a8.shape[1:], slab3),
                  pl.BlockSpec((1, n_seq, w), slab3)],
        out_specs=[pl.BlockSpec((1, tile_rows * S5_BLOCK, LANES), lambda j, i: (j, i, 0)),
                   pl.BlockSpec((1, n_seq, w), slab3)],
        out_shape=[jax.ShapeDtypeStruct((ns, t, LANES), F32),
                   jax.ShapeDtypeStruct((ns, n_seq, w), F32)],
        scratch_shapes=[pltpu.VMEM((tile_rows, w), F32), pltpu.VMEM((tile_rows, w), F32),
                        pltpu.VMEM((1, w), F32)],
        compiler_params=_cparams(("arbitrary", "arbitrary")),
        name="s5_mixer",
    )(u_slabs, pm_hi, pm_lo, tm, qm, step_m, carry_m, a8, h0)


def _pack_s5_state(re, im, ns):
    b = re.shape[0]
    r = re.astype(F32).reshape(b, ns, -1)
    i = im.astype(F32).reshape(b, ns, -1)
    return jnp.transpose(jnp.concatenate([r, i], axis=-1), (1, 0, 2))


def _unpack_s5_state(hc, g, p):
    ns, b, w = hc.shape
    hc = jnp.transpose(hc, (1, 0, 2))
    re = hc[:, :, :w // 2].reshape(b, g, p)
    im = hc[:, :, w // 2:].reshape(b, g, p)
    return re, im


def _gla_kernel(q_ref, k_ref, v_ref, a_ref, r_ref, wa2_ref, ba2_ref, gh_ref, s0_ref,
                o_ref, sout_ref, st_scr, *, carry, n_chunks):
    step = pl.program_id(0)
    dk = GLA_DK
    dv = v_ref.shape[1] // GLA_HEADS
    c = CHUNK
    scale = dk ** -0.5
    ri = lax.broadcasted_iota(jnp.int32, (c, c), 0)
    ci = lax.broadcasted_iota(jnp.int32, (c, c), 1)
    causal = ri >= ci
    tril = causal.astype(BF16)
    eye_dk = (lax.broadcasted_iota(jnp.int32, (dk, dk), 0)
              == lax.broadcasted_iota(jnp.int32, (dk, dk), 1))

    if carry:
        @pl.when(step == 0)
        def _():
            st_scr[...] = s0_ref[0]

    wa_hi, wa_lo = _split_bf16(wa2_ref[...])
    states = [st_scr[h] for h in range(GLA_HEADS)] if carry else None
    for n in range(n_chunks):
        rows = slice(n * c, (n + 1) * c)
        a_hi, a_lo = _split_bf16(a_ref[rows, :])
        logit = _dot(a_hi, wa_hi) + _dot(a_lo, wa_hi) + _dot(a_hi, wa_lo) + ba2_ref[...]
        g = jax.nn.log_sigmoid(logit) * (1.0 / GLA_TAU)
        g1, g2, g3 = _split3_bf16(g)
        bcum = _dot(tril, g1) + _dot(tril, g2) + _dot(tril, g3)
        for h in range(GLA_HEADS):
            ks = slice(h * dk, (h + 1) * dk)
            vs = slice(h * dv, (h + 1) * dv)
            b = bcum[:, ks]
            qh = q_ref[rows, ks].astype(F32) * scale
            kh = k_ref[rows, ks].astype(F32)
            vh = v_ref[rows, vs]
            state = states[h] if carry else s0_ref[n, h]
            qe = (qh * jnp.exp(b)).astype(BF16)
            ke = (kh * jnp.exp(-b)).astype(BF16)
            att = jnp.where(causal, _dot_nt(qe, ke), 0.0)
            o = _dot(att.astype(BF16), vh) + _dot(qe, state.astype(BF16))
            blast = b[c - 1:c, :]
            kd = (kh * jnp.exp(blast - b)).astype(BF16)
            decay = jnp.sum(jnp.where(eye_dk, jnp.exp(blast), 0.0), axis=1, keepdims=True)
            new_state = decay * state + _dot_tn(kd, vh)
            if carry:
                states[h] = new_state
            else:
                sout_ref[n, h] = new_state
            on = _rms(o, gh_ref[...])
            rr = r_ref[rows, vs].astype(F32)
            o_ref[rows, vs] = (on * (rr * jax.nn.sigmoid(rr))).astype(BF16)
    if carry:
        for h in range(GLA_HEADS):
            st_scr[h] = states[h]
        sout_ref[0] = st_scr[...]


def _gla_mixer(q, k, v, a, r, wa2, ba2, g_head, s0, t_start, t, carry, n_chunks):
    dqk = q.shape[1]
    dvt = v.shape[1]
    hh, dk, dv = s0.shape[1:]
    rows = n_chunks * CHUNK
    assert t % rows == 0 and t_start % rows == 0
    off = t_start // rows
    row = lambda i: (i, 0)
    row_in = lambda i: (i + off, 0)
    if carry:
        sblk, smap = (1, hh, dk, dv), (lambda i: (0, 0, 0, 0))
    else:
        sblk, smap = (n_chunks, hh, dk, dv), (lambda i: (i, 0, 0, 0))
    kern = functools.partial(_gla_kernel, carry=carry, n_chunks=n_chunks)
    return pl.pallas_call(
        kern,
        grid=(t // rows,),
        in_specs=[pl.BlockSpec((rows, dqk), row_in), pl.BlockSpec((rows, dqk), row_in),
                  pl.BlockSpec((rows, dvt), row_in), pl.BlockSpec((rows, LANES), row_in),
                  pl.BlockSpec((rows, dvt), row_in),
                  _const_spec(wa2.shape), _const_spec((1, dqk)), _const_spec((1, dv)),
                  pl.BlockSpec(sblk, smap)],
        out_specs=[pl.BlockSpec((rows, dvt), row), pl.BlockSpec(sblk, smap)],
        out_shape=[jax.ShapeDtypeStruct((t, dvt), BF16), jax.ShapeDtypeStruct(s0.shape, F32)],
        scratch_shapes=[pltpu.VMEM((hh, dk, dv), F32)],
        compiler_params=_cparams(("arbitrary",)),
        name="gla_mixer",
    )(q, k, v, a, r, wa2, ba2.reshape(1, dqk), g_head.reshape(1, dv), s0)


def _mixout_kernel(yp_ref, ys_ref, gp_ref, gs_ref, xp_ref, xs_ref, wglu_ref, bglu_ref, gs5_ref,
                   wout_ref, gx_ref, wq_ref, x1_ref, q_ref, *, n_first):
    first = pl.program_id(0) < n_first
    yb = _pick(first, yp_ref, ys_ref)
    y = jnp.concatenate([yb[j] for j in range(yb.shape[0])], axis=1)
    z = jax.nn.gelu(y)
    gate = jax.nn.sigmoid(_dot(z.astype(BF16), wglu_ref[...]) + bglu_ref[...])
    s5 = _rms(z * gate, gs5_ref[...])
    cat = jnp.concatenate([s5.astype(BF16), _pick(first, gp_ref, gs_ref)], axis=1)
    x1 = _pick(first, xp_ref, xs_ref) + _dot(cat, wout_ref[...])
    x1_ref[...] = x1
    q_ref[...] = _dot(_rms(x1, gx_ref[...]).astype(BF16), wq_ref[...]).astype(BF16)


def _mix_out(y_p, y_s, gla_p, gla_s, xp, xs, w_glu, b_glu, g_s5, w_out, g_x, w_q):
    d = xp.shape[1]
    t = xp.shape[0] + xs.shape[0]
    ns = y_p.shape[0]
    ds5 = ns * LANES
    dg = gla_p.shape[1]
    tm = TOKEN_TILE // 2
    n_first = xp.shape[0] // tm
    row = lambda i: (i, 0)
    return pl.pallas_call(
        functools.partial(_mixout_kernel, n_first=n_first),
        grid=(t // tm,),
        in_specs=(_pair_specs((ns, tm, LANES), 1, n_first) + _pair_specs((tm, dg), 0, n_first)
                  + _pair_specs((tm, d), 0, n_first)
                  + [_const_spec((ds5, ds5)), _const_spec((1, ds5)), _const_spec((1, ds5)),
                     _const_spec((ds5 + dg, d)), _const_spec((1, d)), _const_spec((d, d))]),
        out_specs=[pl.BlockSpec((tm, d), row), pl.BlockSpec((tm, d), row)],
        out_shape=[jax.ShapeDtypeStruct((t, d), F32), jax.ShapeDtypeStruct((t, d), BF16)],
        compiler_params=_cparams(("arbitrary",)),
        name="mix_out",
    )(y_p, y_s, gla_p, gla_s, xp, xs, w_glu.astype(BF16), b_glu.reshape(1, ds5),
      g_s5.reshape(1, ds5), w_out.astype(BF16), g_x.reshape(1, d), w_q.astype(BF16))


def _xattn_kernel(q_ref, mk_ref, mv_ref, x1_ref, wo_ref, gf_ref, wrh_ref, wrl_ref,
                  x2_ref, xn_ref, lg_ref, *cache_scratch, nb):
    d = q_ref.shape[1]
    lt = q_ref.shape[0] // nb
    hd = d // X_HEADS
    scale = hd ** -0.5
    if cache_scratch:
        kbuf, vbuf, sem = cache_scratch
        i = pl.program_id(0)

        def fetch(step, slot):
            for b in range(nb):
                for h in range(X_HEADS):
                    for src, dst in ((mk_ref, kbuf), (mv_ref, vbuf)):
                        pltpu.make_async_copy(src.at[step * nb + b, :, h, :],
                                              dst.at[slot, b, :, pl.ds(h * hd, hd)],
                                              sem.at[slot]).start()

        @pl.when(i == 0)
        def _():
            fetch(0, 0)

        @pl.when(i + 1 < pl.num_programs(0))
        def _():
            fetch(i + 1, (i + 1) % 2)

        slot = i % 2
        for buf in (kbuf, vbuf):
            pltpu.make_async_copy(buf.at[slot], buf.at[slot], sem.at[slot]).wait()
        memory = lambda b: (kbuf[slot, b], vbuf[slot, b])
    else:
        memory = lambda b: (mk_ref[b], mv_ref[b])
    for b in range(nb):
        rows = slice(b * lt, (b + 1) * lt)
        mk, mv = (m.astype(BF16) for m in memory(b))
        outs = []
        for h in range(X_HEADS):
            hs = slice(h * hd, (h + 1) * hd)
            s = _dot_nt(q_ref[rows, hs], mk[:, hs]) * scale
            p = jnp.exp(s - jnp.max(s, axis=-1, keepdims=True))
            denom = jnp.sum(p, axis=-1, keepdims=True)
            outs.append((_dot(p.astype(BF16), mv[:, hs]) / denom).astype(BF16))
        o = jnp.concatenate(outs, axis=1)
        x2 = x1_ref[rows, :] + _dot(o, wo_ref[...])
        x2_ref[rows, :] = x2
        xn = _rms(x2, gf_ref[...])
        xn_ref[rows, :] = xn
        hi, lo = _split_bf16(xn)
        lg_ref[:, rows] = (_dot_nt(wrh_ref[...], hi) + _dot_nt(wrh_ref[...], lo)
                           + _dot_nt(wrl_ref[...], hi))


def _xattn(q, x1, mk, mv, w_o, g_ffn, wr_hi, wr_lo, t_start, t, nb, lt):
    d = q.shape[1]
    nm = mk.shape[1]
    ne = wr_hi.shape[0]
    rows = nb * lt
    assert t % rows == 0 and t_start % rows == 0
    off = t_start // rows
    row = lambda i: (i, 0)
    row_in = lambda i: (i + off, 0)
    if mk.ndim == 3:
        assert nb == 1 and mk.shape[0] == 1
        mem_spec = pl.BlockSpec((1, nm, d), lambda i: (0, 0, 0))
        scratch = []
    else:
        assert mk.shape[0] * lt == t
        mem_spec = pl.BlockSpec(memory_space=pl.ANY)
        scratch = [pltpu.VMEM((2, nb, nm, d), F32), pltpu.VMEM((2, nb, nm, d), F32),
                   pltpu.SemaphoreType.DMA((2,))]
    return pl.pallas_call(
        functools.partial(_xattn_kernel, nb=nb),
        grid=(t // rows,),
        in_specs=[pl.BlockSpec((rows, d), row_in), mem_spec, mem_spec,
                  pl.BlockSpec((rows, d), row_in),
                  _const_spec((d, d)), _const_spec((1, d)),
                  _const_spec((ne, d)), _const_spec((ne, d))],
        scratch_shapes=scratch,
        out_specs=[pl.BlockSpec((rows, d), row), pl.BlockSpec((rows, d), row),
                   pl.BlockSpec((ne, rows), lambda i: (0, i))],
        out_shape=[jax.ShapeDtypeStruct((t, d), F32), jax.ShapeDtypeStruct((t, d), F32),
                   jax.ShapeDtypeStruct((ne, t), F32)],
        compiler_params=_cparams(("arbitrary",)),
        name="mem_xattn",
    )(q, mk, mv, x1, w_o, g_ffn.reshape(1, d), wr_hi, wr_lo)


def _route_kernel(lgp_ref, lgs_ref, xnp_ref, xns_ref, bias_ref, eidx_ref, gate_ref, pos_ref, cnt_ref,
                  xn_ref, run_scr, *, n_first):
    ne, tn = lgp_ref.shape
    first = pl.program_id(0) < n_first
    xn_ref[...] = _pick(first, xnp_ref, xns_ref)

    @pl.when(pl.program_id(0) == 0)
    def _():
        run_scr[...] = jnp.zeros_like(run_scr)

    gsz = ne // N_GROUPS
    sc = jax.nn.sigmoid(_pick(first, lgp_ref, lgs_ref))
    sel = sc + bias_ref[...]
    s3 = sel.reshape(N_GROUPS, gsz, tn)
    ie = lax.broadcasted_iota(jnp.int32, s3.shape, 1).astype(F32)
    m1 = jnp.max(s3, axis=1, keepdims=True)
    first = jnp.min(jnp.where(s3 == m1, ie, float(gsz)), axis=1, keepdims=True)
    m2 = jnp.max(jnp.where(ie == first, -jnp.inf, s3), axis=1, keepdims=True)
    gs = m1 + m2
    ig = lax.broadcasted_iota(jnp.int32, gs.shape, 0)
    grank = jnp.zeros(gs.shape, jnp.int32)
    for g in range(N_GROUPS):
        other = gs[g:g + 1]
        ahead = jnp.where(other > gs, 1, jnp.where(other == gs, (ig > g).astype(jnp.int32), 0))
        grank = grank + ahead
    gkeep = jnp.broadcast_to(grank < TOPK_GROUPS, s3.shape)
    v = jnp.where(gkeep, s3, -jnp.inf).reshape(ne, tn)
    iv = lax.broadcasted_iota(jnp.int32, v.shape, 0)
    rank = jnp.zeros(v.shape, jnp.int32)
    for e in range(ne):
        other = v[e:e + 1]
        ahead = jnp.where(other > v, 1, jnp.where(other == v, (iv > e).astype(jnp.int32), 0))
        rank = rank + ahead
    chosen = rank < TOP_K
    gate = jnp.where(chosen, sc, 0.0)
    gate = gate / jnp.sum(gate, axis=0, keepdims=True) * ROUTE_SCALE
    ti = lax.broadcasted_iota(jnp.int32, (tn, tn), 0)
    tj = lax.broadcasted_iota(jnp.int32, (tn, tn), 1)
    incl = _dot(jnp.where(chosen, 1.0, 0.0).astype(BF16), (ti <= tj).astype(BF16))
    pos = run_scr[...] + incl - 1.0
    run_scr[...] = run_scr[...] + incl[:, tn - 1:tn]
    cnt_ref[...] = run_scr[...].astype(jnp.int32)
    ef = iv.astype(F32)
    for k in range(TOP_K):
        hit = rank == k
        pick = lambda a: jnp.sum(jnp.where(hit, a, 0.0), axis=0, keepdims=True)
        eidx_ref[k:k + 1, :] = pick(ef).astype(jnp.int32)
        gate_ref[k:k + 1, :] = pick(gate)
        pos_ref[k:k + 1, :] = pick(pos).astype(jnp.int32)


def _route(lg_p, lg_s, xn_p, xn_s, router_bias):
    ne = lg_p.shape[0]
    d = xn_p.shape[1]
    t = lg_p.shape[1] + lg_s.shape[1]
    tn = TOKEN_TILE
    n_first = lg_p.shape[1] // tn
    kt = lambda i: (0, i)
    return pl.pallas_call(
        functools.partial(_route_kernel, n_first=n_first),
        grid=(t // tn,),
        in_specs=(_pair_specs((ne, tn), 1, n_first) + _pair_specs((tn, d), 0, n_first)
                  + [_const_spec((ne, 1))]),
        out_specs=[pl.BlockSpec((TOP_K, tn), kt), pl.BlockSpec((TOP_K, tn), kt),
                   pl.BlockSpec((TOP_K, tn), kt), pl.BlockSpec((ne, 1), lambda i: (0, 0)),
                   pl.BlockSpec((tn, d), lambda i: (i, 0))],
        out_shape=[jax.ShapeDtypeStruct((TOP_K, t), jnp.int32),
                   jax.ShapeDtypeStruct((TOP_K, t), F32),
                   jax.ShapeDtypeStruct((TOP_K, t), jnp.int32),
                   jax.ShapeDtypeStruct((ne, 1), jnp.int32),
                   jax.ShapeDtypeStruct((t, d), F32)],
        scratch_shapes=[pltpu.VMEM((ne, 1), F32)],
        compiler_params=_cparams(("arbitrary",)),
        name="router",
    )(lg_p, lg_s, xn_p, xn_s, router_bias.astype(F32).reshape(ne, 1))


def _moe_plan(eidx, posk, counts, n_tiles_max):
    ne = counts.shape[0]
    ntile = (counts + EXPERT_TILE - 1) // EXPERT_TILE
    cum = jnp.cumsum(ntile)
    base = (cum - ntile) * EXPERT_TILE
    n_tiles = cum[-1:]
    experts = jnp.arange(ne, dtype=jnp.int32)
    last_used = jnp.max(jnp.where(ntile > 0, experts, 0))
    tiles = jnp.arange(n_tiles_max, dtype=jnp.int32)
    tile_expert = jnp.sum(tiles[:, None] >= cum[None, :], axis=1)
    tile_expert = jnp.minimum(tile_expert, last_used).astype(jnp.int32)
    slot = (jnp.sum(jnp.where(eidx[..., None] == experts, base, 0), axis=-1) + posk)
    slot = slot.astype(jnp.int32)
    n_k, t = slot.shape
    tok_bits = max(t - 1, 1).bit_length()
    assert (n_tiles_max * EXPERT_TILE) << tok_bits < 2 ** 31
    tok = lax.broadcasted_iota(jnp.int32, slot.shape, 1)
    keys = jnp.sort((slot * (1 << tok_bits) + tok).reshape(n_k * t))
    tok_sorted = keys & ((1 << tok_bits) - 1)
    tok_sorted = jnp.concatenate([tok_sorted, jnp.zeros((EXPERT_TILE,), jnp.int32)])
    first_pair = jnp.cumsum(counts) - counts
    first_tile = cum - ntile
    mine = tile_expert[:, None] == experts[None, :]
    lookup = lambda table: jnp.sum(jnp.where(mine, table[None, :], 0), axis=1)
    pair0 = lookup(first_pair) + (tiles - lookup(first_tile)) * EXPERT_TILE
    pair0 = jnp.clip(pair0, 0, n_k * t).astype(jnp.int32)
    return slot, tile_expert, n_tiles.astype(jnp.int32), tok_sorted, pair0


def _row_copy_wait(ref, n_rows, sem):
    pltpu.make_async_copy(ref.at[pl.ds(0, n_rows)], ref.at[pl.ds(0, n_rows)], sem).wait()


def _expert_kernel(te_ref, nt_ref, pair0_ref, tok_ref, xn_hbm, wg_ref, wu_ref, wd_ref, ys_ref,
                   x_a, x_b, wg_scr, wu_scr, wd_scr, sem_a, sem_b):
    i = pl.program_id(0)
    n_tiles = nt_ref[0]

    def issue_row(tile, buf, sem, r, queue):
        t = tok_ref[pair0_ref[tile] + r]
        pltpu.make_async_copy(xn_hbm.at[pl.ds(t, 1)], buf.at[pl.ds(r, 1)],
                              sem).start(priority=queue)

    @pl.when(i == 0)
    def _():
        def body(rb, _):
            for u in range(SUBLANES):
                issue_row(0, x_a, sem_a, rb * SUBLANES + u, u % 2)
            return 0
        lax.fori_loop(0, EXPERT_TILE // SUBLANES, body, 0)

    nxt_tile = jnp.where(i + 1 < n_tiles, i + 1, 0)

    def run(cur, sem_cur, nxt, sem_nxt):
        _row_copy_wait(xn_hbm, EXPERT_TILE, sem_cur)

        @pl.when((i == 0) | (te_ref[i] != te_ref[jnp.maximum(i - 1, 0)]))
        def _():
            wg_scr[...] = wg_ref[0].astype(BF16)
            wu_scr[...] = wu_ref[0].astype(BF16)
            wd_scr[...] = wd_ref[0].astype(BF16)

        for r in range(EXPERT_TILE):
            issue_row(nxt_tile, nxt, sem_nxt, r, r % 2)
        x = cur[...].astype(BF16)
        h = jax.nn.silu(_dot(x, wg_scr[...])) * _dot(x, wu_scr[...])
        ys_ref[...] = _dot(h.astype(BF16), wd_scr[...])

        @pl.when(i == n_tiles - 1)
        def _():
            _row_copy_wait(xn_hbm, EXPERT_TILE, sem_nxt)

    @pl.when((i < n_tiles) & (i % 2 == 0))
    def _():
        run(x_a, sem_a, x_b, sem_b)

    @pl.when((i < n_tiles) & (i % 2 == 1))
    def _():
        run(x_b, sem_b, x_a, sem_a)

    @pl.when(i >= n_tiles)
    def _():
        ys_ref[...] = jnp.zeros_like(ys_ref)


def _experts(xn, tile_expert, n_tiles, pair0, tok_sorted, w_gate, w_up, w_down):
    d = xn.shape[1]
    ne, _, de = w_gate.shape
    n_tiles_max = tile_expert.shape[0]
    out_tile = lambda i, *_: (i, 0)
    wsel = lambda i, te, *_: (te[i], 0, 0)
    return pl.pallas_call(
        _expert_kernel,
        grid_spec=pltpu.PrefetchScalarGridSpec(
            num_scalar_prefetch=4,
            grid=(n_tiles_max,),
            in_specs=[pl.BlockSpec(memory_space=pl.ANY),
                      pl.BlockSpec((1, d, de), wsel), pl.BlockSpec((1, d, de), wsel),
                      pl.BlockSpec((1, de, d), wsel)],
            out_specs=pl.BlockSpec((EXPERT_TILE, d), out_tile),
            scratch_shapes=[pltpu.VMEM((EXPERT_TILE, d), F32), pltpu.VMEM((EXPERT_TILE, d), F32),
                            pltpu.VMEM((d, de), BF16), pltpu.VMEM((d, de), BF16),
                            pltpu.VMEM((de, d), BF16),
                            pltpu.SemaphoreType.DMA, pltpu.SemaphoreType.DMA]),
        out_shape=jax.ShapeDtypeStruct((n_tiles_max * EXPERT_TILE, d), F32),
        compiler_params=_cparams(("arbitrary",)),
        name="moe_experts",
    )(tile_expert, n_tiles, pair0, tok_sorted, xn, w_gate, w_up, w_down)


def _combine_kernel(slot_ref, gate_ref, xn_ref, x2p_ref, x2s_ref, sg_ref, su_ref, sd_ref,
                    gf_ref, ys_hbm, yp_ref, ysamp_ref, buf_a, buf_b, base_scr, y_scr, sem_a, sem_b,
                    *, n_first):
    tm = xn_ref.shape[0]
    i = pl.program_id(0)
    n_steps = pl.num_programs(0)
    first = i < n_first
    rb_rows = SUBLANES

    def issue_rows(tile, buf, sem, r0, n):
        for rr in range(n):
            for k in range(TOP_K):
                s = slot_ref[(tile * tm + r0 + rr) * TOP_K + k]
                pltpu.make_async_copy(ys_hbm.at[pl.ds(s, 1)], buf.at[k, pl.ds(r0 + rr, 1)],
                                      sem).start(priority=k % 2)

    @pl.when(i == 0)
    def _():
        def body(r, _):
            issue_rows(0, buf_a, sem_a, r, 1)
            return 0
        lax.fori_loop(0, tm, body, 0)

    x = xn_ref[...].astype(BF16)
    hs = jax.nn.silu(_dot(x, sg_ref[...])) * _dot(x, su_ref[...])
    base_scr[...] = _pick(first, x2p_ref, x2s_ref) + _dot(hs.astype(BF16), sd_ref[...])
    nxt_tile = jnp.where(i + 1 < n_steps, i + 1, 0)

    def run(cur, sem_cur, nxt, sem_nxt):
        _row_copy_wait(ys_hbm, tm * TOP_K, sem_cur)

        for rb in range(tm // rb_rows):
            r0 = rb * rb_rows
            issue_rows(nxt_tile, nxt, sem_nxt, r0, rb_rows)
            acc = base_scr[pl.ds(r0, rb_rows), :]
            g = gate_ref[pl.ds(r0, rb_rows), :]
            for k in range(TOP_K):
                acc = acc + g[:, k:k + 1] * cur[k, pl.ds(r0, rb_rows), :]
            y_scr[pl.ds(r0, rb_rows), :] = _rms(acc, gf_ref[...])

        @pl.when(i == n_steps - 1)
        def _():
            _row_copy_wait(ys_hbm, tm * TOP_K, sem_nxt)

    @pl.when(i % 2 == 0)
    def _():
        run(buf_a, sem_a, buf_b, sem_b)

    @pl.when(i % 2 == 1)
    def _():
        run(buf_b, sem_b, buf_a, sem_a)

    @pl.when(first)
    def _():
        yp_ref[...] = y_scr[...]

    @pl.when(jnp.logical_not(first))
    def _():
        ysamp_ref[...] = y_scr[...]


def _combine(slot_flat, gates, xn, x2_p, x2_s, ws_gate, ws_up, ws_down, g_final, ys):
    d = x2_p.shape[1]
    tp, tsamp = x2_p.shape[0], x2_s.shape[0]
    dsh = ws_gate.shape[1]
    tm = COMBINE_TILE
    n_first = tp // tm
    row = lambda i, *_: (i, 0)
    pair = lambda: _pair_specs((tm, d), 0, n_first)
    return pl.pallas_call(
        functools.partial(_combine_kernel, n_first=n_first),
        grid_spec=pltpu.PrefetchScalarGridSpec(
            num_scalar_prefetch=1,
            grid=((tp + tsamp) // tm,),
            in_specs=[pl.BlockSpec((tm, TOP_K), row), pl.BlockSpec((tm, d), row)] + pair()
                     + [_const_spec((d, dsh)), _const_spec((d, dsh)), _const_spec((dsh, d)),
                        _const_spec((1, d)), pl.BlockSpec(memory_space=pl.ANY)],
            out_specs=pair(),
            scratch_shapes=[pltpu.VMEM((TOP_K, tm, d), F32), pltpu.VMEM((TOP_K, tm, d), F32),
                            pltpu.VMEM((tm, d), F32), pltpu.VMEM((tm, d), F32),
                            pltpu.SemaphoreType.DMA, pltpu.SemaphoreType.DMA]),
        out_shape=[jax.ShapeDtypeStruct((tp, d), F32), jax.ShapeDtypeStruct((tsamp, d), F32)],
        compiler_params=_cparams(("arbitrary",)),
        name="moe_combine",
    )(slot_flat, gates, xn, x2_p, x2_s, ws_gate.astype(BF16), ws_up.astype(BF16),
      ws_down.astype(BF16), g_final.reshape(1, d), ys)


def _moe(xn, x2_p, x2_s, eidx, gate, posk, counts, w_gate, w_up, w_down, ws_gate, ws_up,
         ws_down, g_final):
    t = xn.shape[0]
    ne = w_gate.shape[0]
    n_tiles_max = t * TOP_K // EXPERT_TILE + ne
    slot, tile_expert, n_tiles, tok_sorted, pair0 = _moe_plan(eidx, posk, counts[:, 0],
                                                              n_tiles_max)
    ys = _experts(xn, tile_expert, n_tiles, pair0, tok_sorted, w_gate, w_up, w_down)
    return _combine(jnp.transpose(slot).reshape(t * TOP_K), jnp.transpose(gate), xn, x2_p, x2_s,
                    ws_gate, ws_up, ws_down, g_final, ys)


def kernel(x_prompt, x_sample, cache_mem_k, cache_mem_v, state_s5_re, state_s5_im, state_gla, mem_prompt, w_in, w_a2, b_a2, lam_re, lam_im, log_dt, b_re, b_im, c_re, c_im, d_skip, w_glu, b_glu, g_s5_out, g_gla_head, w_out, g_mix, g_xattn, g_mem, w_q, w_k, w_v, w_o, g_ffn, w_router, router_bias, w_gate, w_up, w_down, ws_gate, ws_up, ws_down, g_final):
    depth = w_in.shape[0]
    assert depth == 1, "one trunk layer"
    bp, lp, d = x_prompt.shape
    bs, ls, _ = x_sample.shape
    assert bp == 1 and ls == CHUNK and lp % CHUNK == 0
    n_mem = mem_prompt.shape[1]
    g, p = lam_re.shape[1:]
    d_s5 = g * S5_GROUP
    d_qk = GLA_HEADS * GLA_DK
    d_gla = w_out.shape[1] - d_s5
    ns = d_s5 // LANES
    ne = w_router.shape[2]
    tp, tsamp = bp * lp, bs * ls
    xp = x_prompt.reshape(tp, d)
    xs = x_sample.reshape(tsamp, d)
    sq = lambda a: a.reshape(a.shape[1:])

    mk_p, mv_p = _memory_kv(mem_prompt.reshape(n_mem, d), sq(g_mem), sq(w_k), sq(w_v))

    u, q, k, v, a, r = _in_proj(xp, xs, sq(g_mix), sq(w_in), d_s5, d_qk, d_gla)

    ksm, psm, qsm, step_m, carry_m, a8 = _s5_coeffs(
        sq(lam_re), sq(lam_im), sq(log_dt), sq(b_re), sq(b_im), sq(c_re), sq(c_im), sq(d_skip))
    tables = tuple(_s5_expand(ksm, psm, qsm)) + (step_m, carry_m, a8)
    zero_h = jnp.zeros((ns, bp, 2 * GROUPS_PER_SLAB * p), F32)
    rows_p = lp // S5_BLOCK
    y_p, h_p = _s5_mixer(u, tables, zero_h, 0, tp, rows_per_seq=rows_p,
                         tile_rows=math.gcd(rows_p, 512))
    h0_s = _pack_s5_state(sq(state_s5_re), sq(state_s5_im), ns)
    y_s, h_s = _s5_mixer(u, tables, h0_s, tp, tsamp, rows_per_seq=ls // S5_BLOCK,
                         tile_rows=tsamp // S5_BLOCK)

    wa2 = jnp.pad(sq(w_a2), ((0, LANES - GLA_RANK), (0, 0)))
    zero_s = jnp.zeros((bp,) + state_gla.shape[2:], F32)
    gla_p, sg_p = _gla_mixer(q, k, v, a, r, wa2, sq(b_a2), sq(g_gla_head), zero_s, 0, tp,
                             carry=True, n_chunks=4)
    gla_s, sg_s = _gla_mixer(q, k, v, a, r, wa2, sq(b_a2), sq(g_gla_head), sq(state_gla), tp,
                             tsamp, carry=False, n_chunks=4)

    x1, xq = _mix_out(y_p, y_s, gla_p, gla_s, xp, xs, sq(w_glu), sq(b_glu), sq(g_s5_out),
                      sq(w_out), sq(g_xattn), sq(w_q))

    wr_hi, wr_lo = _split_bf16(jnp.transpose(sq(w_router)))
    wo = sq(w_o).astype(BF16)
    x2_p, xn_p, lg_p = _xattn(xq, x1, mk_p[None], mv_p[None], wo, sq(g_ffn), wr_hi, wr_lo,
                              0, tp, nb=1, lt=TOKEN_TILE)
    x2_s, xn_s, lg_s = _xattn(xq, x1, sq(cache_mem_k), sq(cache_mem_v), wo, sq(g_ffn), wr_hi,
                              wr_lo, tp, tsamp, nb=2, lt=ls)

    eidx, gate, posk, counts, xn = _route(lg_p, lg_s, xn_p, xn_s, sq(router_bias))
    y_p2, y_s2 = _moe(xn, x2_p, x2_s, eidx, gate, posk, counts, sq(w_gate), sq(w_up),
                      sq(w_down), sq(ws_gate), sq(ws_up), sq(ws_down), g_final)

    xh = d // X_HEADS
    re_p, im_p = _unpack_s5_state(h_p, g, p)
    re_s, im_s = _unpack_s5_state(h_s, g, p)
    return (y_p2.reshape(bp, lp, d), y_s2.reshape(bs, ls, d),
            mk_p.reshape(1, bp, n_mem, X_HEADS, xh), mv_p.reshape(1, bp, n_mem, X_HEADS, xh),
            re_p[None], im_p[None], sg_p[None], re_s[None], im_s[None], sg_s[None])
```

```python
import functools
import math

import jax
import jax.numpy as jnp
from jax import lax
from jax.experimental import pallas as pl
from jax.experimental.pallas import tpu as pltpu

F32 = jnp.float32
BF16 = jnp.bfloat16

EPS = 1e-6
CHUNK = 64
S5_GROUP = 16
GLA_HEADS = 4
GLA_DK = 128
GLA_RANK = 16
GLA_TAU = 16.0
X_HEADS = 4
N_GROUPS = 8
TOPK_GROUPS = 4
TOP_K = 8
ROUTE_SCALE = 2.5

LANES = 128
SUBLANES = 8
S5_BLOCK = SUBLANES
GROUPS_PER_SLAB = LANES // S5_GROUP
TOKEN_TILE = 512
EXPERT_TILE = 256
COMBINE_TILE = 128
VMEM_LIMIT = 56 * 1024 * 1024


def _cparams(sem):
    return pltpu.CompilerParams(dimension_semantics=sem, vmem_limit_bytes=VMEM_LIMIT)


def _const_spec(shape):
    nd = len(shape)
    return pl.BlockSpec(shape, lambda *_: (0,) * nd, pipeline_mode=pl.Buffered(1))


def _pair_specs(block, axis, n_first):
    def at(f):
        return lambda i, *_: tuple(f(i) if a == axis else 0 for a in range(len(block)))
    return [pl.BlockSpec(block, at(lambda i: jnp.minimum(i, n_first - 1))),
            pl.BlockSpec(block, at(lambda i: jnp.maximum(i - n_first, 0)))]


def _pick(first, a_ref, b_ref):
    return jnp.where(first, a_ref[...], b_ref[...])


def _rms(x, g):
    return x * lax.rsqrt(jnp.mean(x * x, axis=-1, keepdims=True) + EPS) * g


def _split_bf16(x):
    hi = x.astype(BF16)
    lo = (x - hi.astype(F32)).astype(BF16)
    return hi, lo


def _split3_bf16(x):
    a = x.astype(BF16)
    r = x - a.astype(F32)
    b = r.astype(BF16)
    c = (r - b.astype(F32)).astype(BF16)
    return a, b, c


def _dot(a, b):
    return jnp.dot(a, b, preferred_element_type=F32)


def _dot_nt(a, b):
    return lax.dot_general(a, b, (((1,), (1,)), ((), ())), preferred_element_type=F32)


def _dot_tn(a, b):
    return lax.dot_general(a, b, (((0,), (0,)), ((), ())), preferred_element_type=F32)


def _memkv_kernel(mem_ref, g_ref, wk_ref, wv_ref, mk_ref, mv_ref):
    m = _rms(mem_ref[...], g_ref[...]).astype(BF16)
    mk_ref[...] = _dot(m, wk_ref[...].astype(BF16))
    mv_ref[...] = _dot(m, wv_ref[...].astype(BF16))


def _memory_kv(mem, g_mem, wk, wv):
    n, d = mem.shape
    tn = 512
    return pl.pallas_call(
        _memkv_kernel,
        grid=(d // tn,),
        in_specs=[_const_spec((n, d)), _const_spec((1, d)),
                  pl.BlockSpec((d, tn), lambda j: (0, j)),
                  pl.BlockSpec((d, tn), lambda j: (0, j))],
        out_specs=[pl.BlockSpec((n, tn), lambda j: (0, j))] * 2,
        out_shape=[jax.ShapeDtypeStruct((n, d), F32)] * 2,
        compiler_params=_cparams(("arbitrary",)),
        name="memory_kv",
    )(mem, g_mem.reshape(1, d), wk, wv)


def _inproj_kernel(xp_ref, xs_ref, g_ref, wuh_ref, wul_ref, wqkv_ref, wa_ref, wr_ref,
                   u_ref, q_ref, k_ref, v_ref, a_ref, r_ref, *, n_first):
    first = pl.program_id(0) < n_first
    x = _pick(first, xp_ref, xs_ref)
    nx = _rms(x, g_ref[...])
    hi, lo = _split_bf16(nx)
    u = _dot_nt(hi, wuh_ref[...])
    for j in range(u_ref.shape[0]):
        u_ref[j] = u[:, j * LANES:(j + 1) * LANES]

    @pl.when(jnp.logical_not(first))
    def _():
        fix = _dot_nt(lo, wuh_ref[...]) + _dot_nt(hi, wul_ref[...])
        for j in range(u_ref.shape[0]):
            u_ref[j] += fix[:, j * LANES:(j + 1) * LANES]

    qkv = _dot_nt(hi, wqkv_ref[...])
    dq = q_ref.shape[1]
    q_ref[...] = qkv[:, :dq].astype(BF16)
    k_ref[...] = qkv[:, dq:2 * dq].astype(BF16)
    v_ref[...] = qkv[:, 2 * dq:].astype(BF16)
    a_ref[...] = _dot_nt(hi, wa_ref[...]) + _dot_nt(lo, wa_ref[...])
    r_ref[...] = _dot_nt(hi, wr_ref[...]).astype(BF16)


def _in_proj(xp, xs, g_mix, w_in, d_s5, d_qk, d_gla):
    d = xp.shape[1]
    t = xp.shape[0] + xs.shape[0]
    tm = TOKEN_TILE
    n_first = xp.shape[0] // tm
    o1 = d_s5
    o2 = o1 + 2 * d_qk + d_gla
    o3 = o2 + GLA_RANK
    wt = jnp.transpose(w_in)
    wu_hi, wu_lo = _split_bf16(wt[:o1])
    wqkv = wt[o1:o2].astype(BF16)
    wa = jnp.pad(wt[o2:o3], ((0, LANES - GLA_RANK), (0, 0))).astype(BF16)
    wr = wt[o3:].astype(BF16)
    n_slab = d_s5 // LANES
    row = lambda i: (i, 0)
    return pl.pallas_call(
        functools.partial(_inproj_kernel, n_first=n_first),
        grid=(t // tm,),
        in_specs=_pair_specs((tm, d), 0, n_first) + [
            _const_spec((1, d)), _const_spec(wu_hi.shape), _const_spec(wu_lo.shape),
            _const_spec(wqkv.shape), _const_spec(wa.shape), _const_spec(wr.shape)],
        out_specs=[pl.BlockSpec((n_slab, tm, LANES), lambda i: (0, i, 0)),
                   pl.BlockSpec((tm, d_qk), row), pl.BlockSpec((tm, d_qk), row),
                   pl.BlockSpec((tm, d_gla), row), pl.BlockSpec((tm, LANES), row),
                   pl.BlockSpec((tm, d_gla), row)],
        out_shape=[jax.ShapeDtypeStruct((n_slab, t, LANES), F32),
                   jax.ShapeDtypeStruct((t, d_qk), BF16), jax.ShapeDtypeStruct((t, d_qk), BF16),
                   jax.ShapeDtypeStruct((t, d_gla), BF16), jax.ShapeDtypeStruct((t, LANES), F32),
                   jax.ShapeDtypeStruct((t, d_gla), BF16)],
        compiler_params=_cparams(("arbitrary",)),
        name="in_proj",
    )(xp, xs, g_mix.reshape(1, d), wu_hi, wu_lo, wqkv, wa, wr)


def _s5_coeffs(lam_re, lam_im, log_dt, b_re, b_im, c_re, c_im, d_skip):
    g, p = lam_re.shape
    h = b_re.shape[-1]
    nb = S5_BLOCK
    ns = g // GROUPS_PER_SLAB
    gl = GROUPS_PER_SLAB
    dt = jnp.exp(log_dt.astype(F32))[:, None]
    lr, li = lam_re.astype(F32), lam_im.astype(F32)
    mag = jnp.exp(lr * dt)
    ar, ai = mag * jnp.cos(li * dt), mag * jnp.sin(li * dt)
    den = lr * lr + li * li
    nr = ar - 1.0
    cf_r = (nr * lr + ai * li) / den
    cf_i = (ai * lr - nr * li) / den
    bb_r = cf_r[..., None] * b_re - cf_i[..., None] * b_im
    bb_i = cf_r[..., None] * b_im + cf_i[..., None] * b_re

    def cpow(n):
        e = n[:, None, None] * dt[None]
        m = jnp.exp(lr[None] * e)
        return m * jnp.cos(li[None] * e), m * jnp.sin(li[None] * e)

    pr, pi = cpow(jnp.arange(nb + 1, dtype=F32))

    bt_r, bt_i = jnp.swapaxes(bb_r, 1, 2), jnp.swapaxes(bb_i, 1, 2)
    ct_r, ct_i = jnp.swapaxes(c_re, 1, 2), jnp.swapaxes(c_im, 1, 2)

    cb_r = jnp.einsum('gcp,gph->gphc', c_re, bb_r) - jnp.einsum('gcp,gph->gphc', c_im, bb_i)
    cb_i = jnp.einsum('gcp,gph->gphc', c_re, bb_i) + jnp.einsum('gcp,gph->gphc', c_im, bb_r)
    taps = (jnp.einsum('ngp,gphc->nghc', pr[:nb], cb_r)
            - jnp.einsum('ngp,gphc->nghc', pi[:nb], cb_i))
    skip = d_skip[None, :, :, None] * jnp.eye(h, dtype=F32)[None, None]
    taps = taps + jnp.where(jnp.arange(nb)[:, None, None, None] == 0, skip, 0.0)
    ksm = jnp.swapaxes(taps.reshape(nb, ns, gl * h, h), 0, 1)

    wr, wi = pr[nb - 1::-1][:nb], pi[nb - 1::-1][:nb]
    wr, wi = wr[:, :, None, :], wi[:, :, None, :]
    inj = jnp.concatenate([wr * bt_r[None] - wi * bt_i[None],
                           wr * bt_i[None] + wi * bt_r[None]], axis=-1)
    psm = jnp.swapaxes(inj.reshape(nb, ns, gl * h, 2 * p), 0, 1)

    er = jnp.transpose(pr[1:nb + 1], (1, 2, 0))[..., None]
    ei = jnp.transpose(pi[1:nb + 1], (1, 2, 0))[..., None]
    q_r = ct_r[:, :, None, :] * er - ct_i[:, :, None, :] * ei
    q_i = -(ct_r[:, :, None, :] * ei + ct_i[:, :, None, :] * er)
    qq = jnp.stack([q_r.reshape(ns, gl * p, nb * h), q_i.reshape(ns, gl * p, nb * h)], axis=1)
    qsm = qq.reshape(ns, 2 * gl * p, nb * h)

    def slab(xr, xi):
        k = xr.shape[0]
        xr = jnp.transpose(xr.reshape(k, ns, gl * p), (1, 0, 2))
        xi = jnp.transpose(xi.reshape(k, ns, gl * p), (1, 0, 2))
        return jnp.concatenate([xr, xi], axis=-1)

    rows = jnp.arange(SUBLANES, dtype=F32)
    cr, ci = cpow(nb * rows)
    carry_m = slab(cr, ci)
    sr, si = cpow(nb * jnp.array([1.0, 2.0, 4.0], F32))
    keep = (rows[None, :] >= jnp.array([1.0, 2.0, 4.0], F32)[:, None]).astype(F32)
    step_m = slab((sr[:, None] * keep[:, :, None, None]).reshape(3 * SUBLANES, g, p),
                  (si[:, None] * keep[:, :, None, None]).reshape(3 * SUBLANES, g, p))
    step_m = step_m.reshape(ns, 3, SUBLANES, 2 * gl * p)
    a8 = carry_m[:, 1:2]
    return ksm, psm, qsm, step_m, carry_m, a8


def _s5_expand_kernel(ksm_ref, psm_ref, qsm_ref, tm_ref, ph_ref, pl_ref, qm_ref):
    nb = S5_BLOCK
    h = S5_GROUP
    w = ph_ref.shape[2]
    p = w // (2 * GROUPS_PER_SLAB)

    def spread(x, sel, keep):
        a, b, c = _split3_bf16(x)
        return jnp.where(keep, _dot(a, sel) + _dot(b, sel) + _dot(c, sel), 0.0)

    def iota(shape, axis):
        return lax.broadcasted_iota(jnp.int32, shape, axis)

    sel_c = (iota((h, LANES), 1) % h == iota((h, LANES), 0)).astype(BF16)
    keep_t = iota((LANES, LANES), 0) // h == iota((LANES, LANES), 1) // h
    tm_ref[...] = jnp.zeros_like(tm_ref)
    for tau in range(nb):
        blk = spread(ksm_ref[0, tau], sel_c, keep_t).astype(BF16)
        for s in range(nb - tau):
            t = s + tau
            tm_ref[0, s * LANES:(s + 1) * LANES, t * LANES:(t + 1) * LANES] = blk

    half = w // 2
    keep_p = iota((LANES, half), 0) // h == iota((LANES, half), 1) // p
    for ri in range(2):
        sel_p = (iota((2 * p, half), 0) == iota((2 * p, half), 1) % p + ri * p).astype(BF16)
        for s in range(nb):
            blk = spread(psm_ref[0, s], sel_p, keep_p)
            hi, lo = _split_bf16(blk)
            ph_ref[0, s * LANES:(s + 1) * LANES, ri * half:(ri + 1) * half] = hi
            pl_ref[0, s * LANES:(s + 1) * LANES, ri * half:(ri + 1) * half] = lo

    keep_q = (iota((w, LANES), 0) % half) // p == iota((w, LANES), 1) // h
    for t in range(nb):
        sel_q = (iota((nb * h, LANES), 0) == iota((nb * h, LANES), 1) % h + t * h).astype(BF16)
        qm_ref[0, :, t * LANES:(t + 1) * LANES] = spread(qsm_ref[0], sel_q, keep_q).astype(BF16)


def _s5_expand(ksm, psm, qsm):
    ns = ksm.shape[0]
    w = qsm.shape[1]
    k = S5_BLOCK * LANES
    blk = lambda shape: pl.BlockSpec((1,) + shape, lambda j: (j,) + (0,) * len(shape))
    return pl.pallas_call(
        _s5_expand_kernel,
        grid=(ns,),
        in_specs=[blk(ksm.shape[1:]), blk(psm.shape[1:]), blk(qsm.shape[1:])],
        out_specs=[blk((k, k)), blk((k, w)), blk((k, w)), blk((w, k))],
        out_shape=[jax.ShapeDtypeStruct((ns, k, k), BF16), jax.ShapeDtypeStruct((ns, k, w), BF16),
                   jax.ShapeDtypeStruct((ns, k, w), BF16), jax.ShapeDtypeStruct((ns, w, k), BF16)],
        compiler_params=_cparams(("arbitrary",)),
        name="s5_tables",
    )(ksm, psm, qsm)


def _cmul(xr, xi, mr, mi):
    return xr * mr - xi * mi, xr * mi + xi * mr


def _s5_kernel(u_ref, ph_ref, pl_ref, tm_ref, qm_ref, step_ref, carry_ref, a8_ref, h0_ref,
               y_ref, hout_ref, s_scr, hin_scr, c_scr, *, rows_per_seq):
    ti = pl.program_id(1)
    rows = s_scr.shape[0]
    half = s_scr.shape[1] // 2
    nb = S5_BLOCK
    u8 = jnp.concatenate([u_ref[0, pl.ds(s, rows, stride=nb), :] for s in range(nb)], axis=1)
    hi, lo = _split_bf16(u8)
    per_block_seq = rows_per_seq == SUBLANES
    if per_block_seq:
        s_scr[...] = _dot(hi, ph_ref[0]) + _dot(lo, ph_ref[0]) + _dot(hi, pl_ref[0])
    else:
        s_scr[...] = _dot(hi, ph_ref[0])

    if not per_block_seq:
        @pl.when(ti == 0)
        def _():
            c_scr[...] = h0_ref[0]

    not_first = (lax.broadcasted_iota(jnp.int32, (SUBLANES, 1), 0) >= 1).astype(F32)
    a8r, a8i = a8_ref[0, :, :half], a8_ref[0, :, half:]
    cmr, cmi = carry_ref[0, :, :half], carry_ref[0, :, half:]

    def body(rb, _):
        r0 = pl.multiple_of(rb * SUBLANES, SUBLANES)
        sb = s_scr[pl.ds(r0, SUBLANES), :]
        x = pltpu.roll(sb, 1, 0) * not_first
        xr, xi = x[:, :half], x[:, half:]
        for k, sh in enumerate((1, 2, 4)):
            m = step_ref[0, k]
            pr, pi = _cmul(pltpu.roll(xr, sh, 0), pltpu.roll(xi, sh, 0), m[:, :half], m[:, half:])
            xr, xi = xr + pr, xi + pi
        c = h0_ref[0, pl.ds(rb, 1), :] if per_block_seq else c_scr[...]
        cr, ci = c[:, :half], c[:, half:]
        pr, pi = _cmul(cr, ci, cmr, cmi)
        hr, hi_ = xr + pr, xi + pi
        hin_scr[pl.ds(r0, SUBLANES), :half] = hr
        hin_scr[pl.ds(r0, SUBLANES), half:] = hi_
        nr, ni = _cmul(hr[SUBLANES - 1:], hi_[SUBLANES - 1:], a8r, a8i)
        cn = jnp.concatenate([nr, ni], axis=1) + sb[SUBLANES - 1:]
        if per_block_seq:
            hout_ref[0, pl.ds(rb, 1), :] = cn
        else:
            c_scr[...] = cn
        return 0

    lax.fori_loop(0, rows // SUBLANES, body, 0)
    if not per_block_seq:
        hout_ref[0] = c_scr[...]

    y8 = _dot(hi, tm_ref[0]) + _dot(hin_scr[...].astype(BF16), qm_ref[0])
    for t in range(nb):
        y_ref[0, pl.ds(t, rows, stride=nb), :] = y8[:, t * LANES:(t + 1) * LANES]


def _s5_mixer(u_slabs, tables, h0, t_start, t, rows_per_seq, tile_rows):
    tm, pm_hi, pm_lo, qm, step_m, carry_m, a8 = tables
    ns = u_slabs.shape[0]
    rows = t // S5_BLOCK
    n_tiles = rows // tile_rows
    assert rows % tile_rows == 0 and t_start % (tile_rows * S5_BLOCK) == 0
    off = t_start // (tile_rows * S5_BLOCK)
    n_seq = h0.shape[1]
    w = pm_hi.shape[-1]
    slab3 = lambda j, i: (j, 0, 0)
    kern = functools.partial(_s5_kernel, rows_per_seq=rows_per_seq)
    return pl.pallas_call(
        kern,
        grid=(ns, n_tiles),
        in_specs=[pl.BlockSpec((1, tile_rows * S5_BLOCK, LANES), lambda j, i: (j, i + off, 0)),
                  pl.BlockSpec((1,) + pm_hi.shape[1:], slab3),
                  pl.BlockSpec((1,) + pm_lo.shape[1:], slab3),
                  pl.BlockSpec((1,) + tm.shape[1:], slab3),
                  pl.BlockSpec((1,) + qm.shape[1:], slab3),
                  pl.BlockSpec((1,) + step_m.shape[1:], lambda j, i: (j, 0, 0, 0)),
                  pl.BlockSpec((1,) + carry_m.shape[1:], slab3),
                  pl.BlockSpec((1,) + a8.shape[1:], slab3),
                  pl.BlockSpec((1, n_seq, w), slab3)],
        out_specs=[pl.BlockSpec((1, tile_rows * S5_BLOCK, LANES), lambda j, i: (j, i, 0)),
                   pl.BlockSpec((1, n_seq, w), slab3)],
        out_shape=[jax.ShapeDtypeStruct((ns, t, LANES), F32),
                   jax.ShapeDtypeStruct((ns, n_seq, w), F32)],
        scratch_shapes=[pltpu.VMEM((tile_rows, w), F32), pltpu.VMEM((tile_rows, w), F32),
                        pltpu.VMEM((1, w), F32)],
        compiler_params=_cparams(("arbitrary", "arbitrary")),
        name="s5_mixer",
    )(u_slabs, pm_hi, pm_lo, tm, qm, step_m, carry_m, a8, h0)


def _pack_s5_state(re, im, ns):
    b = re.shape[0]
    r = re.astype(F32).reshape(b, ns, -1)
    i = im.astype(F32).reshape(b, ns, -1)
    return jnp.transpose(jnp.concatenate([r, i], axis=-1), (1, 0, 2))


def _unpack_s5_state(hc, g, p):
    ns, b, w = hc.shape
    hc = jnp.transpose(hc, (1, 0, 2))
    re = hc[:, :, :w // 2].reshape(b, g, p)
    im = hc[:, :, w // 2:].reshape(b, g, p)
    return re, im


def _gla_kernel(q_ref, k_ref, v_ref, a_ref, r_ref, wa2_ref, ba2_ref, gh_ref, s0_ref,
                o_ref, sout_ref, st_scr, *, carry, n_chunks):
    step = pl.program_id(0)
    dk = GLA_DK
    dv = v_ref.shape[1] // GLA_HEADS
    c = CHUNK
    scale = dk ** -0.5
    ri = lax.broadcasted_iota(jnp.int32, (c, c), 0)
    ci = lax.broadcasted_iota(jnp.int32, (c, c), 1)
    causal = ri >= ci
    tril = causal.astype(BF16)
    eye_dk = (lax.broadcasted_iota(jnp.int32, (dk, dk), 0)
              == lax.broadcasted_iota(jnp.int32, (dk, dk), 1))

    if carry:
        @pl.when(step == 0)
        def _():
            st_scr[...] = s0_ref[0]

    wa_hi, wa_lo = _split_bf16(wa2_ref[...])
    for n in range(n_chunks):
        rows = slice(n * c, (n + 1) * c)
        a_hi, a_lo = _split_bf16(a_ref[rows, :])
        logit = _dot(a_hi, wa_hi) + _dot(a_lo, wa_hi) + _dot(a_hi, wa_lo) + ba2_ref[...]
        g = jax.nn.log_sigmoid(logit) * (1.0 / GLA_TAU)
        g1, g2, g3 = _split3_bf16(g)
        bcum = _dot(tril, g1) + _dot(tril, g2) + _dot(tril, g3)
        for h in range(GLA_HEADS):
            ks = slice(h * dk, (h + 1) * dk)
            vs = slice(h * dv, (h + 1) * dv)
            b = bcum[:, ks]
            qh = q_ref[rows, ks].astype(F32) * scale
            kh = k_ref[rows, ks].astype(F32)
            vh = v_ref[rows, vs]
            state = s0_ref[n, h] if not carry else st_scr[h]
            qe = (qh * jnp.exp(b)).astype(BF16)
            ke = (kh * jnp.exp(-b)).astype(BF16)
            att = jnp.where(causal, _dot_nt(qe, ke), 0.0)
            o = _dot(att.astype(BF16), vh) + _dot(qe, state.astype(BF16))
            blast = b[c - 1:c, :]
            kd = (kh * jnp.exp(blast - b)).astype(BF16)
            decay = jnp.sum(jnp.where(eye_dk, jnp.exp(blast), 0.0), axis=1, keepdims=True)
            new_state = decay * state + _dot_tn(kd, vh)
            if carry:
                st_scr[h] = new_state
            else:
                sout_ref[n, h] = new_state
            on = _rms(o, gh_ref[...])
            rr = r_ref[rows, vs].astype(F32)
            o_ref[rows, vs] = (on * (rr * jax.nn.sigmoid(rr))).astype(BF16)
    if carry:
        sout_ref[0] = st_scr[...]


def _gla_mixer(q, k, v, a, r, wa2, ba2, g_head, s0, t_start, t, carry, n_chunks):
    dqk = q.shape[1]
    dvt = v.shape[1]
    hh, dk, dv = s0.shape[1:]
    rows = n_chunks * CHUNK
    assert t % rows == 0 and t_start % rows == 0
    off = t_start // rows
    row = lambda i: (i, 0)
    row_in = lambda i: (i + off, 0)
    if carry:
        sblk, smap = (1, hh, dk, dv), (lambda i: (0, 0, 0, 0))
    else:
        sblk, smap = (n_chunks, hh, dk, dv), (lambda i: (i, 0, 0, 0))
    kern = functools.partial(_gla_kernel, carry=carry, n_chunks=n_chunks)
    return pl.pallas_call(
        kern,
        grid=(t // rows,),
        in_specs=[pl.BlockSpec((rows, dqk), row_in), pl.BlockSpec((rows, dqk), row_in),
                  pl.BlockSpec((rows, dvt), row_in), pl.BlockSpec((rows, LANES), row_in),
                  pl.BlockSpec((rows, dvt), row_in),
                  _const_spec(wa2.shape), _const_spec((1, dqk)), _const_spec((1, dv)),
                  pl.BlockSpec(sblk, smap)],
        out_specs=[pl.BlockSpec((rows, dvt), row), pl.BlockSpec(sblk, smap)],
        out_shape=[jax.ShapeDtypeStruct((t, dvt), BF16), jax.ShapeDtypeStruct(s0.shape, F32)],
        scratch_shapes=[pltpu.VMEM((hh, dk, dv), F32)],
        compiler_params=_cparams(("arbitrary",)),
        name="gla_mixer",
    )(q, k, v, a, r, wa2, ba2.reshape(1, dqk), g_head.reshape(1, dv), s0)


def _mixout_kernel(yp_ref, ys_ref, gp_ref, gs_ref, xp_ref, xs_ref, wglu_ref, bglu_ref, gs5_ref,
                   wout_ref, gx_ref, wq_ref, x1_ref, q_ref, *, n_first):
    first = pl.program_id(0) < n_first
    yb = _pick(first, yp_ref, ys_ref)
    y = jnp.concatenate([yb[j] for j in range(yb.shape[0])], axis=1)
    z = jax.nn.gelu(y)
    gate = jax.nn.sigmoid(_dot(z.astype(BF16), wglu_ref[...]) + bglu_ref[...])
    s5 = _rms(z * gate, gs5_ref[...])
    cat = jnp.concatenate([s5.astype(BF16), _pick(first, gp_ref, gs_ref)], axis=1)
    x1 = _pick(first, xp_ref, xs_ref) + _dot(cat, wout_ref[...])
    x1_ref[...] = x1
    q_ref[...] = _dot(_rms(x1, gx_ref[...]).astype(BF16), wq_ref[...]).astype(BF16)


def _mix_out(y_p, y_s, gla_p, gla_s, xp, xs, w_glu, b_glu, g_s5, w_out, g_x, w_q):
    d = xp.shape[1]
    t = xp.shape[0] + xs.shape[0]
    ns = y_p.shape[0]
    ds5 = ns * LANES
    dg = gla_p.shape[1]
    tm = TOKEN_TILE // 2
    n_first = xp.shape[0] // tm
    row = lambda i: (i, 0)
    return pl.pallas_call(
        functools.partial(_mixout_kernel, n_first=n_first),
        grid=(t // tm,),
        in_specs=(_pair_specs((ns, tm, LANES), 1, n_first) + _pair_specs((tm, dg), 0, n_first)
                  + _pair_specs((tm, d), 0, n_first)
                  + [_const_spec((ds5, ds5)), _const_spec((1, ds5)), _const_spec((1, ds5)),
                     _const_spec((ds5 + dg, d)), _const_spec((1, d)), _const_spec((d, d))]),
        out_specs=[pl.BlockSpec((tm, d), row), pl.BlockSpec((tm, d), row)],
        out_shape=[jax.ShapeDtypeStruct((t, d), F32), jax.ShapeDtypeStruct((t, d), BF16)],
        compiler_params=_cparams(("arbitrary",)),
        name="mix_out",
    )(y_p, y_s, gla_p, gla_s, xp, xs, w_glu.astype(BF16), b_glu.reshape(1, ds5),
      g_s5.reshape(1, ds5), w_out.astype(BF16), g_x.reshape(1, d), w_q.astype(BF16))


def _xattn_kernel(q_ref, mk_ref, mv_ref, x1_ref, wo_ref, gf_ref, wrh_ref, wrl_ref,
                  x2_ref, xn_ref, lg_ref, *cache_scratch, nb):
    d = q_ref.shape[1]
    lt = q_ref.shape[0] // nb
    hd = d // X_HEADS
    scale = hd ** -0.5
    if cache_scratch:
        kbuf, vbuf, sem = cache_scratch
        i = pl.program_id(0)

        def fetch(step, slot):
            for b in range(nb):
                for h in range(X_HEADS):
                    for src, dst in ((mk_ref, kbuf), (mv_ref, vbuf)):
                        pltpu.make_async_copy(src.at[step * nb + b, :, h, :],
                                              dst.at[slot, b, :, pl.ds(h * hd, hd)],
                                              sem.at[slot]).start()

        @pl.when(i == 0)
        def _():
            fetch(0, 0)

        @pl.when(i + 1 < pl.num_programs(0))
        def _():
            fetch(i + 1, (i + 1) % 2)

        slot = i % 2
        for buf in (kbuf, vbuf):
            pltpu.make_async_copy(buf.at[slot], buf.at[slot], sem.at[slot]).wait()
        memory = lambda b: (kbuf[slot, b], vbuf[slot, b])
    else:
        memory = lambda b: (mk_ref[b], mv_ref[b])
    for b in range(nb):
        rows = slice(b * lt, (b + 1) * lt)
        mk, mv = (m.astype(BF16) for m in memory(b))
        outs = []
        for h in range(X_HEADS):
            hs = slice(h * hd, (h + 1) * hd)
            s = _dot_nt(q_ref[rows, hs], mk[:, hs]) * scale
            p = jnp.exp(s - jnp.max(s, axis=-1, keepdims=True))
            denom = jnp.sum(p, axis=-1, keepdims=True)
            outs.append((_dot(p.astype(BF16), mv[:, hs]) / denom).astype(BF16))
        o = jnp.concatenate(outs, axis=1)
        x2 = x1_ref[rows, :] + _dot(o, wo_ref[...])
        x2_ref[rows, :] = x2
        xn = _rms(x2, gf_ref[...])
        xn_ref[rows, :] = xn
        hi, lo = _split_bf16(xn)
        lg_ref[:, rows] = (_dot_nt(wrh_ref[...], hi) + _dot_nt(wrh_ref[...], lo)
                           + _dot_nt(wrl_ref[...], hi))


def _xattn(q, x1, mk, mv, w_o, g_ffn, wr_hi, wr_lo, t_start, t, nb, lt):
    d = q.shape[1]
    nm = mk.shape[1]
    ne = wr_hi.shape[0]
    rows = nb * lt
    assert t % rows == 0 and t_start % rows == 0
    off = t_start // rows
    row = lambda i: (i, 0)
    row_in = lambda i: (i + off, 0)
    if mk.ndim == 3:
        assert nb == 1 and mk.shape[0] == 1
        mem_spec = pl.BlockSpec((1, nm, d), lambda i: (0, 0, 0))
        scratch = []
    else:
        assert mk.shape[0] * lt == t
        mem_spec = pl.BlockSpec(memory_space=pl.ANY)
        scratch = [pltpu.VMEM((2, nb, nm, d), F32), pltpu.VMEM((2, nb, nm, d), F32),
                   pltpu.SemaphoreType.DMA((2,))]
    return pl.pallas_call(
        functools.partial(_xattn_kernel, nb=nb),
        grid=(t // rows,),
        in_specs=[pl.BlockSpec((rows, d), row_in), mem_spec, mem_spec,
                  pl.BlockSpec((rows, d), row_in),
                  _const_spec((d, d)), _const_spec((1, d)),
                  _const_spec((ne, d)), _const_spec((ne, d))],
        scratch_shapes=scratch,
        out_specs=[pl.BlockSpec((rows, d), row), pl.BlockSpec((rows, d), row),
                   pl.BlockSpec((ne, rows), lambda i: (0, i))],
        out_shape=[jax.ShapeDtypeStruct((t, d), F32), jax.ShapeDtypeStruct((t, d), F32),
                   jax.ShapeDtypeStruct((ne, t), F32)],
        compiler_params=_cparams(("arbitrary",)),
        name="mem_xattn",
    )(q, mk, mv, x1, w_o, g_ffn.reshape(1, d), wr_hi, wr_lo)


def _route_kernel(lgp_ref, lgs_ref, xnp_ref, xns_ref, bias_ref, eidx_ref, gate_ref, pos_ref, cnt_ref,
                  xn_ref, run_scr, *, n_first):
    ne, tn = lgp_ref.shape
    first = pl.program_id(0) < n_first
    xn_ref[...] = _pick(first, xnp_ref, xns_ref)

    @pl.when(pl.program_id(0) == 0)
    def _():
        run_scr[...] = jnp.zeros_like(run_scr)

    gsz = ne // N_GROUPS
    sc = jax.nn.sigmoid(_pick(first, lgp_ref, lgs_ref))
    sel = sc + bias_ref[...]
    s3 = sel.reshape(N_GROUPS, gsz, tn)
    ie = lax.broadcasted_iota(jnp.int32, s3.shape, 1).astype(F32)
    m1 = jnp.max(s3, axis=1, keepdims=True)
    first = jnp.min(jnp.where(s3 == m1, ie, float(gsz)), axis=1, keepdims=True)
    m2 = jnp.max(jnp.where(ie == first, -jnp.inf, s3), axis=1, keepdims=True)
    gs = m1 + m2
    ig = lax.broadcasted_iota(jnp.int32, gs.shape, 0)
    grank = jnp.zeros(gs.shape, jnp.int32)
    for g in range(N_GROUPS):
        other = gs[g:g + 1]
        ahead = jnp.where(other > gs, 1, jnp.where(other == gs, (ig > g).astype(jnp.int32), 0))
        grank = grank + ahead
    gkeep = jnp.broadcast_to(grank < TOPK_GROUPS, s3.shape)
    v = jnp.where(gkeep, s3, -jnp.inf).reshape(ne, tn)
    iv = lax.broadcasted_iota(jnp.int32, v.shape, 0)
    rank = jnp.zeros(v.shape, jnp.int32)
    for e in range(ne):
        other = v[e:e + 1]
        ahead = jnp.where(other > v, 1, jnp.where(other == v, (iv > e).astype(jnp.int32), 0))
        rank = rank + ahead
    chosen = rank < TOP_K
    gate = jnp.where(chosen, sc, 0.0)
    gate = gate / jnp.sum(gate, axis=0, keepdims=True) * ROUTE_SCALE
    ti = lax.broadcasted_iota(jnp.int32, (tn, tn), 0)
    tj = lax.broadcasted_iota(jnp.int32, (tn, tn), 1)
    incl = _dot(jnp.where(chosen, 1.0, 0.0).astype(BF16), (ti <= tj).astype(BF16))
    pos = run_scr[...] + incl - 1.0
    run_scr[...] = run_scr[...] + incl[:, tn - 1:tn]
    cnt_ref[...] = run_scr[...].astype(jnp.int32)
    ef = iv.astype(F32)
    for k in range(TOP_K):
        hit = rank == k
        pick = lambda a: jnp.sum(jnp.where(hit, a, 0.0), axis=0, keepdims=True)
        eidx_ref[k:k + 1, :] = pick(ef).astype(jnp.int32)
        gate_ref[k:k + 1, :] = pick(gate)
        pos_ref[k:k + 1, :] = pick(pos).astype(jnp.int32)


def _route(lg_p, lg_s, xn_p, xn_s, router_bias):
    ne = lg_p.shape[0]
    d = xn_p.shape[1]
    t = lg_p.shape[1] + lg_s.shape[1]
    tn = TOKEN_TILE
    n_first = lg_p.shape[1] // tn
    kt = lambda i: (0, i)
    return pl.pallas_call(
        functools.partial(_route_kernel, n_first=n_first),
        grid=(t // tn,),
        in_specs=(_pair_specs((ne, tn), 1, n_first) + _pair_specs((tn, d), 0, n_first)
                  + [_const_spec((ne, 1))]),
        out_specs=[pl.BlockSpec((TOP_K, tn), kt), pl.BlockSpec((TOP_K, tn), kt),
                   pl.BlockSpec((TOP_K, tn), kt), pl.BlockSpec((ne, 1), lambda i: (0, 0)),
                   pl.BlockSpec((tn, d), lambda i: (i, 0))],
        out_shape=[jax.ShapeDtypeStruct((TOP_K, t), jnp.int32),
                   jax.ShapeDtypeStruct((TOP_K, t), F32),
                   jax.ShapeDtypeStruct((TOP_K, t), jnp.int32),
                   jax.ShapeDtypeStruct((ne, 1), jnp.int32),
                   jax.ShapeDtypeStruct((t, d), F32)],
        scratch_shapes=[pltpu.VMEM((ne, 1), F32)],
        compiler_params=_cparams(("arbitrary",)),
        name="router",
    )(lg_p, lg_s, xn_p, xn_s, router_bias.astype(F32).reshape(ne, 1))


def _moe_plan(eidx, posk, counts, n_tiles_max):
    ne = counts.shape[0]
    ntile = (counts + EXPERT_TILE - 1) // EXPERT_TILE
    cum = jnp.cumsum(ntile)
    base = (cum - ntile) * EXPERT_TILE
    n_tiles = cum[-1:]
    experts = jnp.arange(ne, dtype=jnp.int32)
    last_used = jnp.max(jnp.where(ntile > 0, experts, 0))
    tiles = jnp.arange(n_tiles_max, dtype=jnp.int32)
    tile_expert = jnp.sum(tiles[:, None] >= cum[None, :], axis=1)
    tile_expert = jnp.minimum(tile_expert, last_used).astype(jnp.int32)
    slot = (jnp.sum(jnp.where(eidx[..., None] == experts, base, 0), axis=-1) + posk)
    slot = slot.astype(jnp.int32)
    n_k, t = slot.shape
    tok_bits = max(t - 1, 1).bit_length()
    assert (n_tiles_max * EXPERT_TILE) << tok_bits < 2 ** 31
    tok = lax.broadcasted_iota(jnp.int32, slot.shape, 1)
    keys = jnp.sort((slot * (1 << tok_bits) + tok).reshape(n_k * t))
    tok_sorted = keys & ((1 << tok_bits) - 1)
    tok_sorted = jnp.concatenate([tok_sorted, jnp.zeros((EXPERT_TILE,), jnp.int32)])
    first_pair = jnp.cumsum(counts) - counts
    first_tile = cum - ntile
    mine = tile_expert[:, None] == experts[None, :]
    lookup = lambda table: jnp.sum(jnp.where(mine, table[None, :], 0), axis=1)
    pair0 = lookup(first_pair) + (tiles - lookup(first_tile)) * EXPERT_TILE
    pair0 = jnp.clip(pair0, 0, n_k * t).astype(jnp.int32)
    used = ntile > 0
    ordinal = jnp.cumsum(used.astype(jnp.int32)) - 1
    later = jnp.flip(lax.cummin(jnp.flip(jnp.where(used, experts, ne))))
    nxt = jnp.concatenate([later[1:], jnp.full((1,), ne, jnp.int32)])
    nxt = jnp.where(nxt == ne, -1, nxt)
    w_ord = lookup(ordinal).astype(jnp.int32)
    w_next = lookup(nxt).astype(jnp.int32)
    return slot, tile_expert, n_tiles.astype(jnp.int32), tok_sorted, pair0, w_ord, w_next


def _row_copy_wait(ref, n_rows, sem):
    pltpu.make_async_copy(ref.at[pl.ds(0, n_rows)], ref.at[pl.ds(0, n_rows)], sem).wait()


def _expert_kernel(te_ref, nt_ref, pair0_ref, tok_ref, word_ref, wnext_ref, xn_hbm, wg_hbm, wu_hbm,
                   wd_hbm, ys_ref, x_a, x_b, wg_f32, wu_f32, wd_f32, wg_scr, wu_scr, wd_scr,
                   sem_a, sem_b, wsem):
    i = pl.program_id(0)
    n_tiles = nt_ref[0]

    def weight_copies(e, buf):
        return [pltpu.make_async_copy(src.at[e], dst.at[buf], wsem.at[buf])
                for src, dst in ((wg_hbm, wg_f32), (wu_hbm, wu_f32), (wd_hbm, wd_f32))]

    def issue_row(tile, buf, sem, r, queue):
        t = tok_ref[pair0_ref[tile] + r]
        pltpu.make_async_copy(xn_hbm.at[pl.ds(t, 1)], buf.at[pl.ds(r, 1)],
                              sem).start(priority=queue)

    @pl.when(i == 0)
    def _():
        def body(rb, _):
            for u in range(SUBLANES):
                issue_row(0, x_a, sem_a, rb * SUBLANES + u, u % 2)
            return 0
        lax.fori_loop(0, EXPERT_TILE // SUBLANES, body, 0)

    nxt_tile = jnp.where(i + 1 < n_tiles, i + 1, 0)

    def run(cur, sem_cur, nxt, sem_nxt):
        _row_copy_wait(xn_hbm, EXPERT_TILE, sem_cur)

        @pl.when((i == 0) | (te_ref[i] != te_ref[jnp.maximum(i - 1, 0)]))
        def _():
            wbuf = word_ref[i] % 2

            @pl.when(i == 0)
            def _():
                for c in weight_copies(te_ref[0], 0):
                    c.start()

            @pl.when(wnext_ref[i] >= 0)
            def _():
                for c in weight_copies(wnext_ref[i], 1 - wbuf):
                    c.start()

            for c in weight_copies(te_ref[i], wbuf):
                c.wait()
            wg_scr[...] = wg_f32[wbuf].astype(BF16)
            wu_scr[...] = wu_f32[wbuf].astype(BF16)
            wd_scr[...] = wd_f32[wbuf].astype(BF16)

        for r in range(EXPERT_TILE):
            issue_row(nxt_tile, nxt, sem_nxt, r, r % 2)
        x = cur[...].astype(BF16)
        h = jax.nn.silu(_dot(x, wg_scr[...])) * _dot(x, wu_scr[...])
        ys_ref[...] = _dot(h.astype(BF16), wd_scr[...])

        @pl.when(i == n_tiles - 1)
        def _():
            _row_copy_wait(xn_hbm, EXPERT_TILE, sem_nxt)

    @pl.when((i < n_tiles) & (i % 2 == 0))
    def _():
        run(x_a, sem_a, x_b, sem_b)

    @pl.when((i < n_tiles) & (i % 2 == 1))
    def _():
        run(x_b, sem_b, x_a, sem_a)

    @pl.when(i >= n_tiles)
    def _():
        ys_ref[...] = jnp.zeros_like(ys_ref)


def _experts(xn, tile_expert, n_tiles, pair0, tok_sorted, w_ord, w_next, w_gate, w_up, w_down):
    d = xn.shape[1]
    ne, _, de = w_gate.shape
    n_tiles_max = tile_expert.shape[0]
    out_tile = lambda i, *_: (i, 0)
    hbm = pl.BlockSpec(memory_space=pl.ANY)
    return pl.pallas_call(
        _expert_kernel,
        grid_spec=pltpu.PrefetchScalarGridSpec(
            num_scalar_prefetch=6,
            grid=(n_tiles_max,),
            in_specs=[hbm, hbm, hbm, hbm],
            out_specs=pl.BlockSpec((EXPERT_TILE, d), out_tile),
            scratch_shapes=[pltpu.VMEM((EXPERT_TILE, d), F32), pltpu.VMEM((EXPERT_TILE, d), F32),
                            pltpu.VMEM((2, d, de), F32), pltpu.VMEM((2, d, de), F32),
                            pltpu.VMEM((2, de, d), F32),
                            pltpu.VMEM((d, de), BF16), pltpu.VMEM((d, de), BF16),
                            pltpu.VMEM((de, d), BF16),
                            pltpu.SemaphoreType.DMA, pltpu.SemaphoreType.DMA,
                            pltpu.SemaphoreType.DMA((2,))]),
        out_shape=jax.ShapeDtypeStruct((n_tiles_max * EXPERT_TILE, d), F32),
        compiler_params=_cparams(("arbitrary",)),
        name="moe_experts",
    )(tile_expert, n_tiles, pair0, tok_sorted, w_ord, w_next, xn, w_gate, w_up, w_down)


def _combine_kernel(slot_ref, gate_ref, xn_ref, x2p_ref, x2s_ref, sg_ref, su_ref, sd_ref,
                    gf_ref, ys_hbm, yp_ref, ysamp_ref, buf_a, buf_b, base_scr, y_scr, sem_a, sem_b,
                    *, n_first):
    tm = xn_ref.shape[0]
    i = pl.program_id(0)
    n_steps = pl.num_programs(0)
    first = i < n_first
    rb_rows = SUBLANES

    def issue_rows(tile, buf, sem, r0, n):
        for rr in range(n):
            for k in range(TOP_K):
                s = slot_ref[(tile * tm + r0 + rr) * TOP_K + k]
                pltpu.make_async_copy(ys_hbm.at[pl.ds(s, 1)], buf.at[k, pl.ds(r0 + rr, 1)],
                                      sem).start(priority=k % 2)

    @pl.when(i == 0)
    def _():
        def body(r, _):
            issue_rows(0, buf_a, sem_a, r, 1)
            return 0
        lax.fori_loop(0, tm, body, 0)

    x = xn_ref[...].astype(BF16)
    hs = jax.nn.silu(_dot(x, sg_ref[...])) * _dot(x, su_ref[...])
    base_scr[...] = _pick(first, x2p_ref, x2s_ref) + _dot(hs.astype(BF16), sd_ref[...])
    nxt_tile = jnp.where(i + 1 < n_steps, i + 1, 0)

    def run(cur, sem_cur, nxt, sem_nxt):
        _row_copy_wait(ys_hbm, tm * TOP_K, sem_cur)

        for rb in range(tm // rb_rows):
            r0 = rb * rb_rows
            issue_rows(nxt_tile, nxt, sem_nxt, r0, rb_rows)
            acc = base_scr[pl.ds(r0, rb_rows), :]
            g = gate_ref[pl.ds(r0, rb_rows), :]
            for k in range(TOP_K):
                acc = acc + g[:, k:k + 1] * cur[k, pl.ds(r0, rb_rows), :]
            y_scr[pl.ds(r0, rb_rows), :] = _rms(acc, gf_ref[...])

        @pl.when(i == n_steps - 1)
        def _():
            _row_copy_wait(ys_hbm, tm * TOP_K, sem_nxt)

    @pl.when(i % 2 == 0)
    def _():
        run(buf_a, sem_a, buf_b, sem_b)

    @pl.when(i % 2 == 1)
    def _():
        run(buf_b, sem_b, buf_a, sem_a)

    @pl.when(first)
    def _():
        yp_ref[...] = y_scr[...]

    @pl.when(jnp.logical_not(first))
    def _():
        ysamp_ref[...] = y_scr[...]


def _combine(slot_flat, gates, xn, x2_p, x2_s, ws_gate, ws_up, ws_down, g_final, ys):
    d = x2_p.shape[1]
    tp, tsamp = x2_p.shape[0], x2_s.shape[0]
    dsh = ws_gate.shape[1]
    tm = COMBINE_TILE
    n_first = tp // tm
    row = lambda i, *_: (i, 0)
    pair = lambda: _pair_specs((tm, d), 0, n_first)
    return pl.pallas_call(
        functools.partial(_combine_kernel, n_first=n_first),
        grid_spec=pltpu.PrefetchScalarGridSpec(
            num_scalar_prefetch=1,
            grid=((tp + tsamp) // tm,),
            in_specs=[pl.BlockSpec((tm, TOP_K), row), pl.BlockSpec((tm, d), row)] + pair()
                     + [_const_spec((d, dsh)), _const_spec((d, dsh)), _const_spec((dsh, d)),
                        _const_spec((1, d)), pl.BlockSpec(memory_space=pl.ANY)],
            out_specs=pair(),
            scratch_shapes=[pltpu.VMEM((TOP_K, tm, d), F32), pltpu.VMEM((TOP_K, tm, d), F32),
                            pltpu.VMEM((tm, d), F32), pltpu.VMEM((tm, d), F32),
                            pltpu.SemaphoreType.DMA, pltpu.SemaphoreType.DMA]),
        out_shape=[jax.ShapeDtypeStruct((tp, d), F32), jax.ShapeDtypeStruct((tsamp, d), F32)],
        compiler_params=_cparams(("arbitrary",)),
        name="moe_combine",
    )(slot_flat, gates, xn, x2_p, x2_s, ws_gate.astype(BF16), ws_up.astype(BF16),
      ws_down.astype(BF16), g_final.reshape(1, d), ys)


def _moe(xn, x2_p, x2_s, eidx, gate, posk, counts, w_gate, w_up, w_down, ws_gate, ws_up,
         ws_down, g_final):
    t = xn.shape[0]
    ne = w_gate.shape[0]
    n_tiles_max = t * TOP_K // EXPERT_TILE + ne
    slot, tile_expert, n_tiles, tok_sorted, pair0, w_ord, w_next = _moe_plan(
        eidx, posk, counts[:, 0], n_tiles_max)
    ys = _experts(xn, tile_expert, n_tiles, pair0, tok_sorted, w_ord, w_next, w_gate, w_up,
                  w_down)
    return _combine(jnp.transpose(slot).reshape(t * TOP_K), jnp.transpose(gate), xn, x2_p, x2_s,
                    ws_gate, ws_up, ws_down, g_final, ys)


def kernel(x_prompt, x_sample, cache_mem_k, cache_mem_v, state_s5_re, state_s5_im, state_gla, mem_prompt, w_in, w_a2, b_a2, lam_re, lam_im, log_dt, b_re, b_im, c_re, c_im, d_skip, w_glu, b_glu, g_s5_out, g_gla_head, w_out, g_mix, g_xattn, g_mem, w_q, w_k, w_v, w_o, g_ffn, w_router, router_bias, w_gate, w_up, w_down, ws_gate, ws_up, ws_down, g_final):
    depth = w_in.shape[0]
    assert depth == 1, "one trunk layer"
    bp, lp, d = x_prompt.shape
    bs, ls, _ = x_sample.shape
    assert bp == 1 and ls == CHUNK and lp % CHUNK == 0
    n_mem = mem_prompt.shape[1]
    g, p = lam_re.shape[1:]
    d_s5 = g * S5_GROUP
    d_qk = GLA_HEADS * GLA_DK
    d_gla = w_out.shape[1] - d_s5
    ns = d_s5 // LANES
    ne = w_router.shape[2]
    tp, tsamp = bp * lp, bs * ls
    xp = x_prompt.reshape(tp, d)
    xs = x_sample.reshape(tsamp, d)
    sq = lambda a: a.reshape(a.shape[1:])

    mk_p, mv_p = _memory_kv(mem_prompt.reshape(n_mem, d), sq(g_mem), sq(w_k), sq(w_v))

    u, q, k, v, a, r = _in_proj(xp, xs, sq(g_mix), sq(w_in), d_s5, d_qk, d_gla)

    ksm, psm, qsm, step_m, carry_m, a8 = _s5_coeffs(
        sq(lam_re), sq(lam_im), sq(log_dt), sq(b_re), sq(b_im), sq(c_re), sq(c_im), sq(d_skip))
    tables = tuple(_s5_expand(ksm, psm, qsm)) + (step_m, carry_m, a8)
    zero_h = jnp.zeros((ns, bp, 2 * GROUPS_PER_SLAB * p), F32)
    rows_p = lp // S5_BLOCK
    y_p, h_p = _s5_mixer(u, tables, zero_h, 0, tp, rows_per_seq=rows_p,
                         tile_rows=math.gcd(rows_p, 512))
    h0_s = _pack_s5_state(sq(state_s5_re), sq(state_s5_im), ns)
    y_s, h_s = _s5_mixer(u, tables, h0_s, tp, tsamp, rows_per_seq=ls // S5_BLOCK,
                         tile_rows=tsamp // S5_BLOCK)

    wa2 = jnp.pad(sq(w_a2), ((0, LANES - GLA_RANK), (0, 0)))
    zero_s = jnp.zeros((bp,) + state_gla.shape[2:], F32)
    gla_p, sg_p = _gla_mixer(q, k, v, a, r, wa2, sq(b_a2), sq(g_gla_head), zero_s, 0, tp,
                             carry=True, n_chunks=4)
    gla_s, sg_s = _gla_mixer(q, k, v, a, r, wa2, sq(b_a2), sq(g_gla_head), sq(state_gla), tp,
                             tsamp, carry=False, n_chunks=4)

    x1, xq = _mix_out(y_p, y_s, gla_p, gla_s, xp, xs, sq(w_glu), sq(b_glu), sq(g_s5_out),
                      sq(w_out), sq(g_xattn), sq(w_q))

    wr_hi, wr_lo = _split_bf16(jnp.transpose(sq(w_router)))
    wo = sq(w_o).astype(BF16)
    x2_p, xn_p, lg_p = _xattn(xq, x1, mk_p[None], mv_p[None], wo, sq(g_ffn), wr_hi, wr_lo,
                              0, tp, nb=1, lt=TOKEN_TILE)
    x2_s, xn_s, lg_s = _xattn(xq, x1, sq(cache_mem_k), sq(cache_mem_v), wo, sq(g_ffn), wr_hi,
                              wr_lo, tp, tsamp, nb=2, lt=ls)

    eidx, gate, posk, counts, xn = _route(lg_p, lg_s, xn_p, xn_s, sq(router_bias))
    y_p2, y_s2 = _moe(xn, x2_p, x2_s, eidx, gate, posk, counts, sq(w_gate), sq(w_up),
                      sq(w_down), sq(ws_gate), sq(ws_up), sq(ws_down), g_final)

    xh = d // X_HEADS
    re_p, im_p = _unpack_s5_state(h_p, g, p)
    re_s, im_s = _unpack_s5_state(h_s, g, p)
    return (y_p2.reshape(bp, lp, d), y_s2.reshape(bs, ls, d),
            mk_p.reshape(1, bp, n_mem, X_HEADS, xh), mv_p.reshape(1, bp, n_mem, X_HEADS, xh),
            re_p[None], im_p[None], sg_p[None], re_s[None], im_s[None], sg_s[None])
```
